```python
import jax, jax.numpy as jnp
from jax import lax
import numpy as np

D_MODEL = 1024
BATCH = 4
SEQ = 8192
DEPTH = 4

GRID_W = 64
CTX_LEN = 256
D_MIX = D_MODEL
D_CONV = D_MIX // 2
D_ATTN = D_MIX - D_CONV
HEAD_DIM = 64
N_HEADS = D_ATTN // HEAD_DIM
N_KV = 2
GQA = N_HEADS // N_KV
CONV_WIDTH = 3
WINDOW = 128
BLOCK = 128
ROPE_THETA = 10000.0
D_FF = 2816
N_EXPERTS = 8
TOP_K = 2
D_FF_EXPERT = 1408
N_DENSE = (DEPTH + 1) // 2
N_MOE = DEPTH // 2
EPS = 1e-6
KV_OFF = 3 * D_CONV + D_ATTN
D_IN = KV_OFF + 2 * N_KV * HEAD_DIM
SPLITS = (D_CONV, 2 * D_CONV, 3 * D_CONV, KV_OFF, KV_OFF + N_KV * HEAD_DIM)

kernel_name = "hybrid_conv_swa_moe_dit_block"


def rmsnorm(x, g):
    xf = x.astype(jnp.float32)
    y = xf * lax.rsqrt(jnp.mean(xf * xf, axis=-1, keepdims=True) + EPS)
    return (y * g.astype(jnp.float32)).astype(x.dtype)


def modulate(x, shift, scale):
    return x * (1 + scale) + shift


def axial_rope_tables(rows):
    row, col = jnp.meshgrid(jnp.arange(rows, dtype=jnp.float32),
                            jnp.arange(GRID_W, dtype=jnp.float32), indexing='ij')
    n_freq = HEAD_DIM // 4
    inv_freq = ROPE_THETA ** (-jnp.arange(n_freq, dtype=jnp.float32) / n_freq)
    ang_r = row.reshape(-1, 1) * inv_freq
    ang_c = col.reshape(-1, 1) * inv_freq
    ang = jnp.concatenate([ang_r, ang_r, ang_c, ang_c], axis=-1)
    return jnp.cos(ang)[None, :, None, :], jnp.sin(ang)[None, :, None, :]


def apply_axial_rope(t, cos, sin):
    tf = t.astype(jnp.float32)
    tr = tf.reshape(t.shape[:-1] + (2, 2, HEAD_DIM // 4))
    rot = jnp.stack([-tr[..., 1, :], tr[..., 0, :]], axis=-2).reshape(t.shape)
    return (tf * cos + rot * sin).astype(t.dtype)


def split_heads_kv(zkv, k_g):
    B, L = zkv.shape[:2]
    k, v = jnp.split(zkv, 2, axis=-1)
    k = rmsnorm(k.reshape(B, L, N_KV, HEAD_DIM), k_g)
    return k, v.reshape(B, L, N_KV, HEAD_DIM)


def mixer_inputs(xn, w_in, q_g, k_g):
    B, L = xn.shape[:2]
    bg, cg, hu, q, k, v = jnp.split(xn @ w_in, SPLITS, axis=-1)
    q = rmsnorm(q.reshape(B, L, N_HEADS, HEAD_DIM), q_g)
    k = rmsnorm(k.reshape(B, L, N_KV, HEAD_DIM), k_g)
    return bg, cg, hu, q, k, v.reshape(B, L, N_KV, HEAD_DIM)


def short_gated_conv(bg, cg, hu, conv_w):
    u = cg * hu
    y = lax.conv_general_dilated(u, conv_w[:, None, :].astype(u.dtype), window_strides=(1,),
                                 padding='SAME', dimension_numbers=('NWC', 'WIO', 'NWC'),
                                 feature_group_count=u.shape[-1])
    return bg * y


def window_attention(q, k, v, kc, vc, sink):
    B, S = q.shape[:2]
    nb = S // BLOCK
    scale = HEAD_DIM ** -0.5
    f32 = jnp.float32
    qb = q.astype(f32).reshape(B, nb, BLOCK, N_KV, GQA, HEAD_DIM)
    pad = ((0, 0), (BLOCK, BLOCK), (0, 0), (0, 0))
    kp = jnp.pad(k.astype(f32), pad)
    vp = jnp.pad(v.astype(f32), pad)

    def bands(t):
        return jnp.concatenate(
            [t[:, o:o + S].reshape(B, nb, BLOCK, N_KV, HEAD_DIM) for o in (0, BLOCK, 2 * BLOCK)], axis=2)

    kb, vb = bands(kp), bands(vp)
    kcf, vcf = kc.astype(f32), vc.astype(f32)
    s_win = jnp.einsum('bnqgrd,bnkgd->bngrqk', qb, kb) * scale
    qi = jnp.arange(BLOCK)[:, None]
    kj = jnp.arange(3 * BLOCK)[None, :]
    key_pos = jnp.arange(nb)[:, None, None] * BLOCK + kj[None] - BLOCK
    valid = (jnp.abs(kj - BLOCK - qi) <= WINDOW)[None] & (key_pos >= 0) & (key_pos < S)
    s_win = jnp.where(valid[None, :, None, None], s_win, -jnp.inf)
    s_ctx = jnp.einsum('bnqgrd,blgd->bngrql', qb, kcf) * scale
    sk = sink.astype(f32).reshape(1, 1, N_KV, GQA, 1, 1)
    m = jnp.maximum(jnp.maximum(s_win.max(-1, keepdims=True), s_ctx.max(-1, keepdims=True)), sk)
    p_win = jnp.exp(s_win - m)
    p_ctx = jnp.exp(s_ctx - m)
    denom = jnp.exp(sk - m) + p_win.sum(-1, keepdims=True) + p_ctx.sum(-1, keepdims=True)
    o = (jnp.einsum('bngrqk,bnkgd->bnqgrd', p_win, vb)
         + jnp.einsum('bngrql,blgd->bnqgrd', p_ctx, vcf))
    o = o / jnp.moveaxis(denom, 4, 2)
    return o.reshape(B, S, D_ATTN).astype(q.dtype)


def context_attention(qc, kc, vc, sink):
    B, L = qc.shape[:2]
    f32 = jnp.float32
    q = qc.astype(f32).reshape(B, L, N_KV, GQA, HEAD_DIM)
    s = jnp.einsum('blgrd,bmgd->bgrlm', q, kc.astype(f32)) * (HEAD_DIM ** -0.5)
    sk = jnp.broadcast_to(sink.astype(f32).reshape(1, N_KV, GQA, 1, 1), s.shape[:-1] + (1,))
    p = jax.nn.softmax(jnp.concatenate([sk, s], axis=-1), axis=-1)[..., 1:]
    o = jnp.einsum('bgrlm,bmgd->blgrd', p, vc.astype(f32))
    return o.reshape(B, L, D_ATTN).astype(qc.dtype)


def merge_groups(y_conv, y_attn, g_conv, g_attn, w_out):
    return jnp.concatenate([rmsnorm(y_conv, g_conv), rmsnorm(y_attn, g_attn)], axis=-1) @ w_out


def swiglu(x, w1, w3, w2):
    return (jax.nn.silu(x @ w1) * (x @ w3)) @ w2


def moe_swiglu(x, router, w1, w3, w2):
    logits = (x @ router).astype(jnp.float32)
    top_v, top_i = lax.top_k(logits, TOP_K)
    gates = jax.nn.softmax(top_v, axis=-1)
    combine = jnp.sum(jax.nn.one_hot(top_i, N_EXPERTS, dtype=jnp.float32) * gates[..., None], axis=-2)
    combine = combine.astype(x.dtype)
    out = jnp.zeros_like(x)
    for e in range(N_EXPERTS):
        out = out + combine[..., e:e + 1] * swiglu(x, w1[e], w3[e], w2[e])
    return out


def channel_mixer(layer, xn, ffn_w1, ffn_w3, ffn_w2, moe_router, moe_w1, moe_w3, moe_w2):
    i = layer // 2
    if layer % 2 == 0:
        return swiglu(xn, ffn_w1[i], ffn_w3[i], ffn_w2[i])
    return moe_swiglu(xn, moe_router[i], moe_w1[i], moe_w3[i], moe_w2[i])


def setup_inputs(seed: int = 0) -> dict:
    key = jax.random.key(seed)
    ks = jax.random.split(key, 24)
    n = jax.random.normal
    f32 = jnp.float32
    D = D_MODEL
    return {
        "x": n(ks[0], (BATCH, SEQ, D), f32),
        "c": n(ks[1], (BATCH, D), f32),
        "ctx": n(ks[2], (BATCH, CTX_LEN, D), f32),
        "c_ctx": n(ks[3], (D,), f32),
        "w_ada": n(ks[4], (DEPTH, D, 6 * D), f32) * (0.5 * D ** -0.5),
        "b_ada": n(ks[5], (DEPTH, 6 * D), f32) * 0.01,
        "norm1_g": 1.0 + 0.02 * n(ks[6], (DEPTH, D), f32),
        "norm2_g": 1.0 + 0.02 * n(ks[7], (DEPTH, D), f32),
        "w_in": n(ks[8], (DEPTH, D, D_IN), f32) * D ** -0.5,
        "conv_w": n(ks[9], (DEPTH, CONV_WIDTH, D_CONV), f32) * CONV_WIDTH ** -0.5,
        "q_norm_g": 1.0 + 0.02 * n(ks[10], (DEPTH, HEAD_DIM), f32),
        "k_norm_g": 1.0 + 0.02 * n(ks[11], (DEPTH, HEAD_DIM), f32),
        "attn_sink": 0.5 * n(ks[12], (DEPTH, N_HEADS), f32),
        "out_norm_conv_g": 1.0 + 0.02 * n(ks[13], (DEPTH, D_CONV), f32),
        "out_norm_attn_g": 1.0 + 0.02 * n(ks[14], (DEPTH, D_ATTN), f32),
        "w_out": n(ks[15], (DEPTH, D_MIX, D), f32) * D_MIX ** -0.5,
        "ffn_w1": n(ks[16], (N_DENSE, D, D_FF), f32) * D ** -0.5,
        "ffn_w3": n(ks[17], (N_DENSE, D, D_FF), f32) * D ** -0.5,
        "ffn_w2": n(ks[18], (N_DENSE, D_FF, D), f32) * D_FF ** -0.5,
        "moe_router": n(ks[19], (N_MOE, D, N_EXPERTS), f32) * D ** -0.5,
        "moe_w1": n(ks[20], (N_MOE, N_EXPERTS, D, D_FF_EXPERT), f32) * D ** -0.5,
        "moe_w3": n(ks[21], (N_MOE, N_EXPERTS, D, D_FF_EXPERT), f32) * D ** -0.5,
        "moe_w2": n(ks[22], (N_MOE, N_EXPERTS, D_FF_EXPERT, D), f32) * D_FF_EXPERT ** -0.5,
    }


def reference(x, c, ctx, c_ctx, w_ada, b_ada, norm1_g, norm2_g, w_in, conv_w, q_norm_g, k_norm_g,
              attn_sink, out_norm_conv_g, out_norm_attn_g, w_out, ffn_w1, ffn_w3, ffn_w2,
              moe_router, moe_w1, moe_w3, moe_w2):
    B, S, D = x.shape
    ROWS = S // GRID_W
    cos, sin = axial_rope_tables(ROWS)
    sc = jax.nn.silu(c)[:, None, :]
    scc = jax.nn.silu(c_ctx)[None, None, :]
    h, hc = x, ctx
    for layer in range(DEPTH):
        last = layer == DEPTH - 1
        sh1, sc1, g1, sh2, sc2, g2 = jnp.split(sc @ w_ada[layer] + b_ada[layer], 6, axis=-1)
        n_mod_c = 2 if last else 6
        mod_c = jnp.split(scc @ w_ada[layer][:, :n_mod_c * D] + b_ada[layer][:n_mod_c * D], n_mod_c, axis=-1)

        xn = modulate(rmsnorm(h, norm1_g[layer]), sh1, sc1)
        xnc = modulate(rmsnorm(hc, norm1_g[layer]), mod_c[0], mod_c[1])
        bg, cg, hu, q, k, v = mixer_inputs(xn, w_in[layer], q_norm_g[layer], k_norm_g[layer])
        q = apply_axial_rope(q, cos, sin)
        k = apply_axial_rope(k, cos, sin)
        if last:
            kc, vc = split_heads_kv(xnc @ w_in[layer][:, KV_OFF:], k_norm_g[layer])
        else:
            bgc, cgc, huc, qc, kc, vc = mixer_inputs(xnc, w_in[layer], q_norm_g[layer], k_norm_g[layer])
        y_conv = short_gated_conv(bg, cg, hu, conv_w[layer])
        y_attn = window_attention(q, k, v, kc, vc, attn_sink[layer])
        y = merge_groups(y_conv, y_attn, out_norm_conv_g[layer], out_norm_attn_g[layer], w_out[layer])
        h = h + g1 * y
        if not last:
            yc_conv = short_gated_conv(bgc, cgc, huc, conv_w[layer])
            yc_attn = context_attention(qc, kc, vc, attn_sink[layer])
            yc = merge_groups(yc_conv, yc_attn, out_norm_conv_g[layer], out_norm_attn_g[layer], w_out[layer])
            hc = hc + mod_c[2] * yc

        xn2 = modulate(rmsnorm(h, norm2_g[layer]), sh2, sc2)
        h = h + g2 * channel_mixer(layer, xn2, ffn_w1, ffn_w3, ffn_w2, moe_router, moe_w1, moe_w3, moe_w2)
        if not last:
            xnc2 = modulate(rmsnorm(hc, norm2_g[layer]), mod_c[3], mod_c[4])
            hc = hc + mod_c[5] * channel_mixer(layer, xnc2, ffn_w1, ffn_w3, ffn_w2,
                                               moe_router, moe_w1, moe_w3, moe_w2)
    return h
```

```python
import functools

import jax
import jax.numpy as jnp
from jax import lax
from jax.experimental import pallas as pl
from jax.experimental.pallas import tpu as pltpu

D_MODEL = 1024
GRID_W = 64
HEAD_DIM = 64
D_CONV = 512
D_ATTN = 512
N_HEADS = 8
N_KV = 2
WINDOW = 128
ROPE_THETA = 10000.0
N_EXPERTS = 8
EPS = 1e-6
KV_OFF = 3 * D_CONV + D_ATTN
D_IN = KV_OFF + 2 * N_KV * HEAD_DIM
LANES = 128
VMEM_LIMIT = 48 * 1024 * 1024

F32 = jnp.float32
BF16 = jnp.bfloat16
NEG_INF = float("-inf")


def _params(n_axes):
    return pltpu.CompilerParams(dimension_semantics=("arbitrary",) * n_axes,
                                vmem_limit_bytes=VMEM_LIMIT)


def _sigmoid(x):
    return 1.0 / (1.0 + jnp.exp(-x))


def _mod_kernel(c_ref, w_ref, b_ref, o_ref):
    c = c_ref[...]
    s = c * _sigmoid(c)
    o_ref[...] = jnp.dot(s, w_ref[...], preferred_element_type=F32) + b_ref[...]


def _modulation(c8, w_ada, b_ada):
    depth, d, n = w_ada.shape
    tn = 1536
    return pl.pallas_call(
        _mod_kernel,
        grid=(depth, n // tn),
        in_specs=[pl.BlockSpec((8, d), lambda l, j: (0, 0)),
                  pl.BlockSpec((None, d, tn), lambda l, j: (l, 0, j)),
                  pl.BlockSpec((None, 1, tn), lambda l, j: (l, 0, j))],
        out_specs=pl.BlockSpec((None, 8, tn), lambda l, j: (l, 0, j)),
        out_shape=jax.ShapeDtypeStruct((depth, 8, n), F32),
        compiler_params=_params(2),
        name="adaln_mod",
    )(c8, w_ada, b_ada.reshape(depth, 1, n))


def _norm_mod(x, g, sh, sc):
    ms = jnp.mean(x * x, axis=-1, keepdims=True)
    return (x * lax.rsqrt(ms + EPS) * g) * (1.0 + sc) + sh


def _inproj_kernel(*refs, rope):
    if rope:
        (h_ref, g_ref, sh_ref, sc_ref, w_ref, gq_ref, gk_ref, gm_ref,
         cos_ref, sa_ref, sb_ref, cu_ref, q_ref, kv_ref) = refs
    else:
        (h_ref, g_ref, sh_ref, sc_ref, w_ref, gq_ref, gk_ref, gm_ref,
         cu_ref, q_ref, kv_ref) = refs
    xn = _norm_mod(h_ref[...], g_ref[...], sh_ref[...], sc_ref[...])
    y = jnp.dot(xn.astype(BF16), w_ref[...], preferred_element_type=F32)
    cu_ref[:, 0:D_CONV] = y[:, 0:D_CONV].astype(BF16)
    cu_ref[:, D_CONV:2 * D_CONV] = (y[:, D_CONV:2 * D_CONV] * y[:, 2 * D_CONV:3 * D_CONV]).astype(BF16)

    gm = gm_ref[...]

    def head_norm(t, gain):
        w = t.shape[1]
        ss = jnp.dot((t * t).astype(BF16), gm[0:w, 0:w], preferred_element_type=F32)
        return t * lax.rsqrt(ss * (1.0 / HEAD_DIM) + EPS) * gain

    def rot(t):
        if not rope:
            return t
        return (t * cos_ref[...] + pltpu.roll(t, LANES - 16, 1) * sa_ref[...]
                + pltpu.roll(t, 16, 1) * sb_ref[...])

    gq = gq_ref[...]
    for j in range(D_ATTN // 256):
        lo = 3 * D_CONV + 256 * j
        qn = head_norm(y[:, lo:lo + 256], gq[:, 256 * j:256 * j + 256])
        for c in range(2):
            q_ref[:, 256 * j + LANES * c:256 * j + LANES * (c + 1)] = rot(
                qn[:, LANES * c:LANES * (c + 1)]).astype(BF16)
    k = rot(head_norm(y[:, KV_OFF:KV_OFF + LANES], gk_ref[...]))
    v = y[:, KV_OFF + LANES:KV_OFF + 2 * LANES]
    kv_ref[:, 0:LANES] = k.astype(BF16)
    kv_ref[:, LANES:2 * LANES] = pltpu.roll(k, HEAD_DIM, 1).astype(BF16)
    kv_ref[:, 2 * LANES:3 * LANES] = v.astype(BF16)
    kv_ref[:, 3 * LANES:4 * LANES] = pltpu.roll(v, HEAD_DIM, 1).astype(BF16)


def _inproj(h, g, sh, sc, w, gq, gk, gm, tables, *, tm, seq):
    t, d = h.shape
    tpb = seq // tm
    rope = tables is not None
    row = lambda i: (i, 0)
    fix = lambda i: (0, 0)
    mod = lambda i: (i // tpb, 0, 0)
    in_specs = [pl.BlockSpec((tm, d), row),
                pl.BlockSpec((1, d), fix),
                pl.BlockSpec((None, 1, d), mod),
                pl.BlockSpec((None, 1, d), mod),
                pl.BlockSpec((d, D_IN), fix),
                pl.BlockSpec((1, D_ATTN), fix),
                pl.BlockSpec((1, LANES), fix),
                pl.BlockSpec((256, 256), fix)]
    args = [h, g, sh, sc, w, gq, gk, gm]
    if rope:
        in_specs += [pl.BlockSpec((tm, LANES), lambda i: (i % tpb, 0))] * 3
        args += list(tables)
    return pl.pallas_call(
        functools.partial(_inproj_kernel, rope=rope),
        grid=(t // tm,),
        in_specs=in_specs,
        out_specs=[pl.BlockSpec((tm, 2 * D_CONV), row),
                   pl.BlockSpec((tm, D_ATTN), row),
                   pl.BlockSpec((tm, 4 * LANES), row)],
        out_shape=[jax.ShapeDtypeStruct((t, 2 * D_CONV), BF16),
                   jax.ShapeDtypeStruct((t, D_ATTN), BF16),
                   jax.ShapeDtypeStruct((t, 4 * LANES), BF16)],
        compiler_params=_params(1),
        name="inproj_rope" if rope else "inproj_ctx",
    )(*args)


_NT = (((1,), (1,)), ((), ()))


def _mixer_kernel(*refs, tq, windowed):
    if windowed:
        (sink_ref, h_ref, cu_ref, cup_ref, cun_ref, q_ref, kvp_ref, kv_ref, kvn_ref, kvc_ref,
         cw_ref, gc_ref, ga_ref, wo_ref, g1_ref, out_ref, kw_ref, ya_ref) = refs
    else:
        (sink_ref, h_ref, cu_ref, q_ref, kvc_ref,
         cw_ref, gc_ref, ga_ref, wo_ref, g1_ref, out_ref, ya_ref) = refs
    i = pl.program_id(1)
    nt = pl.num_programs(1)
    nsub = tq // WINDOW

    cu = cu_ref[...]
    bg = cu[:, 0:D_CONV].astype(F32)
    u = cu[:, D_CONV:2 * D_CONV].astype(F32)
    rows = lax.broadcasted_iota(jnp.int32, (tq, 1), 0)
    if windowed:
        up_row = cup_ref[:, D_CONV:2 * D_CONV].astype(F32)[15:16, :]
        un_row = cun_ref[:, D_CONV:2 * D_CONV].astype(F32)[0:1, :]
        up_row = jnp.where(i > 0, up_row, 0.0)
        un_row = jnp.where(i < nt - 1, un_row, 0.0)
    else:
        up_row = jnp.zeros((1, D_CONV), F32)
        un_row = jnp.zeros((1, D_CONV), F32)
    u_prev = jnp.where(rows == 0, up_row, pltpu.roll(u, 1, 0))
    u_next = jnp.where(rows == tq - 1, un_row, pltpu.roll(u, tq - 1, 0))
    cw = cw_ref[...]
    yc = bg * (cw[0:1, :] * u_prev + cw[1:2, :] * u + cw[2:3, :] * u_next)
    yc = yc * lax.rsqrt(jnp.mean(yc * yc, axis=-1, keepdims=True) + EPS) * gc_ref[...]

    if windowed:
        kw_ref[0:WINDOW, :] = kvp_ref[...]
        kw_ref[WINDOW:WINDOW + tq, :] = kv_ref[...]
        kw_ref[WINDOW + tq:2 * WINDOW + tq, :] = kvn_ref[...]
    lane_lo = lax.broadcasted_iota(jnp.int32, (WINDOW, LANES), 1) < HEAD_DIM
    kvc = kvc_ref[...]
    ctx_parts = [kvc[:, LANES * c:LANES * (c + 1)] for c in range(4)]

    def sub_block(s, carry):
        r0 = pl.multiple_of(s * WINDOW, WINDOW)
        if windowed:
            kwin = kw_ref[pl.ds(r0, 3 * WINDOW), :]
            win_parts = [kwin[:, LANES * c:LANES * (c + 1)] for c in range(4)]
            col = lax.broadcasted_iota(jnp.int32, (WINDOW, 3 * WINDOW), 1)
            rw = lax.broadcasted_iota(jnp.int32, (WINDOW, 3 * WINDOW), 0)
            first = jnp.logical_and(i == 0, s == 0)
            last = jnp.logical_and(i == nt - 1, s == nsub - 1)
            c_lo = jnp.where(first, WINDOW, 0)
            c_hi = jnp.where(last, 2 * WINDOW, 3 * WINDOW)
            valid = (col >= rw) & (col <= rw + 2 * WINDOW) & (col >= c_lo) & (col < c_hi)
        for j in range(N_HEADS // 2):
            grp = j // 2
            qp = q_ref[pl.ds(r0, WINDOW), LANES * j:LANES * (j + 1)]
            zero = jnp.zeros_like(qp)
            halves = []
            for odd in range(2):
                qh = jnp.where(lane_lo, zero, qp) if odd else jnp.where(lane_lo, qp, zero)
                pick = (grp + odd) % 2
                sk = sink_ref[2 * j + odd]
                s_c = lax.dot_general(qh, ctx_parts[pick], _NT, preferred_element_type=F32)
                m = jnp.maximum(jnp.max(s_c, axis=-1, keepdims=True), sk)
                if windowed:
                    s_w = lax.dot_general(qh, win_parts[pick], _NT, preferred_element_type=F32)
                    s_w = jnp.where(valid, s_w, NEG_INF)
                    m = jnp.maximum(m, jnp.max(s_w, axis=-1, keepdims=True))
                p_c = jnp.exp(s_c - m)
                den = jnp.exp(sk - m) + jnp.sum(p_c, axis=-1, keepdims=True)
                o = jnp.dot(p_c.astype(BF16), ctx_parts[2 + pick], preferred_element_type=F32)
                if windowed:
                    p_w = jnp.exp(s_w - m)
                    den = den + jnp.sum(p_w, axis=-1, keepdims=True)
                    o = o + jnp.dot(p_w.astype(BF16), win_parts[2 + pick], preferred_element_type=F32)
                halves.append(o / den)
            ya_ref[pl.ds(r0, WINDOW), LANES * j:LANES * (j + 1)] = jnp.where(lane_lo, halves[0], halves[1])
        return carry

    lax.fori_loop(0, nsub, sub_block, 0)

    ya = ya_ref[...]
    ya = ya * lax.rsqrt(jnp.mean(ya * ya, axis=-1, keepdims=True) + EPS) * ga_ref[...]
    y = (jnp.dot(yc.astype(BF16), wo_ref[0:D_CONV, :], preferred_element_type=F32)
         + jnp.dot(ya.astype(BF16), wo_ref[D_CONV:2 * D_CONV, :], preferred_element_type=F32))
    out_ref[...] = h_ref[...] + g1_ref[...] * y


def _mixer(h, cu, q, kv, kvc, sink, cw, gc, ga, wo, g1, *, tq, seq, ctx_len, windowed):
    t, d = h.shape
    nt = seq // tq
    nb = t // seq
    row = lambda b, i: (b * nt + i, 0)
    fix = lambda b, i: (0, 0)
    smem = pl.BlockSpec(memory_space=pltpu.SMEM)
    tail = [pl.BlockSpec((3, D_CONV), fix),
            pl.BlockSpec((1, D_CONV), fix),
            pl.BlockSpec((1, D_ATTN), fix),
            pl.BlockSpec((d, d), fix),
            pl.BlockSpec((None, 1, d), lambda b, i: (b, 0, 0))]
    ctx_spec = pl.BlockSpec((ctx_len, 4 * LANES), lambda b, i: (b, 0))
    if windowed:
        r16 = tq // 16
        n16 = t // 16
        rw = tq // WINDOW
        nw = t // WINDOW
        in_specs = [smem,
                    pl.BlockSpec((tq, d), row),
                    pl.BlockSpec((tq, 2 * D_CONV), row),
                    pl.BlockSpec((16, 2 * D_CONV), lambda b, i: (jnp.maximum((b * nt + i) * r16 - 1, 0), 0)),
                    pl.BlockSpec((16, 2 * D_CONV), lambda b, i: (jnp.minimum((b * nt + i + 1) * r16, n16 - 1), 0)),
                    pl.BlockSpec((tq, D_ATTN), row),
                    pl.BlockSpec((WINDOW, 4 * LANES), lambda b, i: (jnp.maximum((b * nt + i) * rw - 1, 0), 0)),
                    pl.BlockSpec((tq, 4 * LANES), row),
                    pl.BlockSpec((WINDOW, 4 * LANES), lambda b, i: (jnp.minimum((b * nt + i + 1) * rw, nw - 1), 0)),
                    ctx_spec] + tail
        args = [sink, h, cu, cu, cu, q, kv, kv, kv, kvc, cw, gc, ga, wo, g1]
        scratch = [pltpu.VMEM((tq + 2 * WINDOW, 4 * LANES), BF16), pltpu.VMEM((tq, D_ATTN), F32)]
    else:
        in_specs = [smem,
                    pl.BlockSpec((tq, d), row),
                    pl.BlockSpec((tq, 2 * D_CONV), row),
                    pl.BlockSpec((tq, D_ATTN), row),
                    ctx_spec] + tail
        args = [sink, h, cu, q, kvc, cw, gc, ga, wo, g1]
        scratch = [pltpu.VMEM((tq, D_ATTN), F32)]
    return pl.pallas_call(
        functools.partial(_mixer_kernel, tq=tq, windowed=windowed),
        grid=(nb, nt),
        in_specs=in_specs,
        out_specs=pl.BlockSpec((tq, d), row),
        out_shape=jax.ShapeDtypeStruct((t, d), F32),
        scratch_shapes=scratch,
        compiler_params=_params(2),
        name="mixer_win" if windowed else "mixer_ctx",
    )(*args)


def _ffn_kernel(*refs, moe):
    if moe:
        (h_ref, g_ref, sh_ref, sc_ref, gate_ref, r_ref, w1_ref, w3_ref, w2_ref,
         out_ref, xn_ref, acc_ref, comb_ref) = refs
    else:
        (h_ref, g_ref, sh_ref, sc_ref, gate_ref, w1_ref, w3_ref, w2_ref,
         out_ref, xn_ref, acc_ref) = refs
    e = pl.program_id(1)
    tm = h_ref.shape[0]

    @pl.when(e == 0)
    def _():
        xn = _norm_mod(h_ref[...], g_ref[...], sh_ref[...], sc_ref[...])
        xn_ref[...] = xn.astype(BF16)
        acc_ref[...] = jnp.zeros_like(acc_ref)
        if moe:
            logits = jnp.dot(xn, r_ref[...], preferred_element_type=F32)
            lane = lax.broadcasted_iota(jnp.int32, (tm, LANES), 1)
            lg = jnp.where(lane < N_EXPERTS, logits, NEG_INF)
            m1 = jnp.max(lg, axis=-1, keepdims=True)
            i1 = jnp.min(jnp.where(lg == m1, lane, LANES), axis=-1, keepdims=True)
            lg2 = jnp.where(lane == i1, NEG_INF, lg)
            m2 = jnp.max(lg2, axis=-1, keepdims=True)
            i2 = jnp.min(jnp.where(lg2 == m2, lane, LANES), axis=-1, keepdims=True)
            e2 = jnp.exp(m2 - m1)
            comb_ref[...] = (jnp.where(lane == i1, 1.0 / (1.0 + e2), 0.0)
                             + jnp.where(lane == i2, e2 / (1.0 + e2), 0.0))

    xb = xn_ref[...]
    h1 = jnp.dot(xb, w1_ref[...], preferred_element_type=F32)
    h3 = jnp.dot(xb, w3_ref[...], preferred_element_type=F32)
    a = h1 * _sigmoid(h1) * h3
    if moe:
        lane = lax.broadcasted_iota(jnp.int32, (tm, LANES), 1)
        a = a * jnp.sum(jnp.where(lane == e, comb_ref[...], 0.0), axis=-1, keepdims=True)
    acc_ref[...] += jnp.dot(a.astype(BF16), w2_ref[...], preferred_element_type=F32)

    @pl.when(e == pl.num_programs(1) - 1)
    def _():
        out_ref[...] = h_ref[...] + gate_ref[...] * acc_ref[...]


def _ffn(h, g, sh, sc, gate, w1, w3, w2, router, *, tm, seq):
    t, d = h.shape
    tpb = seq // tm
    moe = router is not None
    row = lambda i, e: (i, 0)
    fix = lambda i, e: (0, 0)
    mod = lambda i, e: (i // tpb, 0, 0)
    in_specs = [pl.BlockSpec((tm, d), row),
                pl.BlockSpec((1, d), fix),
                pl.BlockSpec((None, 1, d), mod),
                pl.BlockSpec((None, 1, d), mod),
                pl.BlockSpec((None, 1, d), mod)]
    args = [h, g, sh, sc, gate]
    scratch = [pltpu.VMEM((tm, d), BF16), pltpu.VMEM((tm, d), F32)]
    if moe:
        ne, _, fe = w1.shape
        in_specs += [pl.BlockSpec((d, LANES), fix),
                     pl.BlockSpec((None, d, fe), lambda i, e: (e, 0, 0)),
                     pl.BlockSpec((None, d, fe), lambda i, e: (e, 0, 0)),
                     pl.BlockSpec((None, fe, d), lambda i, e: (e, 0, 0))]
        args += [router, w1, w3, w2]
        scratch += [pltpu.VMEM((tm, LANES), F32)]
    else:
        ne = 2
        fe = w1.shape[1] // ne
        in_specs += [pl.BlockSpec((d, fe), lambda i, e: (0, e)),
                     pl.BlockSpec((d, fe), lambda i, e: (0, e)),
                     pl.BlockSpec((fe, d), lambda i, e: (e, 0))]
        args += [w1, w3, w2]
    return pl.pallas_call(
        functools.partial(_ffn_kernel, moe=moe),
        grid=(t // tm, ne),
        in_specs=in_specs,
        out_specs=pl.BlockSpec((tm, d), row),
        out_shape=jax.ShapeDtypeStruct((t, d), F32),
        scratch_shapes=scratch,
        compiler_params=_params(2),
        name="ffn_moe" if moe else "ffn_dense",
    )(*args)


def _rope_tables(seq):
    rows = seq // GRID_W
    row, col = jnp.meshgrid(jnp.arange(rows, dtype=F32), jnp.arange(GRID_W, dtype=F32), indexing='ij')
    n_freq = HEAD_DIM // 4
    inv_freq = ROPE_THETA ** (-jnp.arange(n_freq, dtype=F32) / n_freq)
    ang_r = row.reshape(-1, 1) * inv_freq
    ang_c = col.reshape(-1, 1) * inv_freq
    ang = jnp.concatenate([ang_r, ang_r, ang_c, ang_c], axis=-1)
    cos, sin = jnp.cos(ang), jnp.sin(ang)
    first = (jnp.arange(HEAD_DIM) % (2 * n_freq)) < n_freq
    sin_a = jnp.where(first, -sin, 0.0)
    sin_b = jnp.where(first, 0.0, sin)
    rep = LANES // HEAD_DIM
    return tuple(jnp.tile(t, (1, rep)) for t in (cos, sin_a, sin_b))


def kernel(x, c, ctx, c_ctx, w_ada, b_ada, norm1_g, norm2_g, w_in, conv_w, q_norm_g, k_norm_g,
           attn_sink, out_norm_conv_g, out_norm_attn_g, w_out, ffn_w1, ffn_w3, ffn_w2,
           moe_router, moe_w1, moe_w3, moe_w2):
    b, s, d = x.shape
    lc = ctx.shape[1]
    depth = w_ada.shape[0]
    assert d == D_MODEL and s % 512 == 0 and lc % 256 == 0 and b + 1 <= 8

    c8 = jnp.zeros((8, d), F32).at[0:b].set(c).at[b].set(c_ctx)
    mod = _modulation(c8, w_ada, b_ada)

    tables = _rope_tables(s)
    ids = jnp.arange(256)
    gm = (ids[:, None] // HEAD_DIM == ids[None, :] // HEAD_DIM).astype(BF16)
    scale = HEAD_DIM ** -0.5

    h = x.reshape(b * s, d)
    hc = ctx.reshape(b * lc, d)
    for layer in range(depth):
        last = layer == depth - 1
        m = mod[layer]
        lat = [m[0:b, k * d:(k + 1) * d].reshape(b, 1, d) for k in range(6)]
        cx = [jnp.broadcast_to(m[b:b + 1, k * d:(k + 1) * d].reshape(1, 1, d), (b, 1, d)) for k in range(6)]
        w_in_b = w_in[layer].astype(BF16)
        w_out_b = w_out[layer].astype(BF16)
        g1n = norm1_g[layer].reshape(1, d)
        g2n = norm2_g[layer].reshape(1, d)
        gq = (jnp.tile(q_norm_g[layer], N_HEADS) * scale).reshape(1, D_ATTN)
        gk = jnp.tile(k_norm_g[layer], N_KV).reshape(1, LANES)
        gc = out_norm_conv_g[layer].reshape(1, D_CONV)
        ga = out_norm_attn_g[layer].reshape(1, D_ATTN)
        sink = attn_sink[layer]
        cw = conv_w[layer]

        cu, q, kv = _inproj(h, g1n, lat[0], lat[1], w_in_b, gq, gk, gm, tables, tm=512, seq=s)
        cuc, qc, kvc = _inproj(hc, g1n, cx[0], cx[1], w_in_b, gq, gk, gm, None, tm=lc, seq=lc)
        h = _mixer(h, cu, q, kv, kvc, sink, cw, gc, ga, w_out_b, lat[2],
                   tq=512, seq=s, ctx_len=lc, windowed=True)
        if not last:
            hc = _mixer(hc, cuc, qc, None, kvc, sink, cw, gc, ga, w_out_b, cx[2],
                        tq=lc, seq=lc, ctx_len=lc, windowed=False)

        i = layer // 2
        if layer % 2 == 0:
            w1, w3, w2 = ffn_w1[i].astype(BF16), ffn_w3[i].astype(BF16), ffn_w2[i].astype(BF16)
            router = None
        else:
            w1, w3, w2 = moe_w1[i].astype(BF16), moe_w3[i].astype(BF16), moe_w2[i].astype(BF16)
            router = jnp.zeros((d, LANES), F32).at[:, 0:N_EXPERTS].set(moe_router[i])
        h = _ffn(h, g2n, lat[3], lat[4], lat[5], w1, w3, w2, router, tm=512, seq=s)
        if not last:
            hc = _ffn(hc, g2n, cx[3], cx[4], cx[5], w1, w3, w2, router, tm=lc, seq=lc)
    return h.reshape(b, s, d)
```

```python
import functools

import jax
import jax.numpy as jnp
from jax import lax
from jax.experimental import pallas as pl
from jax.experimental.pallas import tpu as pltpu

D_MODEL = 1024
GRID_W = 64
HEAD_DIM = 64
D_CONV = 512
D_ATTN = 512
N_HEADS = 8
N_KV = 2
WINDOW = 128
ROPE_THETA = 10000.0
N_EXPERTS = 8
EPS = 1e-6
KV_OFF = 3 * D_CONV + D_ATTN
D_IN = KV_OFF + 2 * N_KV * HEAD_DIM
LANES = 128
VMEM_LIMIT = 48 * 1024 * 1024

F32 = jnp.float32
BF16 = jnp.bfloat16
NEG_INF = float("-inf")


def _params(n_axes):
    return pltpu.CompilerParams(dimension_semantics=("arbitrary",) * n_axes,
                                vmem_limit_bytes=VMEM_LIMIT)


def _sigmoid(x):
    return 1.0 / (1.0 + jnp.exp(-x))


def _mod_kernel(c_ref, w_ref, b_ref, o_ref):
    c = c_ref[...]
    s = c * _sigmoid(c)
    o_ref[...] = jnp.dot(s, w_ref[...], preferred_element_type=F32) + b_ref[...]


def _modulation(c8, w_ada, b_ada):
    depth, d, n = w_ada.shape
    tn = 1536
    return pl.pallas_call(
        _mod_kernel,
        grid=(depth, n // tn),
        in_specs=[pl.BlockSpec((8, d), lambda l, j: (0, 0)),
                  pl.BlockSpec((None, d, tn), lambda l, j: (l, 0, j)),
                  pl.BlockSpec((None, 1, tn), lambda l, j: (l, 0, j))],
        out_specs=pl.BlockSpec((None, 8, tn), lambda l, j: (l, 0, j)),
        out_shape=jax.ShapeDtypeStruct((depth, 8, n), F32),
        compiler_params=_params(2),
        name="adaln_mod",
    )(c8, w_ada, b_ada.reshape(depth, 1, n))


def _norm_mod(x, g, sh, sc):
    ms = jnp.mean(x * x, axis=-1, keepdims=True)
    return (x * lax.rsqrt(ms + EPS) * g) * (1.0 + sc) + sh


def _inproj_kernel(*refs, rope):
    if rope:
        (h_ref, g_ref, sh_ref, sc_ref, w_ref, gq_ref, gk_ref, gm_ref,
         cos_ref, sa_ref, sb_ref, cu_ref, q_ref, kv_ref) = refs
    else:
        (h_ref, g_ref, sh_ref, sc_ref, w_ref, gq_ref, gk_ref, gm_ref,
         cu_ref, q_ref, kv_ref) = refs
    xn = _norm_mod(h_ref[...], g_ref[...], sh_ref[...], sc_ref[...])
    y = jnp.dot(xn.astype(BF16), w_ref[...], preferred_element_type=F32)
    cu_ref[:, 0:D_CONV] = y[:, 0:D_CONV].astype(BF16)
    cu_ref[:, D_CONV:2 * D_CONV] = (y[:, D_CONV:2 * D_CONV] * y[:, 2 * D_CONV:3 * D_CONV]).astype(BF16)

    gm = gm_ref[...]

    def head_norm(t, gain):
        w = t.shape[1]
        ss = jnp.dot((t * t).astype(BF16), gm[0:w, 0:w], preferred_element_type=F32)
        return t * lax.rsqrt(ss * (1.0 / HEAD_DIM) + EPS) * gain

    def rot(t):
        if not rope:
            return t
        return (t * cos_ref[...] + pltpu.roll(t, LANES - 16, 1) * sa_ref[...]
                + pltpu.roll(t, 16, 1) * sb_ref[...])

    gq = gq_ref[...]
    for j in range(D_ATTN // 256):
        lo = 3 * D_CONV + 256 * j
        qn = head_norm(y[:, lo:lo + 256], gq[:, 256 * j:256 * j + 256])
        for c in range(2):
            q_ref[:, 256 * j + LANES * c:256 * j + LANES * (c + 1)] = rot(
                qn[:, LANES * c:LANES * (c + 1)]).astype(BF16)
    k = rot(head_norm(y[:, KV_OFF:KV_OFF + LANES], gk_ref[...]))
    v = y[:, KV_OFF + LANES:KV_OFF + 2 * LANES]
    kv_ref[:, 0:LANES] = k.astype(BF16)
    kv_ref[:, LANES:2 * LANES] = pltpu.roll(k, HEAD_DIM, 1).astype(BF16)
    kv_ref[:, 2 * LANES:3 * LANES] = v.astype(BF16)
    kv_ref[:, 3 * LANES:4 * LANES] = pltpu.roll(v, HEAD_DIM, 1).astype(BF16)


def _inproj(h, g, sh, sc, w, gq, gk, gm, tables, *, tm, seq):
    t, d = h.shape
    tpb = seq // tm
    rope = tables is not None
    row = lambda i: (i, 0)
    fix = lambda i: (0, 0)
    mod = lambda i: (i // tpb, 0, 0)
    in_specs = [pl.BlockSpec((tm, d), row),
                pl.BlockSpec((1, d), fix),
                pl.BlockSpec((None, 1, d), mod),
                pl.BlockSpec((None, 1, d), mod),
                pl.BlockSpec((d, D_IN), fix),
                pl.BlockSpec((1, D_ATTN), fix),
                pl.BlockSpec((1, LANES), fix),
                pl.BlockSpec((256, 256), fix)]
    args = [h, g, sh, sc, w, gq, gk, gm]
    if rope:
        in_specs += [pl.BlockSpec((tm, LANES), lambda i: (i % tpb, 0))] * 3
        args += list(tables)
    return pl.pallas_call(
        functools.partial(_inproj_kernel, rope=rope),
        grid=(t // tm,),
        in_specs=in_specs,
        out_specs=[pl.BlockSpec((tm, 2 * D_CONV), row),
                   pl.BlockSpec((tm, D_ATTN), row),
                   pl.BlockSpec((tm, 4 * LANES), row)],
        out_shape=[jax.ShapeDtypeStruct((t, 2 * D_CONV), BF16),
                   jax.ShapeDtypeStruct((t, D_ATTN), BF16),
                   jax.ShapeDtypeStruct((t, 4 * LANES), BF16)],
        compiler_params=_params(1),
        name="inproj_rope" if rope else "inproj_ctx",
    )(*args)


_NT = (((1,), (1,)), ((), ()))


def _mixer_kernel(*refs, tq, windowed):
    if windowed:
        (sink_ref, h_ref, cu_ref, cup_ref, cun_ref, q_ref, kvp_ref, kv_ref, kvn_ref, kvc_ref,
         cw_ref, gc_ref, ga_ref, wo_ref, g1_ref, out_ref, kw_ref, ya_ref) = refs
    else:
        (sink_ref, h_ref, cu_ref, q_ref, kvc_ref,
         cw_ref, gc_ref, ga_ref, wo_ref, g1_ref, out_ref, ya_ref) = refs
    i = pl.program_id(1)
    nt = pl.num_programs(1)
    nsub = tq // WINDOW

    cu = cu_ref[...]
    bg = cu[:, 0:D_CONV].astype(F32)
    u = cu[:, D_CONV:2 * D_CONV].astype(F32)
    rows = lax.broadcasted_iota(jnp.int32, (tq, 1), 0)
    if windowed:
        up_row = cup_ref[:, D_CONV:2 * D_CONV].astype(F32)[15:16, :]
        un_row = cun_ref[:, D_CONV:2 * D_CONV].astype(F32)[0:1, :]
        up_row = jnp.where(i > 0, up_row, 0.0)
        un_row = jnp.where(i < nt - 1, un_row, 0.0)
    else:
        up_row = jnp.zeros((1, D_CONV), F32)
        un_row = jnp.zeros((1, D_CONV), F32)
    u_prev = jnp.where(rows == 0, up_row, pltpu.roll(u, 1, 0))
    u_next = jnp.where(rows == tq - 1, un_row, pltpu.roll(u, tq - 1, 0))
    cw = cw_ref[...]
    yc = bg * (cw[0:1, :] * u_prev + cw[1:2, :] * u + cw[2:3, :] * u_next)
    yc = yc * lax.rsqrt(jnp.mean(yc * yc, axis=-1, keepdims=True) + EPS) * gc_ref[...]

    if windowed:
        kw_ref[0:WINDOW, :] = kvp_ref[...]
        kw_ref[WINDOW:WINDOW + tq, :] = kv_ref[...]
        kw_ref[WINDOW + tq:2 * WINDOW + tq, :] = kvn_ref[...]
    lane_lo = lax.broadcasted_iota(jnp.int32, (WINDOW, LANES), 1) < HEAD_DIM
    kvc = kvc_ref[...]
    ctx_parts = [kvc[:, LANES * c:LANES * (c + 1)] for c in range(4)]

    def sub_block(s, carry):
        r0 = pl.multiple_of(s * WINDOW, WINDOW)
        if windowed:
            kwin = kw_ref[pl.ds(r0, 3 * WINDOW), :]
            win_parts = [kwin[:, LANES * c:LANES * (c + 1)] for c in range(4)]
            col = lax.broadcasted_iota(jnp.int32, (WINDOW, 3 * WINDOW), 1)
            rw = lax.broadcasted_iota(jnp.int32, (WINDOW, 3 * WINDOW), 0)
            first = jnp.logical_and(i == 0, s == 0)
            last = jnp.logical_and(i == nt - 1, s == nsub - 1)
            c_lo = jnp.where(first, WINDOW, 0)
            c_hi = jnp.where(last, 2 * WINDOW, 3 * WINDOW)
            valid = (col >= rw) & (col <= rw + 2 * WINDOW) & (col >= c_lo) & (col < c_hi)
        for j in range(N_HEADS // 2):
            grp = j // 2
            qp = q_ref[pl.ds(r0, WINDOW), LANES * j:LANES * (j + 1)]
            zero = jnp.zeros_like(qp)
            halves = []
            for odd in range(2):
                qh = jnp.where(lane_lo, zero, qp) if odd else jnp.where(lane_lo, qp, zero)
                pick = (grp + odd) % 2
                sk = sink_ref[2 * j + odd]
                s_c = lax.dot_general(qh, ctx_parts[pick], _NT, preferred_element_type=F32)
                m = jnp.maximum(jnp.max(s_c, axis=-1, keepdims=True), sk)
                if windowed:
                    s_w = lax.dot_general(qh, win_parts[pick], _NT, preferred_element_type=F32)
                    s_w = jnp.where(valid, s_w, NEG_INF)
                    m = jnp.maximum(m, jnp.max(s_w, axis=-1, keepdims=True))
                p_c = jnp.exp(s_c - m)
                den = jnp.exp(sk - m) + jnp.sum(p_c, axis=-1, keepdims=True)
                o = jnp.dot(p_c.astype(BF16), ctx_parts[2 + pick], preferred_element_type=F32)
                if windowed:
                    p_w = jnp.exp(s_w - m)
                    den = den + jnp.sum(p_w, axis=-1, keepdims=True)
                    o = o + jnp.dot(p_w.astype(BF16), win_parts[2 + pick], preferred_element_type=F32)
                halves.append(o / den)
            ya_ref[pl.ds(r0, WINDOW), LANES * j:LANES * (j + 1)] = jnp.where(lane_lo, halves[0], halves[1])
        return carry

    lax.fori_loop(0, nsub, sub_block, 0)

    ya = ya_ref[...]
    ya = ya * lax.rsqrt(jnp.mean(ya * ya, axis=-1, keepdims=True) + EPS) * ga_ref[...]
    y = (jnp.dot(yc.astype(BF16), wo_ref[0:D_CONV, :], preferred_element_type=F32)
         + jnp.dot(ya.astype(BF16), wo_ref[D_CONV:2 * D_CONV, :], preferred_element_type=F32))
    out_ref[...] = h_ref[...] + g1_ref[...] * y


def _mixer(h, cu, q, kv, kvc, sink, cw, gc, ga, wo, g1, *, tq, seq, ctx_len, windowed):
    t, d = h.shape
    nt = seq // tq
    nb = t // seq
    row = lambda b, i: (b * nt + i, 0)
    fix = lambda b, i: (0, 0)
    smem = pl.BlockSpec(memory_space=pltpu.SMEM)
    tail = [pl.BlockSpec((3, D_CONV), fix),
            pl.BlockSpec((1, D_CONV), fix),
            pl.BlockSpec((1, D_ATTN), fix),
            pl.BlockSpec((d, d), fix),
            pl.BlockSpec((None, 1, d), lambda b, i: (b, 0, 0))]
    ctx_spec = pl.BlockSpec((ctx_len, 4 * LANES), lambda b, i: (b, 0))
    if windowed:
        r16 = tq // 16
        n16 = t // 16
        rw = tq // WINDOW
        nw = t // WINDOW
        in_specs = [smem,
                    pl.BlockSpec((tq, d), row),
                    pl.BlockSpec((tq, 2 * D_CONV), row),
                    pl.BlockSpec((16, 2 * D_CONV), lambda b, i: (jnp.maximum((b * nt + i) * r16 - 1, 0), 0)),
                    pl.BlockSpec((16, 2 * D_CONV), lambda b, i: (jnp.minimum((b * nt + i + 1) * r16, n16 - 1), 0)),
                    pl.BlockSpec((tq, D_ATTN), row),
                    pl.BlockSpec((WINDOW, 4 * LANES), lambda b, i: (jnp.maximum((b * nt + i) * rw - 1, 0), 0)),
                    pl.BlockSpec((tq, 4 * LANES), row),
                    pl.BlockSpec((WINDOW, 4 * LANES), lambda b, i: (jnp.minimum((b * nt + i + 1) * rw, nw - 1), 0)),
                    ctx_spec] + tail
        args = [sink, h, cu, cu, cu, q, kv, kv, kv, kvc, cw, gc, ga, wo, g1]
        scratch = [pltpu.VMEM((tq + 2 * WINDOW, 4 * LANES), BF16), pltpu.VMEM((tq, D_ATTN), F32)]
    else:
        in_specs = [smem,
                    pl.BlockSpec((tq, d), row),
                    pl.BlockSpec((tq, 2 * D_CONV), row),
                    pl.BlockSpec((tq, D_ATTN), row),
                    ctx_spec] + tail
        args = [sink, h, cu, q, kvc, cw, gc, ga, wo, g1]
        scratch = [pltpu.VMEM((tq, D_ATTN), F32)]
    return pl.pallas_call(
        functools.partial(_mixer_kernel, tq=tq, windowed=windowed),
        grid=(nb, nt),
        in_specs=in_specs,
        out_specs=pl.BlockSpec((tq, d), row),
        out_shape=jax.ShapeDtypeStruct((t, d), F32),
        scratch_shapes=scratch,
        compiler_params=_params(2),
        name="mixer_win" if windowed else "mixer_ctx",
    )(*args)


def _ffn_kernel(*refs, moe):
    if moe:
        (h_ref, g_ref, sh_ref, sc_ref, gate_ref, r_ref, w1_ref, w3_ref, w2_ref,
         out_ref, xn_ref, acc_ref, comb_ref) = refs
    else:
        (h_ref, g_ref, sh_ref, sc_ref, gate_ref, w1_ref, w3_ref, w2_ref,
         out_ref, xn_ref, acc_ref) = refs
    e = pl.program_id(1)
    tm = h_ref.shape[0]

    @pl.when(e == 0)
    def _():
        xn = _norm_mod(h_ref[...], g_ref[...], sh_ref[...], sc_ref[...])
        xn_ref[...] = xn.astype(BF16)
        acc_ref[...] = jnp.zeros_like(acc_ref)
        if moe:
            i1, i2, g1, g2 = _top2(jnp.dot(xn, r_ref[...], preferred_element_type=F32))
            lane = lax.broadcasted_iota(jnp.int32, (tm, LANES), 1)
            comb_ref[...] = jnp.where(lane == i1, g1, 0.0) + jnp.where(lane == i2, g2, 0.0)

    xb = xn_ref[...]
    h1 = jnp.dot(xb, w1_ref[...], preferred_element_type=F32)
    h3 = jnp.dot(xb, w3_ref[...], preferred_element_type=F32)
    a = h1 * _sigmoid(h1) * h3
    if moe:
        lane = lax.broadcasted_iota(jnp.int32, (tm, LANES), 1)
        a = a * jnp.sum(jnp.where(lane == e, comb_ref[...], 0.0), axis=-1, keepdims=True)
    acc_ref[...] += jnp.dot(a.astype(BF16), w2_ref[...], preferred_element_type=F32)

    @pl.when(e == pl.num_programs(1) - 1)
    def _():
        out_ref[...] = h_ref[...] + gate_ref[...] * acc_ref[...]


def _ffn(h, g, sh, sc, gate, w1, w3, w2, router, *, tm, seq):
    t, d = h.shape
    tpb = seq // tm
    moe = router is not None
    row = lambda i, e: (i, 0)
    fix = lambda i, e: (0, 0)
    mod = lambda i, e: (i // tpb, 0, 0)
    in_specs = [pl.BlockSpec((tm, d), row),
                pl.BlockSpec((1, d), fix),
                pl.BlockSpec((None, 1, d), mod),
                pl.BlockSpec((None, 1, d), mod),
                pl.BlockSpec((None, 1, d), mod)]
    args = [h, g, sh, sc, gate]
    scratch = [pltpu.VMEM((tm, d), BF16), pltpu.VMEM((tm, d), F32)]
    if moe:
        ne, _, fe = w1.shape
        in_specs += [pl.BlockSpec((d, LANES), fix),
                     pl.BlockSpec((None, d, fe), lambda i, e: (e, 0, 0)),
                     pl.BlockSpec((None, d, fe), lambda i, e: (e, 0, 0)),
                     pl.BlockSpec((None, fe, d), lambda i, e: (e, 0, 0))]
        args += [router, w1, w3, w2]
        scratch += [pltpu.VMEM((tm, LANES), F32)]
    else:
        ne = 2
        fe = w1.shape[1] // ne
        in_specs += [pl.BlockSpec((d, fe), lambda i, e: (0, e)),
                     pl.BlockSpec((d, fe), lambda i, e: (0, e)),
                     pl.BlockSpec((fe, d), lambda i, e: (e, 0))]
        args += [w1, w3, w2]
    return pl.pallas_call(
        functools.partial(_ffn_kernel, moe=moe),
        grid=(t // tm, ne),
        in_specs=in_specs,
        out_specs=pl.BlockSpec((tm, d), row),
        out_shape=jax.ShapeDtypeStruct((t, d), F32),
        scratch_shapes=scratch,
        compiler_params=_params(2),
        name="ffn_moe" if moe else "ffn_dense",
    )(*args)


def _top2(logits):
    lane = lax.broadcasted_iota(jnp.int32, logits.shape, 1)
    lg = jnp.where(lane < N_EXPERTS, logits, NEG_INF)
    m1 = jnp.max(lg, axis=-1, keepdims=True)
    i1 = jnp.min(jnp.where(lg == m1, lane, LANES), axis=-1, keepdims=True)
    lg2 = jnp.where(lane == i1, NEG_INF, lg)
    m2 = jnp.max(lg2, axis=-1, keepdims=True)
    i2 = jnp.min(jnp.where(lg2 == m2, lane, LANES), axis=-1, keepdims=True)
    e2 = jnp.exp(m2 - m1)
    return i1, i2, 1.0 / (1.0 + e2), e2 / (1.0 + e2)


def _router_kernel(h_ref, g_ref, sh_ref, sc_ref, r_ref, xn_ref, route_ref):
    xn = _norm_mod(h_ref[...], g_ref[...], sh_ref[...], sc_ref[...])
    xn_ref[...] = xn
    i1, i2, g1, g2 = _top2(jnp.dot(xn, r_ref[...], preferred_element_type=F32))
    lane = lax.broadcasted_iota(jnp.int32, route_ref.shape, 1)
    route_ref[...] = jnp.where(lane == 0, i1.astype(F32),
                               jnp.where(lane == 1, i2.astype(F32),
                                         jnp.where(lane == 2, g1, jnp.where(lane == 3, g2, 0.0))))


def _router(h, g, sh, sc, router, *, tm, seq):
    t, d = h.shape
    tpb = seq // tm
    row = lambda i: (i, 0)
    fix = lambda i: (0, 0)
    mod = lambda i: (i // tpb, 0, 0)
    return pl.pallas_call(
        _router_kernel,
        grid=(t // tm,),
        in_specs=[pl.BlockSpec((tm, d), row), pl.BlockSpec((1, d), fix),
                  pl.BlockSpec((None, 1, d), mod), pl.BlockSpec((None, 1, d), mod),
                  pl.BlockSpec((d, LANES), fix)],
        out_specs=[pl.BlockSpec((tm, d), row), pl.BlockSpec((tm, LANES), row)],
        out_shape=[jax.ShapeDtypeStruct((t, d), F32), jax.ShapeDtypeStruct((t, LANES), F32)],
        compiler_params=_params(1),
        name="moe_router",
    )(h, g, sh, sc, router)


def _route_plan(route, tr):
    t = route.shape[0]
    ea = route[:, 0:2].astype(jnp.int32).reshape(-1)
    oh = (ea[None, :] == jnp.arange(N_EXPERTS, dtype=jnp.int32)[:, None]).astype(jnp.int32)
    cs = jnp.cumsum(oh, axis=1)
    rank = jnp.sum(oh * cs, axis=0) - 1
    counts = cs[:, -1]
    tiles = (counts + tr - 1) // tr
    tile_end = jnp.cumsum(tiles)
    gstart = (tile_end - tiles) * tr
    pos = jnp.sum(oh * gstart[:, None], axis=0) + rank
    nt = 2 * t // tr + N_EXPERTS
    slot_a = jnp.full((nt * tr,), -1, jnp.int32).at[pos].set(jnp.arange(2 * t, dtype=jnp.int32))
    is_pad = slot_a < 0
    pad_rank = jnp.cumsum(is_pad.astype(jnp.int32)) - 1
    src = jnp.where(is_pad, 0, slot_a // 2)
    dst = jnp.where(is_pad, 2 * t + pad_rank, slot_a)
    tid = jnp.arange(nt, dtype=jnp.int32)
    tile_expert = jnp.minimum(jnp.sum((tid[:, None] >= tile_end[None, :]).astype(jnp.int32), axis=1),
                              N_EXPERTS - 1)
    src = src.reshape(nt, tr)
    dst = dst.reshape(nt, tr)
    lane = jnp.arange(tr, dtype=jnp.int32)[None, :]
    spare = 2 * t + N_EXPERTS * tr
    zeros = jnp.zeros((1, tr), jnp.int32)
    src_rows = jnp.concatenate([src[1:], zeros, zeros, src[0:1]], axis=0)
    dst_rows = jnp.concatenate([spare + tr + lane, dst, spare + lane], axis=0)
    return tile_expert, jnp.concatenate([src_rows, dst_rows], axis=1)


def _expert_kernel(te_ref, idx_hbm, xn_hbm, w1_ref, w3_ref, w2_ref, y_hbm,
                   xb0, xb1, yb0, yb1, idx_smem, sem_x, sem_y, sem_i, *, tr, nt):
    j = pl.program_id(0)
    xbuf = (xb0, xb1)
    ybuf = (yb0, yb1)

    def idx_copy(row, s):
        return pltpu.make_async_copy(idx_hbm.at[row], idx_smem.at[s], sem_i.at[s])

    def gather_row(r, s_idx, s_buf):
        tok = idx_smem[s_idx, r]
        pltpu.make_async_copy(xn_hbm.at[pl.ds(tok, 1)], xbuf[s_buf].at[pl.ds(r, 1)],
                              sem_x.at[s_buf]).start(priority=0)

    def scatter_row(r, s_idx, s_buf):
        dst = idx_smem[s_idx, tr + r]
        pltpu.make_async_copy(ybuf[s_buf].at[pl.ds(r, 1)], y_hbm.at[pl.ds(dst, 1)],
                              sem_y.at[s_buf]).start(priority=1)

    def wait_gather(s):
        pltpu.make_async_copy(xn_hbm.at[pl.ds(0, tr)], xbuf[s], sem_x.at[s]).wait()

    def wait_scatter(s):
        pltpu.make_async_copy(ybuf[s], y_hbm.at[pl.ds(0, tr)], sem_y.at[s]).wait()

    @pl.when(j == 0)
    def _():
        first = idx_copy(nt + 1, 1)
        first.start()
        first.wait()
        yb0[...] = jnp.zeros_like(yb0)
        yb1[...] = jnp.zeros_like(yb1)

        def body(r, c):
            gather_row(r, 1, 0)
            scatter_row(r, 1, 0)
            return c

        lax.fori_loop(0, tr, body, 0, unroll=8)
        idx_copy(0, 0).start()

    def step(cur):
        nxt = 1 - cur
        idx_copy(j, cur).wait()
        wait_gather(cur)
        wait_scatter(cur)
        idx_copy(j + 1, nxt).start()
        for r in range(tr):
            gather_row(r, cur, nxt)
            scatter_row(r, cur, nxt)
        xb = xbuf[cur][...].astype(BF16)
        h1 = jnp.dot(xb, w1_ref[...], preferred_element_type=F32)
        h3 = jnp.dot(xb, w3_ref[...], preferred_element_type=F32)
        a = (h1 * _sigmoid(h1) * h3).astype(BF16)
        ybuf[cur][...] = jnp.dot(a, w2_ref[...], preferred_element_type=F32)

    for cur in range(2):
        pl.when(j % 2 == cur)(functools.partial(step, cur))

    @pl.when(j == nt - 1)
    def _():
        last = (nt - 1) % 2
        idx_copy(nt, 1 - last).wait()

        def body(r, c):
            scatter_row(r, 1 - last, last)
            return c

        lax.fori_loop(0, tr, body, 0, unroll=8)
        wait_scatter(1 - last)
        wait_scatter(last)
        wait_gather(1 - last)


def _experts(xn, tile_expert, idx, w1, w3, w2, *, tr):
    t, d = xn.shape
    nt = idx.shape[0] - 2
    fe = w1.shape[2]
    any_spec = pl.BlockSpec(memory_space=pl.ANY)
    grid_spec = pltpu.PrefetchScalarGridSpec(
        num_scalar_prefetch=1,
        grid=(nt,),
        in_specs=[any_spec, any_spec,
                  pl.BlockSpec((None, d, fe), lambda j, te: (te[j], 0, 0)),
                  pl.BlockSpec((None, d, fe), lambda j, te: (te[j], 0, 0)),
                  pl.BlockSpec((None, fe, d), lambda j, te: (te[j], 0, 0))],
        out_specs=any_spec,
        scratch_shapes=[pltpu.VMEM((tr, d), F32), pltpu.VMEM((tr, d), F32),
                        pltpu.VMEM((tr, d), F32), pltpu.VMEM((tr, d), F32),
                        pltpu.SMEM((2, 2 * tr), jnp.int32),
                        pltpu.SemaphoreType.DMA((2,)), pltpu.SemaphoreType.DMA((2,)),
                        pltpu.SemaphoreType.DMA((2,))])
    return pl.pallas_call(
        functools.partial(_expert_kernel, tr=tr, nt=nt),
        grid_spec=grid_spec,
        out_shape=jax.ShapeDtypeStruct((2 * t + (N_EXPERTS + 2) * tr, d), F32),
        compiler_params=pltpu.CompilerParams(dimension_semantics=("arbitrary",),
                                             vmem_limit_bytes=VMEM_LIMIT,
                                             disable_bounds_checks=True),
        name="moe_experts",
    )(tile_expert, idx, xn, w1, w3, w2)


def _combine_kernel(h_ref, gate_ref, route_ref, y_ref, out_ref):
    rt = route_ref[...]
    y = y_ref[...]
    mix = rt[:, 2:3] * y[:, 0:D_MODEL] + rt[:, 3:4] * y[:, D_MODEL:2 * D_MODEL]
    out_ref[...] = h_ref[...] + gate_ref[...] * mix


def _combine(h, gate, route, y2, *, tm, seq):
    t, d = h.shape
    tpb = seq // tm
    row = lambda i: (i, 0)
    return pl.pallas_call(
        _combine_kernel,
        grid=(t // tm,),
        in_specs=[pl.BlockSpec((tm, d), row),
                  pl.BlockSpec((None, 1, d), lambda i: (i // tpb, 0, 0)),
                  pl.BlockSpec((tm, LANES), row),
                  pl.BlockSpec((tm, 2 * d), row)],
        out_specs=pl.BlockSpec((tm, d), row),
        out_shape=jax.ShapeDtypeStruct((t, d), F32),
        compiler_params=_params(1),
        name="moe_combine",
    )(h, gate, route, y2.reshape(-1, 2 * d))


def _moe(h, g, sh, sc, gate, w1, w3, w2, router, *, seq, tr):
    xn, route = _router(h, g, sh, sc, router, tm=512, seq=seq)
    tile_expert, idx = _route_plan(route, tr)
    y2 = _experts(xn, tile_expert, idx, w1, w3, w2, tr=tr)
    return _combine(h, gate, route, y2, tm=512, seq=seq)


def _rope_tables(seq):
    rows = seq // GRID_W
    row, col = jnp.meshgrid(jnp.arange(rows, dtype=F32), jnp.arange(GRID_W, dtype=F32), indexing='ij')
    n_freq = HEAD_DIM // 4
    inv_freq = ROPE_THETA ** (-jnp.arange(n_freq, dtype=F32) / n_freq)
    ang_r = row.reshape(-1, 1) * inv_freq
    ang_c = col.reshape(-1, 1) * inv_freq
    ang = jnp.concatenate([ang_r, ang_r, ang_c, ang_c], axis=-1)
    cos, sin = jnp.cos(ang), jnp.sin(ang)
    first = (jnp.arange(HEAD_DIM) % (2 * n_freq)) < n_freq
    sin_a = jnp.where(first, -sin, 0.0)
    sin_b = jnp.where(first, 0.0, sin)
    rep = LANES // HEAD_DIM
    return tuple(jnp.tile(t, (1, rep)) for t in (cos, sin_a, sin_b))


def kernel(x, c, ctx, c_ctx, w_ada, b_ada, norm1_g, norm2_g, w_in, conv_w, q_norm_g, k_norm_g,
           attn_sink, out_norm_conv_g, out_norm_attn_g, w_out, ffn_w1, ffn_w3, ffn_w2,
           moe_router, moe_w1, moe_w3, moe_w2):
    b, s, d = x.shape
    lc = ctx.shape[1]
    depth = w_ada.shape[0]
    assert d == D_MODEL and s % 512 == 0 and lc % 256 == 0 and b + 1 <= 8

    c8 = jnp.zeros((8, d), F32).at[0:b].set(c).at[b].set(c_ctx)
    mod = _modulation(c8, w_ada, b_ada)

    tables = _rope_tables(s)
    ids = jnp.arange(256)
    gm = (ids[:, None] // HEAD_DIM == ids[None, :] // HEAD_DIM).astype(BF16)
    scale = HEAD_DIM ** -0.5

    h = x.reshape(b * s, d)
    hc = ctx.reshape(b * lc, d)
    for layer in range(depth):
        last = layer == depth - 1
        m = mod[layer]
        lat = [m[0:b, k * d:(k + 1) * d].reshape(b, 1, d) for k in range(6)]
        cx = [jnp.broadcast_to(m[b:b + 1, k * d:(k + 1) * d].reshape(1, 1, d), (b, 1, d)) for k in range(6)]
        w_in_b = w_in[layer].astype(BF16)
        w_out_b = w_out[layer].astype(BF16)
        g1n = norm1_g[layer].reshape(1, d)
        g2n = norm2_g[layer].reshape(1, d)
        gq = (jnp.tile(q_norm_g[layer], N_HEADS) * scale).reshape(1, D_ATTN)
        gk = jnp.tile(k_norm_g[layer], N_KV).reshape(1, LANES)
        gc = out_norm_conv_g[layer].reshape(1, D_CONV)
        ga = out_norm_attn_g[layer].reshape(1, D_ATTN)
        sink = attn_sink[layer]
        cw = conv_w[layer]

        cu, q, kv = _inproj(h, g1n, lat[0], lat[1], w_in_b, gq, gk, gm, tables, tm=512, seq=s)
        cuc, qc, kvc = _inproj(hc, g1n, cx[0], cx[1], w_in_b, gq, gk, gm, None, tm=lc, seq=lc)
        h = _mixer(h, cu, q, kv, kvc, sink, cw, gc, ga, w_out_b, lat[2],
                   tq=512, seq=s, ctx_len=lc, windowed=True)
        if not last:
            hc = _mixer(hc, cuc, qc, None, kvc, sink, cw, gc, ga, w_out_b, cx[2],
                        tq=lc, seq=lc, ctx_len=lc, windowed=False)

        i = layer // 2
        if layer % 2 == 0:
            w1, w3, w2 = ffn_w1[i].astype(BF16), ffn_w3[i].astype(BF16), ffn_w2[i].astype(BF16)
            router = None
        else:
            w1, w3, w2 = moe_w1[i].astype(BF16), moe_w3[i].astype(BF16), moe_w2[i].astype(BF16)
            router = jnp.zeros((d, LANES), F32).at[:, 0:N_EXPERTS].set(moe_router[i])
        if router is None:
            h = _ffn(h, g2n, lat[3], lat[4], lat[5], w1, w3, w2, None, tm=512, seq=s)
        else:
            h = _moe(h, g2n, lat[3], lat[4], lat[5], w1, w3, w2, router, seq=s, tr=512)
        if not last:
            hc = _ffn(hc, g2n, cx[3], cx[4], cx[5], w1, w3, w2, router, tm=lc, seq=lc)
    return h.reshape(b, s, d)
```

```python
import functools

import jax
import jax.numpy as jnp
from jax import lax
from jax.experimental import pallas as pl
from jax.experimental.pallas import tpu as pltpu

D_MODEL = 1024
GRID_W = 64
HEAD_DIM = 64
D_CONV = 512
D_ATTN = 512
N_HEADS = 8
N_KV = 2
WINDOW = 128
ROPE_THETA = 10000.0
N_EXPERTS = 8
EPS = 1e-6
KV_OFF = 3 * D_CONV + D_ATTN
D_IN = KV_OFF + 2 * N_KV * HEAD_DIM
LANES = 128
VMEM_LIMIT = 48 * 1024 * 1024

F32 = jnp.float32
BF16 = jnp.bfloat16
NEG_INF = float("-inf")


def _params(n_axes):
    return pltpu.CompilerParams(dimension_semantics=("arbitrary",) * n_axes,
                                vmem_limit_bytes=VMEM_LIMIT)


def _sigmoid(x):
    return 1.0 / (1.0 + jnp.exp(-x))


def _mod_kernel(c_ref, w_ref, b_ref, o_ref):
    c = c_ref[...]
    s = c * _sigmoid(c)
    o_ref[...] = jnp.dot(s, w_ref[...], preferred_element_type=F32) + b_ref[...]


def _modulation(c8, w_ada, b_ada):
    depth, d, n = w_ada.shape
    tn = 1536
    return pl.pallas_call(
        _mod_kernel,
        grid=(depth, n // tn),
        in_specs=[pl.BlockSpec((8, d), lambda l, j: (0, 0)),
                  pl.BlockSpec((None, d, tn), lambda l, j: (l, 0, j)),
                  pl.BlockSpec((None, 1, tn), lambda l, j: (l, 0, j))],
        out_specs=pl.BlockSpec((None, 8, tn), lambda l, j: (l, 0, j)),
        out_shape=jax.ShapeDtypeStruct((depth, 8, n), F32),
        compiler_params=_params(2),
        name="adaln_mod",
    )(c8, w_ada, b_ada.reshape(depth, 1, n))


def _norm_mod(x, g, sh, sc):
    ms = jnp.mean(x * x, axis=-1, keepdims=True)
    return (x * lax.rsqrt(ms + EPS) * g) * (1.0 + sc) + sh


def _inproj_kernel(*refs, rope):
    if rope:
        (h_ref, g_ref, sh_ref, sc_ref, w_ref, gq_ref, gk_ref, gm_ref,
         cos_ref, sa_ref, sb_ref, cu_ref, q_ref, kv_ref) = refs
    else:
        (h_ref, g_ref, sh_ref, sc_ref, w_ref, gq_ref, gk_ref, gm_ref,
         cu_ref, q_ref, kv_ref) = refs
    xn = _norm_mod(h_ref[...], g_ref[...], sh_ref[...], sc_ref[...])
    y = jnp.dot(xn.astype(BF16), w_ref[...], preferred_element_type=F32)
    cu_ref[:, 0:D_CONV] = y[:, 0:D_CONV].astype(BF16)
    cu_ref[:, D_CONV:2 * D_CONV] = (y[:, D_CONV:2 * D_CONV] * y[:, 2 * D_CONV:3 * D_CONV]).astype(BF16)

    gm = gm_ref[...]

    def head_norm(t, gain):
        w = t.shape[1]
        ss = jnp.dot((t * t).astype(BF16), gm[0:w, 0:w], preferred_element_type=F32)
        return t * lax.rsqrt(ss * (1.0 / HEAD_DIM) + EPS) * gain

    def rot(t):
        if not rope:
            return t
        return (t * cos_ref[...] + pltpu.roll(t, LANES - 16, 1) * sa_ref[...]
                + pltpu.roll(t, 16, 1) * sb_ref[...])

    gq = gq_ref[...]
    for j in range(D_ATTN // 256):
        lo = 3 * D_CONV + 256 * j
        qn = head_norm(y[:, lo:lo + 256], gq[:, 256 * j:256 * j + 256])
        for c in range(2):
            q_ref[:, 256 * j + LANES * c:256 * j + LANES * (c + 1)] = rot(
                qn[:, LANES * c:LANES * (c + 1)]).astype(BF16)
    k = rot(head_norm(y[:, KV_OFF:KV_OFF + LANES], gk_ref[...]))
    v = y[:, KV_OFF + LANES:KV_OFF + 2 * LANES]
    kv_ref[:, 0:LANES] = k.astype(BF16)
    kv_ref[:, LANES:2 * LANES] = pltpu.roll(k, HEAD_DIM, 1).astype(BF16)
    kv_ref[:, 2 * LANES:3 * LANES] = v.astype(BF16)
    kv_ref[:, 3 * LANES:4 * LANES] = pltpu.roll(v, HEAD_DIM, 1).astype(BF16)


def _inproj(h, g, sh, sc, w, gq, gk, gm, tables, *, tm, seq):
    t, d = h.shape
    tpb = seq // tm
    rope = tables is not None
    row = lambda i: (i, 0)
    fix = lambda i: (0, 0)
    mod = lambda i: (i // tpb, 0, 0)
    in_specs = [pl.BlockSpec((tm, d), row),
                pl.BlockSpec((1, d), fix),
                pl.BlockSpec((None, 1, d), mod),
                pl.BlockSpec((None, 1, d), mod),
                pl.BlockSpec((d, D_IN), fix),
                pl.BlockSpec((1, D_ATTN), fix),
                pl.BlockSpec((1, LANES), fix),
                pl.BlockSpec((256, 256), fix)]
    args = [h, g, sh, sc, w, gq, gk, gm]
    if rope:
        in_specs += [pl.BlockSpec((tm, LANES), lambda i: (i % tpb, 0))] * 3
        args += list(tables)
    return pl.pallas_call(
        functools.partial(_inproj_kernel, rope=rope),
        grid=(t // tm,),
        in_specs=in_specs,
        out_specs=[pl.BlockSpec((tm, 2 * D_CONV), row),
                   pl.BlockSpec((tm, D_ATTN), row),
                   pl.BlockSpec((tm, 4 * LANES), row)],
        out_shape=[jax.ShapeDtypeStruct((t, 2 * D_CONV), BF16),
                   jax.ShapeDtypeStruct((t, D_ATTN), BF16),
                   jax.ShapeDtypeStruct((t, 4 * LANES), BF16)],
        compiler_params=_params(1),
        name="inproj_rope" if rope else "inproj_ctx",
    )(*args)


_NT = (((1,), (1,)), ((), ()))


def _mixer_kernel(*refs, tq, windowed):
    if windowed:
        (sink_ref, h_ref, cu_ref, cup_ref, cun_ref, q_ref, kvp_ref, kv_ref, kvn_ref, kvc_ref,
         cw_ref, gc_ref, ga_ref, wo_ref, g1_ref, out_ref, kw_ref, ya_ref) = refs
    else:
        (sink_ref, h_ref, cu_ref, q_ref, kvc_ref,
         cw_ref, gc_ref, ga_ref, wo_ref, g1_ref, out_ref, ya_ref) = refs
    i = pl.program_id(1)
    nt = pl.num_programs(1)
    nsub = tq // WINDOW

    cu = cu_ref[...]
    bg = cu[:, 0:D_CONV].astype(F32)
    u = cu[:, D_CONV:2 * D_CONV].astype(F32)
    rows = lax.broadcasted_iota(jnp.int32, (tq, 1), 0)
    if windowed:
        up_row = cup_ref[:, D_CONV:2 * D_CONV].astype(F32)[15:16, :]
        un_row = cun_ref[:, D_CONV:2 * D_CONV].astype(F32)[0:1, :]
        up_row = jnp.where(i > 0, up_row, 0.0)
        un_row = jnp.where(i < nt - 1, un_row, 0.0)
    else:
        up_row = jnp.zeros((1, D_CONV), F32)
        un_row = jnp.zeros((1, D_CONV), F32)
    u_prev = jnp.where(rows == 0, up_row, pltpu.roll(u, 1, 0))
    u_next = jnp.where(rows == tq - 1, un_row, pltpu.roll(u, tq - 1, 0))
    cw = cw_ref[...]
    yc = bg * (cw[0:1, :] * u_prev + cw[1:2, :] * u + cw[2:3, :] * u_next)
    yc = yc * lax.rsqrt(jnp.mean(yc * yc, axis=-1, keepdims=True) + EPS) * gc_ref[...]

    if windowed:
        kw_ref[0:WINDOW, :] = kvp_ref[...]
        kw_ref[WINDOW:WINDOW + tq, :] = kv_ref[...]
        kw_ref[WINDOW + tq:2 * WINDOW + tq, :] = kvn_ref[...]
    lane_lo = lax.broadcasted_iota(jnp.int32, (WINDOW, LANES), 1) < HEAD_DIM
    kvc = kvc_ref[...]
    ctx_parts = [kvc[:, LANES * c:LANES * (c + 1)] for c in range(4)]

    def sub_block(s, carry):
        r0 = pl.multiple_of(s * WINDOW, WINDOW)
        if windowed:
            kwin = kw_ref[pl.ds(r0, 3 * WINDOW), :]
            win_parts = [kwin[:, LANES * c:LANES * (c + 1)] for c in range(4)]
            col = lax.broadcasted_iota(jnp.int32, (WINDOW, 3 * WINDOW), 1)
            rw = lax.broadcasted_iota(jnp.int32, (WINDOW, 3 * WINDOW), 0)
            first = jnp.logical_and(i == 0, s == 0)
            last = jnp.logical_and(i == nt - 1, s == nsub - 1)
            c_lo = jnp.where(first, WINDOW, 0)
            c_hi = jnp.where(last, 2 * WINDOW, 3 * WINDOW)
            valid = (col >= rw) & (col <= rw + 2 * WINDOW) & (col >= c_lo) & (col < c_hi)
        for j in range(N_HEADS // 2):
            grp = j // 2
            qp = q_ref[pl.ds(r0, WINDOW), LANES * j:LANES * (j + 1)]
            zero = jnp.zeros_like(qp)
            halves = []
            for odd in range(2):
                qh = jnp.where(lane_lo, zero, qp) if odd else jnp.where(lane_lo, qp, zero)
                pick = (grp + odd) % 2
                sk = sink_ref[2 * j + odd]
                s_c = lax.dot_general(qh, ctx_parts[pick], _NT, preferred_element_type=F32)
                m = jnp.maximum(jnp.max(s_c, axis=-1, keepdims=True), sk)
                if windowed:
                    s_w = lax.dot_general(qh, win_parts[pick], _NT, preferred_element_type=F32)
                    s_w = jnp.where(valid, s_w, NEG_INF)
                    m = jnp.maximum(m, jnp.max(s_w, axis=-1, keepdims=True))
                p_c = jnp.exp(s_c - m)
                den = jnp.exp(sk - m) + jnp.sum(p_c, axis=-1, keepdims=True)
                o = jnp.dot(p_c.astype(BF16), ctx_parts[2 + pick], preferred_element_type=F32)
                if windowed:
                    p_w = jnp.exp(s_w - m)
                    den = den + jnp.sum(p_w, axis=-1, keepdims=True)
                    o = o + jnp.dot(p_w.astype(BF16), win_parts[2 + pick], preferred_element_type=F32)
                halves.append(o / den)
            ya_ref[pl.ds(r0, WINDOW), LANES * j:LANES * (j + 1)] = jnp.where(lane_lo, halves[0], halves[1])
        return carry

    lax.fori_loop(0, nsub, sub_block, 0)

    ya = ya_ref[...]
    ya = ya * lax.rsqrt(jnp.mean(ya * ya, axis=-1, keepdims=True) + EPS) * ga_ref[...]
    y = (jnp.dot(yc.astype(BF16), wo_ref[0:D_CONV, :], preferred_element_type=F32)
         + jnp.dot(ya.astype(BF16), wo_ref[D_CONV:2 * D_CONV, :], preferred_element_type=F32))
    out_ref[...] = h_ref[...] + g1_ref[...] * y


def _mixer(h, cu, q, kv, kvc, sink, cw, gc, ga, wo, g1, *, tq, seq, ctx_len, windowed):
    t, d = h.shape
    nt = seq // tq
    nb = t // seq
    row = lambda b, i: (b * nt + i, 0)
    fix = lambda b, i: (0, 0)
    smem = pl.BlockSpec(memory_space=pltpu.SMEM)
    tail = [pl.BlockSpec((3, D_CONV), fix),
            pl.BlockSpec((1, D_CONV), fix),
            pl.BlockSpec((1, D_ATTN), fix),
            pl.BlockSpec((d, d), fix),
            pl.BlockSpec((None, 1, d), lambda b, i: (b, 0, 0))]
    ctx_spec = pl.BlockSpec((ctx_len, 4 * LANES), lambda b, i: (b, 0))
    if windowed:
        r16 = tq // 16
        n16 = t // 16
        rw = tq // WINDOW
        nw = t // WINDOW
        in_specs = [smem,
                    pl.BlockSpec((tq, d), row),
                    pl.BlockSpec((tq, 2 * D_CONV), row),
                    pl.BlockSpec((16, 2 * D_CONV), lambda b, i: (jnp.maximum((b * nt + i) * r16 - 1, 0), 0)),
                    pl.BlockSpec((16, 2 * D_CONV), lambda b, i: (jnp.minimum((b * nt + i + 1) * r16, n16 - 1), 0)),
                    pl.BlockSpec((tq, D_ATTN), row),
                    pl.BlockSpec((WINDOW, 4 * LANES), lambda b, i: (jnp.maximum((b * nt + i) * rw - 1, 0), 0)),
                    pl.BlockSpec((tq, 4 * LANES), row),
                    pl.BlockSpec((WINDOW, 4 * LANES), lambda b, i: (jnp.minimum((b * nt + i + 1) * rw, nw - 1), 0)),
                    ctx_spec] + tail
        args = [sink, h, cu, cu, cu, q, kv, kv, kv, kvc, cw, gc, ga, wo, g1]
        scratch = [pltpu.VMEM((tq + 2 * WINDOW, 4 * LANES), BF16), pltpu.VMEM((tq, D_ATTN), F32)]
    else:
        in_specs = [smem,
                    pl.BlockSpec((tq, d), row),
                    pl.BlockSpec((tq, 2 * D_CONV), row),
                    pl.BlockSpec((tq, D_ATTN), row),
                    ctx_spec] + tail
        args = [sink, h, cu, q, kvc, cw, gc, ga, wo, g1]
        scratch = [pltpu.VMEM((tq, D_ATTN), F32)]
    return pl.pallas_call(
        functools.partial(_mixer_kernel, tq=tq, windowed=windowed),
        grid=(nb, nt),
        in_specs=in_specs,
        out_specs=pl.BlockSpec((tq, d), row),
        out_shape=jax.ShapeDtypeStruct((t, d), F32),
        scratch_shapes=scratch,
        compiler_params=_params(2),
        name="mixer_win" if windowed else "mixer_ctx",
    )(*args)


def _ffn_kernel(*refs, moe):
    if moe:
        (h_ref, g_ref, sh_ref, sc_ref, gate_ref, r_ref, w1_ref, w3_ref, w2_ref,
         out_ref, xn_ref, acc_ref, comb_ref) = refs
    else:
        (h_ref, g_ref, sh_ref, sc_ref, gate_ref, w1_ref, w3_ref, w2_ref,
         out_ref, xn_ref, acc_ref) = refs
    e = pl.program_id(1)
    tm = h_ref.shape[0]

    @pl.when(e == 0)
    def _():
        xn = _norm_mod(h_ref[...], g_ref[...], sh_ref[...], sc_ref[...])
        xn_ref[...] = xn.astype(BF16)
        acc_ref[...] = jnp.zeros_like(acc_ref)
        if moe:
            i1, i2, g1, g2 = _top2(jnp.dot(xn, r_ref[...], preferred_element_type=F32))
            lane = lax.broadcasted_iota(jnp.int32, (tm, LANES), 1)
            comb_ref[...] = jnp.where(lane == i1, g1, 0.0) + jnp.where(lane == i2, g2, 0.0)

    xb = xn_ref[...]
    h1 = jnp.dot(xb, w1_ref[...], preferred_element_type=F32)
    h3 = jnp.dot(xb, w3_ref[...], preferred_element_type=F32)
    a = h1 * _sigmoid(h1) * h3
    if moe:
        lane = lax.broadcasted_iota(jnp.int32, (tm, LANES), 1)
        a = a * jnp.sum(jnp.where(lane == e, comb_ref[...], 0.0), axis=-1, keepdims=True)
    acc_ref[...] += jnp.dot(a.astype(BF16), w2_ref[...], preferred_element_type=F32)

    @pl.when(e == pl.num_programs(1) - 1)
    def _():
        out_ref[...] = h_ref[...] + gate_ref[...] * acc_ref[...]


def _ffn(h, g, sh, sc, gate, w1, w3, w2, router, *, tm, seq):
    t, d = h.shape
    tpb = seq // tm
    moe = router is not None
    row = lambda i, e: (i, 0)
    fix = lambda i, e: (0, 0)
    mod = lambda i, e: (i // tpb, 0, 0)
    in_specs = [pl.BlockSpec((tm, d), row),
                pl.BlockSpec((1, d), fix),
                pl.BlockSpec((None, 1, d), mod),
                pl.BlockSpec((None, 1, d), mod),
                pl.BlockSpec((None, 1, d), mod)]
    args = [h, g, sh, sc, gate]
    scratch = [pltpu.VMEM((tm, d), BF16), pltpu.VMEM((tm, d), F32)]
    if moe:
        ne, _, fe = w1.shape
        in_specs += [pl.BlockSpec((d, LANES), fix),
                     pl.BlockSpec((None, d, fe), lambda i, e: (e, 0, 0)),
                     pl.BlockSpec((None, d, fe), lambda i, e: (e, 0, 0)),
                     pl.BlockSpec((None, fe, d), lambda i, e: (e, 0, 0))]
        args += [router, w1, w3, w2]
        scratch += [pltpu.VMEM((tm, LANES), F32)]
    else:
        ne = 2
        fe = w1.shape[1] // ne
        in_specs += [pl.BlockSpec((d, fe), lambda i, e: (0, e)),
                     pl.BlockSpec((d, fe), lambda i, e: (0, e)),
                     pl.BlockSpec((fe, d), lambda i, e: (e, 0))]
        args += [w1, w3, w2]
    return pl.pallas_call(
        functools.partial(_ffn_kernel, moe=moe),
        grid=(t // tm, ne),
        in_specs=in_specs,
        out_specs=pl.BlockSpec((tm, d), row),
        out_shape=jax.ShapeDtypeStruct((t, d), F32),
        scratch_shapes=scratch,
        compiler_params=_params(2),
        name="ffn_moe" if moe else "ffn_dense",
    )(*args)


def _top2(logits):
    lane = lax.broadcasted_iota(jnp.int32, logits.shape, 1)
    lg = jnp.where(lane < N_EXPERTS, logits, NEG_INF)
    m1 = jnp.max(lg, axis=-1, keepdims=True)
    i1 = jnp.min(jnp.where(lg == m1, lane, LANES), axis=-1, keepdims=True)
    lg2 = jnp.where(lane == i1, NEG_INF, lg)
    m2 = jnp.max(lg2, axis=-1, keepdims=True)
    i2 = jnp.min(jnp.where(lg2 == m2, lane, LANES), axis=-1, keepdims=True)
    e2 = jnp.exp(m2 - m1)
    return i1, i2, 1.0 / (1.0 + e2), e2 / (1.0 + e2)


def _router_kernel(h_ref, g_ref, sh_ref, sc_ref, r_ref, tri_ref, xn_ref, route_ref, cnt_ref, base_ref):
    tm = h_ref.shape[0]

    @pl.when(pl.program_id(0) == 0)
    def _():
        base_ref[...] = jnp.zeros_like(base_ref)

    xn = _norm_mod(h_ref[...], g_ref[...], sh_ref[...], sc_ref[...])
    xn_ref[...] = xn.reshape(xn_ref.shape)
    i1, i2, g1, g2 = _top2(jnp.dot(xn, r_ref[...], preferred_element_type=F32))
    lane = lax.broadcasted_iota(jnp.int32, (tm, LANES), 1)
    hit1 = lane == i1
    hit2 = lane == i2
    chosen = jnp.where(jnp.logical_or(hit1, hit2), 1.0, 0.0)
    before = base_ref[...] + jnp.dot(tri_ref[...], chosen.astype(BF16), preferred_element_type=F32)
    r1 = jnp.sum(jnp.where(hit1, before, 0.0), axis=-1, keepdims=True)
    r2 = jnp.sum(jnp.where(hit2, before, 0.0), axis=-1, keepdims=True)
    base_ref[...] += jnp.sum(chosen, axis=0, keepdims=True)
    cnt_ref[...] = base_ref[...]
    fields = (i1.astype(F32), i2.astype(F32), g1, g2, r1, r2)
    route = jnp.zeros((tm, LANES), F32)
    for k, f in enumerate(fields):
        route = jnp.where(lane == k, f, route)
    route_ref[...] = route


def _router(h, g, sh, sc, router, *, tm, seq):
    t, d = h.shape
    tpb = seq // tm
    row = lambda i: (i, 0)
    fix = lambda i: (0, 0)
    mod = lambda i: (i // tpb, 0, 0)
    ids = jnp.arange(tm)
    tri = (ids[None, :] < ids[:, None]).astype(BF16)
    return pl.pallas_call(
        _router_kernel,
        grid=(t // tm,),
        in_specs=[pl.BlockSpec((tm, d), row), pl.BlockSpec((1, d), fix),
                  pl.BlockSpec((None, 1, d), mod), pl.BlockSpec((None, 1, d), mod),
                  pl.BlockSpec((d, LANES), fix), pl.BlockSpec((tm, tm), fix)],
        out_specs=[pl.BlockSpec((tm, d // LANES, LANES), lambda i: (i, 0, 0)),
                   pl.BlockSpec((tm, LANES), row),
                   pl.BlockSpec((1, LANES), fix)],
        out_shape=[jax.ShapeDtypeStruct((t, d // LANES, LANES), F32),
                   jax.ShapeDtypeStruct((t, LANES), F32),
                   jax.ShapeDtypeStruct((1, LANES), F32)],
        scratch_shapes=[pltpu.VMEM((1, LANES), F32)],
        compiler_params=_params(1),
        name="moe_router",
    )(h, g, sh, sc, router, tri)


def _route_plan(route, counts, tr, tm):
    t = route.shape[0]
    counts = counts[0, 0:N_EXPERTS].astype(jnp.int32)
    tiles = (counts + tr - 1) // tr
    tile_end = jnp.cumsum(tiles)
    tile_start = tile_end - tiles
    experts = jnp.arange(N_EXPERTS, dtype=jnp.int32)

    def position(e, r):
        start = jnp.sum(jnp.where(e[:, None] == experts[None, :], tile_start[None, :] * tr, 0), axis=1)
        return start + r

    pos1 = position(route[:, 0].astype(jnp.int32), route[:, 4].astype(jnp.int32))
    pos2 = position(route[:, 1].astype(jnp.int32), route[:, 5].astype(jnp.int32))
    nt = 2 * t // tr + N_EXPERTS
    tid = jnp.arange(nt, dtype=jnp.int32)
    tile_expert = jnp.minimum(jnp.sum((tid[:, None] >= tile_end[None, :]).astype(jnp.int32), axis=1),
                              N_EXPERTS - 1)
    in_tile = tid - jnp.sum(jnp.where(tile_expert[:, None] == experts[None, :], tile_start[None, :], 0), axis=1)
    own = jnp.sum(jnp.where(tile_expert[:, None] == experts[None, :], counts[None, :], 0), axis=1)
    n_valid = jnp.where(tid < tile_end[-1], jnp.clip(own - in_tile * tr, 0, tr), 0)
    table = jnp.concatenate([pos1.reshape(t // tm, tm), pos2.reshape(t // tm, tm)], axis=1)
    table = jnp.concatenate([table, jnp.zeros((2, 2 * tm), jnp.int32)], axis=0)
    tail = tile_end[-1] + experts
    pad_tiles = jnp.concatenate([jnp.where(tiles > 0, tile_end - 1, -1), jnp.where(tail < nt, tail, -1)])
    return tile_expert, n_valid, table, pad_tiles


def _row_copies(idx_smem, s_idx, tm, make):
    base = s_idx * (2 * tm)

    def body(r, c):
        make(r, idx_smem[base + r], idx_smem[base + tm + r])
        return c

    lax.fori_loop(0, tm, body, 0, unroll=8)


def _dispatch_kernel(zt_ref, idx_hbm, xn_hbm, xg_hbm, idx_smem, zbuf, sem_d, sem_i, sem_z, *, tm, nt, tr):
    j = pl.program_id(0)
    slot = j % 2
    other = 1 - slot

    @pl.when(j == 0)
    def _():
        zbuf[...] = jnp.zeros_like(zbuf)
        for k in range(zt_ref.shape[0]):
            fill = pltpu.make_async_copy(zbuf, xg_hbm.at[pl.ds(jnp.maximum(zt_ref[k], 0) * tr, tr)], sem_z)
            pl.when(zt_ref[k] >= 0)(fill.start)
        for k in range(zt_ref.shape[0]):
            fill = pltpu.make_async_copy(zbuf, xg_hbm.at[pl.ds(0, tr)], sem_z)
            pl.when(zt_ref[k] >= 0)(fill.wait)

    def idx_copy(row, s):
        return pltpu.make_async_copy(idx_hbm.at[row], idx_smem.at[pl.ds(s * 2 * tm, 2 * tm)], sem_i.at[s])

    def wait_rows(s):
        for _ in range(2):
            pltpu.make_async_copy(xn_hbm.at[pl.ds(0, tm)], xg_hbm.at[pl.ds(0, tm)], sem_d.at[s]).wait()

    @pl.when(j == 0)
    def _():
        idx_copy(0, 0).start()

    idx_copy(j, slot).wait()
    idx_copy(j + 1, other).start()

    def make(r, p1, p2):
        src = xn_hbm.at[j * tm + r]
        pltpu.make_async_copy(src, xg_hbm.at[p1], sem_d.at[slot]).start(priority=0)
        pltpu.make_async_copy(src, xg_hbm.at[p2], sem_d.at[slot]).start(priority=1)

    _row_copies(idx_smem, slot, tm, make)

    @pl.when(j > 0)
    def _():
        wait_rows(other)

    @pl.when(j == nt - 1)
    def _():
        wait_rows(slot)
        idx_copy(j + 1, other).wait()


def _dispatch(xn3, table, pad_tiles, *, tm, tr, n_rows):
    t = xn3.shape[0]
    nt = t // tm
    any_spec = pl.BlockSpec(memory_space=pl.ANY)
    grid_spec = pltpu.PrefetchScalarGridSpec(
        num_scalar_prefetch=1,
        grid=(nt,),
        in_specs=[any_spec, any_spec],
        out_specs=any_spec,
        scratch_shapes=[pltpu.SMEM((4 * tm,), jnp.int32),
                        pltpu.VMEM((tr,) + xn3.shape[1:], F32),
                        pltpu.SemaphoreType.DMA((2,)), pltpu.SemaphoreType.DMA((2,)),
                        pltpu.SemaphoreType.DMA])
    return pl.pallas_call(
        functools.partial(_dispatch_kernel, tm=tm, nt=nt, tr=tr),
        grid_spec=grid_spec,
        out_shape=jax.ShapeDtypeStruct((n_rows,) + xn3.shape[1:], F32),
        compiler_params=pltpu.CompilerParams(dimension_semantics=("arbitrary",),
                                             disable_bounds_checks=True),
        name="moe_dispatch",
    )(pad_tiles, table, xn3)


def _expert_kernel(te_ref, nv_ref, x_ref, w1_ref, w3_ref, w2_ref, y_ref):
    tr = x_ref.shape[0]
    nv = nv_ref[pl.program_id(0)]

    @pl.when(nv > 0)
    def _():
        x = x_ref[...].reshape(tr, D_MODEL).astype(BF16)
        h1 = jnp.dot(x, w1_ref[...], preferred_element_type=F32)
        h3 = jnp.dot(x, w3_ref[...], preferred_element_type=F32)
        a = (h1 * _sigmoid(h1) * h3).astype(BF16)
        y_ref[...] = jnp.dot(a, w2_ref[...], preferred_element_type=F32).reshape(y_ref.shape)

    @pl.when(nv == 0)
    def _():
        y_ref[...] = jnp.zeros_like(y_ref)


def _experts(xg3, tile_expert, n_valid, w1, w3, w2, *, tr):
    n_rows, sl, ln = xg3.shape
    d = sl * ln
    fe = w1.shape[2]
    rows = lambda j, te, nv: (j, 0, 0)
    wsel = lambda j, te, nv: (te[j], 0, 0)
    grid_spec = pltpu.PrefetchScalarGridSpec(
        num_scalar_prefetch=2,
        grid=(n_rows // tr,),
        in_specs=[pl.BlockSpec((tr, sl, ln), rows),
                  pl.BlockSpec((None, d, fe), wsel),
                  pl.BlockSpec((None, d, fe), wsel),
                  pl.BlockSpec((None, fe, d), wsel)],
        out_specs=pl.BlockSpec((tr, sl, ln), rows))
    return pl.pallas_call(
        _expert_kernel,
        grid_spec=grid_spec,
        out_shape=jax.ShapeDtypeStruct(xg3.shape, F32),
        compiler_params=_params(1),
        name="moe_experts",
    )(tile_expert, n_valid, xg3, w1, w3, w2)


def _combine_kernel(idx_hbm, h_ref, gate_ref, route_ref, yg_hbm, out_ref,
                    y1buf, y2buf, idx_smem, sem_y, sem_i, *, tm, nt):
    j = pl.program_id(0)
    slot = j % 2
    other = 1 - slot

    def idx_copy(row, s):
        return pltpu.make_async_copy(idx_hbm.at[row], idx_smem.at[pl.ds(s * 2 * tm, 2 * tm)], sem_i.at[s])

    def fetch(s_idx, s_buf):
        def make(r, p1, p2):
            pltpu.make_async_copy(yg_hbm.at[p1], y1buf.at[s_buf, r], sem_y.at[s_buf]).start(priority=0)
            pltpu.make_async_copy(yg_hbm.at[p2], y2buf.at[s_buf, r], sem_y.at[s_buf]).start(priority=1)

        _row_copies(idx_smem, s_idx, tm, make)

    def wait_rows(s):
        pltpu.make_async_copy(yg_hbm.at[pl.ds(0, tm)], y1buf.at[s], sem_y.at[s]).wait()
        pltpu.make_async_copy(yg_hbm.at[pl.ds(0, tm)], y2buf.at[s], sem_y.at[s]).wait()

    @pl.when(j == 0)
    def _():
        first = idx_copy(0, 0)
        first.start()
        first.wait()
        fetch(0, 0)
        idx_copy(1, 1).start()

    idx_copy(j + 1, other).wait()
    fetch(other, other)
    idx_copy(j + 2, slot).start()
    wait_rows(slot)
    rt = route_ref[...]
    y1 = y1buf[slot].reshape(tm, D_MODEL)
    y2 = y2buf[slot].reshape(tm, D_MODEL)
    out_ref[...] = h_ref[...] + gate_ref[...] * (rt[:, 2:3] * y1 + rt[:, 3:4] * y2)

    @pl.when(j == nt - 1)
    def _():
        wait_rows(other)
        idx_copy(j + 2, slot).wait()


def _combine(h, gate, route, table, yg3, *, tm, seq):
    t, d = h.shape
    tpb = seq // tm
    nt = t // tm
    sl, ln = yg3.shape[1:]
    row = lambda i: (i, 0)
    any_spec = pl.BlockSpec(memory_space=pl.ANY)
    return pl.pallas_call(
        functools.partial(_combine_kernel, tm=tm, nt=nt),
        grid=(nt,),
        in_specs=[any_spec,
                  pl.BlockSpec((tm, d), row),
                  pl.BlockSpec((None, 1, d), lambda i: (i // tpb, 0, 0)),
                  pl.BlockSpec((tm, LANES), row),
                  any_spec],
        out_specs=pl.BlockSpec((tm, d), row),
        out_shape=jax.ShapeDtypeStruct((t, d), F32),
        scratch_shapes=[pltpu.VMEM((2, tm, sl, ln), F32), pltpu.VMEM((2, tm, sl, ln), F32),
                        pltpu.SMEM((4 * tm,), jnp.int32),
                        pltpu.SemaphoreType.DMA((2,)), pltpu.SemaphoreType.DMA((2,))],
        compiler_params=pltpu.CompilerParams(dimension_semantics=("arbitrary",),
                                             vmem_limit_bytes=VMEM_LIMIT,
                                             disable_bounds_checks=True),
        name="moe_combine",
    )(table, h, gate, route, yg3)


def _moe(h, g, sh, sc, gate, w1, w3, w2, router, *, seq, tr, tm):
    t = h.shape[0]
    xn3, route, counts = _router(h, g, sh, sc, router, tm=tm, seq=seq)
    tile_expert, n_valid, table, pad_tiles = _route_plan(route, counts, tr, tm)
    xg3 = _dispatch(xn3, table, pad_tiles, tm=tm, tr=tr, n_rows=2 * t + N_EXPERTS * tr)
    yg3 = _experts(xg3, tile_expert, n_valid, w1, w3, w2, tr=tr)
    return _combine(h, gate, route, table, yg3, tm=tm, seq=seq)


def _rope_tables(seq):
    rows = seq // GRID_W
    row, col = jnp.meshgrid(jnp.arange(rows, dtype=F32), jnp.arange(GRID_W, dtype=F32), indexing='ij')
    n_freq = HEAD_DIM // 4
    inv_freq = ROPE_THETA ** (-jnp.arange(n_freq, dtype=F32) / n_freq)
    ang_r = row.reshape(-1, 1) * inv_freq
    ang_c = col.reshape(-1, 1) * inv_freq
    ang = jnp.concatenate([ang_r, ang_r, ang_c, ang_c], axis=-1)
    cos, sin = jnp.cos(ang), jnp.sin(ang)
    first = (jnp.arange(HEAD_DIM) % (2 * n_freq)) < n_freq
    sin_a = jnp.where(first, -sin, 0.0)
    sin_b = jnp.where(first, 0.0, sin)
    rep = LANES // HEAD_DIM
    return tuple(jnp.tile(t, (1, rep)) for t in (cos, sin_a, sin_b))


def kernel(x, c, ctx, c_ctx, w_ada, b_ada, norm1_g, norm2_g, w_in, conv_w, q_norm_g, k_norm_g,
           attn_sink, out_norm_conv_g, out_norm_attn_g, w_out, ffn_w1, ffn_w3, ffn_w2,
           moe_router, moe_w1, moe_w3, moe_w2):
    b, s, d = x.shape
    lc = ctx.shape[1]
    depth = w_ada.shape[0]
    assert d == D_MODEL and s % 512 == 0 and lc % 256 == 0 and b + 1 <= 8

    c8 = jnp.zeros((8, d), F32).at[0:b].set(c).at[b].set(c_ctx)
    mod = _modulation(c8, w_ada, b_ada)

    tables = _rope_tables(s)
    ids = jnp.arange(256)
    gm = (ids[:, None] // HEAD_DIM == ids[None, :] // HEAD_DIM).astype(BF16)
    scale = HEAD_DIM ** -0.5

    h = x.reshape(b * s, d)
    hc = ctx.reshape(b * lc, d)
    for layer in range(depth):
        last = layer == depth - 1
        m = mod[layer]
        lat = [m[0:b, k * d:(k + 1) * d].reshape(b, 1, d) for k in range(6)]
        cx = [jnp.broadcast_to(m[b:b + 1, k * d:(k + 1) * d].reshape(1, 1, d), (b, 1, d)) for k in range(6)]
        w_in_b = w_in[layer].astype(BF16)
        w_out_b = w_out[layer].astype(BF16)
        g1n = norm1_g[layer].reshape(1, d)
        g2n = norm2_g[layer].reshape(1, d)
        gq = (jnp.tile(q_norm_g[layer], N_HEADS) * scale).reshape(1, D_ATTN)
        gk = jnp.tile(k_norm_g[layer], N_KV).reshape(1, LANES)
        gc = out_norm_conv_g[layer].reshape(1, D_CONV)
        ga = out_norm_attn_g[layer].reshape(1, D_ATTN)
        sink = attn_sink[layer]
        cw = conv_w[layer]

        cu, q, kv = _inproj(h, g1n, lat[0], lat[1], w_in_b, gq, gk, gm, tables, tm=512, seq=s)
        cuc, qc, kvc = _inproj(hc, g1n, cx[0], cx[1], w_in_b, gq, gk, gm, None, tm=lc, seq=lc)
        h = _mixer(h, cu, q, kv, kvc, sink, cw, gc, ga, w_out_b, lat[2],
                   tq=512, seq=s, ctx_len=lc, windowed=True)
        if not last:
            hc = _mixer(hc, cuc, qc, None, kvc, sink, cw, gc, ga, w_out_b, cx[2],
                        tq=lc, seq=lc, ctx_len=lc, windowed=False)

        i = layer // 2
        if layer % 2 == 0:
            w1, w3, w2 = ffn_w1[i].astype(BF16), ffn_w3[i].astype(BF16), ffn_w2[i].astype(BF16)
            router = None
        else:
            w1, w3, w2 = moe_w1[i].astype(BF16), moe_w3[i].astype(BF16), moe_w2[i].astype(BF16)
            router = jnp.zeros((d, LANES), F32).at[:, 0:N_EXPERTS].set(moe_router[i])
        if router is None:
            h = _ffn(h, g2n, lat[3], lat[4], lat[5], w1, w3, w2, None, tm=512, seq=s)
        else:
            h = _moe(h, g2n, lat[3], lat[4], lat[5], w1, w3, w2, router, seq=s, tr=512, tm=512)
        if not last:
            hc = _ffn(hc, g2n, cx[3], cx[4], cx[5], w1, w3, w2, router, tm=lc, seq=lc)
    return h.reshape(b, s, d)
```

```python
import functools

import jax
import jax.numpy as jnp
from jax import lax
from jax.experimental import pallas as pl
from jax.experimental.pallas import tpu as pltpu

D_MODEL = 1024
GRID_W = 64
HEAD_DIM = 64
D_CONV = 512
D_ATTN = 512
N_HEADS = 8
N_KV = 2
WINDOW = 128
ROPE_THETA = 10000.0
N_EXPERTS = 8
EPS = 1e-6
KV_OFF = 3 * D_CONV + D_ATTN
D_IN = KV_OFF + 2 * N_KV * HEAD_DIM
LANES = 128
VMEM_LIMIT = 48 * 1024 * 1024

F32 = jnp.float32
BF16 = jnp.bfloat16
NEG_INF = float("-inf")


def _params(n_axes):
    return pltpu.CompilerParams(dimension_semantics=("arbitrary",) * n_axes,
                                vmem_limit_bytes=VMEM_LIMIT)


def _sigmoid(x):
    return 1.0 / (1.0 + jnp.exp(-x))


def _mod_kernel(c_ref, w_ref, b_ref, o_ref):
    c = c_ref[...]
    s = c * _sigmoid(c)
    o_ref[...] = jnp.dot(s, w_ref[...], preferred_element_type=F32) + b_ref[...]


def _modulation(c8, w_ada, b_ada):
    depth, d, n = w_ada.shape
    tn = 1536
    return pl.pallas_call(
        _mod_kernel,
        grid=(depth, n // tn),
        in_specs=[pl.BlockSpec((8, d), lambda l, j: (0, 0)),
                  pl.BlockSpec((None, d, tn), lambda l, j: (l, 0, j)),
                  pl.BlockSpec((None, 1, tn), lambda l, j: (l, 0, j))],
        out_specs=pl.BlockSpec((None, 8, tn), lambda l, j: (l, 0, j)),
        out_shape=jax.ShapeDtypeStruct((depth, 8, n), F32),
        compiler_params=_params(2),
        name="adaln_mod",
    )(c8, w_ada, b_ada.reshape(depth, 1, n))


def _norm_mod(x, g, sh, sc):
    ms = jnp.mean(x * x, axis=-1, keepdims=True)
    return (x * lax.rsqrt(ms + EPS) * g) * (1.0 + sc) + sh


def _inproj_kernel(*refs, rope):
    if rope:
        (h_ref, g_ref, sh_ref, sc_ref, w_ref, gq_ref, gk_ref, gm_ref,
         cos_ref, sa_ref, sb_ref, cu_ref, q_ref, kv_ref) = refs
    else:
        (h_ref, g_ref, sh_ref, sc_ref, w_ref, gq_ref, gk_ref, gm_ref,
         cu_ref, q_ref, kv_ref) = refs
    xn = _norm_mod(h_ref[...], g_ref[...], sh_ref[...], sc_ref[...])
    y = jnp.dot(xn.astype(BF16), w_ref[...], preferred_element_type=F32)
    cu_ref[:, 0:D_CONV] = y[:, 0:D_CONV].astype(BF16)
    cu_ref[:, D_CONV:2 * D_CONV] = (y[:, D_CONV:2 * D_CONV] * y[:, 2 * D_CONV:3 * D_CONV]).astype(BF16)

    gm = gm_ref[...]

    def head_norm(t, gain):
        w = t.shape[1]
        ss = jnp.dot((t * t).astype(BF16), gm[0:w, 0:w], preferred_element_type=F32)
        return t * lax.rsqrt(ss * (1.0 / HEAD_DIM) + EPS) * gain

    def rot(t):
        if not rope:
            return t
        return (t * cos_ref[...] + pltpu.roll(t, LANES - 16, 1) * sa_ref[...]
                + pltpu.roll(t, 16, 1) * sb_ref[...])

    gq = gq_ref[...]
    for j in range(D_ATTN // 256):
        lo = 3 * D_CONV + 256 * j
        qn = head_norm(y[:, lo:lo + 256], gq[:, 256 * j:256 * j + 256])
        for c in range(2):
            q_ref[:, 256 * j + LANES * c:256 * j + LANES * (c + 1)] = rot(
                qn[:, LANES * c:LANES * (c + 1)]).astype(BF16)
    k = rot(head_norm(y[:, KV_OFF:KV_OFF + LANES], gk_ref[...]))
    v = y[:, KV_OFF + LANES:KV_OFF + 2 * LANES]
    kv_ref[:, 0:LANES] = k.astype(BF16)
    kv_ref[:, LANES:2 * LANES] = pltpu.roll(k, HEAD_DIM, 1).astype(BF16)
    kv_ref[:, 2 * LANES:3 * LANES] = v.astype(BF16)
    kv_ref[:, 3 * LANES:4 * LANES] = pltpu.roll(v, HEAD_DIM, 1).astype(BF16)


def _inproj(h, g, sh, sc, w, gq, gk, gm, tables, *, tm, seq):
    t, d = h.shape
    tpb = seq // tm
    rope = tables is not None
    row = lambda i: (i, 0)
    fix = lambda i: (0, 0)
    mod = lambda i: (i // tpb, 0, 0)
    in_specs = [pl.BlockSpec((tm, d), row),
                pl.BlockSpec((1, d), fix),
                pl.BlockSpec((None, 1, d), mod),
                pl.BlockSpec((None, 1, d), mod),
                pl.BlockSpec((d, D_IN), fix),
                pl.BlockSpec((1, D_ATTN), fix),
                pl.BlockSpec((1, LANES), fix),
                pl.BlockSpec((256, 256), fix)]
    args = [h, g, sh, sc, w, gq, gk, gm]
    if rope:
        in_specs += [pl.BlockSpec((tm, LANES), lambda i: (i % tpb, 0))] * 3
        args += list(tables)
    return pl.pallas_call(
        functools.partial(_inproj_kernel, rope=rope),
        grid=(t // tm,),
        in_specs=in_specs,
        out_specs=[pl.BlockSpec((tm, 2 * D_CONV), row),
                   pl.BlockSpec((tm, D_ATTN), row),
                   pl.BlockSpec((tm, 4 * LANES), row)],
        out_shape=[jax.ShapeDtypeStruct((t, 2 * D_CONV), BF16),
                   jax.ShapeDtypeStruct((t, D_ATTN), BF16),
                   jax.ShapeDtypeStruct((t, 4 * LANES), BF16)],
        compiler_params=_params(1),
        name="inproj_rope" if rope else "inproj_ctx",
    )(*args)


_NT = (((1,), (1,)), ((), ()))


def _mixer_kernel(*refs, tq, windowed):
    if windowed:
        (sink_ref, h_ref, cu_ref, cup_ref, cun_ref, q_ref, kvp_ref, kv_ref, kvn_ref, kvc_ref,
         cw_ref, gc_ref, ga_ref, wo_ref, g1_ref, out_ref, kw_ref, ya_ref) = refs
    else:
        (sink_ref, h_ref, cu_ref, q_ref, kvc_ref,
         cw_ref, gc_ref, ga_ref, wo_ref, g1_ref, out_ref, ya_ref) = refs
    i = pl.program_id(1)
    nt = pl.num_programs(1)
    nsub = tq // WINDOW

    cu = cu_ref[...]
    bg = cu[:, 0:D_CONV].astype(F32)
    u = cu[:, D_CONV:2 * D_CONV].astype(F32)
    rows = lax.broadcasted_iota(jnp.int32, (tq, 1), 0)
    if windowed:
        up_row = cup_ref[:, D_CONV:2 * D_CONV].astype(F32)[15:16, :]
        un_row = cun_ref[:, D_CONV:2 * D_CONV].astype(F32)[0:1, :]
        up_row = jnp.where(i > 0, up_row, 0.0)
        un_row = jnp.where(i < nt - 1, un_row, 0.0)
    else:
        up_row = jnp.zeros((1, D_CONV), F32)
        un_row = jnp.zeros((1, D_CONV), F32)
    u_prev = jnp.where(rows == 0, up_row, pltpu.roll(u, 1, 0))
    u_next = jnp.where(rows == tq - 1, un_row, pltpu.roll(u, tq - 1, 0))
    cw = cw_ref[...]
    yc = bg * (cw[0:1, :] * u_prev + cw[1:2, :] * u + cw[2:3, :] * u_next)
    yc = yc * lax.rsqrt(jnp.mean(yc * yc, axis=-1, keepdims=True) + EPS) * gc_ref[...]

    if windowed:
        kw_ref[0:WINDOW, :] = kvp_ref[...]
        kw_ref[WINDOW:WINDOW + tq, :] = kv_ref[...]
        kw_ref[WINDOW + tq:2 * WINDOW + tq, :] = kvn_ref[...]
    lane_lo = lax.broadcasted_iota(jnp.int32, (WINDOW, LANES), 1) < HEAD_DIM
    kvc = kvc_ref[...]
    ctx_parts = [kvc[:, LANES * c:LANES * (c + 1)] for c in range(4)]

    def sub_block(s, carry):
        r0 = pl.multiple_of(s * WINDOW, WINDOW)
        if windowed:
            kwin = kw_ref[pl.ds(r0, 3 * WINDOW), :]
            win_parts = [kwin[:, LANES * c:LANES * (c + 1)] for c in range(4)]
            col = lax.broadcasted_iota(jnp.int32, (WINDOW, 3 * WINDOW), 1)
            rw = lax.broadcasted_iota(jnp.int32, (WINDOW, 3 * WINDOW), 0)
            first = jnp.logical_and(i == 0, s == 0)
            last = jnp.logical_and(i == nt - 1, s == nsub - 1)
            c_lo = jnp.where(first, WINDOW, 0)
            c_hi = jnp.where(last, 2 * WINDOW, 3 * WINDOW)
            valid = (col >= rw) & (col <= rw + 2 * WINDOW) & (col >= c_lo) & (col < c_hi)
        for j in range(N_HEADS // 2):
            grp = j // 2
            qp = q_ref[pl.ds(r0, WINDOW), LANES * j:LANES * (j + 1)]
            zero = jnp.zeros_like(qp)
            halves = []
            for odd in range(2):
                qh = jnp.where(lane_lo, zero, qp) if odd else jnp.where(lane_lo, qp, zero)
                pick = (grp + odd) % 2
                sk = sink_ref[2 * j + odd]
                s_c = lax.dot_general(qh, ctx_parts[pick], _NT, preferred_element_type=F32)
                m = jnp.maximum(jnp.max(s_c, axis=-1, keepdims=True), sk)
                if windowed:
                    s_w = lax.dot_general(qh, win_parts[pick], _NT, preferred_element_type=F32)
                    s_w = jnp.where(valid, s_w, NEG_INF)
                    m = jnp.maximum(m, jnp.max(s_w, axis=-1, keepdims=True))
                p_c = jnp.exp(s_c - m)
                den = jnp.exp(sk - m) + jnp.sum(p_c, axis=-1, keepdims=True)
                o = jnp.dot(p_c.astype(BF16), ctx_parts[2 + pick], preferred_element_type=F32)
                if windowed:
                    p_w = jnp.exp(s_w - m)
                    den = den + jnp.sum(p_w, axis=-1, keepdims=True)
                    o = o + jnp.dot(p_w.astype(BF16), win_parts[2 + pick], preferred_element_type=F32)
                halves.append(o / den)
            ya_ref[pl.ds(r0, WINDOW), LANES * j:LANES * (j + 1)] = jnp.where(lane_lo, halves[0], halves[1])
        return carry

    lax.fori_loop(0, nsub, sub_block, 0)

    ya = ya_ref[...]
    ya = ya * lax.rsqrt(jnp.mean(ya * ya, axis=-1, keepdims=True) + EPS) * ga_ref[...]
    y = (jnp.dot(yc.astype(BF16), wo_ref[0:D_CONV, :], preferred_element_type=F32)
         + jnp.dot(ya.astype(BF16), wo_ref[D_CONV:2 * D_CONV, :], preferred_element_type=F32))
    out_ref[...] = h_ref[...] + g1_ref[...] * y


def _mixer(h, cu, q, kv, kvc, sink, cw, gc, ga, wo, g1, *, tq, seq, ctx_len, windowed):
    t, d = h.shape
    nt = seq // tq
    nb = t // seq
    row = lambda b, i: (b * nt + i, 0)
    fix = lambda b, i: (0, 0)
    smem = pl.BlockSpec(memory_space=pltpu.SMEM)
    tail = [pl.BlockSpec((3, D_CONV), fix),
            pl.BlockSpec((1, D_CONV), fix),
            pl.BlockSpec((1, D_ATTN), fix),
            pl.BlockSpec((d, d), fix),
            pl.BlockSpec((None, 1, d), lambda b, i: (b, 0, 0))]
    ctx_spec = pl.BlockSpec((ctx_len, 4 * LANES), lambda b, i: (b, 0))
    if windowed:
        r16 = tq // 16
        n16 = t // 16
        rw = tq // WINDOW
        nw = t // WINDOW
        in_specs = [smem,
                    pl.BlockSpec((tq, d), row),
                    pl.BlockSpec((tq, 2 * D_CONV), row),
                    pl.BlockSpec((16, 2 * D_CONV), lambda b, i: (jnp.maximum((b * nt + i) * r16 - 1, 0), 0)),
                    pl.BlockSpec((16, 2 * D_CONV), lambda b, i: (jnp.minimum((b * nt + i + 1) * r16, n16 - 1), 0)),
                    pl.BlockSpec((tq, D_ATTN), row),
                    pl.BlockSpec((WINDOW, 4 * LANES), lambda b, i: (jnp.maximum((b * nt + i) * rw - 1, 0), 0)),
                    pl.BlockSpec((tq, 4 * LANES), row),
                    pl.BlockSpec((WINDOW, 4 * LANES), lambda b, i: (jnp.minimum((b * nt + i + 1) * rw, nw - 1), 0)),
                    ctx_spec] + tail
        args = [sink, h, cu, cu, cu, q, kv, kv, kv, kvc, cw, gc, ga, wo, g1]
        scratch = [pltpu.VMEM((tq + 2 * WINDOW, 4 * LANES), BF16), pltpu.VMEM((tq, D_ATTN), F32)]
    else:
        in_specs = [smem,
                    pl.BlockSpec((tq, d), row),
                    pl.BlockSpec((tq, 2 * D_CONV), row),
                    pl.BlockSpec((tq, D_ATTN), row),
                    ctx_spec] + tail
        args = [sink, h, cu, q, kvc, cw, gc, ga, wo, g1]
        scratch = [pltpu.VMEM((tq, D_ATTN), F32)]
    return pl.pallas_call(
        functools.partial(_mixer_kernel, tq=tq, windowed=windowed),
        grid=(nb, nt),
        in_specs=in_specs,
        out_specs=pl.BlockSpec((tq, d), row),
        out_shape=jax.ShapeDtypeStruct((t, d), F32),
        scratch_shapes=scratch,
        compiler_params=_params(2),
        name="mixer_win" if windowed else "mixer_ctx",
    )(*args)


def _ffn_kernel(*refs, moe):
    if moe:
        (h_ref, g_ref, sh_ref, sc_ref, gate_ref, r_ref, w1_ref, w3_ref, w2_ref,
         out_ref, xn_ref, acc_ref, comb_ref) = refs
    else:
        (h_ref, g_ref, sh_ref, sc_ref, gate_ref, w1_ref, w3_ref, w2_ref,
         out_ref, xn_ref, acc_ref) = refs
    e = pl.program_id(1)
    tm = h_ref.shape[0]

    @pl.when(e == 0)
    def _():
        xn = _norm_mod(h_ref[...], g_ref[...], sh_ref[...], sc_ref[...])
        xn_ref[...] = xn.astype(BF16)
        acc_ref[...] = jnp.zeros_like(acc_ref)
        if moe:
            i1, i2, g1, g2 = _top2(jnp.dot(xn, r_ref[...], preferred_element_type=F32))
            lane = lax.broadcasted_iota(jnp.int32, (tm, LANES), 1)
            comb_ref[...] = jnp.where(lane == i1, g1, 0.0) + jnp.where(lane == i2, g2, 0.0)

    xb = xn_ref[...]
    h1 = jnp.dot(xb, w1_ref[...], preferred_element_type=F32)
    h3 = jnp.dot(xb, w3_ref[...], preferred_element_type=F32)
    a = h1 * _sigmoid(h1) * h3
    if moe:
        lane = lax.broadcasted_iota(jnp.int32, (tm, LANES), 1)
        a = a * jnp.sum(jnp.where(lane == e, comb_ref[...], 0.0), axis=-1, keepdims=True)
    acc_ref[...] += jnp.dot(a.astype(BF16), w2_ref[...], preferred_element_type=F32)

    @pl.when(e == pl.num_programs(1) - 1)
    def _():
        out_ref[...] = h_ref[...] + gate_ref[...] * acc_ref[...]


def _ffn(h, g, sh, sc, gate, w1, w3, w2, router, *, tm, seq):
    t, d = h.shape
    tpb = seq // tm
    moe = router is not None
    row = lambda i, e: (i, 0)
    fix = lambda i, e: (0, 0)
    mod = lambda i, e: (i // tpb, 0, 0)
    in_specs = [pl.BlockSpec((tm, d), row),
                pl.BlockSpec((1, d), fix),
                pl.BlockSpec((None, 1, d), mod),
                pl.BlockSpec((None, 1, d), mod),
                pl.BlockSpec((None, 1, d), mod)]
    args = [h, g, sh, sc, gate]
    scratch = [pltpu.VMEM((tm, d), BF16), pltpu.VMEM((tm, d), F32)]
    if moe:
        ne, _, fe = w1.shape
        in_specs += [pl.BlockSpec((d, LANES), fix),
                     pl.BlockSpec((None, d, fe), lambda i, e: (e, 0, 0)),
                     pl.BlockSpec((None, d, fe), lambda i, e: (e, 0, 0)),
                     pl.BlockSpec((None, fe, d), lambda i, e: (e, 0, 0))]
        args += [router, w1, w3, w2]
        scratch += [pltpu.VMEM((tm, LANES), F32)]
    else:
        ne = 2
        fe = w1.shape[1] // ne
        in_specs += [pl.BlockSpec((d, fe), lambda i, e: (0, e)),
                     pl.BlockSpec((d, fe), lambda i, e: (0, e)),
                     pl.BlockSpec((fe, d), lambda i, e: (e, 0))]
        args += [w1, w3, w2]
    return pl.pallas_call(
        functools.partial(_ffn_kernel, moe=moe),
        grid=(t // tm, ne),
        in_specs=in_specs,
        out_specs=pl.BlockSpec((tm, d), row),
        out_shape=jax.ShapeDtypeStruct((t, d), F32),
        scratch_shapes=scratch,
        compiler_params=_params(2),
        name="ffn_moe" if moe else "ffn_dense",
    )(*args)


def _top2(logits):
    lane = lax.broadcasted_iota(jnp.int32, logits.shape, 1)
    lg = jnp.where(lane < N_EXPERTS, logits, NEG_INF)
    m1 = jnp.max(lg, axis=-1, keepdims=True)
    i1 = jnp.min(jnp.where(lg == m1, lane, LANES), axis=-1, keepdims=True)
    lg2 = jnp.where(lane == i1, NEG_INF, lg)
    m2 = jnp.max(lg2, axis=-1, keepdims=True)
    i2 = jnp.min(jnp.where(lg2 == m2, lane, LANES), axis=-1, keepdims=True)
    e2 = jnp.exp(m2 - m1)
    return i1, i2, 1.0 / (1.0 + e2), e2 / (1.0 + e2)


def _router_kernel(h_ref, g_ref, sh_ref, sc_ref, r_ref, tri_ref, xn_ref, route_ref, cnt_ref, base_ref):
    tm = h_ref.shape[0]

    @pl.when(pl.program_id(0) == 0)
    def _():
        base_ref[...] = jnp.zeros_like(base_ref)

    xn = _norm_mod(h_ref[...], g_ref[...], sh_ref[...], sc_ref[...])
    xn_ref[...] = xn.reshape(xn_ref.shape)
    i1, i2, g1, g2 = _top2(jnp.dot(xn, r_ref[...], preferred_element_type=F32))
    lane = lax.broadcasted_iota(jnp.int32, (tm, LANES), 1)
    hit1 = lane == i1
    hit2 = lane == i2
    chosen = jnp.where(jnp.logical_or(hit1, hit2), 1.0, 0.0)
    before = base_ref[...] + jnp.dot(tri_ref[...], chosen.astype(BF16), preferred_element_type=F32)
    r1 = jnp.sum(jnp.where(hit1, before, 0.0), axis=-1, keepdims=True)
    r2 = jnp.sum(jnp.where(hit2, before, 0.0), axis=-1, keepdims=True)
    base_ref[...] += jnp.sum(chosen, axis=0, keepdims=True)
    cnt_ref[...] = base_ref[...]
    fields = (i1.astype(F32), i2.astype(F32), g1, g2, r1, r2)
    route = jnp.zeros((tm, LANES), F32)
    for k, f in enumerate(fields):
        route = jnp.where(lane == k, f, route)
    route_ref[...] = route


def _router(h, g, sh, sc, router, *, tm, seq):
    t, d = h.shape
    tpb = seq // tm
    row = lambda i: (i, 0)
    fix = lambda i: (0, 0)
    mod = lambda i: (i // tpb, 0, 0)
    ids = jnp.arange(tm)
    tri = (ids[None, :] < ids[:, None]).astype(BF16)
    return pl.pallas_call(
        _router_kernel,
        grid=(t // tm,),
        in_specs=[pl.BlockSpec((tm, d), row), pl.BlockSpec((1, d), fix),
                  pl.BlockSpec((None, 1, d), mod), pl.BlockSpec((None, 1, d), mod),
                  pl.BlockSpec((d, LANES), fix), pl.BlockSpec((tm, tm), fix)],
        out_specs=[pl.BlockSpec((tm, d // LANES, LANES), lambda i: (i, 0, 0)),
                   pl.BlockSpec((tm, LANES), row),
                   pl.BlockSpec((1, LANES), fix)],
        out_shape=[jax.ShapeDtypeStruct((t, d // LANES, LANES), F32),
                   jax.ShapeDtypeStruct((t, LANES), F32),
                   jax.ShapeDtypeStruct((1, LANES), F32)],
        scratch_shapes=[pltpu.VMEM((1, LANES), F32)],
        compiler_params=_params(1),
        name="moe_router",
    )(h, g, sh, sc, router, tri)


def _route_plan(route, counts, tr, tm):
    t = route.shape[0]
    counts = counts[0, 0:N_EXPERTS].astype(jnp.int32)
    tiles = (counts + tr - 1) // tr
    tile_end = jnp.cumsum(tiles)
    tile_start = tile_end - tiles
    experts = jnp.arange(N_EXPERTS, dtype=jnp.int32)

    def position(e, r):
        start = jnp.sum(jnp.where(e[:, None] == experts[None, :], tile_start[None, :] * tr, 0), axis=1)
        return start + r

    pos1 = position(route[:, 0].astype(jnp.int32), route[:, 4].astype(jnp.int32))
    pos2 = position(route[:, 1].astype(jnp.int32), route[:, 5].astype(jnp.int32))
    nt = 2 * t // tr + N_EXPERTS
    tid = jnp.arange(nt, dtype=jnp.int32)
    tile_expert = jnp.minimum(jnp.sum((tid[:, None] >= tile_end[None, :]).astype(jnp.int32), axis=1),
                              N_EXPERTS - 1)
    in_tile = tid - jnp.sum(jnp.where(tile_expert[:, None] == experts[None, :], tile_start[None, :], 0), axis=1)
    own = jnp.sum(jnp.where(tile_expert[:, None] == experts[None, :], counts[None, :], 0), axis=1)
    n_valid = jnp.where(tid < tile_end[-1], jnp.clip(own - in_tile * tr, 0, tr), 0)
    table = jnp.concatenate([pos1.reshape(t // tm, tm), pos2.reshape(t // tm, tm)], axis=1)
    table = jnp.concatenate([table, jnp.zeros((2, 2 * tm), jnp.int32)], axis=0)
    tail = tile_end[-1] + experts
    pad_tiles = jnp.concatenate([jnp.where(tiles > 0, tile_end - 1, -1), jnp.where(tail < nt, tail, -1)])
    return tile_expert, n_valid, table, pad_tiles


def _row_copies(idx_smem, s_idx, tm, make):
    base = s_idx * (2 * tm)

    def body(r, c):
        make(r, idx_smem[base + r], idx_smem[base + tm + r])
        return c

    lax.fori_loop(0, tm, body, 0, unroll=8)


def _dispatch_kernel(zt_ref, idx_hbm, xn_hbm, xg_hbm, idx_smem, zbuf, xbuf,
                     sem_d, sem_i, sem_z, sem_in, *, tm, nt, tr):
    j = pl.program_id(0)
    slot = j % 2
    other = 1 - slot
    cur = j % 3
    nxt = (j + 1) % 3

    def in_copy(tile, s):
        return pltpu.make_async_copy(xn_hbm.at[pl.ds(tile * tm, tm)], xbuf.at[s], sem_in.at[s])

    @pl.when(j == 0)
    def _():
        zbuf[...] = jnp.zeros_like(zbuf)
        for k in range(zt_ref.shape[0]):
            fill = pltpu.make_async_copy(zbuf, xg_hbm.at[pl.ds(jnp.maximum(zt_ref[k], 0) * tr, tr)], sem_z)
            pl.when(zt_ref[k] >= 0)(fill.start)
        for k in range(zt_ref.shape[0]):
            fill = pltpu.make_async_copy(zbuf, xg_hbm.at[pl.ds(0, tr)], sem_z)
            pl.when(zt_ref[k] >= 0)(fill.wait)

    def idx_copy(row, s):
        return pltpu.make_async_copy(idx_hbm.at[row], idx_smem.at[pl.ds(s * 2 * tm, 2 * tm)], sem_i.at[s])

    def wait_rows(s):
        for _ in range(2):
            pltpu.make_async_copy(xbuf.at[s], xg_hbm.at[pl.ds(0, tm)], sem_d.at[s]).wait()

    @pl.when(j == 0)
    def _():
        idx_copy(0, 0).start()
        in_copy(0, 0).start()

    @pl.when(j >= 2)
    def _():
        wait_rows(nxt)

    @pl.when(j + 1 < nt)
    def _():
        in_copy(j + 1, nxt).start()

    idx_copy(j, slot).wait()
    idx_copy(j + 1, other).start()
    in_copy(j, cur).wait()

    def make(r, p1, p2):
        src = xbuf.at[cur, r]
        pltpu.make_async_copy(src, xg_hbm.at[p1], sem_d.at[cur]).start(priority=0)
        pltpu.make_async_copy(src, xg_hbm.at[p2], sem_d.at[cur]).start(priority=1)

    _row_copies(idx_smem, slot, tm, make)

    @pl.when(j == nt - 1)
    def _():
        if nt > 1:
            wait_rows((nt - 2) % 3)
        wait_rows((nt - 1) % 3)
        idx_copy(j + 1, other).wait()


def _dispatch(xn3, table, pad_tiles, *, tm, tr, n_rows):
    t = xn3.shape[0]
    nt = t // tm
    any_spec = pl.BlockSpec(memory_space=pl.ANY)
    grid_spec = pltpu.PrefetchScalarGridSpec(
        num_scalar_prefetch=1,
        grid=(nt,),
        in_specs=[any_spec, any_spec],
        out_specs=any_spec,
        scratch_shapes=[pltpu.SMEM((4 * tm,), jnp.int32),
                        pltpu.VMEM((tr,) + xn3.shape[1:], F32),
                        pltpu.VMEM((3, tm) + xn3.shape[1:], F32),
                        pltpu.SemaphoreType.DMA((3,)), pltpu.SemaphoreType.DMA((2,)),
                        pltpu.SemaphoreType.DMA, pltpu.SemaphoreType.DMA((3,))])
    return pl.pallas_call(
        functools.partial(_dispatch_kernel, tm=tm, nt=nt, tr=tr),
        grid_spec=grid_spec,
        out_shape=jax.ShapeDtypeStruct((n_rows,) + xn3.shape[1:], F32),
        compiler_params=pltpu.CompilerParams(dimension_semantics=("arbitrary",),
                                             vmem_limit_bytes=VMEM_LIMIT,
                                             disable_bounds_checks=True),
        name="moe_dispatch",
    )(pad_tiles, table, xn3)


def _expert_kernel(te_ref, nv_ref, x_ref, w1_ref, w3_ref, w2_ref, y_ref):
    tr = x_ref.shape[0]
    nv = nv_ref[pl.program_id(0)]

    @pl.when(nv > 0)
    def _():
        x = x_ref[...].reshape(tr, D_MODEL).astype(BF16)
        h1 = jnp.dot(x, w1_ref[...], preferred_element_type=F32)
        h3 = jnp.dot(x, w3_ref[...], preferred_element_type=F32)
        a = (h1 * _sigmoid(h1) * h3).astype(BF16)
        y_ref[...] = jnp.dot(a, w2_ref[...], preferred_element_type=F32).reshape(y_ref.shape)

    @pl.when(nv == 0)
    def _():
        y_ref[...] = jnp.zeros_like(y_ref)


def _experts(xg3, tile_expert, n_valid, w1, w3, w2, *, tr):
    n_rows, sl, ln = xg3.shape
    d = sl * ln
    fe = w1.shape[2]
    rows = lambda j, te, nv: (j, 0, 0)
    wsel = lambda j, te, nv: (te[j], 0, 0)
    grid_spec = pltpu.PrefetchScalarGridSpec(
        num_scalar_prefetch=2,
        grid=(n_rows // tr,),
        in_specs=[pl.BlockSpec((tr, sl, ln), rows),
                  pl.BlockSpec((None, d, fe), wsel),
                  pl.BlockSpec((None, d, fe), wsel),
                  pl.BlockSpec((None, fe, d), wsel)],
        out_specs=pl.BlockSpec((tr, sl, ln), rows))
    return pl.pallas_call(
        _expert_kernel,
        grid_spec=grid_spec,
        out_shape=jax.ShapeDtypeStruct(xg3.shape, F32),
        compiler_params=_params(1),
        name="moe_experts",
    )(tile_expert, n_valid, xg3, w1, w3, w2)


def _combine_kernel(idx_hbm, h_ref, gate_ref, route_ref, yg_hbm, out_ref,
                    y1buf, y2buf, idx_smem, sem_y, sem_i, *, tm, nt):
    j = pl.program_id(0)
    slot = j % 2
    other = 1 - slot

    def idx_copy(row, s):
        return pltpu.make_async_copy(idx_hbm.at[row], idx_smem.at[pl.ds(s * 2 * tm, 2 * tm)], sem_i.at[s])

    def fetch(s_idx, s_buf):
        def make(r, p1, p2):
            pltpu.make_async_copy(yg_hbm.at[p1], y1buf.at[s_buf, r], sem_y.at[s_buf]).start(priority=0)
            pltpu.make_async_copy(yg_hbm.at[p2], y2buf.at[s_buf, r], sem_y.at[s_buf]).start(priority=1)

        _row_copies(idx_smem, s_idx, tm, make)

    def wait_rows(s):
        pltpu.make_async_copy(yg_hbm.at[pl.ds(0, tm)], y1buf.at[s], sem_y.at[s]).wait()
        pltpu.make_async_copy(yg_hbm.at[pl.ds(0, tm)], y2buf.at[s], sem_y.at[s]).wait()

    @pl.when(j == 0)
    def _():
        first = idx_copy(0, 0)
        first.start()
        first.wait()
        fetch(0, 0)
        idx_copy(1, 1).start()

    idx_copy(j + 1, other).wait()
    fetch(other, other)
    idx_copy(j + 2, slot).start()
    wait_rows(slot)
    rt = route_ref[...]
    y1 = y1buf[slot].reshape(tm, D_MODEL)
    y2 = y2buf[slot].reshape(tm, D_MODEL)
    out_ref[...] = h_ref[...] + gate_ref[...] * (rt[:, 2:3] * y1 + rt[:, 3:4] * y2)

    @pl.when(j == nt - 1)
    def _():
        wait_rows(other)
        idx_copy(j + 2, slot).wait()


def _combine(h, gate, route, table, yg3, *, tm, seq):
    t, d = h.shape
    tpb = seq // tm
    nt = t // tm
    sl, ln = yg3.shape[1:]
    row = lambda i: (i, 0)
    any_spec = pl.BlockSpec(memory_space=pl.ANY)
    return pl.pallas_call(
        functools.partial(_combine_kernel, tm=tm, nt=nt),
        grid=(nt,),
        in_specs=[any_spec,
                  pl.BlockSpec((tm, d), row),
                  pl.BlockSpec((None, 1, d), lambda i: (i // tpb, 0, 0)),
                  pl.BlockSpec((tm, LANES), row),
                  any_spec],
        out_specs=pl.BlockSpec((tm, d), row),
        out_shape=jax.ShapeDtypeStruct((t, d), F32),
        scratch_shapes=[pltpu.VMEM((2, tm, sl, ln), F32), pltpu.VMEM((2, tm, sl, ln), F32),
                        pltpu.SMEM((4 * tm,), jnp.int32),
                        pltpu.SemaphoreType.DMA((2,)), pltpu.SemaphoreType.DMA((2,))],
        compiler_params=pltpu.CompilerParams(dimension_semantics=("arbitrary",),
                                             vmem_limit_bytes=VMEM_LIMIT,
                                             disable_bounds_checks=True),
        name="moe_combine",
    )(table, h, gate, route, yg3)


def _moe(h, g, sh, sc, gate, w1, w3, w2, router, *, seq, tr, tm):
    t = h.shape[0]
    xn3, route, counts = _router(h, g, sh, sc, router, tm=tm, seq=seq)
    tile_expert, n_valid, table, pad_tiles = _route_plan(route, counts, tr, tm)
    xg3 = _dispatch(xn3, table, pad_tiles, tm=tm, tr=tr, n_rows=2 * t + N_EXPERTS * tr)
    yg3 = _experts(xg3, tile_expert, n_valid, w1, w3, w2, tr=tr)
    return _combine(h, gate, route, table, yg3, tm=tm, seq=seq)


def _rope_tables(seq):
    rows = seq // GRID_W
    row, col = jnp.meshgrid(jnp.arange(rows, dtype=F32), jnp.arange(GRID_W, dtype=F32), indexing='ij')
    n_freq = HEAD_DIM // 4
    inv_freq = ROPE_THETA ** (-jnp.arange(n_freq, dtype=F32) / n_freq)
    ang_r = row.reshape(-1, 1) * inv_freq
    ang_c = col.reshape(-1, 1) * inv_freq
    ang = jnp.concatenate([ang_r, ang_r, ang_c, ang_c], axis=-1)
    cos, sin = jnp.cos(ang), jnp.sin(ang)
    first = (jnp.arange(HEAD_DIM) % (2 * n_freq)) < n_freq
    sin_a = jnp.where(first, -sin, 0.0)
    sin_b = jnp.where(first, 0.0, sin)
    rep = LANES // HEAD_DIM
    return tuple(jnp.tile(t, (1, rep)) for t in (cos, sin_a, sin_b))


def kernel(x, c, ctx, c_ctx, w_ada, b_ada, norm1_g, norm2_g, w_in, conv_w, q_norm_g, k_norm_g,
           attn_sink, out_norm_conv_g, out_norm_attn_g, w_out, ffn_w1, ffn_w3, ffn_w2,
           moe_router, moe_w1, moe_w3, moe_w2):
    b, s, d = x.shape
    lc = ctx.shape[1]
    depth = w_ada.shape[0]
    assert d == D_MODEL and s % 512 == 0 and lc % 256 == 0 and b + 1 <= 8

    c8 = jnp.zeros((8, d), F32).at[0:b].set(c).at[b].set(c_ctx)
    mod = _modulation(c8, w_ada, b_ada)

    tables = _rope_tables(s)
    ids = jnp.arange(256)
    gm = (ids[:, None] // HEAD_DIM == ids[None, :] // HEAD_DIM).astype(BF16)
    scale = HEAD_DIM ** -0.5

    h = x.reshape(b * s, d)
    hc = ctx.reshape(b * lc, d)
    for layer in range(depth):
        last = layer == depth - 1
        m = mod[layer]
        lat = [m[0:b, k * d:(k + 1) * d].reshape(b, 1, d) for k in range(6)]
        cx = [jnp.broadcast_to(m[b:b + 1, k * d:(k + 1) * d].reshape(1, 1, d), (b, 1, d)) for k in range(6)]
        w_in_b = w_in[layer].astype(BF16)
        w_out_b = w_out[layer].astype(BF16)
        g1n = norm1_g[layer].reshape(1, d)
        g2n = norm2_g[layer].reshape(1, d)
        gq = (jnp.tile(q_norm_g[layer], N_HEADS) * scale).reshape(1, D_ATTN)
        gk = jnp.tile(k_norm_g[layer], N_KV).reshape(1, LANES)
        gc = out_norm_conv_g[layer].reshape(1, D_CONV)
        ga = out_norm_attn_g[layer].reshape(1, D_ATTN)
        sink = attn_sink[layer]
        cw = conv_w[layer]

        cu, q, kv = _inproj(h, g1n, lat[0], lat[1], w_in_b, gq, gk, gm, tables, tm=512, seq=s)
        cuc, qc, kvc = _inproj(hc, g1n, cx[0], cx[1], w_in_b, gq, gk, gm, None, tm=lc, seq=lc)
        h = _mixer(h, cu, q, kv, kvc, sink, cw, gc, ga, w_out_b, lat[2],
                   tq=512, seq=s, ctx_len=lc, windowed=True)
        if not last:
            hc = _mixer(hc, cuc, qc, None, kvc, sink, cw, gc, ga, w_out_b, cx[2],
                        tq=lc, seq=lc, ctx_len=lc, windowed=False)

        i = layer // 2
        if layer % 2 == 0:
            w1, w3, w2 = ffn_w1[i].astype(BF16), ffn_w3[i].astype(BF16), ffn_w2[i].astype(BF16)
            router = None
        else:
            w1, w3, w2 = moe_w1[i].astype(BF16), moe_w3[i].astype(BF16), moe_w2[i].astype(BF16)
            router = jnp.zeros((d, LANES), F32).at[:, 0:N_EXPERTS].set(moe_router[i])
        if router is None:
            h = _ffn(h, g2n, lat[3], lat[4], lat[5], w1, w3, w2, None, tm=512, seq=s)
        else:
            h = _moe(h, g2n, lat[3], lat[4], lat[5], w1, w3, w2, router, seq=s, tr=512, tm=512)
        if not last:
            hc = _ffn(hc, g2n, cx[3], cx[4], cx[5], w1, w3, w2, router, tm=lc, seq=lc)
    return h.reshape(b, s, d)
```

```python
import functools

import jax
import jax.numpy as jnp
from jax import lax
from jax.experimental import pallas as pl
from jax.experimental.pallas import tpu as pltpu

D_MODEL = 1024
GRID_W = 64
HEAD_DIM = 64
D_CONV = 512
D_ATTN = 512
N_HEADS = 8
N_KV = 2
WINDOW = 128
ROPE_THETA = 10000.0
N_EXPERTS = 8
EPS = 1e-6
KV_OFF = 3 * D_CONV + D_ATTN
D_IN = KV_OFF + 2 * N_KV * HEAD_DIM
LANES = 128
VMEM_LIMIT = 48 * 1024 * 1024

F32 = jnp.float32
BF16 = jnp.bfloat16
NEG_INF = float("-inf")


def _params(n_axes):
    return pltpu.CompilerParams(dimension_semantics=("arbitrary",) * n_axes,
                                vmem_limit_bytes=VMEM_LIMIT)


def _sigmoid(x):
    return 1.0 / (1.0 + jnp.exp(-x))


def _mod_kernel(c_ref, w_ref, b_ref, o_ref):
    c = c_ref[...]
    s = c * _sigmoid(c)
    o_ref[...] = jnp.dot(s, w_ref[...], preferred_element_type=F32) + b_ref[...]


def _modulation(c8, w_ada, b_ada):
    depth, d, n = w_ada.shape
    tn = 1536
    return pl.pallas_call(
        _mod_kernel,
        grid=(depth, n // tn),
        in_specs=[pl.BlockSpec((8, d), lambda l, j: (0, 0)),
                  pl.BlockSpec((None, d, tn), lambda l, j: (l, 0, j)),
                  pl.BlockSpec((None, 1, tn), lambda l, j: (l, 0, j))],
        out_specs=pl.BlockSpec((None, 8, tn), lambda l, j: (l, 0, j)),
        out_shape=jax.ShapeDtypeStruct((depth, 8, n), F32),
        compiler_params=_params(2),
        name="adaln_mod",
    )(c8, w_ada, b_ada.reshape(depth, 1, n))


def _norm_mod(x, g, sh, sc):
    ms = jnp.mean(x * x, axis=-1, keepdims=True)
    return (x * lax.rsqrt(ms + EPS) * g) * (1.0 + sc) + sh


def _inproj_kernel(*refs, rope):
    if rope:
        (h_ref, g_ref, sh_ref, sc_ref, w_ref, gq_ref, gk_ref, gm_ref,
         cos_ref, sa_ref, sb_ref, cu_ref, q_ref, kv_ref) = refs
    else:
        (h_ref, g_ref, sh_ref, sc_ref, w_ref, gq_ref, gk_ref, gm_ref,
         cu_ref, q_ref, kv_ref) = refs
    xn = _norm_mod(h_ref[...], g_ref[...], sh_ref[...], sc_ref[...])
    y = jnp.dot(xn.astype(BF16), w_ref[...], preferred_element_type=F32)
    cu_ref[:, 0:D_CONV] = y[:, 0:D_CONV].astype(BF16)
    cu_ref[:, D_CONV:2 * D_CONV] = (y[:, D_CONV:2 * D_CONV] * y[:, 2 * D_CONV:3 * D_CONV]).astype(BF16)

    gm = gm_ref[...]

    def head_norm(t, gain):
        w = t.shape[1]
        ss = jnp.dot((t * t).astype(BF16), gm[0:w, 0:w], preferred_element_type=F32)
        return t * lax.rsqrt(ss * (1.0 / HEAD_DIM) + EPS) * gain

    def rot(t):
        if not rope:
            return t
        return (t * cos_ref[...] + pltpu.roll(t, LANES - 16, 1) * sa_ref[...]
                + pltpu.roll(t, 16, 1) * sb_ref[...])

    gq = gq_ref[...]
    for j in range(D_ATTN // 256):
        lo = 3 * D_CONV + 256 * j
        qn = head_norm(y[:, lo:lo + 256], gq[:, 256 * j:256 * j + 256])
        for c in range(2):
            q_ref[:, 256 * j + LANES * c:256 * j + LANES * (c + 1)] = rot(
                qn[:, LANES * c:LANES * (c + 1)]).astype(BF16)
    k = rot(head_norm(y[:, KV_OFF:KV_OFF + LANES], gk_ref[...]))
    v = y[:, KV_OFF + LANES:KV_OFF + 2 * LANES]
    lo = lax.broadcasted_iota(jnp.int32, k.shape, 1) < HEAD_DIM
    for c, t in enumerate((k, v)):
        sw = pltpu.roll(t, HEAD_DIM, 1)
        kv_ref[:, 2 * c * LANES:(2 * c + 1) * LANES] = jnp.where(lo, t, sw).astype(BF16)
        kv_ref[:, (2 * c + 1) * LANES:(2 * c + 2) * LANES] = jnp.where(lo, sw, t).astype(BF16)


def _inproj(h, g, sh, sc, w, gq, gk, gm, tables, *, tm, seq):
    t, d = h.shape
    tpb = seq // tm
    rope = tables is not None
    row = lambda i: (i, 0)
    fix = lambda i: (0, 0)
    mod = lambda i: (i // tpb, 0, 0)
    in_specs = [pl.BlockSpec((tm, d), row),
                pl.BlockSpec((1, d), fix),
                pl.BlockSpec((None, 1, d), mod),
                pl.BlockSpec((None, 1, d), mod),
                pl.BlockSpec((d, D_IN), fix),
                pl.BlockSpec((1, D_ATTN), fix),
                pl.BlockSpec((1, LANES), fix),
                pl.BlockSpec((256, 256), fix)]
    args = [h, g, sh, sc, w, gq, gk, gm]
    if rope:
        in_specs += [pl.BlockSpec((tm, LANES), lambda i: (i % tpb, 0))] * 3
        args += list(tables)
    return pl.pallas_call(
        functools.partial(_inproj_kernel, rope=rope),
        grid=(t // tm,),
        in_specs=in_specs,
        out_specs=[pl.BlockSpec((tm, 2 * D_CONV), row),
                   pl.BlockSpec((tm, D_ATTN), row),
                   pl.BlockSpec((tm, 4 * LANES), row)],
        out_shape=[jax.ShapeDtypeStruct((t, 2 * D_CONV), BF16),
                   jax.ShapeDtypeStruct((t, D_ATTN), BF16),
                   jax.ShapeDtypeStruct((t, 4 * LANES), BF16)],
        compiler_params=_params(1),
        name="inproj_rope" if rope else "inproj_ctx",
    )(*args)


_NT = (((1,), (1,)), ((), ()))


def _mixer_kernel(*refs, tq, windowed):
    if windowed:
        (sink_ref, h_ref, cu_ref, cup_ref, cun_ref, q_ref, kvp_ref, kv_ref, kvn_ref, kvc_ref,
         cw_ref, gc_ref, ga_ref, wo_ref, g1_ref, out_ref, kw_ref, ya_ref) = refs
    else:
        (sink_ref, h_ref, cu_ref, q_ref, kvc_ref,
         cw_ref, gc_ref, ga_ref, wo_ref, g1_ref, out_ref, ya_ref) = refs
    i = pl.program_id(1)
    nt = pl.num_programs(1)
    nsub = tq // WINDOW

    cu = cu_ref[...]
    bg = cu[:, 0:D_CONV].astype(F32)
    u = cu[:, D_CONV:2 * D_CONV].astype(F32)
    rows = lax.broadcasted_iota(jnp.int32, (tq, 1), 0)
    if windowed:
        up_row = cup_ref[:, D_CONV:2 * D_CONV].astype(F32)[15:16, :]
        un_row = cun_ref[:, D_CONV:2 * D_CONV].astype(F32)[0:1, :]
        up_row = jnp.where(i > 0, up_row, 0.0)
        un_row = jnp.where(i < nt - 1, un_row, 0.0)
    else:
        up_row = jnp.zeros((1, D_CONV), F32)
        un_row = jnp.zeros((1, D_CONV), F32)
    u_prev = jnp.where(rows == 0, up_row, pltpu.roll(u, 1, 0))
    u_next = jnp.where(rows == tq - 1, un_row, pltpu.roll(u, tq - 1, 0))
    cw = cw_ref[...]
    yc = bg * (cw[0:1, :] * u_prev + cw[1:2, :] * u + cw[2:3, :] * u_next)
    yc = yc * lax.rsqrt(jnp.mean(yc * yc, axis=-1, keepdims=True) + EPS) * gc_ref[...]

    if windowed:
        kw_ref[0:WINDOW, :] = kvp_ref[...]
        kw_ref[WINDOW:WINDOW + tq, :] = kv_ref[...]
        kw_ref[WINDOW + tq:2 * WINDOW + tq, :] = kvn_ref[...]
    lane_lo = lax.broadcasted_iota(jnp.int32, (WINDOW, LANES), 1) < HEAD_DIM
    kvc = kvc_ref[...]
    gqa = N_HEADS // N_KV
    head_of_row = lax.broadcasted_iota(jnp.int32, (gqa * WINDOW, 1), 0) // WINDOW
    if windowed:
        col = lax.broadcasted_iota(jnp.int32, (WINDOW, 3 * WINDOW), 1)
        rw = lax.broadcasted_iota(jnp.int32, (WINDOW, 3 * WINDOW), 0)
        band = (col >= rw) & (col <= rw + 2 * WINDOW)

    def sub_block(s, carry):
        r0 = pl.multiple_of(s * WINDOW, WINDOW)
        if windowed:
            kwin = kw_ref[pl.ds(r0, 3 * WINDOW), :]
            first = jnp.logical_and(i == 0, s == 0)
            last = jnp.logical_and(i == nt - 1, s == nsub - 1)
            c_lo = jnp.where(first, WINDOW, 0)
            c_hi = jnp.where(last, 2 * WINDOW, 3 * WINDOW)
            valid = band & (col >= c_lo) & (col < c_hi)
        for grp in range(N_KV):
            rows, sk = [], jnp.zeros((gqa * WINDOW, 1), F32)
            for hh in range(gqa):
                head = gqa * grp + hh
                qp = q_ref[pl.ds(r0, WINDOW), LANES * (head // 2):LANES * (head // 2 + 1)]
                zero = jnp.zeros_like(qp)
                rows.append(jnp.where(lane_lo, zero, qp) if head % 2 else jnp.where(lane_lo, qp, zero))
                sk = jnp.where(head_of_row == hh, sink_ref[head], sk)
            q4 = jnp.concatenate(rows, axis=0)
            s_c = lax.dot_general(q4, kvc[:, LANES * grp:LANES * (grp + 1)], _NT, preferred_element_type=F32)
            m = jnp.maximum(jnp.max(s_c, axis=-1, keepdims=True), sk)
            if windowed:
                s_w = lax.dot_general(q4, kwin[:, LANES * grp:LANES * (grp + 1)], _NT,
                                      preferred_element_type=F32)
                s_w = jnp.where(valid[None], s_w.reshape(gqa, WINDOW, 3 * WINDOW), NEG_INF)
                s_w = s_w.reshape(gqa * WINDOW, 3 * WINDOW)
                m = jnp.maximum(m, jnp.max(s_w, axis=-1, keepdims=True))
            p_c = jnp.exp(s_c - m)
            den = jnp.exp(sk - m) + jnp.sum(p_c, axis=-1, keepdims=True)
            o = jnp.dot(p_c.astype(BF16), kvc[:, LANES * (2 + grp):LANES * (3 + grp)],
                        preferred_element_type=F32)
            if windowed:
                p_w = jnp.exp(s_w - m)
                den = den + jnp.sum(p_w, axis=-1, keepdims=True)
                o = o + jnp.dot(p_w.astype(BF16), kwin[:, LANES * (2 + grp):LANES * (3 + grp)],
                                preferred_element_type=F32)
            o = o / den
            for pr in range(gqa // 2):
                pair = (gqa // 2) * grp + pr
                even = o[2 * pr * WINDOW:(2 * pr + 1) * WINDOW]
                odd = o[(2 * pr + 1) * WINDOW:(2 * pr + 2) * WINDOW]
                ya_ref[pl.ds(r0, WINDOW), LANES * pair:LANES * (pair + 1)] = jnp.where(lane_lo, even, odd)
        return carry

    lax.fori_loop(0, nsub, sub_block, 0)

    ya = ya_ref[...]
    ya = ya * lax.rsqrt(jnp.mean(ya * ya, axis=-1, keepdims=True) + EPS) * ga_ref[...]
    y = (jnp.dot(yc.astype(BF16), wo_ref[0:D_CONV, :], preferred_element_type=F32)
         + jnp.dot(ya.astype(BF16), wo_ref[D_CONV:2 * D_CONV, :], preferred_element_type=F32))
    out_ref[...] = h_ref[...] + g1_ref[...] * y


def _mixer(h, cu, q, kv, kvc, sink, cw, gc, ga, wo, g1, *, tq, seq, ctx_len, windowed):
    t, d = h.shape
    nt = seq // tq
    nb = t // seq
    row = lambda b, i: (b * nt + i, 0)
    fix = lambda b, i: (0, 0)
    smem = pl.BlockSpec(memory_space=pltpu.SMEM)
    tail = [pl.BlockSpec((3, D_CONV), fix),
            pl.BlockSpec((1, D_CONV), fix),
            pl.BlockSpec((1, D_ATTN), fix),
            pl.BlockSpec((d, d), fix),
            pl.BlockSpec((None, 1, d), lambda b, i: (b, 0, 0))]
    ctx_spec = pl.BlockSpec((ctx_len, 4 * LANES), lambda b, i: (b, 0))
    if windowed:
        r16 = tq // 16
        n16 = t // 16
        rw = tq // WINDOW
        nw = t // WINDOW
        in_specs = [smem,
                    pl.BlockSpec((tq, d), row),
                    pl.BlockSpec((tq, 2 * D_CONV), row),
                    pl.BlockSpec((16, 2 * D_CONV), lambda b, i: (jnp.maximum((b * nt + i) * r16 - 1, 0), 0)),
                    pl.BlockSpec((16, 2 * D_CONV), lambda b, i: (jnp.minimum((b * nt + i + 1) * r16, n16 - 1), 0)),
                    pl.BlockSpec((tq, D_ATTN), row),
                    pl.BlockSpec((WINDOW, 4 * LANES), lambda b, i: (jnp.maximum((b * nt + i) * rw - 1, 0), 0)),
                    pl.BlockSpec((tq, 4 * LANES), row),
                    pl.BlockSpec((WINDOW, 4 * LANES), lambda b, i: (jnp.minimum((b * nt + i + 1) * rw, nw - 1), 0)),
                    ctx_spec] + tail
        args = [sink, h, cu, cu, cu, q, kv, kv, kv, kvc, cw, gc, ga, wo, g1]
        scratch = [pltpu.VMEM((tq + 2 * WINDOW, 4 * LANES), BF16), pltpu.VMEM((tq, D_ATTN), F32)]
    else:
        in_specs = [smem,
                    pl.BlockSpec((tq, d), row),
                    pl.BlockSpec((tq, 2 * D_CONV), row),
                    pl.BlockSpec((tq, D_ATTN), row),
                    ctx_spec] + tail
        args = [sink, h, cu, q, kvc, cw, gc, ga, wo, g1]
        scratch = [pltpu.VMEM((tq, D_ATTN), F32)]
    return pl.pallas_call(
        functools.partial(_mixer_kernel, tq=tq, windowed=windowed),
        grid=(nb, nt),
        in_specs=in_specs,
        out_specs=pl.BlockSpec((tq, d), row),
        out_shape=jax.ShapeDtypeStruct((t, d), F32),
        scratch_shapes=scratch,
        compiler_params=_params(2),
        name="mixer_win" if windowed else "mixer_ctx",
    )(*args)


def _ffn_kernel(*refs, moe):
    if moe:
        (h_ref, g_ref, sh_ref, sc_ref, gate_ref, r_ref, w1_ref, w3_ref, w2_ref,
         out_ref, xn_ref, acc_ref, comb_ref) = refs
    else:
        (h_ref, g_ref, sh_ref, sc_ref, gate_ref, w1_ref, w3_ref, w2_ref,
         out_ref, xn_ref, acc_ref) = refs
    e = pl.program_id(1)
    tm = h_ref.shape[0]

    @pl.when(e == 0)
    def _():
        xn = _norm_mod(h_ref[...], g_ref[...], sh_ref[...], sc_ref[...])
        xn_ref[...] = xn.astype(BF16)
        acc_ref[...] = jnp.zeros_like(acc_ref)
        if moe:
            i1, i2, g1, g2 = _top2(jnp.dot(xn, r_ref[...], preferred_element_type=F32))
            lane = lax.broadcasted_iota(jnp.int32, (tm, LANES), 1)
            comb_ref[...] = jnp.where(lane == i1, g1, 0.0) + jnp.where(lane == i2, g2, 0.0)

    xb = xn_ref[...]
    h1 = jnp.dot(xb, w1_ref[...], preferred_element_type=F32)
    h3 = jnp.dot(xb, w3_ref[...], preferred_element_type=F32)
    a = h1 * _sigmoid(h1) * h3
    if moe:
        lane = lax.broadcasted_iota(jnp.int32, (tm, LANES), 1)
        a = a * jnp.sum(jnp.where(lane == e, comb_ref[...], 0.0), axis=-1, keepdims=True)
    acc_ref[...] += jnp.dot(a.astype(BF16), w2_ref[...], preferred_element_type=F32)

    @pl.when(e == pl.num_programs(1) - 1)
    def _():
        out_ref[...] = h_ref[...] + gate_ref[...] * acc_ref[...]


def _ffn(h, g, sh, sc, gate, w1, w3, w2, router, *, tm, seq):
    t, d = h.shape
    tpb = seq // tm
    moe = router is not None
    row = lambda i, e: (i, 0)
    fix = lambda i, e: (0, 0)
    mod = lambda i, e: (i // tpb, 0, 0)
    in_specs = [pl.BlockSpec((tm, d), row),
                pl.BlockSpec((1, d), fix),
                pl.BlockSpec((None, 1, d), mod),
                pl.BlockSpec((None, 1, d), mod),
                pl.BlockSpec((None, 1, d), mod)]
    args = [h, g, sh, sc, gate]
    scratch = [pltpu.VMEM((tm, d), BF16), pltpu.VMEM((tm, d), F32)]
    if moe:
        ne, _, fe = w1.shape
        in_specs += [pl.BlockSpec((d, LANES), fix),
                     pl.BlockSpec((None, d, fe), lambda i, e: (e, 0, 0)),
                     pl.BlockSpec((None, d, fe), lambda i, e: (e, 0, 0)),
                     pl.BlockSpec((None, fe, d), lambda i, e: (e, 0, 0))]
        args += [router, w1, w3, w2]
        scratch += [pltpu.VMEM((tm, LANES), F32)]
    else:
        ne = 2
        fe = w1.shape[1] // ne
        in_specs += [pl.BlockSpec((d, fe), lambda i, e: (0, e)),
                     pl.BlockSpec((d, fe), lambda i, e: (0, e)),
                     pl.BlockSpec((fe, d), lambda i, e: (e, 0))]
        args += [w1, w3, w2]
    return pl.pallas_call(
        functools.partial(_ffn_kernel, moe=moe),
        grid=(t // tm, ne),
        in_specs=in_specs,
        out_specs=pl.BlockSpec((tm, d), row),
        out_shape=jax.ShapeDtypeStruct((t, d), F32),
        scratch_shapes=scratch,
        compiler_params=_params(2),
        name="ffn_moe" if moe else "ffn_dense",
    )(*args)


def _top2(logits):
    lane = lax.broadcasted_iota(jnp.int32, logits.shape, 1)
    lg = jnp.where(lane < N_EXPERTS, logits, NEG_INF)
    m1 = jnp.max(lg, axis=-1, keepdims=True)
    i1 = jnp.min(jnp.where(lg == m1, lane, LANES), axis=-1, keepdims=True)
    lg2 = jnp.where(lane == i1, NEG_INF, lg)
    m2 = jnp.max(lg2, axis=-1, keepdims=True)
    i2 = jnp.min(jnp.where(lg2 == m2, lane, LANES), axis=-1, keepdims=True)
    e2 = jnp.exp(m2 - m1)
    return i1, i2, 1.0 / (1.0 + e2), e2 / (1.0 + e2)


def _router_kernel(h_ref, g_ref, sh_ref, sc_ref, r_ref, tri_ref, xn_ref, route_ref, cnt_ref, base_ref):
    tm = h_ref.shape[0]

    @pl.when(pl.program_id(0) == 0)
    def _():
        base_ref[...] = jnp.zeros_like(base_ref)

    xn = _norm_mod(h_ref[...], g_ref[...], sh_ref[...], sc_ref[...])
    xn_ref[...] = xn.reshape(xn_ref.shape)
    i1, i2, g1, g2 = _top2(jnp.dot(xn, r_ref[...], preferred_element_type=F32))
    lane = lax.broadcasted_iota(jnp.int32, (tm, LANES), 1)
    hit1 = lane == i1
    hit2 = lane == i2
    chosen = jnp.where(jnp.logical_or(hit1, hit2), 1.0, 0.0)
    before = base_ref[...] + jnp.dot(tri_ref[...], chosen.astype(BF16), preferred_element_type=F32)
    r1 = jnp.sum(jnp.where(hit1, before, 0.0), axis=-1, keepdims=True)
    r2 = jnp.sum(jnp.where(hit2, before, 0.0), axis=-1, keepdims=True)
    base_ref[...] += jnp.sum(chosen, axis=0, keepdims=True)
    cnt_ref[...] = base_ref[...]
    fields = (i1.astype(F32), i2.astype(F32), g1, g2, r1, r2)
    route = jnp.zeros((tm, LANES), F32)
    for k, f in enumerate(fields):
        route = jnp.where(lane == k, f, route)
    route_ref[...] = route


def _router(h, g, sh, sc, router, *, tm, seq):
    t, d = h.shape
    tpb = seq // tm
    row = lambda i: (i, 0)
    fix = lambda i: (0, 0)
    mod = lambda i: (i // tpb, 0, 0)
    ids = jnp.arange(tm)
    tri = (ids[None, :] < ids[:, None]).astype(BF16)
    return pl.pallas_call(
        _router_kernel,
        grid=(t // tm,),
        in_specs=[pl.BlockSpec((tm, d), row), pl.BlockSpec((1, d), fix),
                  pl.BlockSpec((None, 1, d), mod), pl.BlockSpec((None, 1, d), mod),
                  pl.BlockSpec((d, LANES), fix), pl.BlockSpec((tm, tm), fix)],
        out_specs=[pl.BlockSpec((tm, d // LANES, LANES), lambda i: (i, 0, 0)),
                   pl.BlockSpec((tm, LANES), row),
                   pl.BlockSpec((1, LANES), fix)],
        out_shape=[jax.ShapeDtypeStruct((t, d // LANES, LANES), F32),
                   jax.ShapeDtypeStruct((t, LANES), F32),
                   jax.ShapeDtypeStruct((1, LANES), F32)],
        scratch_shapes=[pltpu.VMEM((1, LANES), F32)],
        compiler_params=_params(1),
        name="moe_router",
    )(h, g, sh, sc, router, tri)


def _route_plan(route, counts, tr, tm):
    t = route.shape[0]
    counts = counts[0, 0:N_EXPERTS].astype(jnp.int32)
    tiles = (counts + tr - 1) // tr
    tile_end = jnp.cumsum(tiles)
    tile_start = tile_end - tiles
    experts = jnp.arange(N_EXPERTS, dtype=jnp.int32)

    def position(e, r):
        start = jnp.sum(jnp.where(e[:, None] == experts[None, :], tile_start[None, :] * tr, 0), axis=1)
        return start + r

    pos1 = position(route[:, 0].astype(jnp.int32), route[:, 4].astype(jnp.int32))
    pos2 = position(route[:, 1].astype(jnp.int32), route[:, 5].astype(jnp.int32))
    nt = 2 * t // tr + N_EXPERTS
    tid = jnp.arange(nt, dtype=jnp.int32)
    tile_expert = jnp.minimum(jnp.sum((tid[:, None] >= tile_end[None, :]).astype(jnp.int32), axis=1),
                              N_EXPERTS - 1)
    in_tile = tid - jnp.sum(jnp.where(tile_expert[:, None] == experts[None, :], tile_start[None, :], 0), axis=1)
    own = jnp.sum(jnp.where(tile_expert[:, None] == experts[None, :], counts[None, :], 0), axis=1)
    n_valid = jnp.where(tid < tile_end[-1], jnp.clip(own - in_tile * tr, 0, tr), 0)
    table = jnp.concatenate([pos1.reshape(t // tm, tm), pos2.reshape(t // tm, tm)], axis=1)
    table = jnp.concatenate([table, jnp.zeros((2, 2 * tm), jnp.int32)], axis=0)
    tail = tile_end[-1] + experts
    pad_tiles = jnp.concatenate([jnp.where(tiles > 0, tile_end - 1, -1), jnp.where(tail < nt, tail, -1)])
    return tile_expert, n_valid, table, pad_tiles


def _row_copies(idx_smem, s_idx, tm, make):
    base = s_idx * (2 * tm)

    def body(r, c):
        make(r, idx_smem[base + r], idx_smem[base + tm + r])
        return c

    lax.fori_loop(0, tm, body, 0, unroll=8)


def _dispatch_kernel(zt_ref, idx_hbm, xn_hbm, xg_hbm, idx_smem, zbuf, xbuf,
                     sem_d, sem_i, sem_z, sem_in, *, tm, nt, tr):
    j = pl.program_id(0)
    slot = j % 2
    other = 1 - slot
    cur = j % 3
    nxt = (j + 1) % 3

    def in_copy(tile, s):
        return pltpu.make_async_copy(xn_hbm.at[pl.ds(tile * tm, tm)], xbuf.at[s], sem_in.at[s])

    @pl.when(j == 0)
    def _():
        zbuf[...] = jnp.zeros_like(zbuf)
        for k in range(zt_ref.shape[0]):
            fill = pltpu.make_async_copy(zbuf, xg_hbm.at[pl.ds(jnp.maximum(zt_ref[k], 0) * tr, tr)], sem_z)
            pl.when(zt_ref[k] >= 0)(fill.start)
        for k in range(zt_ref.shape[0]):
            fill = pltpu.make_async_copy(zbuf, xg_hbm.at[pl.ds(0, tr)], sem_z)
            pl.when(zt_ref[k] >= 0)(fill.wait)

    def idx_copy(row, s):
        return pltpu.make_async_copy(idx_hbm.at[row], idx_smem.at[pl.ds(s * 2 * tm, 2 * tm)], sem_i.at[s])

    def wait_rows(s):
        for _ in range(2):
            pltpu.make_async_copy(xbuf.at[s], xg_hbm.at[pl.ds(0, tm)], sem_d.at[s]).wait()

    @pl.when(j == 0)
    def _():
        idx_copy(0, 0).start()
        in_copy(0, 0).start()

    @pl.when(j >= 2)
    def _():
        wait_rows(nxt)

    @pl.when(j + 1 < nt)
    def _():
        in_copy(j + 1, nxt).start()

    idx_copy(j, slot).wait()
    idx_copy(j + 1, other).start()
    in_copy(j, cur).wait()

    def make(r, p1, p2):
        src = xbuf.at[cur, r]
        pltpu.make_async_copy(src, xg_hbm.at[p1], sem_d.at[cur]).start(priority=0)
        pltpu.make_async_copy(src, xg_hbm.at[p2], sem_d.at[cur]).start(priority=1)

    _row_copies(idx_smem, slot, tm, make)

    @pl.when(j == nt - 1)
    def _():
        if nt > 1:
            wait_rows((nt - 2) % 3)
        wait_rows((nt - 1) % 3)
        idx_copy(j + 1, other).wait()


def _dispatch(xn3, table, pad_tiles, *, tm, tr, n_rows):
    t = xn3.shape[0]
    nt = t // tm
    any_spec = pl.BlockSpec(memory_space=pl.ANY)
    grid_spec = pltpu.PrefetchScalarGridSpec(
        num_scalar_prefetch=1,
        grid=(nt,),
        in_specs=[any_spec, any_spec],
        out_specs=any_spec,
        scratch_shapes=[pltpu.SMEM((4 * tm,), jnp.int32),
                        pltpu.VMEM((tr,) + xn3.shape[1:], F32),
                        pltpu.VMEM((3, tm) + xn3.shape[1:], F32),
                        pltpu.SemaphoreType.DMA((3,)), pltpu.SemaphoreType.DMA((2,)),
                        pltpu.SemaphoreType.DMA, pltpu.SemaphoreType.DMA((3,))])
    return pl.pallas_call(
        functools.partial(_dispatch_kernel, tm=tm, nt=nt, tr=tr),
        grid_spec=grid_spec,
        out_shape=jax.ShapeDtypeStruct((n_rows,) + xn3.shape[1:], F32),
        compiler_params=pltpu.CompilerParams(dimension_semantics=("arbitrary",),
                                             vmem_limit_bytes=VMEM_LIMIT,
                                             disable_bounds_checks=True),
        name="moe_dispatch",
    )(pad_tiles, table, xn3)


def _expert_kernel(te_ref, nv_ref, x_ref, w1_ref, w3_ref, w2_ref, y_ref):
    tr = x_ref.shape[0]
    nv = nv_ref[pl.program_id(0)]

    @pl.when(nv > 0)
    def _():
        x = x_ref[...].reshape(tr, D_MODEL).astype(BF16)
        h1 = jnp.dot(x, w1_ref[...], preferred_element_type=F32)
        h3 = jnp.dot(x, w3_ref[...], preferred_element_type=F32)
        a = (h1 * _sigmoid(h1) * h3).astype(BF16)
        y_ref[...] = jnp.dot(a, w2_ref[...], preferred_element_type=F32).reshape(y_ref.shape)

    @pl.when(nv == 0)
    def _():
        y_ref[...] = jnp.zeros_like(y_ref)


def _experts(xg3, tile_expert, n_valid, w1, w3, w2, *, tr):
    n_rows, sl, ln = xg3.shape
    d = sl * ln
    fe = w1.shape[2]
    rows = lambda j, te, nv: (j, 0, 0)
    wsel = lambda j, te, nv: (te[j], 0, 0)
    grid_spec = pltpu.PrefetchScalarGridSpec(
        num_scalar_prefetch=2,
        grid=(n_rows // tr,),
        in_specs=[pl.BlockSpec((tr, sl, ln), rows),
                  pl.BlockSpec((None, d, fe), wsel),
                  pl.BlockSpec((None, d, fe), wsel),
                  pl.BlockSpec((None, fe, d), wsel)],
        out_specs=pl.BlockSpec((tr, sl, ln), rows))
    return pl.pallas_call(
        _expert_kernel,
        grid_spec=grid_spec,
        out_shape=jax.ShapeDtypeStruct(xg3.shape, F32),
        compiler_params=_params(1),
        name="moe_experts",
    )(tile_expert, n_valid, xg3, w1, w3, w2)


def _combine_kernel(idx_hbm, h_ref, gate_ref, route_ref, yg_hbm, out_ref,
                    y1buf, y2buf, idx_smem, sem_y, sem_i, *, tm, nt):
    j = pl.program_id(0)
    slot = j % 2
    other = 1 - slot

    def idx_copy(row, s):
        return pltpu.make_async_copy(idx_hbm.at[row], idx_smem.at[pl.ds(s * 2 * tm, 2 * tm)], sem_i.at[s])

    def fetch(s_idx, s_buf):
        def make(r, p1, p2):
            pltpu.make_async_copy(yg_hbm.at[p1], y1buf.at[s_buf, r], sem_y.at[s_buf]).start(priority=0)
            pltpu.make_async_copy(yg_hbm.at[p2], y2buf.at[s_buf, r], sem_y.at[s_buf]).start(priority=1)

        _row_copies(idx_smem, s_idx, tm, make)

    def wait_rows(s):
        pltpu.make_async_copy(yg_hbm.at[pl.ds(0, tm)], y1buf.at[s], sem_y.at[s]).wait()
        pltpu.make_async_copy(yg_hbm.at[pl.ds(0, tm)], y2buf.at[s], sem_y.at[s]).wait()

    @pl.when(j == 0)
    def _():
        first = idx_copy(0, 0)
        first.start()
        first.wait()
        fetch(0, 0)
        idx_copy(1, 1).start()

    idx_copy(j + 1, other).wait()
    fetch(other, other)
    idx_copy(j + 2, slot).start()
    wait_rows(slot)
    rt = route_ref[...]
    y1 = y1buf[slot].reshape(tm, D_MODEL)
    y2 = y2buf[slot].reshape(tm, D_MODEL)
    out_ref[...] = h_ref[...] + gate_ref[...] * (rt[:, 2:3] * y1 + rt[:, 3:4] * y2)

    @pl.when(j == nt - 1)
    def _():
        wait_rows(other)
        idx_copy(j + 2, slot).wait()


def _combine(h, gate, route, table, yg3, *, tm, seq):
    t, d = h.shape
    tpb = seq // tm
    nt = t // tm
    sl, ln = yg3.shape[1:]
    row = lambda i: (i, 0)
    any_spec = pl.BlockSpec(memory_space=pl.ANY)
    return pl.pallas_call(
        functools.partial(_combine_kernel, tm=tm, nt=nt),
        grid=(nt,),
        in_specs=[any_spec,
                  pl.BlockSpec((tm, d), row),
                  pl.BlockSpec((None, 1, d), lambda i: (i // tpb, 0, 0)),
                  pl.BlockSpec((tm, LANES), row),
                  any_spec],
        out_specs=pl.BlockSpec((tm, d), row),
        out_shape=jax.ShapeDtypeStruct((t, d), F32),
        scratch_shapes=[pltpu.VMEM((2, tm, sl, ln), F32), pltpu.VMEM((2, tm, sl, ln), F32),
                        pltpu.SMEM((4 * tm,), jnp.int32),
                        pltpu.SemaphoreType.DMA((2,)), pltpu.SemaphoreType.DMA((2,))],
        compiler_params=pltpu.CompilerParams(dimension_semantics=("arbitrary",),
                                             vmem_limit_bytes=VMEM_LIMIT,
                                             disable_bounds_checks=True),
        name="moe_combine",
    )(table, h, gate, route, yg3)


def _moe(h, g, sh, sc, gate, w1, w3, w2, router, *, seq, tr, tm):
    t = h.shape[0]
    xn3, route, counts = _router(h, g, sh, sc, router, tm=tm, seq=seq)
    tile_expert, n_valid, table, pad_tiles = _route_plan(route, counts, tr, tm)
    xg3 = _dispatch(xn3, table, pad_tiles, tm=tm, tr=tr, n_rows=2 * t + N_EXPERTS * tr)
    yg3 = _experts(xg3, tile_expert, n_valid, w1, w3, w2, tr=tr)
    return _combine(h, gate, route, table, yg3, tm=tm, seq=seq)


def _rope_tables(seq):
    rows = seq // GRID_W
    row, col = jnp.meshgrid(jnp.arange(rows, dtype=F32), jnp.arange(GRID_W, dtype=F32), indexing='ij')
    n_freq = HEAD_DIM // 4
    inv_freq = ROPE_THETA ** (-jnp.arange(n_freq, dtype=F32) / n_freq)
    ang_r = row.reshape(-1, 1) * inv_freq
    ang_c = col.reshape(-1, 1) * inv_freq
    ang = jnp.concatenate([ang_r, ang_r, ang_c, ang_c], axis=-1)
    cos, sin = jnp.cos(ang), jnp.sin(ang)
    first = (jnp.arange(HEAD_DIM) % (2 * n_freq)) < n_freq
    sin_a = jnp.where(first, -sin, 0.0)
    sin_b = jnp.where(first, 0.0, sin)
    rep = LANES // HEAD_DIM
    return tuple(jnp.tile(t, (1, rep)) for t in (cos, sin_a, sin_b))


def kernel(x, c, ctx, c_ctx, w_ada, b_ada, norm1_g, norm2_g, w_in, conv_w, q_norm_g, k_norm_g,
           attn_sink, out_norm_conv_g, out_norm_attn_g, w_out, ffn_w1, ffn_w3, ffn_w2,
           moe_router, moe_w1, moe_w3, moe_w2):
    b, s, d = x.shape
    lc = ctx.shape[1]
    depth = w_ada.shape[0]
    assert d == D_MODEL and s % 512 == 0 and lc % 256 == 0 and b + 1 <= 8

    c8 = jnp.zeros((8, d), F32).at[0:b].set(c).at[b].set(c_ctx)
    mod = _modulation(c8, w_ada, b_ada)

    tables = _rope_tables(s)
    ids = jnp.arange(256)
    gm = (ids[:, None] // HEAD_DIM == ids[None, :] // HEAD_DIM).astype(BF16)
    scale = HEAD_DIM ** -0.5

    h = x.reshape(b * s, d)
    hc = ctx.reshape(b * lc, d)
    for layer in range(depth):
        last = layer == depth - 1
        m = mod[layer]
        lat = [m[0:b, k * d:(k + 1) * d].reshape(b, 1, d) for k in range(6)]
        cx = [jnp.broadcast_to(m[b:b + 1, k * d:(k + 1) * d].reshape(1, 1, d), (b, 1, d)) for k in range(6)]
        w_in_b = w_in[layer].astype(BF16)
        w_out_b = w_out[layer].astype(BF16)
        g1n = norm1_g[layer].reshape(1, d)
        g2n = norm2_g[layer].reshape(1, d)
        gq = (jnp.tile(q_norm_g[layer], N_HEADS) * scale).reshape(1, D_ATTN)
        gk = jnp.tile(k_norm_g[layer], N_KV).reshape(1, LANES)
        gc = out_norm_conv_g[layer].reshape(1, D_CONV)
        ga = out_norm_attn_g[layer].reshape(1, D_ATTN)
        sink = attn_sink[layer]
        cw = conv_w[layer]

        cu, q, kv = _inproj(h, g1n, lat[0], lat[1], w_in_b, gq, gk, gm, tables, tm=512, seq=s)
        cuc, qc, kvc = _inproj(hc, g1n, cx[0], cx[1], w_in_b, gq, gk, gm, None, tm=lc, seq=lc)
        h = _mixer(h, cu, q, kv, kvc, sink, cw, gc, ga, w_out_b, lat[2],
                   tq=512, seq=s, ctx_len=lc, windowed=True)
        if not last:
            hc = _mixer(hc, cuc, qc, None, kvc, sink, cw, gc, ga, w_out_b, cx[2],
                        tq=lc, seq=lc, ctx_len=lc, windowed=False)

        i = layer // 2
        if layer % 2 == 0:
            w1, w3, w2 = ffn_w1[i].astype(BF16), ffn_w3[i].astype(BF16), ffn_w2[i].astype(BF16)
            router = None
        else:
            w1, w3, w2 = moe_w1[i].astype(BF16), moe_w3[i].astype(BF16), moe_w2[i].astype(BF16)
            router = jnp.zeros((d, LANES), F32).at[:, 0:N_EXPERTS].set(moe_router[i])
        if router is None:
            h = _ffn(h, g2n, lat[3], lat[4], lat[5], w1, w3, w2, None, tm=512, seq=s)
        else:
            h = _moe(h, g2n, lat[3], lat[4], lat[5], w1, w3, w2, router, seq=s, tr=512, tm=512)
        if not last:
            hc = _ffn(hc, g2n, cx[3], cx[4], cx[5], w1, w3, w2, router, tm=lc, seq=lc)
    return h.reshape(b, s, d)
```

```python
import functools

import jax
import jax.numpy as jnp
from jax import lax
from jax.experimental import pallas as pl
from jax.experimental.pallas import tpu as pltpu

D_MODEL = 1024
GRID_W = 64
HEAD_DIM = 64
D_CONV = 512
D_ATTN = 512
N_HEADS = 8
N_KV = 2
WINDOW = 128
ROPE_THETA = 10000.0
N_EXPERTS = 8
EPS = 1e-6
KV_OFF = 3 * D_CONV + D_ATTN
D_IN = KV_OFF + 2 * N_KV * HEAD_DIM
LANES = 128
VMEM_LIMIT = 48 * 1024 * 1024

F32 = jnp.float32
BF16 = jnp.bfloat16
NEG_INF = float("-inf")


def _params(n_axes):
    return pltpu.CompilerParams(dimension_semantics=("arbitrary",) * n_axes,
                                vmem_limit_bytes=VMEM_LIMIT)


def _sigmoid(x):
    return 1.0 / (1.0 + jnp.exp(-x))


def _mod_kernel(c_ref, w_ref, b_ref, o_ref):
    c = c_ref[...]
    s = c * _sigmoid(c)
    o_ref[...] = jnp.dot(s, w_ref[...], preferred_element_type=F32) + b_ref[...]


def _modulation(c8, w_ada, b_ada):
    depth, d, n = w_ada.shape
    tn = 1536
    return pl.pallas_call(
        _mod_kernel,
        grid=(depth, n // tn),
        in_specs=[pl.BlockSpec((8, d), lambda l, j: (0, 0)),
                  pl.BlockSpec((None, d, tn), lambda l, j: (l, 0, j)),
                  pl.BlockSpec((None, 1, tn), lambda l, j: (l, 0, j))],
        out_specs=pl.BlockSpec((None, 8, tn), lambda l, j: (l, 0, j)),
        out_shape=jax.ShapeDtypeStruct((depth, 8, n), F32),
        compiler_params=_params(2),
        name="adaln_mod",
    )(c8, w_ada, b_ada.reshape(depth, 1, n))


def _norm_mod(x, g, sh, sc):
    ms = jnp.mean(x * x, axis=-1, keepdims=True)
    return (x * lax.rsqrt(ms + EPS) * g) * (1.0 + sc) + sh


def _inproj_kernel(*refs, rope):
    if rope:
        (h_ref, g_ref, sh_ref, sc_ref, w_ref, gq_ref, gk_ref, gm_ref,
         cos_ref, sa_ref, sb_ref, cu_ref, q_ref, kv_ref) = refs
    else:
        (h_ref, g_ref, sh_ref, sc_ref, w_ref, gq_ref, gk_ref, gm_ref,
         cu_ref, q_ref, kv_ref) = refs
    xn = _norm_mod(h_ref[...], g_ref[...], sh_ref[...], sc_ref[...])
    y = jnp.dot(xn.astype(BF16), w_ref[...], preferred_element_type=F32)
    cu_ref[:, 0:D_CONV] = y[:, 0:D_CONV].astype(BF16)
    cu_ref[:, D_CONV:2 * D_CONV] = (y[:, D_CONV:2 * D_CONV] * y[:, 2 * D_CONV:3 * D_CONV]).astype(BF16)

    gm = gm_ref[...]

    def head_norm(t, gain):
        w = t.shape[1]
        ss = jnp.dot((t * t).astype(BF16), gm[0:w, 0:w], preferred_element_type=F32)
        return t * lax.rsqrt(ss * (1.0 / HEAD_DIM) + EPS) * gain

    def rot(t):
        if not rope:
            return t
        return (t * cos_ref[...] + pltpu.roll(t, LANES - 16, 1) * sa_ref[...]
                + pltpu.roll(t, 16, 1) * sb_ref[...])

    gq = gq_ref[...]
    for j in range(D_ATTN // 256):
        lo = 3 * D_CONV + 256 * j
        qn = head_norm(y[:, lo:lo + 256], gq[:, 256 * j:256 * j + 256])
        for c in range(2):
            q_ref[:, 256 * j + LANES * c:256 * j + LANES * (c + 1)] = rot(
                qn[:, LANES * c:LANES * (c + 1)]).astype(BF16)
    k = rot(head_norm(y[:, KV_OFF:KV_OFF + LANES], gk_ref[...]))
    v = y[:, KV_OFF + LANES:KV_OFF + 2 * LANES]
    lo = lax.broadcasted_iota(jnp.int32, k.shape, 1) < HEAD_DIM
    for c, t in enumerate((k, v)):
        sw = pltpu.roll(t, HEAD_DIM, 1)
        kv_ref[:, 2 * c * LANES:(2 * c + 1) * LANES] = jnp.where(lo, t, sw).astype(BF16)
        kv_ref[:, (2 * c + 1) * LANES:(2 * c + 2) * LANES] = jnp.where(lo, sw, t).astype(BF16)


def _inproj(h, g, sh, sc, w, gq, gk, gm, tables, *, tm, seq):
    t, d = h.shape
    tpb = seq // tm
    rope = tables is not None
    row = lambda i: (i, 0)
    fix = lambda i: (0, 0)
    mod = lambda i: (i // tpb, 0, 0)
    in_specs = [pl.BlockSpec((tm, d), row),
                pl.BlockSpec((1, d), fix),
                pl.BlockSpec((None, 1, d), mod),
                pl.BlockSpec((None, 1, d), mod),
                pl.BlockSpec((d, D_IN), fix),
                pl.BlockSpec((1, D_ATTN), fix),
                pl.BlockSpec((1, LANES), fix),
                pl.BlockSpec((256, 256), fix)]
    args = [h, g, sh, sc, w, gq, gk, gm]
    if rope:
        in_specs += [pl.BlockSpec((tm, LANES), lambda i: (i % tpb, 0))] * 3
        args += list(tables)
    return pl.pallas_call(
        functools.partial(_inproj_kernel, rope=rope),
        grid=(t // tm,),
        in_specs=in_specs,
        out_specs=[pl.BlockSpec((tm, 2 * D_CONV), row),
                   pl.BlockSpec((tm, D_ATTN), row),
                   pl.BlockSpec((tm, 4 * LANES), row)],
        out_shape=[jax.ShapeDtypeStruct((t, 2 * D_CONV), BF16),
                   jax.ShapeDtypeStruct((t, D_ATTN), BF16),
                   jax.ShapeDtypeStruct((t, 4 * LANES), BF16)],
        compiler_params=_params(1),
        name="inproj_rope" if rope else "inproj_ctx",
    )(*args)


_NT = (((1,), (1,)), ((), ()))


def _mixer_kernel(*refs, tq, windowed):
    if windowed:
        (sink_ref, h_ref, cu_ref, cup_ref, cun_ref, q_ref, kvp_ref, kv_ref, kvn_ref, kvc_ref,
         cw_ref, gc_ref, ga_ref, wo_ref, g1_ref, out_ref, kw_ref, ya_ref) = refs
    else:
        (sink_ref, h_ref, cu_ref, q_ref, kvc_ref,
         cw_ref, gc_ref, ga_ref, wo_ref, g1_ref, out_ref, ya_ref) = refs
    i = pl.program_id(1)
    nt = pl.num_programs(1)
    nsub = tq // WINDOW

    cu = cu_ref[...]
    bg = cu[:, 0:D_CONV].astype(F32)
    u = cu[:, D_CONV:2 * D_CONV].astype(F32)
    rows = lax.broadcasted_iota(jnp.int32, (tq, 1), 0)
    if windowed:
        up_row = cup_ref[:, D_CONV:2 * D_CONV].astype(F32)[15:16, :]
        un_row = cun_ref[:, D_CONV:2 * D_CONV].astype(F32)[0:1, :]
        up_row = jnp.where(i > 0, up_row, 0.0)
        un_row = jnp.where(i < nt - 1, un_row, 0.0)
    else:
        up_row = jnp.zeros((1, D_CONV), F32)
        un_row = jnp.zeros((1, D_CONV), F32)
    u_prev = jnp.where(rows == 0, up_row, pltpu.roll(u, 1, 0))
    u_next = jnp.where(rows == tq - 1, un_row, pltpu.roll(u, tq - 1, 0))
    cw = cw_ref[...]
    yc = bg * (cw[0:1, :] * u_prev + cw[1:2, :] * u + cw[2:3, :] * u_next)
    yc = yc * lax.rsqrt(jnp.mean(yc * yc, axis=-1, keepdims=True) + EPS) * gc_ref[...]

    if windowed:
        kw_ref[0:WINDOW, :] = kvp_ref[...]
        kw_ref[WINDOW:WINDOW + tq, :] = kv_ref[...]
        kw_ref[WINDOW + tq:2 * WINDOW + tq, :] = kvn_ref[...]
    lane_lo = lax.broadcasted_iota(jnp.int32, (WINDOW, LANES), 1) < HEAD_DIM
    kvc = kvc_ref[...]
    gqa = N_HEADS // N_KV
    head_of_row = lax.broadcasted_iota(jnp.int32, (gqa * WINDOW, 1), 0) // WINDOW
    if windowed:
        col = lax.broadcasted_iota(jnp.int32, (WINDOW, 3 * WINDOW), 1)
        rw = lax.broadcasted_iota(jnp.int32, (WINDOW, 3 * WINDOW), 0)
        band = (col >= rw) & (col <= rw + 2 * WINDOW)

    def sub_block(s, carry):
        r0 = pl.multiple_of(s * WINDOW, WINDOW)
        if windowed:
            kwin = kw_ref[pl.ds(r0, 3 * WINDOW), :]
            first = jnp.logical_and(i == 0, s == 0)
            last = jnp.logical_and(i == nt - 1, s == nsub - 1)
            c_lo = jnp.where(first, WINDOW, 0)
            c_hi = jnp.where(last, 2 * WINDOW, 3 * WINDOW)
            valid = band & (col >= c_lo) & (col < c_hi)
        for grp in range(N_KV):
            rows, sk = [], jnp.zeros((gqa * WINDOW, 1), F32)
            for hh in range(gqa):
                head = gqa * grp + hh
                qp = q_ref[pl.ds(r0, WINDOW), LANES * (head // 2):LANES * (head // 2 + 1)]
                zero = jnp.zeros_like(qp)
                rows.append(jnp.where(lane_lo, zero, qp) if head % 2 else jnp.where(lane_lo, qp, zero))
                sk = jnp.where(head_of_row == hh, sink_ref[head], sk)
            q4 = jnp.concatenate(rows, axis=0)
            s_c = lax.dot_general(q4, kvc[:, LANES * grp:LANES * (grp + 1)], _NT, preferred_element_type=F32)
            m = jnp.maximum(jnp.max(s_c, axis=-1, keepdims=True), sk)
            if windowed:
                s_w = lax.dot_general(q4, kwin[:, LANES * grp:LANES * (grp + 1)], _NT,
                                      preferred_element_type=F32)
                s_w = jnp.where(valid[None], s_w.reshape(gqa, WINDOW, 3 * WINDOW), NEG_INF)
                s_w = s_w.reshape(gqa * WINDOW, 3 * WINDOW)
                m = jnp.maximum(m, jnp.max(s_w, axis=-1, keepdims=True))
            p_c = jnp.exp(s_c - m)
            den = jnp.exp(sk - m) + jnp.sum(p_c, axis=-1, keepdims=True)
            o = jnp.dot(p_c.astype(BF16), kvc[:, LANES * (2 + grp):LANES * (3 + grp)],
                        preferred_element_type=F32)
            if windowed:
                p_w = jnp.exp(s_w - m)
                den = den + jnp.sum(p_w, axis=-1, keepdims=True)
                o = o + jnp.dot(p_w.astype(BF16), kwin[:, LANES * (2 + grp):LANES * (3 + grp)],
                                preferred_element_type=F32)
            o = o / den
            for pr in range(gqa // 2):
                pair = (gqa // 2) * grp + pr
                even = o[2 * pr * WINDOW:(2 * pr + 1) * WINDOW]
                odd = o[(2 * pr + 1) * WINDOW:(2 * pr + 2) * WINDOW]
                ya_ref[pl.ds(r0, WINDOW), LANES * pair:LANES * (pair + 1)] = jnp.where(lane_lo, even, odd)
        return carry

    lax.fori_loop(0, nsub, sub_block, 0)

    ya = ya_ref[...]
    ya = ya * lax.rsqrt(jnp.mean(ya * ya, axis=-1, keepdims=True) + EPS) * ga_ref[...]
    y = (jnp.dot(yc.astype(BF16), wo_ref[0:D_CONV, :], preferred_element_type=F32)
         + jnp.dot(ya.astype(BF16), wo_ref[D_CONV:2 * D_CONV, :], preferred_element_type=F32))
    out_ref[...] = h_ref[...] + g1_ref[...] * y


def _mixer(h, cu, q, kv, kvc, sink, cw, gc, ga, wo, g1, *, tq, seq, ctx_len, windowed):
    t, d = h.shape
    nt = seq // tq
    nb = t // seq
    row = lambda b, i: (b * nt + i, 0)
    fix = lambda b, i: (0, 0)
    smem = pl.BlockSpec(memory_space=pltpu.SMEM)
    tail = [pl.BlockSpec((3, D_CONV), fix),
            pl.BlockSpec((1, D_CONV), fix),
            pl.BlockSpec((1, D_ATTN), fix),
            pl.BlockSpec((d, d), fix),
            pl.BlockSpec((None, 1, d), lambda b, i: (b, 0, 0))]
    ctx_spec = pl.BlockSpec((ctx_len, 4 * LANES), lambda b, i: (b, 0))
    if windowed:
        r16 = tq // 16
        n16 = t // 16
        rw = tq // WINDOW
        nw = t // WINDOW
        in_specs = [smem,
                    pl.BlockSpec((tq, d), row),
                    pl.BlockSpec((tq, 2 * D_CONV), row),
                    pl.BlockSpec((16, 2 * D_CONV), lambda b, i: (jnp.maximum((b * nt + i) * r16 - 1, 0), 0)),
                    pl.BlockSpec((16, 2 * D_CONV), lambda b, i: (jnp.minimum((b * nt + i + 1) * r16, n16 - 1), 0)),
                    pl.BlockSpec((tq, D_ATTN), row),
                    pl.BlockSpec((WINDOW, 4 * LANES), lambda b, i: (jnp.maximum((b * nt + i) * rw - 1, 0), 0)),
                    pl.BlockSpec((tq, 4 * LANES), row),
                    pl.BlockSpec((WINDOW, 4 * LANES), lambda b, i: (jnp.minimum((b * nt + i + 1) * rw, nw - 1), 0)),
                    ctx_spec] + tail
        args = [sink, h, cu, cu, cu, q, kv, kv, kv, kvc, cw, gc, ga, wo, g1]
        scratch = [pltpu.VMEM((tq + 2 * WINDOW, 4 * LANES), BF16), pltpu.VMEM((tq, D_ATTN), F32)]
    else:
        in_specs = [smem,
                    pl.BlockSpec((tq, d), row),
                    pl.BlockSpec((tq, 2 * D_CONV), row),
                    pl.BlockSpec((tq, D_ATTN), row),
                    ctx_spec] + tail
        args = [sink, h, cu, q, kvc, cw, gc, ga, wo, g1]
        scratch = [pltpu.VMEM((tq, D_ATTN), F32)]
    return pl.pallas_call(
        functools.partial(_mixer_kernel, tq=tq, windowed=windowed),
        grid=(nb, nt),
        in_specs=in_specs,
        out_specs=pl.BlockSpec((tq, d), row),
        out_shape=jax.ShapeDtypeStruct((t, d), F32),
        scratch_shapes=scratch,
        compiler_params=_params(2),
        name="mixer_win" if windowed else "mixer_ctx",
    )(*args)


def _ffn_kernel(*refs, moe):
    if moe:
        (h_ref, g_ref, sh_ref, sc_ref, gate_ref, r_ref, w1_ref, w3_ref, w2_ref,
         out_ref, xn_ref, acc_ref, comb_ref) = refs
    else:
        (h_ref, g_ref, sh_ref, sc_ref, gate_ref, w1_ref, w3_ref, w2_ref,
         out_ref, xn_ref, acc_ref) = refs
    e = pl.program_id(1)
    tm = h_ref.shape[0]

    @pl.when(e == 0)
    def _():
        xn = _norm_mod(h_ref[...], g_ref[...], sh_ref[...], sc_ref[...])
        xn_ref[...] = xn.astype(BF16)
        acc_ref[...] = jnp.zeros_like(acc_ref)
        if moe:
            i1, i2, g1, g2 = _top2(jnp.dot(xn, r_ref[...], preferred_element_type=F32))
            lane = lax.broadcasted_iota(jnp.int32, (tm, LANES), 1)
            comb_ref[...] = jnp.where(lane == i1, g1, 0.0) + jnp.where(lane == i2, g2, 0.0)

    xb = xn_ref[...]
    h1 = jnp.dot(xb, w1_ref[...], preferred_element_type=F32)
    h3 = jnp.dot(xb, w3_ref[...], preferred_element_type=F32)
    a = h1 * _sigmoid(h1) * h3
    if moe:
        lane = lax.broadcasted_iota(jnp.int32, (tm, LANES), 1)
        a = a * jnp.sum(jnp.where(lane == e, comb_ref[...], 0.0), axis=-1, keepdims=True)
    acc_ref[...] += jnp.dot(a.astype(BF16), w2_ref[...], preferred_element_type=F32)

    @pl.when(e == pl.num_programs(1) - 1)
    def _():
        out_ref[...] = h_ref[...] + gate_ref[...] * acc_ref[...]


def _ffn(h, g, sh, sc, gate, w1, w3, w2, router, *, tm, seq):
    t, d = h.shape
    tpb = seq // tm
    moe = router is not None
    row = lambda i, e: (i, 0)
    fix = lambda i, e: (0, 0)
    mod = lambda i, e: (i // tpb, 0, 0)
    in_specs = [pl.BlockSpec((tm, d), row),
                pl.BlockSpec((1, d), fix),
                pl.BlockSpec((None, 1, d), mod),
                pl.BlockSpec((None, 1, d), mod),
                pl.BlockSpec((None, 1, d), mod)]
    args = [h, g, sh, sc, gate]
    scratch = [pltpu.VMEM((tm, d), BF16), pltpu.VMEM((tm, d), F32)]
    if moe:
        ne, _, fe = w1.shape
        in_specs += [pl.BlockSpec((d, LANES), fix),
                     pl.BlockSpec((None, d, fe), lambda i, e: (e, 0, 0)),
                     pl.BlockSpec((None, d, fe), lambda i, e: (e, 0, 0)),
                     pl.BlockSpec((None, fe, d), lambda i, e: (e, 0, 0))]
        args += [router, w1, w3, w2]
        scratch += [pltpu.VMEM((tm, LANES), F32)]
    else:
        ne = 2
        fe = w1.shape[1] // ne
        in_specs += [pl.BlockSpec((d, fe), lambda i, e: (0, e)),
                     pl.BlockSpec((d, fe), lambda i, e: (0, e)),
                     pl.BlockSpec((fe, d), lambda i, e: (e, 0))]
        args += [w1, w3, w2]
    return pl.pallas_call(
        functools.partial(_ffn_kernel, moe=moe),
        grid=(t // tm, ne),
        in_specs=in_specs,
        out_specs=pl.BlockSpec((tm, d), row),
        out_shape=jax.ShapeDtypeStruct((t, d), F32),
        scratch_shapes=scratch,
        compiler_params=_params(2),
        name="ffn_moe" if moe else "ffn_dense",
    )(*args)


def _dense_ffn_kernel(h_ref, g_ref, sh_ref, sc_ref, gate_ref, w1_ref, w3_ref, w2_ref, out_ref, *, parts):
    rows = h_ref.shape[0] // parts
    for p in range(parts):
        sl = slice(p * rows, (p + 1) * rows)
        hp = h_ref[sl, :]
        xb = _norm_mod(hp, g_ref[...], sh_ref[...], sc_ref[...]).astype(BF16)
        h1 = jnp.dot(xb, w1_ref[...], preferred_element_type=F32)
        h3 = jnp.dot(xb, w3_ref[...], preferred_element_type=F32)
        a = (h1 * _sigmoid(h1) * h3).astype(BF16)
        y = jnp.dot(a, w2_ref[...], preferred_element_type=F32)
        out_ref[sl, :] = hp + gate_ref[...] * y


def _dense_ffn(h, g, sh, sc, gate, w1, w3, w2, *, tm, seq, parts):
    t, d = h.shape
    f = w1.shape[1]
    tpb = seq // tm
    row = lambda i: (i, 0)
    fix = lambda i: (0, 0)
    mod = lambda i: (i // tpb, 0, 0)
    once = pl.Buffered(1)
    return pl.pallas_call(
        functools.partial(_dense_ffn_kernel, parts=parts),
        grid=(t // tm,),
        in_specs=[pl.BlockSpec((tm, d), row),
                  pl.BlockSpec((1, d), fix),
                  pl.BlockSpec((None, 1, d), mod),
                  pl.BlockSpec((None, 1, d), mod),
                  pl.BlockSpec((None, 1, d), mod),
                  pl.BlockSpec((d, f), fix, pipeline_mode=once),
                  pl.BlockSpec((d, f), fix, pipeline_mode=once),
                  pl.BlockSpec((f, d), fix, pipeline_mode=once)],
        out_specs=pl.BlockSpec((tm, d), row),
        out_shape=jax.ShapeDtypeStruct((t, d), F32),
        compiler_params=_params(1),
        name="ffn_dense",
    )(h, g, sh, sc, gate, w1, w3, w2)


def _top2(logits):
    lane = lax.broadcasted_iota(jnp.int32, logits.shape, 1)
    lg = jnp.where(lane < N_EXPERTS, logits, NEG_INF)
    m1 = jnp.max(lg, axis=-1, keepdims=True)
    i1 = jnp.min(jnp.where(lg == m1, lane, LANES), axis=-1, keepdims=True)
    lg2 = jnp.where(lane == i1, NEG_INF, lg)
    m2 = jnp.max(lg2, axis=-1, keepdims=True)
    i2 = jnp.min(jnp.where(lg2 == m2, lane, LANES), axis=-1, keepdims=True)
    e2 = jnp.exp(m2 - m1)
    return i1, i2, 1.0 / (1.0 + e2), e2 / (1.0 + e2)


def _router_kernel(h_ref, g_ref, sh_ref, sc_ref, r_ref, tri_ref, xn_ref, route_ref, cnt_ref, base_ref):
    tm = h_ref.shape[0]

    @pl.when(pl.program_id(0) == 0)
    def _():
        base_ref[...] = jnp.zeros_like(base_ref)

    xn = _norm_mod(h_ref[...], g_ref[...], sh_ref[...], sc_ref[...])
    xn_ref[...] = xn.reshape(xn_ref.shape)
    i1, i2, g1, g2 = _top2(jnp.dot(xn, r_ref[...], preferred_element_type=F32))
    lane = lax.broadcasted_iota(jnp.int32, (tm, LANES), 1)
    hit1 = lane == i1
    hit2 = lane == i2
    chosen = jnp.where(jnp.logical_or(hit1, hit2), 1.0, 0.0)
    before = base_ref[...] + jnp.dot(tri_ref[...], chosen.astype(BF16), preferred_element_type=F32)
    r1 = jnp.sum(jnp.where(hit1, before, 0.0), axis=-1, keepdims=True)
    r2 = jnp.sum(jnp.where(hit2, before, 0.0), axis=-1, keepdims=True)
    base_ref[...] += jnp.sum(chosen, axis=0, keepdims=True)
    cnt_ref[...] = base_ref[...]
    fields = (i1.astype(F32), i2.astype(F32), g1, g2, r1, r2)
    route = jnp.zeros((tm, LANES), F32)
    for k, f in enumerate(fields):
        route = jnp.where(lane == k, f, route)
    route_ref[...] = route


def _router(h, g, sh, sc, router, *, tm, seq):
    t, d = h.shape
    tpb = seq // tm
    row = lambda i: (i, 0)
    fix = lambda i: (0, 0)
    mod = lambda i: (i // tpb, 0, 0)
    ids = jnp.arange(tm)
    tri = (ids[None, :] < ids[:, None]).astype(BF16)
    return pl.pallas_call(
        _router_kernel,
        grid=(t // tm,),
        in_specs=[pl.BlockSpec((tm, d), row), pl.BlockSpec((1, d), fix),
                  pl.BlockSpec((None, 1, d), mod), pl.BlockSpec((None, 1, d), mod),
                  pl.BlockSpec((d, LANES), fix), pl.BlockSpec((tm, tm), fix)],
        out_specs=[pl.BlockSpec((tm, d // LANES, LANES), lambda i: (i, 0, 0)),
                   pl.BlockSpec((tm, LANES), row),
                   pl.BlockSpec((1, LANES), fix)],
        out_shape=[jax.ShapeDtypeStruct((t, d // LANES, LANES), F32),
                   jax.ShapeDtypeStruct((t, LANES), F32),
                   jax.ShapeDtypeStruct((1, LANES), F32)],
        scratch_shapes=[pltpu.VMEM((1, LANES), F32)],
        compiler_params=_params(1),
        name="moe_router",
    )(h, g, sh, sc, router, tri)


def _route_plan(route, counts, tr, tm):
    t = route.shape[0]
    counts = counts[0, 0:N_EXPERTS].astype(jnp.int32)
    tiles = (counts + tr - 1) // tr
    tile_end = jnp.cumsum(tiles)
    tile_start = tile_end - tiles
    experts = jnp.arange(N_EXPERTS, dtype=jnp.int32)

    def position(e, r):
        start = jnp.sum(jnp.where(e[:, None] == experts[None, :], tile_start[None, :] * tr, 0), axis=1)
        return start + r

    pos1 = position(route[:, 0].astype(jnp.int32), route[:, 4].astype(jnp.int32))
    pos2 = position(route[:, 1].astype(jnp.int32), route[:, 5].astype(jnp.int32))
    nt = 2 * t // tr + N_EXPERTS
    tid = jnp.arange(nt, dtype=jnp.int32)
    tile_expert = jnp.minimum(jnp.sum((tid[:, None] >= tile_end[None, :]).astype(jnp.int32), axis=1),
                              N_EXPERTS - 1)
    in_tile = tid - jnp.sum(jnp.where(tile_expert[:, None] == experts[None, :], tile_start[None, :], 0), axis=1)
    own = jnp.sum(jnp.where(tile_expert[:, None] == experts[None, :], counts[None, :], 0), axis=1)
    n_valid = jnp.where(tid < tile_end[-1], jnp.clip(own - in_tile * tr, 0, tr), 0)
    table = jnp.concatenate([pos1.reshape(t // tm, tm), pos2.reshape(t // tm, tm)], axis=1)
    table = jnp.concatenate([table, jnp.zeros((2, 2 * tm), jnp.int32)], axis=0)
    tail = tile_end[-1] + experts
    pad_tiles = jnp.concatenate([jnp.where(tiles > 0, tile_end - 1, -1), jnp.where(tail < nt, tail, -1)])
    return tile_expert, n_valid, table, pad_tiles


def _row_copies(idx_smem, s_idx, tm, make):
    base = s_idx * (2 * tm)

    def body(r, c):
        make(r, idx_smem[base + r], idx_smem[base + tm + r])
        return c

    lax.fori_loop(0, tm, body, 0, unroll=8)


def _dispatch_kernel(zt_ref, idx_hbm, xn_hbm, xg_hbm, idx_smem, zbuf, xbuf,
                     sem_d, sem_i, sem_z, sem_in, *, tm, nt, tr):
    j = pl.program_id(0)
    slot = j % 2
    other = 1 - slot
    cur = j % 3
    nxt = (j + 1) % 3

    def in_copy(tile, s):
        return pltpu.make_async_copy(xn_hbm.at[pl.ds(tile * tm, tm)], xbuf.at[s], sem_in.at[s])

    @pl.when(j == 0)
    def _():
        zbuf[...] = jnp.zeros_like(zbuf)
        for k in range(zt_ref.shape[0]):
            fill = pltpu.make_async_copy(zbuf, xg_hbm.at[pl.ds(jnp.maximum(zt_ref[k], 0) * tr, tr)], sem_z)
            pl.when(zt_ref[k] >= 0)(fill.start)
        for k in range(zt_ref.shape[0]):
            fill = pltpu.make_async_copy(zbuf, xg_hbm.at[pl.ds(0, tr)], sem_z)
            pl.when(zt_ref[k] >= 0)(fill.wait)

    def idx_copy(row, s):
        return pltpu.make_async_copy(idx_hbm.at[row], idx_smem.at[pl.ds(s * 2 * tm, 2 * tm)], sem_i.at[s])

    def wait_rows(s):
        for _ in range(2):
            pltpu.make_async_copy(xbuf.at[s], xg_hbm.at[pl.ds(0, tm)], sem_d.at[s]).wait()

    @pl.when(j == 0)
    def _():
        idx_copy(0, 0).start()
        in_copy(0, 0).start()

    @pl.when(j >= 2)
    def _():
        wait_rows(nxt)

    @pl.when(j + 1 < nt)
    def _():
        in_copy(j + 1, nxt).start()

    idx_copy(j, slot).wait()
    idx_copy(j + 1, other).start()
    in_copy(j, cur).wait()

    def make(r, p1, p2):
        src = xbuf.at[cur, r]
        pltpu.make_async_copy(src, xg_hbm.at[p1], sem_d.at[cur]).start(priority=0)
        pltpu.make_async_copy(src, xg_hbm.at[p2], sem_d.at[cur]).start(priority=1)

    _row_copies(idx_smem, slot, tm, make)

    @pl.when(j == nt - 1)
    def _():
        if nt > 1:
            wait_rows((nt - 2) % 3)
        wait_rows((nt - 1) % 3)
        idx_copy(j + 1, other).wait()


def _dispatch(xn3, table, pad_tiles, *, tm, tr, n_rows):
    t = xn3.shape[0]
    nt = t // tm
    any_spec = pl.BlockSpec(memory_space=pl.ANY)
    grid_spec = pltpu.PrefetchScalarGridSpec(
        num_scalar_prefetch=1,
        grid=(nt,),
        in_specs=[any_spec, any_spec],
        out_specs=any_spec,
        scratch_shapes=[pltpu.SMEM((4 * tm,), jnp.int32),
                        pltpu.VMEM((tr,) + xn3.shape[1:], F32),
                        pltpu.VMEM((3, tm) + xn3.shape[1:], F32),
                        pltpu.SemaphoreType.DMA((3,)), pltpu.SemaphoreType.DMA((2,)),
                        pltpu.SemaphoreType.DMA, pltpu.SemaphoreType.DMA((3,))])
    return pl.pallas_call(
        functools.partial(_dispatch_kernel, tm=tm, nt=nt, tr=tr),
        grid_spec=grid_spec,
        out_shape=jax.ShapeDtypeStruct((n_rows,) + xn3.shape[1:], F32),
        compiler_params=pltpu.CompilerParams(dimension_semantics=("arbitrary",),
                                             vmem_limit_bytes=VMEM_LIMIT,
                                             disable_bounds_checks=True),
        name="moe_dispatch",
    )(pad_tiles, table, xn3)


def _expert_kernel(te_ref, nv_ref, x_ref, w1_ref, w3_ref, w2_ref, y_ref):
    tr = x_ref.shape[0]
    nv = nv_ref[pl.program_id(0)]

    @pl.when(nv > 0)
    def _():
        x = x_ref[...].reshape(tr, D_MODEL).astype(BF16)
        h1 = jnp.dot(x, w1_ref[...], preferred_element_type=F32)
        h3 = jnp.dot(x, w3_ref[...], preferred_element_type=F32)
        a = (h1 * _sigmoid(h1) * h3).astype(BF16)
        y_ref[...] = jnp.dot(a, w2_ref[...], preferred_element_type=F32).reshape(y_ref.shape)

    @pl.when(nv == 0)
    def _():
        y_ref[...] = jnp.zeros_like(y_ref)


def _experts(xg3, tile_expert, n_valid, w1, w3, w2, *, tr):
    n_rows, sl, ln = xg3.shape
    d = sl * ln
    fe = w1.shape[2]
    rows = lambda j, te, nv: (j, 0, 0)
    wsel = lambda j, te, nv: (te[j], 0, 0)
    grid_spec = pltpu.PrefetchScalarGridSpec(
        num_scalar_prefetch=2,
        grid=(n_rows // tr,),
        in_specs=[pl.BlockSpec((tr, sl, ln), rows),
                  pl.BlockSpec((None, d, fe), wsel),
                  pl.BlockSpec((None, d, fe), wsel),
                  pl.BlockSpec((None, fe, d), wsel)],
        out_specs=pl.BlockSpec((tr, sl, ln), rows))
    return pl.pallas_call(
        _expert_kernel,
        grid_spec=grid_spec,
        out_shape=jax.ShapeDtypeStruct(xg3.shape, F32),
        compiler_params=_params(1),
        name="moe_experts",
    )(tile_expert, n_valid, xg3, w1, w3, w2)


def _combine_kernel(idx_hbm, h_ref, gate_ref, route_ref, yg_hbm, out_ref,
                    y1buf, y2buf, idx_smem, sem_y, sem_i, *, tm, nt):
    j = pl.program_id(0)
    slot = j % 2
    other = 1 - slot

    def idx_copy(row, s):
        return pltpu.make_async_copy(idx_hbm.at[row], idx_smem.at[pl.ds(s * 2 * tm, 2 * tm)], sem_i.at[s])

    def fetch(s_idx, s_buf):
        def make(r, p1, p2):
            pltpu.make_async_copy(yg_hbm.at[p1], y1buf.at[s_buf, r], sem_y.at[s_buf]).start(priority=0)
            pltpu.make_async_copy(yg_hbm.at[p2], y2buf.at[s_buf, r], sem_y.at[s_buf]).start(priority=1)

        _row_copies(idx_smem, s_idx, tm, make)

    def wait_rows(s):
        pltpu.make_async_copy(yg_hbm.at[pl.ds(0, tm)], y1buf.at[s], sem_y.at[s]).wait()
        pltpu.make_async_copy(yg_hbm.at[pl.ds(0, tm)], y2buf.at[s], sem_y.at[s]).wait()

    @pl.when(j == 0)
    def _():
        first = idx_copy(0, 0)
        first.start()
        first.wait()
        fetch(0, 0)
        idx_copy(1, 1).start()

    idx_copy(j + 1, other).wait()
    fetch(other, other)
    idx_copy(j + 2, slot).start()
    wait_rows(slot)
    rt = route_ref[...]
    y1 = y1buf[slot].reshape(tm, D_MODEL)
    y2 = y2buf[slot].reshape(tm, D_MODEL)
    out_ref[...] = h_ref[...] + gate_ref[...] * (rt[:, 2:3] * y1 + rt[:, 3:4] * y2)

    @pl.when(j == nt - 1)
    def _():
        wait_rows(other)
        idx_copy(j + 2, slot).wait()


def _combine(h, gate, route, table, yg3, *, tm, seq):
    t, d = h.shape
    tpb = seq // tm
    nt = t // tm
    sl, ln = yg3.shape[1:]
    row = lambda i: (i, 0)
    any_spec = pl.BlockSpec(memory_space=pl.ANY)
    return pl.pallas_call(
        functools.partial(_combine_kernel, tm=tm, nt=nt),
        grid=(nt,),
        in_specs=[any_spec,
                  pl.BlockSpec((tm, d), row),
                  pl.BlockSpec((None, 1, d), lambda i: (i // tpb, 0, 0)),
                  pl.BlockSpec((tm, LANES), row),
                  any_spec],
        out_specs=pl.BlockSpec((tm, d), row),
        out_shape=jax.ShapeDtypeStruct((t, d), F32),
        scratch_shapes=[pltpu.VMEM((2, tm, sl, ln), F32), pltpu.VMEM((2, tm, sl, ln), F32),
                        pltpu.SMEM((4 * tm,), jnp.int32),
                        pltpu.SemaphoreType.DMA((2,)), pltpu.SemaphoreType.DMA((2,))],
        compiler_params=pltpu.CompilerParams(dimension_semantics=("arbitrary",),
                                             vmem_limit_bytes=VMEM_LIMIT,
                                             disable_bounds_checks=True),
        name="moe_combine",
    )(table, h, gate, route, yg3)


def _moe(h, g, sh, sc, gate, w1, w3, w2, router, *, seq, tr, tm):
    t = h.shape[0]
    xn3, route, counts = _router(h, g, sh, sc, router, tm=tm, seq=seq)
    tile_expert, n_valid, table, pad_tiles = _route_plan(route, counts, tr, tm)
    xg3 = _dispatch(xn3, table, pad_tiles, tm=tm, tr=tr, n_rows=2 * t + N_EXPERTS * tr)
    yg3 = _experts(xg3, tile_expert, n_valid, w1, w3, w2, tr=tr)
    return _combine(h, gate, route, table, yg3, tm=tm, seq=seq)


def _rope_tables(seq):
    rows = seq // GRID_W
    row, col = jnp.meshgrid(jnp.arange(rows, dtype=F32), jnp.arange(GRID_W, dtype=F32), indexing='ij')
    n_freq = HEAD_DIM // 4
    inv_freq = ROPE_THETA ** (-jnp.arange(n_freq, dtype=F32) / n_freq)
    ang_r = row.reshape(-1, 1) * inv_freq
    ang_c = col.reshape(-1, 1) * inv_freq
    ang = jnp.concatenate([ang_r, ang_r, ang_c, ang_c], axis=-1)
    cos, sin = jnp.cos(ang), jnp.sin(ang)
    first = (jnp.arange(HEAD_DIM) % (2 * n_freq)) < n_freq
    sin_a = jnp.where(first, -sin, 0.0)
    sin_b = jnp.where(first, 0.0, sin)
    rep = LANES // HEAD_DIM
    return tuple(jnp.tile(t, (1, rep)) for t in (cos, sin_a, sin_b))


def kernel(x, c, ctx, c_ctx, w_ada, b_ada, norm1_g, norm2_g, w_in, conv_w, q_norm_g, k_norm_g,
           attn_sink, out_norm_conv_g, out_norm_attn_g, w_out, ffn_w1, ffn_w3, ffn_w2,
           moe_router, moe_w1, moe_w3, moe_w2):
    b, s, d = x.shape
    lc = ctx.shape[1]
    depth = w_ada.shape[0]
    assert d == D_MODEL and s % 512 == 0 and lc % 256 == 0 and b + 1 <= 8

    c8 = jnp.zeros((8, d), F32).at[0:b].set(c).at[b].set(c_ctx)
    mod = _modulation(c8, w_ada, b_ada)

    tables = _rope_tables(s)
    ids = jnp.arange(256)
    gm = (ids[:, None] // HEAD_DIM == ids[None, :] // HEAD_DIM).astype(BF16)
    scale = HEAD_DIM ** -0.5

    h = x.reshape(b * s, d)
    hc = ctx.reshape(b * lc, d)
    for layer in range(depth):
        last = layer == depth - 1
        m = mod[layer]
        lat = [m[0:b, k * d:(k + 1) * d].reshape(b, 1, d) for k in range(6)]
        cx = [jnp.broadcast_to(m[b:b + 1, k * d:(k + 1) * d].reshape(1, 1, d), (b, 1, d)) for k in range(6)]
        w_in_b = w_in[layer].astype(BF16)
        w_out_b = w_out[layer].astype(BF16)
        g1n = norm1_g[layer].reshape(1, d)
        g2n = norm2_g[layer].reshape(1, d)
        gq = (jnp.tile(q_norm_g[layer], N_HEADS) * scale).reshape(1, D_ATTN)
        gk = jnp.tile(k_norm_g[layer], N_KV).reshape(1, LANES)
        gc = out_norm_conv_g[layer].reshape(1, D_CONV)
        ga = out_norm_attn_g[layer].reshape(1, D_ATTN)
        sink = attn_sink[layer]
        cw = conv_w[layer]

        cu, q, kv = _inproj(h, g1n, lat[0], lat[1], w_in_b, gq, gk, gm, tables, tm=512, seq=s)
        cuc, qc, kvc = _inproj(hc, g1n, cx[0], cx[1], w_in_b, gq, gk, gm, None, tm=lc, seq=lc)
        h = _mixer(h, cu, q, kv, kvc, sink, cw, gc, ga, w_out_b, lat[2],
                   tq=512, seq=s, ctx_len=lc, windowed=True)
        if not last:
            hc = _mixer(hc, cuc, qc, None, kvc, sink, cw, gc, ga, w_out_b, cx[2],
                        tq=lc, seq=lc, ctx_len=lc, windowed=False)

        i = layer // 2
        if layer % 2 == 0:
            w1, w3, w2 = ffn_w1[i].astype(BF16), ffn_w3[i].astype(BF16), ffn_w2[i].astype(BF16)
            router = None
        else:
            w1, w3, w2 = moe_w1[i].astype(BF16), moe_w3[i].astype(BF16), moe_w2[i].astype(BF16)
            router = jnp.zeros((d, LANES), F32).at[:, 0:N_EXPERTS].set(moe_router[i])
        if router is None:
            h = _dense_ffn(h, g2n, lat[3], lat[4], lat[5], w1, w3, w2, tm=512, seq=s, parts=2)
        else:
            h = _moe(h, g2n, lat[3], lat[4], lat[5], w1, w3, w2, router, seq=s, tr=512, tm=512)
        if not last:
            hc = _ffn(hc, g2n, cx[3], cx[4], cx[5], w1, w3, w2, router, tm=lc, seq=lc)
    return h.reshape(b, s, d)
```

```python
import functools

import jax
import jax.numpy as jnp
from jax import lax
from jax.experimental import pallas as pl
from jax.experimental.pallas import tpu as pltpu

D_MODEL = 1024
GRID_W = 64
HEAD_DIM = 64
D_CONV = 512
D_ATTN = 512
N_HEADS = 8
N_KV = 2
WINDOW = 128
ROPE_THETA = 10000.0
N_EXPERTS = 8
EPS = 1e-6
KV_OFF = 3 * D_CONV + D_ATTN
D_IN = KV_OFF + 2 * N_KV * HEAD_DIM
LANES = 128
VMEM_LIMIT = 48 * 1024 * 1024

F32 = jnp.float32
BF16 = jnp.bfloat16
NEG_INF = float("-inf")


def _params(n_axes):
    return pltpu.CompilerParams(dimension_semantics=("arbitrary",) * n_axes,
                                vmem_limit_bytes=VMEM_LIMIT)


def _sigmoid(x):
    return 1.0 / (1.0 + jnp.exp(-x))


def _mod_kernel(c_ref, w_ref, b_ref, o_ref):
    c = c_ref[...]
    s = c * _sigmoid(c)
    o_ref[...] = jnp.dot(s, w_ref[...], preferred_element_type=F32) + b_ref[...]


def _modulation(c8, w_ada, b_ada):
    depth, d, n = w_ada.shape
    tn = 1536
    return pl.pallas_call(
        _mod_kernel,
        grid=(depth, n // tn),
        in_specs=[pl.BlockSpec((8, d), lambda l, j: (0, 0)),
                  pl.BlockSpec((None, d, tn), lambda l, j: (l, 0, j)),
                  pl.BlockSpec((None, 1, tn), lambda l, j: (l, 0, j))],
        out_specs=pl.BlockSpec((None, 8, tn), lambda l, j: (l, 0, j)),
        out_shape=jax.ShapeDtypeStruct((depth, 8, n), F32),
        compiler_params=_params(2),
        name="adaln_mod",
    )(c8, w_ada, b_ada.reshape(depth, 1, n))


def _norm_mod(x, g, sh, sc):
    ms = jnp.mean(x * x, axis=-1, keepdims=True)
    return (x * lax.rsqrt(ms + EPS) * g) * (1.0 + sc) + sh


def _inproj_kernel(*refs, rope, parts):
    if rope:
        (h_ref, g_ref, sh_ref, sc_ref, w_ref, gq_ref, gk_ref, gm_ref,
         cos_ref, sa_ref, sb_ref, cu_ref, q_ref, kv_ref) = refs
    else:
        (h_ref, g_ref, sh_ref, sc_ref, w_ref, gq_ref, gk_ref, gm_ref,
         cu_ref, q_ref, kv_ref) = refs
    gm = gm_ref[...]
    gq = gq_ref[...]
    rows = h_ref.shape[0] // parts

    def head_norm(t, gain):
        w = t.shape[1]
        ss = jnp.dot((t * t).astype(BF16), gm[0:w, 0:w], preferred_element_type=F32)
        return t * lax.rsqrt(ss * (1.0 / HEAD_DIM) + EPS) * gain

    for p in range(parts):
        sl = slice(p * rows, (p + 1) * rows)

        def rot(t):
            if not rope:
                return t
            return (t * cos_ref[sl, :] + pltpu.roll(t, LANES - 16, 1) * sa_ref[sl, :]
                    + pltpu.roll(t, 16, 1) * sb_ref[sl, :])

        xn = _norm_mod(h_ref[sl, :], g_ref[...], sh_ref[...], sc_ref[...])
        y = jnp.dot(xn.astype(BF16), w_ref[...], preferred_element_type=F32)
        cu_ref[sl, 0:D_CONV] = y[:, 0:D_CONV].astype(BF16)
        cu_ref[sl, D_CONV:2 * D_CONV] = (y[:, D_CONV:2 * D_CONV] * y[:, 2 * D_CONV:3 * D_CONV]).astype(BF16)
        for j in range(D_ATTN // 256):
            lo = 3 * D_CONV + 256 * j
            qn = head_norm(y[:, lo:lo + 256], gq[:, 256 * j:256 * j + 256])
            for c in range(2):
                q_ref[sl, 256 * j + LANES * c:256 * j + LANES * (c + 1)] = rot(
                    qn[:, LANES * c:LANES * (c + 1)]).astype(BF16)
        k = rot(head_norm(y[:, KV_OFF:KV_OFF + LANES], gk_ref[...]))
        v = y[:, KV_OFF + LANES:KV_OFF + 2 * LANES]
        lo_half = lax.broadcasted_iota(jnp.int32, k.shape, 1) < HEAD_DIM
        for c, t in enumerate((k, v)):
            sw = pltpu.roll(t, HEAD_DIM, 1)
            kv_ref[sl, 2 * c * LANES:(2 * c + 1) * LANES] = jnp.where(lo_half, t, sw).astype(BF16)
            kv_ref[sl, (2 * c + 1) * LANES:(2 * c + 2) * LANES] = jnp.where(lo_half, sw, t).astype(BF16)


def _inproj(h, g, sh, sc, w, gq, gk, gm, tables, *, tm, seq):
    t, d = h.shape
    tpb = seq // tm
    rope = tables is not None
    row = lambda i: (i, 0)
    fix = lambda i: (0, 0)
    mod = lambda i: (i // tpb, 0, 0)
    in_specs = [pl.BlockSpec((tm, d), row),
                pl.BlockSpec((1, d), fix),
                pl.BlockSpec((None, 1, d), mod),
                pl.BlockSpec((None, 1, d), mod),
                pl.BlockSpec((d, D_IN), fix),
                pl.BlockSpec((1, D_ATTN), fix),
                pl.BlockSpec((1, LANES), fix),
                pl.BlockSpec((256, 256), fix)]
    args = [h, g, sh, sc, w, gq, gk, gm]
    if rope:
        in_specs += [pl.BlockSpec((tm, LANES), lambda i: (i % tpb, 0))] * 3
        args += list(tables)
    return pl.pallas_call(
        functools.partial(_inproj_kernel, rope=rope, parts=2),
        grid=(t // tm,),
        in_specs=in_specs,
        out_specs=[pl.BlockSpec((tm, 2 * D_CONV), row),
                   pl.BlockSpec((tm, D_ATTN), row),
                   pl.BlockSpec((tm, 4 * LANES), row)],
        out_shape=[jax.ShapeDtypeStruct((t, 2 * D_CONV), BF16),
                   jax.ShapeDtypeStruct((t, D_ATTN), BF16),
                   jax.ShapeDtypeStruct((t, 4 * LANES), BF16)],
        compiler_params=_params(1),
        name="inproj_rope" if rope else "inproj_ctx",
    )(*args)


_NT = (((1,), (1,)), ((), ()))


def _mixer_kernel(*refs, tq, windowed):
    if windowed:
        (sink_ref, h_ref, cu_ref, cup_ref, cun_ref, q_ref, kvp_ref, kv_ref, kvn_ref, kvc_ref,
         cw_ref, gc_ref, ga_ref, wo_ref, g1_ref, out_ref, kw_ref, ya_ref) = refs
    else:
        (sink_ref, h_ref, cu_ref, q_ref, kvc_ref,
         cw_ref, gc_ref, ga_ref, wo_ref, g1_ref, out_ref, ya_ref) = refs
    i = pl.program_id(1)
    nt = pl.num_programs(1)
    nsub = tq // WINDOW

    cu = cu_ref[...]
    bg = cu[:, 0:D_CONV].astype(F32)
    u = cu[:, D_CONV:2 * D_CONV].astype(F32)
    rows = lax.broadcasted_iota(jnp.int32, (tq, 1), 0)
    if windowed:
        up_row = cup_ref[:, D_CONV:2 * D_CONV].astype(F32)[15:16, :]
        un_row = cun_ref[:, D_CONV:2 * D_CONV].astype(F32)[0:1, :]
        up_row = jnp.where(i > 0, up_row, 0.0)
        un_row = jnp.where(i < nt - 1, un_row, 0.0)
    else:
        up_row = jnp.zeros((1, D_CONV), F32)
        un_row = jnp.zeros((1, D_CONV), F32)
    u_prev = jnp.where(rows == 0, up_row, pltpu.roll(u, 1, 0))
    u_next = jnp.where(rows == tq - 1, un_row, pltpu.roll(u, tq - 1, 0))
    cw = cw_ref[...]
    yc = bg * (cw[0:1, :] * u_prev + cw[1:2, :] * u + cw[2:3, :] * u_next)
    yc = yc * lax.rsqrt(jnp.mean(yc * yc, axis=-1, keepdims=True) + EPS) * gc_ref[...]

    if windowed:
        kw_ref[0:WINDOW, :] = kvp_ref[...]
        kw_ref[WINDOW:WINDOW + tq, :] = kv_ref[...]
        kw_ref[WINDOW + tq:2 * WINDOW + tq, :] = kvn_ref[...]
    lane_lo = lax.broadcasted_iota(jnp.int32, (WINDOW, LANES), 1) < HEAD_DIM
    kvc = kvc_ref[...]
    gqa = N_HEADS // N_KV
    head_of_row = lax.broadcasted_iota(jnp.int32, (gqa * WINDOW, 1), 0) // WINDOW
    if windowed:
        col = lax.broadcasted_iota(jnp.int32, (WINDOW, 3 * WINDOW), 1)
        rw = lax.broadcasted_iota(jnp.int32, (WINDOW, 3 * WINDOW), 0)
        band = (col >= rw) & (col <= rw + 2 * WINDOW)

    def sub_block(s):
        r0 = s * WINDOW
        if windowed:
            kwin = kw_ref[pl.ds(r0, 3 * WINDOW), :]
            valid = band
            if s == 0:
                valid = valid & (col >= jnp.where(i == 0, WINDOW, 0))
            if s == nsub - 1:
                valid = valid & (col < jnp.where(i == nt - 1, 2 * WINDOW, 3 * WINDOW))
        for grp in range(N_KV):
            rows, sk = [], jnp.zeros((gqa * WINDOW, 1), F32)
            for hh in range(gqa):
                head = gqa * grp + hh
                qp = q_ref[pl.ds(r0, WINDOW), LANES * (head // 2):LANES * (head // 2 + 1)]
                zero = jnp.zeros_like(qp)
                rows.append(jnp.where(lane_lo, zero, qp) if head % 2 else jnp.where(lane_lo, qp, zero))
                sk = jnp.where(head_of_row == hh, sink_ref[head], sk)
            q4 = jnp.concatenate(rows, axis=0)
            s_c = lax.dot_general(q4, kvc[:, LANES * grp:LANES * (grp + 1)], _NT, preferred_element_type=F32)
            m = jnp.maximum(jnp.max(s_c, axis=-1, keepdims=True), sk)
            if windowed:
                s_w = lax.dot_general(q4, kwin[:, LANES * grp:LANES * (grp + 1)], _NT,
                                      preferred_element_type=F32)
                s_w = jnp.where(valid[None], s_w.reshape(gqa, WINDOW, 3 * WINDOW), NEG_INF)
                s_w = s_w.reshape(gqa * WINDOW, 3 * WINDOW)
                m = jnp.maximum(m, jnp.max(s_w, axis=-1, keepdims=True))
            p_c = jnp.exp(s_c - m)
            den = jnp.exp(sk - m) + jnp.sum(p_c, axis=-1, keepdims=True)
            o = jnp.dot(p_c.astype(BF16), kvc[:, LANES * (2 + grp):LANES * (3 + grp)],
                        preferred_element_type=F32)
            if windowed:
                p_w = jnp.exp(s_w - m)
                den = den + jnp.sum(p_w, axis=-1, keepdims=True)
                o = o + jnp.dot(p_w.astype(BF16), kwin[:, LANES * (2 + grp):LANES * (3 + grp)],
                                preferred_element_type=F32)
            o = o / den
            for pr in range(gqa // 2):
                pair = (gqa // 2) * grp + pr
                even = o[2 * pr * WINDOW:(2 * pr + 1) * WINDOW]
                odd = o[(2 * pr + 1) * WINDOW:(2 * pr + 2) * WINDOW]
                ya_ref[pl.ds(r0, WINDOW), LANES * pair:LANES * (pair + 1)] = jnp.where(lane_lo, even, odd)

    part = 2 * WINDOW
    ycb = yc.astype(BF16)
    for p in range(tq // part):
        sub_block(2 * p)
        sub_block(2 * p + 1)
        sl = slice(p * part, (p + 1) * part)
        ya = ya_ref[sl, :]
        ya = ya * lax.rsqrt(jnp.mean(ya * ya, axis=-1, keepdims=True) + EPS) * ga_ref[...]
        y = (jnp.dot(ycb[sl], wo_ref[0:D_CONV, :], preferred_element_type=F32)
             + jnp.dot(ya.astype(BF16), wo_ref[D_CONV:2 * D_CONV, :], preferred_element_type=F32))
        out_ref[sl, :] = h_ref[sl, :] + g1_ref[...] * y


def _mixer(h, cu, q, kv, kvc, sink, cw, gc, ga, wo, g1, *, tq, seq, ctx_len, windowed):
    t, d = h.shape
    nt = seq // tq
    nb = t // seq
    row = lambda b, i: (b * nt + i, 0)
    fix = lambda b, i: (0, 0)
    smem = pl.BlockSpec(memory_space=pltpu.SMEM)
    tail = [pl.BlockSpec((3, D_CONV), fix),
            pl.BlockSpec((1, D_CONV), fix),
            pl.BlockSpec((1, D_ATTN), fix),
            pl.BlockSpec((d, d), fix),
            pl.BlockSpec((None, 1, d), lambda b, i: (b, 0, 0))]
    ctx_spec = pl.BlockSpec((ctx_len, 4 * LANES), lambda b, i: (b, 0))
    if windowed:
        r16 = tq // 16
        n16 = t // 16
        rw = tq // WINDOW
        nw = t // WINDOW
        in_specs = [smem,
                    pl.BlockSpec((tq, d), row),
                    pl.BlockSpec((tq, 2 * D_CONV), row),
                    pl.BlockSpec((16, 2 * D_CONV), lambda b, i: (jnp.maximum((b * nt + i) * r16 - 1, 0), 0)),
                    pl.BlockSpec((16, 2 * D_CONV), lambda b, i: (jnp.minimum((b * nt + i + 1) * r16, n16 - 1), 0)),
                    pl.BlockSpec((tq, D_ATTN), row),
                    pl.BlockSpec((WINDOW, 4 * LANES), lambda b, i: (jnp.maximum((b * nt + i) * rw - 1, 0), 0)),
                    pl.BlockSpec((tq, 4 * LANES), row),
                    pl.BlockSpec((WINDOW, 4 * LANES), lambda b, i: (jnp.minimum((b * nt + i + 1) * rw, nw - 1), 0)),
                    ctx_spec] + tail
        args = [sink, h, cu, cu, cu, q, kv, kv, kv, kvc, cw, gc, ga, wo, g1]
        scratch = [pltpu.VMEM((tq + 2 * WINDOW, 4 * LANES), BF16), pltpu.VMEM((tq, D_ATTN), F32)]
    else:
        in_specs = [smem,
                    pl.BlockSpec((tq, d), row),
                    pl.BlockSpec((tq, 2 * D_CONV), row),
                    pl.BlockSpec((tq, D_ATTN), row),
                    ctx_spec] + tail
        args = [sink, h, cu, q, kvc, cw, gc, ga, wo, g1]
        scratch = [pltpu.VMEM((tq, D_ATTN), F32)]
    return pl.pallas_call(
        functools.partial(_mixer_kernel, tq=tq, windowed=windowed),
        grid=(nb, nt),
        in_specs=in_specs,
        out_specs=pl.BlockSpec((tq, d), row),
        out_shape=jax.ShapeDtypeStruct((t, d), F32),
        scratch_shapes=scratch,
        compiler_params=_params(2),
        name="mixer_win" if windowed else "mixer_ctx",
    )(*args)


def _ffn_kernel(*refs, moe):
    if moe:
        (h_ref, g_ref, sh_ref, sc_ref, gate_ref, r_ref, w1_ref, w3_ref, w2_ref,
         out_ref, xn_ref, acc_ref, comb_ref) = refs
    else:
        (h_ref, g_ref, sh_ref, sc_ref, gate_ref, w1_ref, w3_ref, w2_ref,
         out_ref, xn_ref, acc_ref) = refs
    e = pl.program_id(1)
    tm = h_ref.shape[0]

    @pl.when(e == 0)
    def _():
        xn = _norm_mod(h_ref[...], g_ref[...], sh_ref[...], sc_ref[...])
        xn_ref[...] = xn.astype(BF16)
        acc_ref[...] = jnp.zeros_like(acc_ref)
        if moe:
            i1, i2, g1, g2 = _top2(jnp.dot(xn, r_ref[...], preferred_element_type=F32))
            lane = lax.broadcasted_iota(jnp.int32, (tm, LANES), 1)
            comb_ref[...] = jnp.where(lane == i1, g1, 0.0) + jnp.where(lane == i2, g2, 0.0)

    xb = xn_ref[...]
    h1 = jnp.dot(xb, w1_ref[...], preferred_element_type=F32)
    h3 = jnp.dot(xb, w3_ref[...], preferred_element_type=F32)
    a = h1 * _sigmoid(h1) * h3
    if moe:
        lane = lax.broadcasted_iota(jnp.int32, (tm, LANES), 1)
        a = a * jnp.sum(jnp.where(lane == e, comb_ref[...], 0.0), axis=-1, keepdims=True)
    acc_ref[...] += jnp.dot(a.astype(BF16), w2_ref[...], preferred_element_type=F32)

    @pl.when(e == pl.num_programs(1) - 1)
    def _():
        out_ref[...] = h_ref[...] + gate_ref[...] * acc_ref[...]


def _ffn(h, g, sh, sc, gate, w1, w3, w2, router, *, tm, seq):
    t, d = h.shape
    tpb = seq // tm
    moe = router is not None
    row = lambda i, e: (i, 0)
    fix = lambda i, e: (0, 0)
    mod = lambda i, e: (i // tpb, 0, 0)
    in_specs = [pl.BlockSpec((tm, d), row),
                pl.BlockSpec((1, d), fix),
                pl.BlockSpec((None, 1, d), mod),
                pl.BlockSpec((None, 1, d), mod),
                pl.BlockSpec((None, 1, d), mod)]
    args = [h, g, sh, sc, gate]
    scratch = [pltpu.VMEM((tm, d), BF16), pltpu.VMEM((tm, d), F32)]
    if moe:
        ne, _, fe = w1.shape
        in_specs += [pl.BlockSpec((d, LANES), fix),
                     pl.BlockSpec((None, d, fe), lambda i, e: (e, 0, 0)),
                     pl.BlockSpec((None, d, fe), lambda i, e: (e, 0, 0)),
                     pl.BlockSpec((None, fe, d), lambda i, e: (e, 0, 0))]
        args += [router, w1, w3, w2]
        scratch += [pltpu.VMEM((tm, LANES), F32)]
    else:
        ne = 2
        fe = w1.shape[1] // ne
        in_specs += [pl.BlockSpec((d, fe), lambda i, e: (0, e)),
                     pl.BlockSpec((d, fe), lambda i, e: (0, e)),
                     pl.BlockSpec((fe, d), lambda i, e: (e, 0))]
        args += [w1, w3, w2]
    return pl.pallas_call(
        functools.partial(_ffn_kernel, moe=moe),
        grid=(t // tm, ne),
        in_specs=in_specs,
        out_specs=pl.BlockSpec((tm, d), row),
        out_shape=jax.ShapeDtypeStruct((t, d), F32),
        scratch_shapes=scratch,
        compiler_params=_params(2),
        name="ffn_moe" if moe else "ffn_dense",
    )(*args)


def _dense_ffn_kernel(h_ref, g_ref, sh_ref, sc_ref, gate_ref, w1_ref, w3_ref, w2_ref, out_ref, *, parts):
    rows = h_ref.shape[0] // parts
    for p in range(parts):
        sl = slice(p * rows, (p + 1) * rows)
        hp = h_ref[sl, :]
        xb = _norm_mod(hp, g_ref[...], sh_ref[...], sc_ref[...]).astype(BF16)
        h1 = jnp.dot(xb, w1_ref[...], preferred_element_type=F32)
        h3 = jnp.dot(xb, w3_ref[...], preferred_element_type=F32)
        a = (h1 * _sigmoid(h1) * h3).astype(BF16)
        y = jnp.dot(a, w2_ref[...], preferred_element_type=F32)
        out_ref[sl, :] = hp + gate_ref[...] * y


def _dense_ffn(h, g, sh, sc, gate, w1, w3, w2, *, tm, seq, parts):
    t, d = h.shape
    f = w1.shape[1]
    tpb = seq // tm
    row = lambda i: (i, 0)
    fix = lambda i: (0, 0)
    mod = lambda i: (i // tpb, 0, 0)
    once = pl.Buffered(1)
    return pl.pallas_call(
        functools.partial(_dense_ffn_kernel, parts=parts),
        grid=(t // tm,),
        in_specs=[pl.BlockSpec((tm, d), row),
                  pl.BlockSpec((1, d), fix),
                  pl.BlockSpec((None, 1, d), mod),
                  pl.BlockSpec((None, 1, d), mod),
                  pl.BlockSpec((None, 1, d), mod),
                  pl.BlockSpec((d, f), fix, pipeline_mode=once),
                  pl.BlockSpec((d, f), fix, pipeline_mode=once),
                  pl.BlockSpec((f, d), fix, pipeline_mode=once)],
        out_specs=pl.BlockSpec((tm, d), row),
        out_shape=jax.ShapeDtypeStruct((t, d), F32),
        compiler_params=_params(1),
        name="ffn_dense",
    )(h, g, sh, sc, gate, w1, w3, w2)


def _top2(logits):
    lane = lax.broadcasted_iota(jnp.int32, logits.shape, 1)
    lg = jnp.where(lane < N_EXPERTS, logits, NEG_INF)
    m1 = jnp.max(lg, axis=-1, keepdims=True)
    i1 = jnp.min(jnp.where(lg == m1, lane, LANES), axis=-1, keepdims=True)
    lg2 = jnp.where(lane == i1, NEG_INF, lg)
    m2 = jnp.max(lg2, axis=-1, keepdims=True)
    i2 = jnp.min(jnp.where(lg2 == m2, lane, LANES), axis=-1, keepdims=True)
    e2 = jnp.exp(m2 - m1)
    return i1, i2, 1.0 / (1.0 + e2), e2 / (1.0 + e2)


def _router_kernel(h_ref, g_ref, sh_ref, sc_ref, r_ref, tri_ref, xn_ref, route_ref, cnt_ref, base_ref):
    tm = h_ref.shape[0]

    @pl.when(pl.program_id(0) == 0)
    def _():
        base_ref[...] = jnp.zeros_like(base_ref)

    xn = _norm_mod(h_ref[...], g_ref[...], sh_ref[...], sc_ref[...])
    xn_ref[...] = xn.reshape(xn_ref.shape)
    i1, i2, g1, g2 = _top2(jnp.dot(xn, r_ref[...], preferred_element_type=F32))
    lane = lax.broadcasted_iota(jnp.int32, (tm, LANES), 1)
    hit1 = lane == i1
    hit2 = lane == i2
    chosen = jnp.where(jnp.logical_or(hit1, hit2), 1.0, 0.0)
    before = base_ref[...] + jnp.dot(tri_ref[...], chosen.astype(BF16), preferred_element_type=F32)
    r1 = jnp.sum(jnp.where(hit1, before, 0.0), axis=-1, keepdims=True)
    r2 = jnp.sum(jnp.where(hit2, before, 0.0), axis=-1, keepdims=True)
    base_ref[...] += jnp.sum(chosen, axis=0, keepdims=True)
    cnt_ref[...] = base_ref[...]
    fields = (i1.astype(F32), i2.astype(F32), g1, g2, r1, r2)
    route = jnp.zeros((tm, LANES), F32)
    for k, f in enumerate(fields):
        route = jnp.where(lane == k, f, route)
    route_ref[...] = route


def _router(h, g, sh, sc, router, *, tm, seq):
    t, d = h.shape
    tpb = seq // tm
    row = lambda i: (i, 0)
    fix = lambda i: (0, 0)
    mod = lambda i: (i // tpb, 0, 0)
    ids = jnp.arange(tm)
    tri = (ids[None, :] < ids[:, None]).astype(BF16)
    return pl.pallas_call(
        _router_kernel,
        grid=(t // tm,),
        in_specs=[pl.BlockSpec((tm, d), row), pl.BlockSpec((1, d), fix),
                  pl.BlockSpec((None, 1, d), mod), pl.BlockSpec((None, 1, d), mod),
                  pl.BlockSpec((d, LANES), fix), pl.BlockSpec((tm, tm), fix)],
        out_specs=[pl.BlockSpec((tm, d // LANES, LANES), lambda i: (i, 0, 0)),
                   pl.BlockSpec((tm, LANES), row),
                   pl.BlockSpec((1, LANES), fix)],
        out_shape=[jax.ShapeDtypeStruct((t, d // LANES, LANES), F32),
                   jax.ShapeDtypeStruct((t, LANES), F32),
                   jax.ShapeDtypeStruct((1, LANES), F32)],
        scratch_shapes=[pltpu.VMEM((1, LANES), F32)],
        compiler_params=_params(1),
        name="moe_router",
    )(h, g, sh, sc, router, tri)


def _route_plan(route, counts, tr, tm):
    t = route.shape[0]
    counts = counts[0, 0:N_EXPERTS].astype(jnp.int32)
    tiles = (counts + tr - 1) // tr
    tile_end = jnp.cumsum(tiles)
    tile_start = tile_end - tiles
    experts = jnp.arange(N_EXPERTS, dtype=jnp.int32)

    def position(e, r):
        start = jnp.sum(jnp.where(e[:, None] == experts[None, :], tile_start[None, :] * tr, 0), axis=1)
        return start + r

    pos1 = position(route[:, 0].astype(jnp.int32), route[:, 4].astype(jnp.int32))
    pos2 = position(route[:, 1].astype(jnp.int32), route[:, 5].astype(jnp.int32))
    nt = 2 * t // tr + N_EXPERTS
    tid = jnp.arange(nt, dtype=jnp.int32)
    tile_expert = jnp.minimum(jnp.sum((tid[:, None] >= tile_end[None, :]).astype(jnp.int32), axis=1),
                              N_EXPERTS - 1)
    in_tile = tid - jnp.sum(jnp.where(tile_expert[:, None] == experts[None, :], tile_start[None, :], 0), axis=1)
    own = jnp.sum(jnp.where(tile_expert[:, None] == experts[None, :], counts[None, :], 0), axis=1)
    n_valid = jnp.where(tid < tile_end[-1], jnp.clip(own - in_tile * tr, 0, tr), 0)
    table = jnp.concatenate([pos1.reshape(t // tm, tm), pos2.reshape(t // tm, tm)], axis=1)
    table = jnp.concatenate([table, jnp.zeros((2, 2 * tm), jnp.int32)], axis=0)
    tail = tile_end[-1] + experts
    pad_tiles = jnp.concatenate([jnp.where(tiles > 0, tile_end - 1, -1), jnp.where(tail < nt, tail, -1)])
    return tile_expert, n_valid, table, pad_tiles


def _row_copies(idx_smem, s_idx, tm, make):
    base = s_idx * (2 * tm)

    def body(r, c):
        make(r, idx_smem[base + r], idx_smem[base + tm + r])
        return c

    lax.fori_loop(0, tm, body, 0, unroll=8)


def _dispatch_kernel(zt_ref, idx_hbm, xn_hbm, xg_hbm, idx_smem, zbuf, xbuf,
                     sem_d, sem_i, sem_z, sem_in, *, tm, nt, tr):
    j = pl.program_id(0)
    slot = j % 2
    other = 1 - slot
    cur = j % 3
    nxt = (j + 1) % 3

    def in_copy(tile, s):
        return pltpu.make_async_copy(xn_hbm.at[pl.ds(tile * tm, tm)], xbuf.at[s], sem_in.at[s])

    @pl.when(j == 0)
    def _():
        zbuf[...] = jnp.zeros_like(zbuf)
        for k in range(zt_ref.shape[0]):
            fill = pltpu.make_async_copy(zbuf, xg_hbm.at[pl.ds(jnp.maximum(zt_ref[k], 0) * tr, tr)], sem_z)
            pl.when(zt_ref[k] >= 0)(fill.start)
        for k in range(zt_ref.shape[0]):
            fill = pltpu.make_async_copy(zbuf, xg_hbm.at[pl.ds(0, tr)], sem_z)
            pl.when(zt_ref[k] >= 0)(fill.wait)

    def idx_copy(row, s):
        return pltpu.make_async_copy(idx_hbm.at[row], idx_smem.at[pl.ds(s * 2 * tm, 2 * tm)], sem_i.at[s])

    def wait_rows(s):
        for _ in range(2):
            pltpu.make_async_copy(xbuf.at[s], xg_hbm.at[pl.ds(0, tm)], sem_d.at[s]).wait()

    @pl.when(j == 0)
    def _():
        idx_copy(0, 0).start()
        in_copy(0, 0).start()

    @pl.when(j >= 2)
    def _():
        wait_rows(nxt)

    @pl.when(j + 1 < nt)
    def _():
        in_copy(j + 1, nxt).start()

    idx_copy(j, slot).wait()
    idx_copy(j + 1, other).start()
    in_copy(j, cur).wait()

    def make(r, p1, p2):
        src = xbuf.at[cur, r]
        pltpu.make_async_copy(src, xg_hbm.at[p1], sem_d.at[cur]).start(priority=0)
        pltpu.make_async_copy(src, xg_hbm.at[p2], sem_d.at[cur]).start(priority=1)

    _row_copies(idx_smem, slot, tm, make)

    @pl.when(j == nt - 1)
    def _():
        if nt > 1:
            wait_rows((nt - 2) % 3)
        wait_rows((nt - 1) % 3)
        idx_copy(j + 1, other).wait()


def _dispatch(xn3, table, pad_tiles, *, tm, tr, n_rows):
    t = xn3.shape[0]
    nt = t // tm
    any_spec = pl.BlockSpec(memory_space=pl.ANY)
    grid_spec = pltpu.PrefetchScalarGridSpec(
        num_scalar_prefetch=1,
        grid=(nt,),
        in_specs=[any_spec, any_spec],
        out_specs=any_spec,
        scratch_shapes=[pltpu.SMEM((4 * tm,), jnp.int32),
                        pltpu.VMEM((tr,) + xn3.shape[1:], F32),
                        pltpu.VMEM((3, tm) + xn3.shape[1:], F32),
                        pltpu.SemaphoreType.DMA((3,)), pltpu.SemaphoreType.DMA((2,)),
                        pltpu.SemaphoreType.DMA, pltpu.SemaphoreType.DMA((3,))])
    return pl.pallas_call(
        functools.partial(_dispatch_kernel, tm=tm, nt=nt, tr=tr),
        grid_spec=grid_spec,
        out_shape=jax.ShapeDtypeStruct((n_rows,) + xn3.shape[1:], F32),
        compiler_params=pltpu.CompilerParams(dimension_semantics=("arbitrary",),
                                             vmem_limit_bytes=VMEM_LIMIT,
                                             disable_bounds_checks=True),
        name="moe_dispatch",
    )(pad_tiles, table, xn3)


def _expert_kernel(te_ref, nv_ref, x_ref, w1_ref, w3_ref, w2_ref, y_ref):
    tr = x_ref.shape[0]
    nv = nv_ref[pl.program_id(0)]

    @pl.when(nv > 0)
    def _():
        half = tr // 2
        for p in range(2):
            x = x_ref[p * half:(p + 1) * half].reshape(half, D_MODEL).astype(BF16)
            h1 = jnp.dot(x, w1_ref[...], preferred_element_type=F32)
            h3 = jnp.dot(x, w3_ref[...], preferred_element_type=F32)
            a = (h1 * _sigmoid(h1) * h3).astype(BF16)
            y = jnp.dot(a, w2_ref[...], preferred_element_type=F32)
            y_ref[p * half:(p + 1) * half] = y.reshape((half,) + y_ref.shape[1:])

    @pl.when(nv == 0)
    def _():
        y_ref[...] = jnp.zeros_like(y_ref)


def _experts(xg3, tile_expert, n_valid, w1, w3, w2, *, tr):
    n_rows, sl, ln = xg3.shape
    d = sl * ln
    fe = w1.shape[2]
    rows = lambda j, te, nv: (j, 0, 0)
    wsel = lambda j, te, nv: (te[j], 0, 0)
    grid_spec = pltpu.PrefetchScalarGridSpec(
        num_scalar_prefetch=2,
        grid=(n_rows // tr,),
        in_specs=[pl.BlockSpec((tr, sl, ln), rows),
                  pl.BlockSpec((None, d, fe), wsel),
                  pl.BlockSpec((None, d, fe), wsel),
                  pl.BlockSpec((None, fe, d), wsel)],
        out_specs=pl.BlockSpec((tr, sl, ln), rows))
    return pl.pallas_call(
        _expert_kernel,
        grid_spec=grid_spec,
        out_shape=jax.ShapeDtypeStruct(xg3.shape, F32),
        compiler_params=_params(1),
        name="moe_experts",
    )(tile_expert, n_valid, xg3, w1, w3, w2)


def _combine_kernel(idx_hbm, h_ref, gate_ref, route_ref, yg_hbm, out_ref,
                    y1buf, y2buf, idx_smem, sem_y, sem_i, *, tm, nt):
    j = pl.program_id(0)
    slot = j % 2
    other = 1 - slot

    def idx_copy(row, s):
        return pltpu.make_async_copy(idx_hbm.at[row], idx_smem.at[pl.ds(s * 2 * tm, 2 * tm)], sem_i.at[s])

    def fetch(s_idx, s_buf):
        def make(r, p1, p2):
            pltpu.make_async_copy(yg_hbm.at[p1], y1buf.at[s_buf, r], sem_y.at[s_buf]).start(priority=0)
            pltpu.make_async_copy(yg_hbm.at[p2], y2buf.at[s_buf, r], sem_y.at[s_buf]).start(priority=1)

        _row_copies(idx_smem, s_idx, tm, make)

    def wait_rows(s):
        pltpu.make_async_copy(yg_hbm.at[pl.ds(0, tm)], y1buf.at[s], sem_y.at[s]).wait()
        pltpu.make_async_copy(yg_hbm.at[pl.ds(0, tm)], y2buf.at[s], sem_y.at[s]).wait()

    @pl.when(j == 0)
    def _():
        first = idx_copy(0, 0)
        first.start()
        first.wait()
        fetch(0, 0)
        idx_copy(1, 1).start()

    idx_copy(j + 1, other).wait()
    fetch(other, other)
    idx_copy(j + 2, slot).start()
    wait_rows(slot)
    rt = route_ref[...]
    y1 = y1buf[slot].reshape(tm, D_MODEL)
    y2 = y2buf[slot].reshape(tm, D_MODEL)
    out_ref[...] = h_ref[...] + gate_ref[...] * (rt[:, 2:3] * y1 + rt[:, 3:4] * y2)

    @pl.when(j == nt - 1)
    def _():
        wait_rows(other)
        idx_copy(j + 2, slot).wait()


def _combine(h, gate, route, table, yg3, *, tm, seq):
    t, d = h.shape
    tpb = seq // tm
    nt = t // tm
    sl, ln = yg3.shape[1:]
    row = lambda i: (i, 0)
    any_spec = pl.BlockSpec(memory_space=pl.ANY)
    return pl.pallas_call(
        functools.partial(_combine_kernel, tm=tm, nt=nt),
        grid=(nt,),
        in_specs=[any_spec,
                  pl.BlockSpec((tm, d), row),
                  pl.BlockSpec((None, 1, d), lambda i: (i // tpb, 0, 0)),
                  pl.BlockSpec((tm, LANES), row),
                  any_spec],
        out_specs=pl.BlockSpec((tm, d), row),
        out_shape=jax.ShapeDtypeStruct((t, d), F32),
        scratch_shapes=[pltpu.VMEM((2, tm, sl, ln), F32), pltpu.VMEM((2, tm, sl, ln), F32),
                        pltpu.SMEM((4 * tm,), jnp.int32),
                        pltpu.SemaphoreType.DMA((2,)), pltpu.SemaphoreType.DMA((2,))],
        compiler_params=pltpu.CompilerParams(dimension_semantics=("arbitrary",),
                                             vmem_limit_bytes=VMEM_LIMIT,
                                             disable_bounds_checks=True),
        name="moe_combine",
    )(table, h, gate, route, yg3)


def _moe(h, g, sh, sc, gate, w1, w3, w2, router, *, seq, tr, tm):
    t = h.shape[0]
    xn3, route, counts = _router(h, g, sh, sc, router, tm=tm, seq=seq)
    tile_expert, n_valid, table, pad_tiles = _route_plan(route, counts, tr, tm)
    xg3 = _dispatch(xn3, table, pad_tiles, tm=tm, tr=tr, n_rows=2 * t + N_EXPERTS * tr)
    yg3 = _experts(xg3, tile_expert, n_valid, w1, w3, w2, tr=tr)
    return _combine(h, gate, route, table, yg3, tm=tm, seq=seq)


def _rope_tables(seq):
    rows = seq // GRID_W
    row, col = jnp.meshgrid(jnp.arange(rows, dtype=F32), jnp.arange(GRID_W, dtype=F32), indexing='ij')
    n_freq = HEAD_DIM // 4
    inv_freq = ROPE_THETA ** (-jnp.arange(n_freq, dtype=F32) / n_freq)
    ang_r = row.reshape(-1, 1) * inv_freq
    ang_c = col.reshape(-1, 1) * inv_freq
    ang = jnp.concatenate([ang_r, ang_r, ang_c, ang_c], axis=-1)
    cos, sin = jnp.cos(ang), jnp.sin(ang)
    first = (jnp.arange(HEAD_DIM) % (2 * n_freq)) < n_freq
    sin_a = jnp.where(first, -sin, 0.0)
    sin_b = jnp.where(first, 0.0, sin)
    rep = LANES // HEAD_DIM
    return tuple(jnp.tile(t, (1, rep)) for t in (cos, sin_a, sin_b))


def kernel(x, c, ctx, c_ctx, w_ada, b_ada, norm1_g, norm2_g, w_in, conv_w, q_norm_g, k_norm_g,
           attn_sink, out_norm_conv_g, out_norm_attn_g, w_out, ffn_w1, ffn_w3, ffn_w2,
           moe_router, moe_w1, moe_w3, moe_w2):
    b, s, d = x.shape
    lc = ctx.shape[1]
    depth = w_ada.shape[0]
    assert d == D_MODEL and s % 512 == 0 and lc % 256 == 0 and b + 1 <= 8

    c8 = jnp.zeros((8, d), F32).at[0:b].set(c).at[b].set(c_ctx)
    mod = _modulation(c8, w_ada, b_ada)

    tables = _rope_tables(s)
    ids = jnp.arange(256)
    gm = (ids[:, None] // HEAD_DIM == ids[None, :] // HEAD_DIM).astype(BF16)
    scale = HEAD_DIM ** -0.5

    h = x.reshape(b * s, d)
    hc = ctx.reshape(b * lc, d)
    for layer in range(depth):
        last = layer == depth - 1
        m = mod[layer]
        lat = [m[0:b, k * d:(k + 1) * d].reshape(b, 1, d) for k in range(6)]
        cx = [jnp.broadcast_to(m[b:b + 1, k * d:(k + 1) * d].reshape(1, 1, d), (b, 1, d)) for k in range(6)]
        w_in_b = w_in[layer].astype(BF16)
        w_out_b = w_out[layer].astype(BF16)
        g1n = norm1_g[layer].reshape(1, d)
        g2n = norm2_g[layer].reshape(1, d)
        gq = (jnp.tile(q_norm_g[layer], N_HEADS) * scale).reshape(1, D_ATTN)
        gk = jnp.tile(k_norm_g[layer], N_KV).reshape(1, LANES)
        gc = out_norm_conv_g[layer].reshape(1, D_CONV)
        ga = out_norm_attn_g[layer].reshape(1, D_ATTN)
        sink = attn_sink[layer]
        cw = conv_w[layer]

        cu, q, kv = _inproj(h, g1n, lat[0], lat[1], w_in_b, gq, gk, gm, tables, tm=512, seq=s)
        cuc, qc, kvc = _inproj(hc, g1n, cx[0], cx[1], w_in_b, gq, gk, gm, None, tm=lc, seq=lc)
        h = _mixer(h, cu, q, kv, kvc, sink, cw, gc, ga, w_out_b, lat[2],
                   tq=512, seq=s, ctx_len=lc, windowed=True)
        if not last:
            hc = _mixer(hc, cuc, qc, None, kvc, sink, cw, gc, ga, w_out_b, cx[2],
                        tq=lc, seq=lc, ctx_len=lc, windowed=False)

        i = layer // 2
        if layer % 2 == 0:
            w1, w3, w2 = ffn_w1[i].astype(BF16), ffn_w3[i].astype(BF16), ffn_w2[i].astype(BF16)
            router = None
        else:
            w1, w3, w2 = moe_w1[i].astype(BF16), moe_w3[i].astype(BF16), moe_w2[i].astype(BF16)
            router = jnp.zeros((d, LANES), F32).at[:, 0:N_EXPERTS].set(moe_router[i])
        if router is None:
            h = _dense_ffn(h, g2n, lat[3], lat[4], lat[5], w1, w3, w2, tm=512, seq=s, parts=2)
        else:
            h = _moe(h, g2n, lat[3], lat[4], lat[5], w1, w3, w2, router, seq=s, tr=512, tm=512)
        if not last:
            hc = _ffn(hc, g2n, cx[3], cx[4], cx[5], w1, w3, w2, router, tm=lc, seq=lc)
    return h.reshape(b, s, d)
```

```python
import functools

import jax
import jax.numpy as jnp
from jax import lax
from jax.experimental import pallas as pl
from jax.experimental.pallas import tpu as pltpu

D_MODEL = 1024
GRID_W = 64
HEAD_DIM = 64
D_CONV = 512
D_ATTN = 512
N_HEADS = 8
N_KV = 2
WINDOW = 128
ROPE_THETA = 10000.0
N_EXPERTS = 8
EPS = 1e-6
KV_OFF = 3 * D_CONV + D_ATTN
D_IN = KV_OFF + 2 * N_KV * HEAD_DIM
LANES = 128
VMEM_LIMIT = 48 * 1024 * 1024

F32 = jnp.float32
BF16 = jnp.bfloat16
NEG_INF = float("-inf")


def _params(n_axes):
    return pltpu.CompilerParams(dimension_semantics=("arbitrary",) * n_axes,
                                vmem_limit_bytes=VMEM_LIMIT)


def _sigmoid(x):
    return 1.0 / (1.0 + jnp.exp(-x))


def _mod_kernel(c_ref, w_ref, b_ref, o_ref):
    c = c_ref[...]
    s = c * _sigmoid(c)
    o_ref[...] = jnp.dot(s, w_ref[...], preferred_element_type=F32) + b_ref[...]


def _modulation(c8, w_ada, b_ada):
    depth, d, n = w_ada.shape
    tn = 1536
    return pl.pallas_call(
        _mod_kernel,
        grid=(depth, n // tn),
        in_specs=[pl.BlockSpec((8, d), lambda l, j: (0, 0)),
                  pl.BlockSpec((None, d, tn), lambda l, j: (l, 0, j)),
                  pl.BlockSpec((None, 1, tn), lambda l, j: (l, 0, j))],
        out_specs=pl.BlockSpec((None, 8, tn), lambda l, j: (l, 0, j)),
        out_shape=jax.ShapeDtypeStruct((depth, 8, n), F32),
        compiler_params=_params(2),
        name="adaln_mod",
    )(c8, w_ada, b_ada.reshape(depth, 1, n))


def _norm_mod(x, g, sh, sc):
    ms = jnp.mean(x * x, axis=-1, keepdims=True)
    return (x * lax.rsqrt(ms + EPS) * g) * (1.0 + sc) + sh


def _inproj_kernel(*refs, rope, parts):
    if rope:
        (h_ref, g_ref, sh_ref, sc_ref, w_ref, gq_ref, gk_ref, gm_ref,
         cos_ref, sa_ref, sb_ref, cu_ref, q_ref, kv_ref) = refs
    else:
        (h_ref, g_ref, sh_ref, sc_ref, w_ref, gq_ref, gk_ref, gm_ref,
         cu_ref, q_ref, kv_ref) = refs
    gm = gm_ref[...]
    gq = gq_ref[...]
    rows = h_ref.shape[0] // parts

    def head_norm(t, gain):
        w = t.shape[1]
        ss = jnp.dot((t * t).astype(BF16), gm[0:w, 0:w], preferred_element_type=F32)
        return t * lax.rsqrt(ss * (1.0 / HEAD_DIM) + EPS) * gain

    for p in range(parts):
        sl = slice(p * rows, (p + 1) * rows)

        def rot(t):
            if not rope:
                return t
            return (t * cos_ref[sl, :] + pltpu.roll(t, LANES - 16, 1) * sa_ref[sl, :]
                    + pltpu.roll(t, 16, 1) * sb_ref[sl, :])

        xn = _norm_mod(h_ref[sl, :], g_ref[...], sh_ref[...], sc_ref[...])
        y = jnp.dot(xn.astype(BF16), w_ref[...], preferred_element_type=F32)
        cu_ref[sl, 0:D_CONV] = y[:, 0:D_CONV].astype(BF16)
        cu_ref[sl, D_CONV:2 * D_CONV] = (y[:, D_CONV:2 * D_CONV] * y[:, 2 * D_CONV:3 * D_CONV]).astype(BF16)
        for j in range(D_ATTN // 256):
            lo = 3 * D_CONV + 256 * j
            qn = head_norm(y[:, lo:lo + 256], gq[:, 256 * j:256 * j + 256])
            for c in range(2):
                q_ref[sl, 256 * j + LANES * c:256 * j + LANES * (c + 1)] = rot(
                    qn[:, LANES * c:LANES * (c + 1)]).astype(BF16)
        k = rot(head_norm(y[:, KV_OFF:KV_OFF + LANES], gk_ref[...]))
        v = y[:, KV_OFF + LANES:KV_OFF + 2 * LANES]
        lo_half = lax.broadcasted_iota(jnp.int32, k.shape, 1) < HEAD_DIM
        for c, t in enumerate((k, v)):
            sw = pltpu.roll(t, HEAD_DIM, 1)
            kv_ref[sl, 2 * c * LANES:(2 * c + 1) * LANES] = jnp.where(lo_half, t, sw).astype(BF16)
            kv_ref[sl, (2 * c + 1) * LANES:(2 * c + 2) * LANES] = jnp.where(lo_half, sw, t).astype(BF16)


def _inproj(h, g, sh, sc, w, gq, gk, gm, tables, *, tm, seq):
    t, d = h.shape
    tpb = seq // tm
    rope = tables is not None
    row = lambda i: (i, 0)
    fix = lambda i: (0, 0)
    mod = lambda i: (i // tpb, 0, 0)
    in_specs = [pl.BlockSpec((tm, d), row),
                pl.BlockSpec((1, d), fix),
                pl.BlockSpec((None, 1, d), mod),
                pl.BlockSpec((None, 1, d), mod),
                pl.BlockSpec((d, D_IN), fix),
                pl.BlockSpec((1, D_ATTN), fix),
                pl.BlockSpec((1, LANES), fix),
                pl.BlockSpec((256, 256), fix)]
    args = [h, g, sh, sc, w, gq, gk, gm]
    if rope:
        in_specs += [pl.BlockSpec((tm, LANES), lambda i: (i % tpb, 0))] * 3
        args += list(tables)
    return pl.pallas_call(
        functools.partial(_inproj_kernel, rope=rope, parts=2),
        grid=(t // tm,),
        in_specs=in_specs,
        out_specs=[pl.BlockSpec((tm, 2 * D_CONV), row),
                   pl.BlockSpec((tm, D_ATTN), row),
                   pl.BlockSpec((tm, 4 * LANES), row)],
        out_shape=[jax.ShapeDtypeStruct((t, 2 * D_CONV), BF16),
                   jax.ShapeDtypeStruct((t, D_ATTN), BF16),
                   jax.ShapeDtypeStruct((t, 4 * LANES), BF16)],
        compiler_params=_params(1),
        name="inproj_rope" if rope else "inproj_ctx",
    )(*args)


_NT = (((1,), (1,)), ((), ()))


def _mixer_kernel(*refs, tq, windowed):
    if windowed:
        (sink_ref, h_ref, cu_ref, cup_ref, cun_ref, q_ref, kvp_ref, kv_ref, kvn_ref, kvc_ref,
         cw_ref, gc_ref, ga_ref, wo_ref, g1_ref, out_ref, kw_ref, ya_ref) = refs
    else:
        (sink_ref, h_ref, cu_ref, q_ref, kvc_ref,
         cw_ref, gc_ref, ga_ref, wo_ref, g1_ref, out_ref, ya_ref) = refs
    i = pl.program_id(1)
    nt = pl.num_programs(1)
    nsub = tq // WINDOW

    cu = cu_ref[...]
    bg = cu[:, 0:D_CONV].astype(F32)
    u = cu[:, D_CONV:2 * D_CONV].astype(F32)
    rows = lax.broadcasted_iota(jnp.int32, (tq, 1), 0)
    if windowed:
        up_row = cup_ref[:, D_CONV:2 * D_CONV].astype(F32)[15:16, :]
        un_row = cun_ref[:, D_CONV:2 * D_CONV].astype(F32)[0:1, :]
        up_row = jnp.where(i > 0, up_row, 0.0)
        un_row = jnp.where(i < nt - 1, un_row, 0.0)
    else:
        up_row = jnp.zeros((1, D_CONV), F32)
        un_row = jnp.zeros((1, D_CONV), F32)
    u_prev = jnp.where(rows == 0, up_row, pltpu.roll(u, 1, 0))
    u_next = jnp.where(rows == tq - 1, un_row, pltpu.roll(u, tq - 1, 0))
    cw = cw_ref[...]
    yc = bg * (cw[0:1, :] * u_prev + cw[1:2, :] * u + cw[2:3, :] * u_next)
    yc = yc * lax.rsqrt(jnp.mean(yc * yc, axis=-1, keepdims=True) + EPS) * gc_ref[...]

    if windowed:
        kw_ref[0:WINDOW, :] = kvp_ref[...]
        kw_ref[WINDOW:WINDOW + tq, :] = kv_ref[...]
        kw_ref[WINDOW + tq:2 * WINDOW + tq, :] = kvn_ref[...]
    lane_lo = lax.broadcasted_iota(jnp.int32, (WINDOW, LANES), 1) < HEAD_DIM
    kvc = kvc_ref[...]
    gqa = N_HEADS // N_KV
    head_of_row = lax.broadcasted_iota(jnp.int32, (gqa * WINDOW, 1), 0) // WINDOW
    if windowed:
        col = lax.broadcasted_iota(jnp.int32, (WINDOW, 3 * WINDOW), 1)
        rw = lax.broadcasted_iota(jnp.int32, (WINDOW, 3 * WINDOW), 0)
        band = (col >= rw) & (col <= rw + 2 * WINDOW)

    def sub_block(s):
        r0 = s * WINDOW
        if windowed:
            kwin = kw_ref[pl.ds(r0, 3 * WINDOW), :]
            valid = band
            if s == 0:
                valid = valid & (col >= jnp.where(i == 0, WINDOW, 0))
            if s == nsub - 1:
                valid = valid & (col < jnp.where(i == nt - 1, 2 * WINDOW, 3 * WINDOW))
        for grp in range(N_KV):
            rows, sk = [], jnp.zeros((gqa * WINDOW, 1), F32)
            for hh in range(gqa):
                head = gqa * grp + hh
                qp = q_ref[pl.ds(r0, WINDOW), LANES * (head // 2):LANES * (head // 2 + 1)]
                zero = jnp.zeros_like(qp)
                rows.append(jnp.where(lane_lo, zero, qp) if head % 2 else jnp.where(lane_lo, qp, zero))
                sk = jnp.where(head_of_row == hh, sink_ref[head], sk)
            q4 = jnp.concatenate(rows, axis=0)
            s_c = lax.dot_general(q4, kvc[:, LANES * grp:LANES * (grp + 1)], _NT, preferred_element_type=F32)
            m = jnp.maximum(jnp.max(s_c, axis=-1, keepdims=True), sk)
            if windowed:
                s_w = lax.dot_general(q4, kwin[:, LANES * grp:LANES * (grp + 1)], _NT,
                                      preferred_element_type=F32)
                s_w = jnp.where(valid[None], s_w.reshape(gqa, WINDOW, 3 * WINDOW), NEG_INF)
                s_w = s_w.reshape(gqa * WINDOW, 3 * WINDOW)
                m = jnp.maximum(m, jnp.max(s_w, axis=-1, keepdims=True))
            p_c = jnp.exp(s_c - m)
            den = jnp.exp(sk - m) + jnp.sum(p_c, axis=-1, keepdims=True)
            o = jnp.dot(p_c.astype(BF16), kvc[:, LANES * (2 + grp):LANES * (3 + grp)],
                        preferred_element_type=F32)
            if windowed:
                p_w = jnp.exp(s_w - m)
                den = den + jnp.sum(p_w, axis=-1, keepdims=True)
                o = o + jnp.dot(p_w.astype(BF16), kwin[:, LANES * (2 + grp):LANES * (3 + grp)],
                                preferred_element_type=F32)
            o = o / den
            for pr in range(gqa // 2):
                pair = (gqa // 2) * grp + pr
                even = o[2 * pr * WINDOW:(2 * pr + 1) * WINDOW]
                odd = o[(2 * pr + 1) * WINDOW:(2 * pr + 2) * WINDOW]
                ya_ref[pl.ds(r0, WINDOW), LANES * pair:LANES * (pair + 1)] = jnp.where(lane_lo, even, odd)

    part = 2 * WINDOW
    ycb = yc.astype(BF16)
    for p in range(tq // part):
        sub_block(2 * p)
        sub_block(2 * p + 1)
        sl = slice(p * part, (p + 1) * part)
        ya = ya_ref[sl, :]
        ya = ya * lax.rsqrt(jnp.mean(ya * ya, axis=-1, keepdims=True) + EPS) * ga_ref[...]
        y = (jnp.dot(ycb[sl], wo_ref[0:D_CONV, :], preferred_element_type=F32)
             + jnp.dot(ya.astype(BF16), wo_ref[D_CONV:2 * D_CONV, :], preferred_element_type=F32))
        out_ref[sl, :] = h_ref[sl, :] + g1_ref[...] * y


def _mixer(h, cu, q, kv, kvc, sink, cw, gc, ga, wo, g1, *, tq, seq, ctx_len, windowed):
    t, d = h.shape
    nt = seq // tq
    nb = t // seq
    row = lambda b, i: (b * nt + i, 0)
    fix = lambda b, i: (0, 0)
    smem = pl.BlockSpec(memory_space=pltpu.SMEM)
    tail = [pl.BlockSpec((3, D_CONV), fix),
            pl.BlockSpec((1, D_CONV), fix),
            pl.BlockSpec((1, D_ATTN), fix),
            pl.BlockSpec((d, d), fix),
            pl.BlockSpec((None, 1, d), lambda b, i: (b, 0, 0))]
    ctx_spec = pl.BlockSpec((ctx_len, 4 * LANES), lambda b, i: (b, 0))
    if windowed:
        r16 = tq // 16
        n16 = t // 16
        rw = tq // WINDOW
        nw = t // WINDOW
        in_specs = [smem,
                    pl.BlockSpec((tq, d), row),
                    pl.BlockSpec((tq, 2 * D_CONV), row),
                    pl.BlockSpec((16, 2 * D_CONV), lambda b, i: (jnp.maximum((b * nt + i) * r16 - 1, 0), 0)),
                    pl.BlockSpec((16, 2 * D_CONV), lambda b, i: (jnp.minimum((b * nt + i + 1) * r16, n16 - 1), 0)),
                    pl.BlockSpec((tq, D_ATTN), row),
                    pl.BlockSpec((WINDOW, 4 * LANES), lambda b, i: (jnp.maximum((b * nt + i) * rw - 1, 0), 0)),
                    pl.BlockSpec((tq, 4 * LANES), row),
                    pl.BlockSpec((WINDOW, 4 * LANES), lambda b, i: (jnp.minimum((b * nt + i + 1) * rw, nw - 1), 0)),
                    ctx_spec] + tail
        args = [sink, h, cu, cu, cu, q, kv, kv, kv, kvc, cw, gc, ga, wo, g1]
        scratch = [pltpu.VMEM((tq + 2 * WINDOW, 4 * LANES), BF16), pltpu.VMEM((tq, D_ATTN), F32)]
    else:
        in_specs = [smem,
                    pl.BlockSpec((tq, d), row),
                    pl.BlockSpec((tq, 2 * D_CONV), row),
                    pl.BlockSpec((tq, D_ATTN), row),
                    ctx_spec] + tail
        args = [sink, h, cu, q, kvc, cw, gc, ga, wo, g1]
        scratch = [pltpu.VMEM((tq, D_ATTN), F32)]
    return pl.pallas_call(
        functools.partial(_mixer_kernel, tq=tq, windowed=windowed),
        grid=(nb, nt),
        in_specs=in_specs,
        out_specs=pl.BlockSpec((tq, d), row),
        out_shape=jax.ShapeDtypeStruct((t, d), F32),
        scratch_shapes=scratch,
        compiler_params=_params(2),
        name="mixer_win" if windowed else "mixer_ctx",
    )(*args)


def _ffn_kernel(*refs, moe):
    if moe:
        (h_ref, g_ref, sh_ref, sc_ref, gate_ref, r_ref, w1_ref, w3_ref, w2_ref,
         out_ref, xn_ref, acc_ref, comb_ref) = refs
    else:
        (h_ref, g_ref, sh_ref, sc_ref, gate_ref, w1_ref, w3_ref, w2_ref,
         out_ref, xn_ref, acc_ref) = refs
    e = pl.program_id(1)
    tm = h_ref.shape[0]

    @pl.when(e == 0)
    def _():
        xn = _norm_mod(h_ref[...], g_ref[...], sh_ref[...], sc_ref[...])
        xn_ref[...] = xn.astype(BF16)
        acc_ref[...] = jnp.zeros_like(acc_ref)
        if moe:
            i1, i2, g1, g2 = _top2(jnp.dot(xn, r_ref[...], preferred_element_type=F32))
            lane = lax.broadcasted_iota(jnp.int32, (tm, LANES), 1)
            comb_ref[...] = jnp.where(lane == i1, g1, 0.0) + jnp.where(lane == i2, g2, 0.0)

    xb = xn_ref[...]
    h1 = jnp.dot(xb, w1_ref[...], preferred_element_type=F32)
    h3 = jnp.dot(xb, w3_ref[...], preferred_element_type=F32)
    a = h1 * _sigmoid(h1) * h3
    if moe:
        lane = lax.broadcasted_iota(jnp.int32, (tm, LANES), 1)
        a = a * jnp.sum(jnp.where(lane == e, comb_ref[...], 0.0), axis=-1, keepdims=True)
    acc_ref[...] += jnp.dot(a.astype(BF16), w2_ref[...], preferred_element_type=F32)

    @pl.when(e == pl.num_programs(1) - 1)
    def _():
        out_ref[...] = h_ref[...] + gate_ref[...] * acc_ref[...]


def _ffn(h, g, sh, sc, gate, w1, w3, w2, router, *, tm, seq):
    t, d = h.shape
    tpb = seq // tm
    moe = router is not None
    row = lambda i, e: (i, 0)
    fix = lambda i, e: (0, 0)
    mod = lambda i, e: (i // tpb, 0, 0)
    in_specs = [pl.BlockSpec((tm, d), row),
                pl.BlockSpec((1, d), fix),
                pl.BlockSpec((None, 1, d), mod),
                pl.BlockSpec((None, 1, d), mod),
                pl.BlockSpec((None, 1, d), mod)]
    args = [h, g, sh, sc, gate]
    scratch = [pltpu.VMEM((tm, d), BF16), pltpu.VMEM((tm, d), F32)]
    if moe:
        ne, _, fe = w1.shape
        in_specs += [pl.BlockSpec((d, LANES), fix),
                     pl.BlockSpec((None, d, fe), lambda i, e: (e, 0, 0)),
                     pl.BlockSpec((None, d, fe), lambda i, e: (e, 0, 0)),
                     pl.BlockSpec((None, fe, d), lambda i, e: (e, 0, 0))]
        args += [router, w1, w3, w2]
        scratch += [pltpu.VMEM((tm, LANES), F32)]
    else:
        ne = 2
        fe = w1.shape[1] // ne
        in_specs += [pl.BlockSpec((d, fe), lambda i, e: (0, e)),
                     pl.BlockSpec((d, fe), lambda i, e: (0, e)),
                     pl.BlockSpec((fe, d), lambda i, e: (e, 0))]
        args += [w1, w3, w2]
    return pl.pallas_call(
        functools.partial(_ffn_kernel, moe=moe),
        grid=(t // tm, ne),
        in_specs=in_specs,
        out_specs=pl.BlockSpec((tm, d), row),
        out_shape=jax.ShapeDtypeStruct((t, d), F32),
        scratch_shapes=scratch,
        compiler_params=_params(2),
        name="ffn_moe" if moe else "ffn_dense",
    )(*args)


def _dense_ffn_kernel(h_ref, g_ref, sh_ref, sc_ref, gate_ref, w1_ref, w3_ref, w2_ref, out_ref, *, parts):
    rows = h_ref.shape[0] // parts
    for p in range(parts):
        sl = slice(p * rows, (p + 1) * rows)
        hp = h_ref[sl, :]
        xb = _norm_mod(hp, g_ref[...], sh_ref[...], sc_ref[...]).astype(BF16)
        h1 = jnp.dot(xb, w1_ref[...], preferred_element_type=F32)
        h3 = jnp.dot(xb, w3_ref[...], preferred_element_type=F32)
        a = (h1 * _sigmoid(h1) * h3).astype(BF16)
        y = jnp.dot(a, w2_ref[...], preferred_element_type=F32)
        out_ref[sl, :] = hp + gate_ref[...] * y


def _dense_ffn(h, g, sh, sc, gate, w1, w3, w2, *, tm, seq, parts):
    t, d = h.shape
    f = w1.shape[1]
    tpb = seq // tm
    row = lambda i: (i, 0)
    fix = lambda i: (0, 0)
    mod = lambda i: (i // tpb, 0, 0)
    once = pl.Buffered(1)
    return pl.pallas_call(
        functools.partial(_dense_ffn_kernel, parts=parts),
        grid=(t // tm,),
        in_specs=[pl.BlockSpec((tm, d), row),
                  pl.BlockSpec((1, d), fix),
                  pl.BlockSpec((None, 1, d), mod),
                  pl.BlockSpec((None, 1, d), mod),
                  pl.BlockSpec((None, 1, d), mod),
                  pl.BlockSpec((d, f), fix, pipeline_mode=once),
                  pl.BlockSpec((d, f), fix, pipeline_mode=once),
                  pl.BlockSpec((f, d), fix, pipeline_mode=once)],
        out_specs=pl.BlockSpec((tm, d), row),
        out_shape=jax.ShapeDtypeStruct((t, d), F32),
        compiler_params=_params(1),
        name="ffn_dense",
    )(h, g, sh, sc, gate, w1, w3, w2)


def _top2(logits):
    lane = lax.broadcasted_iota(jnp.int32, logits.shape, 1)
    lg = jnp.where(lane < N_EXPERTS, logits, NEG_INF)
    m1 = jnp.max(lg, axis=-1, keepdims=True)
    i1 = jnp.min(jnp.where(lg == m1, lane, LANES), axis=-1, keepdims=True)
    lg2 = jnp.where(lane == i1, NEG_INF, lg)
    m2 = jnp.max(lg2, axis=-1, keepdims=True)
    i2 = jnp.min(jnp.where(lg2 == m2, lane, LANES), axis=-1, keepdims=True)
    e2 = jnp.exp(m2 - m1)
    return i1, i2, 1.0 / (1.0 + e2), e2 / (1.0 + e2)


def _router_kernel(h_ref, g_ref, sh_ref, sc_ref, r_ref, tri_ref, xn_ref, route_ref, rt_ref, cnt_ref,
                   base_ref):
    tm = h_ref.shape[0]

    @pl.when(pl.program_id(0) == 0)
    def _():
        base_ref[...] = jnp.zeros_like(base_ref)

    xn = _norm_mod(h_ref[...], g_ref[...], sh_ref[...], sc_ref[...])
    xn_ref[...] = xn.reshape(xn_ref.shape)
    i1, i2, g1, g2 = _top2(jnp.dot(xn, r_ref[...], preferred_element_type=F32))
    lane = lax.broadcasted_iota(jnp.int32, (tm, LANES), 1)
    hit1 = lane == i1
    hit2 = lane == i2
    chosen = jnp.where(jnp.logical_or(hit1, hit2), 1.0, 0.0)
    before = base_ref[...] + jnp.dot(tri_ref[...], chosen.astype(BF16), preferred_element_type=F32)
    r1 = jnp.sum(jnp.where(hit1, before, 0.0), axis=-1, keepdims=True)
    r2 = jnp.sum(jnp.where(hit2, before, 0.0), axis=-1, keepdims=True)
    base_ref[...] += jnp.sum(chosen, axis=0, keepdims=True)
    cnt_ref[...] = base_ref[...]
    fields = (i1.astype(F32), i2.astype(F32), g1, g2, r1, r2)
    route = jnp.zeros((tm, LANES), F32)
    for k, f in enumerate(fields):
        route = jnp.where(lane == k, f, route)
    route_ref[...] = route
    rt_ref[...] = route.T[0:8, :]


def _router(h, g, sh, sc, router, *, tm, seq):
    t, d = h.shape
    tpb = seq // tm
    row = lambda i: (i, 0)
    fix = lambda i: (0, 0)
    mod = lambda i: (i // tpb, 0, 0)
    ids = jnp.arange(tm)
    tri = (ids[None, :] < ids[:, None]).astype(BF16)
    return pl.pallas_call(
        _router_kernel,
        grid=(t // tm,),
        in_specs=[pl.BlockSpec((tm, d), row), pl.BlockSpec((1, d), fix),
                  pl.BlockSpec((None, 1, d), mod), pl.BlockSpec((None, 1, d), mod),
                  pl.BlockSpec((d, LANES), fix), pl.BlockSpec((tm, tm), fix)],
        out_specs=[pl.BlockSpec((tm, d // LANES, LANES), lambda i: (i, 0, 0)),
                   pl.BlockSpec((tm, LANES), row),
                   pl.BlockSpec((None, 8, tm), lambda i: (i, 0, 0)),
                   pl.BlockSpec((1, LANES), fix)],
        out_shape=[jax.ShapeDtypeStruct((t, d // LANES, LANES), F32),
                   jax.ShapeDtypeStruct((t, LANES), F32),
                   jax.ShapeDtypeStruct((t // tm, 8, tm), F32),
                   jax.ShapeDtypeStruct((1, LANES), F32)],
        scratch_shapes=[pltpu.VMEM((1, LANES), F32)],
        compiler_params=_params(1),
        name="moe_router",
    )(h, g, sh, sc, router, tri)


def _route_plan(t, counts, tr):
    counts = counts[0, 0:N_EXPERTS].astype(jnp.int32)
    tiles = (counts + tr - 1) // tr
    tile_end = jnp.cumsum(tiles)
    tile_start = tile_end - tiles
    experts = jnp.arange(N_EXPERTS, dtype=jnp.int32)

    nt = 2 * t // tr + N_EXPERTS
    tid = jnp.arange(nt, dtype=jnp.int32)
    tile_expert = jnp.minimum(jnp.sum((tid[:, None] >= tile_end[None, :]).astype(jnp.int32), axis=1),
                              N_EXPERTS - 1)
    in_tile = tid - jnp.sum(jnp.where(tile_expert[:, None] == experts[None, :], tile_start[None, :], 0), axis=1)
    own = jnp.sum(jnp.where(tile_expert[:, None] == experts[None, :], counts[None, :], 0), axis=1)
    n_valid = jnp.where(tid < tile_end[-1], jnp.clip(own - in_tile * tr, 0, tr), 0)
    tail = tile_end[-1] + experts
    pad_tiles = jnp.concatenate([jnp.where(tiles > 0, tile_end - 1, -1), jnp.where(tail < nt, tail, -1)])
    return tile_expert, n_valid, tile_start * tr, pad_tiles


def _table_kernel(start_ref, rt_ref, tab_ref, *, nt):
    tm = rt_ref.shape[1]

    def position(e, r):
        start = jnp.zeros_like(r)
        for k in range(N_EXPERTS):
            start = jnp.where(e == float(k), start_ref[k].astype(F32), start)
        return (start + r).astype(jnp.int32)

    live = pl.program_id(0) < nt
    rt = rt_ref[...]
    for c in range(2):
        pos = position(rt[c:c + 1, :], rt[4 + c:5 + c, :])
        tab_ref[:, c * tm:(c + 1) * tm] = jnp.where(live, pos, 0)


def _position_table(route_t, row_start):
    nt, fields, tm = route_t.shape
    grid_spec = pltpu.PrefetchScalarGridSpec(
        num_scalar_prefetch=1,
        grid=(nt + 2,),
        in_specs=[pl.BlockSpec((None, fields, tm), lambda j, st: (jnp.minimum(j, nt - 1), 0, 0))],
        out_specs=pl.BlockSpec((None, 1, 2 * tm), lambda j, st: (j, 0, 0)))
    return pl.pallas_call(
        functools.partial(_table_kernel, nt=nt),
        grid_spec=grid_spec,
        out_shape=jax.ShapeDtypeStruct((nt + 2, 1, 2 * tm), jnp.int32),
        compiler_params=_params(1),
        name="moe_table",
    )(row_start, route_t)


def _row_copies(idx_smem, s_idx, tm, make):
    base = s_idx * (2 * tm)

    def body(r, c):
        make(r, idx_smem[base + r], idx_smem[base + tm + r])
        return c

    lax.fori_loop(0, tm, body, 0, unroll=8)


def _dispatch_kernel(zt_ref, idx_hbm, xn_hbm, xg_hbm, idx_smem, zbuf, xbuf,
                     sem_d, sem_i, sem_z, sem_in, *, tm, nt, tr):
    j = pl.program_id(0)
    slot = j % 2
    other = 1 - slot
    cur = j % 3
    nxt = (j + 1) % 3

    def in_copy(tile, s):
        return pltpu.make_async_copy(xn_hbm.at[pl.ds(tile * tm, tm)], xbuf.at[s], sem_in.at[s])

    @pl.when(j == 0)
    def _():
        zbuf[...] = jnp.zeros_like(zbuf)
        for k in range(zt_ref.shape[0]):
            fill = pltpu.make_async_copy(zbuf, xg_hbm.at[pl.ds(jnp.maximum(zt_ref[k], 0) * tr, tr)], sem_z)
            pl.when(zt_ref[k] >= 0)(fill.start)
        for k in range(zt_ref.shape[0]):
            fill = pltpu.make_async_copy(zbuf, xg_hbm.at[pl.ds(0, tr)], sem_z)
            pl.when(zt_ref[k] >= 0)(fill.wait)

    def idx_copy(row, s):
        return pltpu.make_async_copy(idx_hbm.at[row, 0], idx_smem.at[pl.ds(s * 2 * tm, 2 * tm)], sem_i.at[s])

    def wait_rows(s):
        for _ in range(2):
            pltpu.make_async_copy(xbuf.at[s], xg_hbm.at[pl.ds(0, tm)], sem_d.at[s]).wait()

    @pl.when(j == 0)
    def _():
        idx_copy(0, 0).start()
        in_copy(0, 0).start()

    @pl.when(j >= 2)
    def _():
        wait_rows(nxt)

    @pl.when(j + 1 < nt)
    def _():
        in_copy(j + 1, nxt).start()

    idx_copy(j, slot).wait()
    idx_copy(j + 1, other).start()
    in_copy(j, cur).wait()

    def make(r, p1, p2):
        src = xbuf.at[cur, r]
        pltpu.make_async_copy(src, xg_hbm.at[p1], sem_d.at[cur]).start(priority=0)
        pltpu.make_async_copy(src, xg_hbm.at[p2], sem_d.at[cur]).start(priority=1)

    _row_copies(idx_smem, slot, tm, make)

    @pl.when(j == nt - 1)
    def _():
        if nt > 1:
            wait_rows((nt - 2) % 3)
        wait_rows((nt - 1) % 3)
        idx_copy(j + 1, other).wait()


def _dispatch(xn3, table, pad_tiles, *, tm, tr, n_rows):
    t = xn3.shape[0]
    nt = t // tm
    any_spec = pl.BlockSpec(memory_space=pl.ANY)
    grid_spec = pltpu.PrefetchScalarGridSpec(
        num_scalar_prefetch=1,
        grid=(nt,),
        in_specs=[any_spec, any_spec],
        out_specs=any_spec,
        scratch_shapes=[pltpu.SMEM((4 * tm,), jnp.int32),
                        pltpu.VMEM((tr,) + xn3.shape[1:], F32),
                        pltpu.VMEM((3, tm) + xn3.shape[1:], F32),
                        pltpu.SemaphoreType.DMA((3,)), pltpu.SemaphoreType.DMA((2,)),
                        pltpu.SemaphoreType.DMA, pltpu.SemaphoreType.DMA((3,))])
    return pl.pallas_call(
        functools.partial(_dispatch_kernel, tm=tm, nt=nt, tr=tr),
        grid_spec=grid_spec,
        out_shape=jax.ShapeDtypeStruct((n_rows,) + xn3.shape[1:], F32),
        compiler_params=pltpu.CompilerParams(dimension_semantics=("arbitrary",),
                                             vmem_limit_bytes=VMEM_LIMIT,
                                             disable_bounds_checks=True),
        name="moe_dispatch",
    )(pad_tiles, table, xn3)


def _expert_kernel(te_ref, nv_ref, x_ref, w1_ref, w3_ref, w2_ref, y_ref):
    tr = x_ref.shape[0]
    nv = nv_ref[pl.program_id(0)]

    @pl.when(nv > 0)
    def _():
        half = tr // 2
        for p in range(2):
            x = x_ref[p * half:(p + 1) * half].reshape(half, D_MODEL).astype(BF16)
            h1 = jnp.dot(x, w1_ref[...], preferred_element_type=F32)
            h3 = jnp.dot(x, w3_ref[...], preferred_element_type=F32)
            a = (h1 * _sigmoid(h1) * h3).astype(BF16)
            y = jnp.dot(a, w2_ref[...], preferred_element_type=F32)
            y_ref[p * half:(p + 1) * half] = y.reshape((half,) + y_ref.shape[1:])

    @pl.when(nv == 0)
    def _():
        y_ref[...] = jnp.zeros_like(y_ref)


def _experts(xg3, tile_expert, n_valid, w1, w3, w2, *, tr):
    n_rows, sl, ln = xg3.shape
    d = sl * ln
    fe = w1.shape[2]
    rows = lambda j, te, nv: (j, 0, 0)
    wsel = lambda j, te, nv: (te[j], 0, 0)
    grid_spec = pltpu.PrefetchScalarGridSpec(
        num_scalar_prefetch=2,
        grid=(n_rows // tr,),
        in_specs=[pl.BlockSpec((tr, sl, ln), rows),
                  pl.BlockSpec((None, d, fe), wsel),
                  pl.BlockSpec((None, d, fe), wsel),
                  pl.BlockSpec((None, fe, d), wsel)],
        out_specs=pl.BlockSpec((tr, sl, ln), rows))
    return pl.pallas_call(
        _expert_kernel,
        grid_spec=grid_spec,
        out_shape=jax.ShapeDtypeStruct(xg3.shape, F32),
        compiler_params=_params(1),
        name="moe_experts",
    )(tile_expert, n_valid, xg3, w1, w3, w2)


def _combine_kernel(idx_hbm, h_ref, gate_ref, route_ref, yg_hbm, out_ref,
                    y1buf, y2buf, idx_smem, sem_y, sem_i, *, tm, nt):
    j = pl.program_id(0)
    slot = j % 2
    other = 1 - slot

    def idx_copy(row, s):
        return pltpu.make_async_copy(idx_hbm.at[row, 0], idx_smem.at[pl.ds(s * 2 * tm, 2 * tm)], sem_i.at[s])

    def fetch(s_idx, s_buf):
        def make(r, p1, p2):
            pltpu.make_async_copy(yg_hbm.at[p1], y1buf.at[s_buf, r], sem_y.at[s_buf]).start(priority=0)
            pltpu.make_async_copy(yg_hbm.at[p2], y2buf.at[s_buf, r], sem_y.at[s_buf]).start(priority=1)

        _row_copies(idx_smem, s_idx, tm, make)

    def wait_rows(s):
        pltpu.make_async_copy(yg_hbm.at[pl.ds(0, tm)], y1buf.at[s], sem_y.at[s]).wait()
        pltpu.make_async_copy(yg_hbm.at[pl.ds(0, tm)], y2buf.at[s], sem_y.at[s]).wait()

    @pl.when(j == 0)
    def _():
        first = idx_copy(0, 0)
        first.start()
        first.wait()
        fetch(0, 0)
        idx_copy(1, 1).start()

    idx_copy(j + 1, other).wait()
    fetch(other, other)
    idx_copy(j + 2, slot).start()
    wait_rows(slot)
    rt = route_ref[...]
    y1 = y1buf[slot].reshape(tm, D_MODEL)
    y2 = y2buf[slot].reshape(tm, D_MODEL)
    out_ref[...] = h_ref[...] + gate_ref[...] * (rt[:, 2:3] * y1 + rt[:, 3:4] * y2)

    @pl.when(j == nt - 1)
    def _():
        wait_rows(other)
        idx_copy(j + 2, slot).wait()


def _combine(h, gate, route, table, yg3, *, tm, seq):
    t, d = h.shape
    tpb = seq // tm
    nt = t // tm
    sl, ln = yg3.shape[1:]
    row = lambda i: (i, 0)
    any_spec = pl.BlockSpec(memory_space=pl.ANY)
    return pl.pallas_call(
        functools.partial(_combine_kernel, tm=tm, nt=nt),
        grid=(nt,),
        in_specs=[any_spec,
                  pl.BlockSpec((tm, d), row),
                  pl.BlockSpec((None, 1, d), lambda i: (i // tpb, 0, 0)),
                  pl.BlockSpec((tm, LANES), row),
                  any_spec],
        out_specs=pl.BlockSpec((tm, d), row),
        out_shape=jax.ShapeDtypeStruct((t, d), F32),
        scratch_shapes=[pltpu.VMEM((2, tm, sl, ln), F32), pltpu.VMEM((2, tm, sl, ln), F32),
                        pltpu.SMEM((4 * tm,), jnp.int32),
                        pltpu.SemaphoreType.DMA((2,)), pltpu.SemaphoreType.DMA((2,))],
        compiler_params=pltpu.CompilerParams(dimension_semantics=("arbitrary",),
                                             vmem_limit_bytes=VMEM_LIMIT,
                                             disable_bounds_checks=True),
        name="moe_combine",
    )(table, h, gate, route, yg3)


def _moe(h, g, sh, sc, gate, w1, w3, w2, router, *, seq, tr, tm):
    t = h.shape[0]
    xn3, route, route_t, counts = _router(h, g, sh, sc, router, tm=tm, seq=seq)
    tile_expert, n_valid, row_start, pad_tiles = _route_plan(t, counts, tr)
    table = _position_table(route_t, row_start)
    xg3 = _dispatch(xn3, table, pad_tiles, tm=tm, tr=tr, n_rows=2 * t + N_EXPERTS * tr)
    yg3 = _experts(xg3, tile_expert, n_valid, w1, w3, w2, tr=tr)
    return _combine(h, gate, route, table, yg3, tm=tm, seq=seq)


def _rope_tables(seq):
    rows = seq // GRID_W
    row, col = jnp.meshgrid(jnp.arange(rows, dtype=F32), jnp.arange(GRID_W, dtype=F32), indexing='ij')
    n_freq = HEAD_DIM // 4
    inv_freq = ROPE_THETA ** (-jnp.arange(n_freq, dtype=F32) / n_freq)
    ang_r = row.reshape(-1, 1) * inv_freq
    ang_c = col.reshape(-1, 1) * inv_freq
    ang = jnp.concatenate([ang_r, ang_r, ang_c, ang_c], axis=-1)
    cos, sin = jnp.cos(ang), jnp.sin(ang)
    first = (jnp.arange(HEAD_DIM) % (2 * n_freq)) < n_freq
    sin_a = jnp.where(first, -sin, 0.0)
    sin_b = jnp.where(first, 0.0, sin)
    rep = LANES // HEAD_DIM
    return tuple(jnp.tile(t, (1, rep)) for t in (cos, sin_a, sin_b))


def kernel(x, c, ctx, c_ctx, w_ada, b_ada, norm1_g, norm2_g, w_in, conv_w, q_norm_g, k_norm_g,
           attn_sink, out_norm_conv_g, out_norm_attn_g, w_out, ffn_w1, ffn_w3, ffn_w2,
           moe_router, moe_w1, moe_w3, moe_w2):
    b, s, d = x.shape
    lc = ctx.shape[1]
    depth = w_ada.shape[0]
    assert d == D_MODEL and s % 512 == 0 and lc % 256 == 0 and b + 1 <= 8

    c8 = jnp.zeros((8, d), F32).at[0:b].set(c).at[b].set(c_ctx)
    mod = _modulation(c8, w_ada, b_ada)

    tables = _rope_tables(s)
    ids = jnp.arange(256)
    gm = (ids[:, None] // HEAD_DIM == ids[None, :] // HEAD_DIM).astype(BF16)
    scale = HEAD_DIM ** -0.5

    h = x.reshape(b * s, d)
    hc = ctx.reshape(b * lc, d)
    for layer in range(depth):
        last = layer == depth - 1
        m = mod[layer]
        lat = [m[0:b, k * d:(k + 1) * d].reshape(b, 1, d) for k in range(6)]
        cx = [jnp.broadcast_to(m[b:b + 1, k * d:(k + 1) * d].reshape(1, 1, d), (b, 1, d)) for k in range(6)]
        w_in_b = w_in[layer].astype(BF16)
        w_out_b = w_out[layer].astype(BF16)
        g1n = norm1_g[layer].reshape(1, d)
        g2n = norm2_g[layer].reshape(1, d)
        gq = (jnp.tile(q_norm_g[layer], N_HEADS) * scale).reshape(1, D_ATTN)
        gk = jnp.tile(k_norm_g[layer], N_KV).reshape(1, LANES)
        gc = out_norm_conv_g[layer].reshape(1, D_CONV)
        ga = out_norm_attn_g[layer].reshape(1, D_ATTN)
        sink = attn_sink[layer]
        cw = conv_w[layer]

        cu, q, kv = _inproj(h, g1n, lat[0], lat[1], w_in_b, gq, gk, gm, tables, tm=512, seq=s)
        cuc, qc, kvc = _inproj(hc, g1n, cx[0], cx[1], w_in_b, gq, gk, gm, None, tm=lc, seq=lc)
        h = _mixer(h, cu, q, kv, kvc, sink, cw, gc, ga, w_out_b, lat[2],
                   tq=512, seq=s, ctx_len=lc, windowed=True)
        if not last:
            hc = _mixer(hc, cuc, qc, None, kvc, sink, cw, gc, ga, w_out_b, cx[2],
                        tq=lc, seq=lc, ctx_len=lc, windowed=False)

        i = layer // 2
        if layer % 2 == 0:
            w1, w3, w2 = ffn_w1[i].astype(BF16), ffn_w3[i].astype(BF16), ffn_w2[i].astype(BF16)
            router = None
        else:
            w1, w3, w2 = moe_w1[i].astype(BF16), moe_w3[i].astype(BF16), moe_w2[i].astype(BF16)
            router = jnp.zeros((d, LANES), F32).at[:, 0:N_EXPERTS].set(moe_router[i])
        if router is None:
            h = _dense_ffn(h, g2n, lat[3], lat[4], lat[5], w1, w3, w2, tm=512, seq=s, parts=2)
        else:
            h = _moe(h, g2n, lat[3], lat[4], lat[5], w1, w3, w2, router, seq=s, tr=512, tm=512)
        if not last:
            hc = _ffn(hc, g2n, cx[3], cx[4], cx[5], w1, w3, w2, router, tm=lc, seq=lc)
    return h.reshape(b, s, d)
```

```python
import functools

import jax
import jax.numpy as jnp
from jax import lax
from jax.experimental import pallas as pl
from jax.experimental.pallas import tpu as pltpu

D_MODEL = 1024
GRID_W = 64
HEAD_DIM = 64
D_CONV = 512
D_ATTN = 512
N_HEADS = 8
N_KV = 2
WINDOW = 128
ROPE_THETA = 10000.0
N_EXPERTS = 8
EPS = 1e-6
KV_OFF = 3 * D_CONV + D_ATTN
D_IN = KV_OFF + 2 * N_KV * HEAD_DIM
LANES = 128
VMEM_LIMIT = 48 * 1024 * 1024

F32 = jnp.float32
BF16 = jnp.bfloat16
NEG_INF = float("-inf")


def _params(n_axes):
    return pltpu.CompilerParams(dimension_semantics=("arbitrary",) * n_axes,
                                vmem_limit_bytes=VMEM_LIMIT)


def _sigmoid(x):
    return 1.0 / (1.0 + jnp.exp(-x))


def _mod_kernel(c_ref, w_ref, b_ref, o_ref):
    c = c_ref[...]
    s = c * _sigmoid(c)
    o_ref[...] = jnp.dot(s, w_ref[...], preferred_element_type=F32) + b_ref[...]


def _modulation(c8, w_ada, b_ada):
    depth, d, n = w_ada.shape
    tn = 1536
    return pl.pallas_call(
        _mod_kernel,
        grid=(depth, n // tn),
        in_specs=[pl.BlockSpec((8, d), lambda l, j: (0, 0)),
                  pl.BlockSpec((None, d, tn), lambda l, j: (l, 0, j)),
                  pl.BlockSpec((None, 1, tn), lambda l, j: (l, 0, j))],
        out_specs=pl.BlockSpec((None, 8, tn), lambda l, j: (l, 0, j)),
        out_shape=jax.ShapeDtypeStruct((depth, 8, n), F32),
        compiler_params=_params(2),
        name="adaln_mod",
    )(c8, w_ada, b_ada.reshape(depth, 1, n))


def _norm_mod(x, g, sh, sc):
    ms = jnp.mean(x * x, axis=-1, keepdims=True)
    return (x * lax.rsqrt(ms + EPS) * g) * (1.0 + sc) + sh


def _inproj_kernel(*refs, rope, parts):
    if rope:
        (h_ref, g_ref, sh_ref, sc_ref, w_ref, gq_ref, gk_ref, gm_ref,
         cos_ref, sa_ref, sb_ref, cu_ref, q_ref, kv_ref) = refs
    else:
        (h_ref, g_ref, sh_ref, sc_ref, w_ref, gq_ref, gk_ref, gm_ref,
         cu_ref, q_ref, kv_ref) = refs
    gm = gm_ref[...]
    gq = gq_ref[...]
    rows = h_ref.shape[0] // parts

    def head_norm(t, gain):
        w = t.shape[1]
        ss = jnp.dot((t * t).astype(BF16), gm[0:w, 0:w], preferred_element_type=F32)
        return t * lax.rsqrt(ss * (1.0 / HEAD_DIM) + EPS) * gain

    for p in range(parts):
        sl = slice(p * rows, (p + 1) * rows)

        def rot(t):
            if not rope:
                return t
            return (t * cos_ref[sl, :] + pltpu.roll(t, LANES - 16, 1) * sa_ref[sl, :]
                    + pltpu.roll(t, 16, 1) * sb_ref[sl, :])

        xn = _norm_mod(h_ref[sl, :], g_ref[...], sh_ref[...], sc_ref[...])
        y = jnp.dot(xn.astype(BF16), w_ref[...], preferred_element_type=F32)
        cu_ref[sl, 0:D_CONV] = y[:, 0:D_CONV].astype(BF16)
        cu_ref[sl, D_CONV:2 * D_CONV] = (y[:, D_CONV:2 * D_CONV] * y[:, 2 * D_CONV:3 * D_CONV]).astype(BF16)
        for j in range(D_ATTN // 256):
            lo = 3 * D_CONV + 256 * j
            qn = head_norm(y[:, lo:lo + 256], gq[:, 256 * j:256 * j + 256])
            for c in range(2):
                q_ref[sl, 256 * j + LANES * c:256 * j + LANES * (c + 1)] = rot(
                    qn[:, LANES * c:LANES * (c + 1)]).astype(BF16)
        k = rot(head_norm(y[:, KV_OFF:KV_OFF + LANES], gk_ref[...]))
        v = y[:, KV_OFF + LANES:KV_OFF + 2 * LANES]
        lo_half = lax.broadcasted_iota(jnp.int32, k.shape, 1) < HEAD_DIM
        for c, t in enumerate((k, v)):
            sw = pltpu.roll(t, HEAD_DIM, 1)
            kv_ref[sl, 2 * c * LANES:(2 * c + 1) * LANES] = jnp.where(lo_half, t, sw).astype(BF16)
            kv_ref[sl, (2 * c + 1) * LANES:(2 * c + 2) * LANES] = jnp.where(lo_half, sw, t).astype(BF16)


def _inproj(h, g, sh, sc, w, gq, gk, gm, tables, *, li, tm, seq):
    t, d = h.shape
    tpb = seq // tm
    rope = tables is not None
    row = lambda i: (i, 0)
    fix = lambda i: (0, 0)
    mod = lambda i: (i // tpb, 0, 0)
    in_specs = [pl.BlockSpec((tm, d), row),
                pl.BlockSpec((1, d), fix),
                pl.BlockSpec((None, 1, d), mod),
                pl.BlockSpec((None, 1, d), mod),
                pl.BlockSpec((None, d, D_IN), lambda i: (li, 0, 0)),
                pl.BlockSpec((1, D_ATTN), fix),
                pl.BlockSpec((1, LANES), fix),
                pl.BlockSpec((256, 256), fix)]
    args = [h, g, sh, sc, w, gq, gk, gm]
    if rope:
        in_specs += [pl.BlockSpec((tm, LANES), lambda i: (i % tpb, 0))] * 3
        args += list(tables)
    return pl.pallas_call(
        functools.partial(_inproj_kernel, rope=rope, parts=2),
        grid=(t // tm,),
        in_specs=in_specs,
        out_specs=[pl.BlockSpec((tm, 2 * D_CONV), row),
                   pl.BlockSpec((tm, D_ATTN), row),
                   pl.BlockSpec((tm, 4 * LANES), row)],
        out_shape=[jax.ShapeDtypeStruct((t, 2 * D_CONV), BF16),
                   jax.ShapeDtypeStruct((t, D_ATTN), BF16),
                   jax.ShapeDtypeStruct((t, 4 * LANES), BF16)],
        compiler_params=_params(1),
        name="inproj_rope" if rope else "inproj_ctx",
    )(*args)


_NT = (((1,), (1,)), ((), ()))


def _mixer_kernel(*refs, tq, windowed):
    if windowed:
        (sink_ref, h_ref, cu_ref, cup_ref, cun_ref, q_ref, kvp_ref, kv_ref, kvn_ref, kvc_ref,
         cw_ref, gc_ref, ga_ref, wo_ref, g1_ref, out_ref, kw_ref, ya_ref) = refs
    else:
        (sink_ref, h_ref, cu_ref, q_ref, kvc_ref,
         cw_ref, gc_ref, ga_ref, wo_ref, g1_ref, out_ref, ya_ref) = refs
    i = pl.program_id(1)
    nt = pl.num_programs(1)
    nsub = tq // WINDOW

    cu = cu_ref[...]
    bg = cu[:, 0:D_CONV].astype(F32)
    u = cu[:, D_CONV:2 * D_CONV].astype(F32)
    rows = lax.broadcasted_iota(jnp.int32, (tq, 1), 0)
    if windowed:
        up_row = cup_ref[:, D_CONV:2 * D_CONV].astype(F32)[15:16, :]
        un_row = cun_ref[:, D_CONV:2 * D_CONV].astype(F32)[0:1, :]
        up_row = jnp.where(i > 0, up_row, 0.0)
        un_row = jnp.where(i < nt - 1, un_row, 0.0)
    else:
        up_row = jnp.zeros((1, D_CONV), F32)
        un_row = jnp.zeros((1, D_CONV), F32)
    u_prev = jnp.where(rows == 0, up_row, pltpu.roll(u, 1, 0))
    u_next = jnp.where(rows == tq - 1, un_row, pltpu.roll(u, tq - 1, 0))
    cw = cw_ref[...]
    yc = bg * (cw[0:1, :] * u_prev + cw[1:2, :] * u + cw[2:3, :] * u_next)
    yc = yc * lax.rsqrt(jnp.mean(yc * yc, axis=-1, keepdims=True) + EPS) * gc_ref[...]

    if windowed:
        kw_ref[0:WINDOW, :] = kvp_ref[...]
        kw_ref[WINDOW:WINDOW + tq, :] = kv_ref[...]
        kw_ref[WINDOW + tq:2 * WINDOW + tq, :] = kvn_ref[...]
    lane_lo = lax.broadcasted_iota(jnp.int32, (WINDOW, LANES), 1) < HEAD_DIM
    kvc = kvc_ref[...]
    gqa = N_HEADS // N_KV
    head_of_row = lax.broadcasted_iota(jnp.int32, (gqa * WINDOW, 1), 0) // WINDOW
    if windowed:
        col = lax.broadcasted_iota(jnp.int32, (WINDOW, 3 * WINDOW), 1)
        rw = lax.broadcasted_iota(jnp.int32, (WINDOW, 3 * WINDOW), 0)
        band = (col >= rw) & (col <= rw + 2 * WINDOW)

    def sub_block(s):
        r0 = s * WINDOW
        if windowed:
            kwin = kw_ref[pl.ds(r0, 3 * WINDOW), :]
            valid = band
            if s == 0:
                valid = valid & (col >= jnp.where(i == 0, WINDOW, 0))
            if s == nsub - 1:
                valid = valid & (col < jnp.where(i == nt - 1, 2 * WINDOW, 3 * WINDOW))
        for grp in range(N_KV):
            rows, sk = [], jnp.zeros((gqa * WINDOW, 1), F32)
            for hh in range(gqa):
                head = gqa * grp + hh
                qp = q_ref[pl.ds(r0, WINDOW), LANES * (head // 2):LANES * (head // 2 + 1)]
                zero = jnp.zeros_like(qp)
                rows.append(jnp.where(lane_lo, zero, qp) if head % 2 else jnp.where(lane_lo, qp, zero))
                sk = jnp.where(head_of_row == hh, sink_ref[head], sk)
            q4 = jnp.concatenate(rows, axis=0)
            s_c = lax.dot_general(q4, kvc[:, LANES * grp:LANES * (grp + 1)], _NT, preferred_element_type=F32)
            m = jnp.maximum(jnp.max(s_c, axis=-1, keepdims=True), sk)
            if windowed:
                s_w = lax.dot_general(q4, kwin[:, LANES * grp:LANES * (grp + 1)], _NT,
                                      preferred_element_type=F32)
                s_w = jnp.where(valid[None], s_w.reshape(gqa, WINDOW, 3 * WINDOW), NEG_INF)
                s_w = s_w.reshape(gqa * WINDOW, 3 * WINDOW)
                m = jnp.maximum(m, jnp.max(s_w, axis=-1, keepdims=True))
            p_c = jnp.exp(s_c - m)
            den = jnp.exp(sk - m) + jnp.sum(p_c, axis=-1, keepdims=True)
            o = jnp.dot(p_c.astype(BF16), kvc[:, LANES * (2 + grp):LANES * (3 + grp)],
                        preferred_element_type=F32)
            if windowed:
                p_w = jnp.exp(s_w - m)
                den = den + jnp.sum(p_w, axis=-1, keepdims=True)
                o = o + jnp.dot(p_w.astype(BF16), kwin[:, LANES * (2 + grp):LANES * (3 + grp)],
                                preferred_element_type=F32)
            o = o / den
            for pr in range(gqa // 2):
                pair = (gqa // 2) * grp + pr
                even = o[2 * pr * WINDOW:(2 * pr + 1) * WINDOW]
                odd = o[(2 * pr + 1) * WINDOW:(2 * pr + 2) * WINDOW]
                ya_ref[pl.ds(r0, WINDOW), LANES * pair:LANES * (pair + 1)] = jnp.where(lane_lo, even, odd)

    part = 2 * WINDOW
    ycb = yc.astype(BF16)
    for p in range(tq // part):
        sub_block(2 * p)
        sub_block(2 * p + 1)
        sl = slice(p * part, (p + 1) * part)
        ya = ya_ref[sl, :]
        ya = ya * lax.rsqrt(jnp.mean(ya * ya, axis=-1, keepdims=True) + EPS) * ga_ref[...]
        y = (jnp.dot(ycb[sl], wo_ref[0:D_CONV, :], preferred_element_type=F32)
             + jnp.dot(ya.astype(BF16), wo_ref[D_CONV:2 * D_CONV, :], preferred_element_type=F32))
        out_ref[sl, :] = h_ref[sl, :] + g1_ref[...] * y


def _mixer(h, cu, q, kv, kvc, sink, cw, gc, ga, wo, g1, *, li, tq, seq, ctx_len, windowed):
    t, d = h.shape
    nt = seq // tq
    nb = t // seq
    row = lambda b, i: (b * nt + i, 0)
    fix = lambda b, i: (0, 0)
    smem = pl.BlockSpec(memory_space=pltpu.SMEM)
    tail = [pl.BlockSpec((3, D_CONV), fix),
            pl.BlockSpec((1, D_CONV), fix),
            pl.BlockSpec((1, D_ATTN), fix),
            pl.BlockSpec((None, d, d), lambda b, i: (li, 0, 0)),
            pl.BlockSpec((None, 1, d), lambda b, i: (b, 0, 0))]
    ctx_spec = pl.BlockSpec((ctx_len, 4 * LANES), lambda b, i: (b, 0))
    if windowed:
        r16 = tq // 16
        n16 = t // 16
        rw = tq // WINDOW
        nw = t // WINDOW
        in_specs = [smem,
                    pl.BlockSpec((tq, d), row),
                    pl.BlockSpec((tq, 2 * D_CONV), row),
                    pl.BlockSpec((16, 2 * D_CONV), lambda b, i: (jnp.maximum((b * nt + i) * r16 - 1, 0), 0)),
                    pl.BlockSpec((16, 2 * D_CONV), lambda b, i: (jnp.minimum((b * nt + i + 1) * r16, n16 - 1), 0)),
                    pl.BlockSpec((tq, D_ATTN), row),
                    pl.BlockSpec((WINDOW, 4 * LANES), lambda b, i: (jnp.maximum((b * nt + i) * rw - 1, 0), 0)),
                    pl.BlockSpec((tq, 4 * LANES), row),
                    pl.BlockSpec((WINDOW, 4 * LANES), lambda b, i: (jnp.minimum((b * nt + i + 1) * rw, nw - 1), 0)),
                    ctx_spec] + tail
        args = [sink, h, cu, cu, cu, q, kv, kv, kv, kvc, cw, gc, ga, wo, g1]
        scratch = [pltpu.VMEM((tq + 2 * WINDOW, 4 * LANES), BF16), pltpu.VMEM((tq, D_ATTN), F32)]
    else:
        in_specs = [smem,
                    pl.BlockSpec((tq, d), row),
                    pl.BlockSpec((tq, 2 * D_CONV), row),
                    pl.BlockSpec((tq, D_ATTN), row),
                    ctx_spec] + tail
        args = [sink, h, cu, q, kvc, cw, gc, ga, wo, g1]
        scratch = [pltpu.VMEM((tq, D_ATTN), F32)]
    return pl.pallas_call(
        functools.partial(_mixer_kernel, tq=tq, windowed=windowed),
        grid=(nb, nt),
        in_specs=in_specs,
        out_specs=pl.BlockSpec((tq, d), row),
        out_shape=jax.ShapeDtypeStruct((t, d), F32),
        scratch_shapes=scratch,
        compiler_params=_params(2),
        name="mixer_win" if windowed else "mixer_ctx",
    )(*args)


def _ffn_kernel(*refs, moe):
    if moe:
        (h_ref, g_ref, sh_ref, sc_ref, gate_ref, r_ref, w1_ref, w3_ref, w2_ref,
         out_ref, xn_ref, acc_ref, comb_ref) = refs
    else:
        (h_ref, g_ref, sh_ref, sc_ref, gate_ref, w1_ref, w3_ref, w2_ref,
         out_ref, xn_ref, acc_ref) = refs
    e = pl.program_id(1)
    tm = h_ref.shape[0]

    @pl.when(e == 0)
    def _():
        xn = _norm_mod(h_ref[...], g_ref[...], sh_ref[...], sc_ref[...])
        xn_ref[...] = xn.astype(BF16)
        acc_ref[...] = jnp.zeros_like(acc_ref)
        if moe:
            i1, i2, g1, g2 = _top2(jnp.dot(xn, r_ref[...], preferred_element_type=F32))
            lane = lax.broadcasted_iota(jnp.int32, (tm, LANES), 1)
            comb_ref[...] = jnp.where(lane == i1, g1, 0.0) + jnp.where(lane == i2, g2, 0.0)

    xb = xn_ref[...]
    h1 = jnp.dot(xb, w1_ref[...], preferred_element_type=F32)
    h3 = jnp.dot(xb, w3_ref[...], preferred_element_type=F32)
    a = h1 * _sigmoid(h1) * h3
    if moe:
        lane = lax.broadcasted_iota(jnp.int32, (tm, LANES), 1)
        a = a * jnp.sum(jnp.where(lane == e, comb_ref[...], 0.0), axis=-1, keepdims=True)
    acc_ref[...] += jnp.dot(a.astype(BF16), w2_ref[...], preferred_element_type=F32)

    @pl.when(e == pl.num_programs(1) - 1)
    def _():
        out_ref[...] = h_ref[...] + gate_ref[...] * acc_ref[...]


def _ffn(h, g, sh, sc, gate, w1, w3, w2, router, *, li, tm, seq):
    t, d = h.shape
    tpb = seq // tm
    moe = router is not None
    row = lambda i, e: (i, 0)
    fix = lambda i, e: (0, 0)
    mod = lambda i, e: (i // tpb, 0, 0)
    in_specs = [pl.BlockSpec((tm, d), row),
                pl.BlockSpec((1, d), fix),
                pl.BlockSpec((None, 1, d), mod),
                pl.BlockSpec((None, 1, d), mod),
                pl.BlockSpec((None, 1, d), mod)]
    args = [h, g, sh, sc, gate]
    scratch = [pltpu.VMEM((tm, d), BF16), pltpu.VMEM((tm, d), F32)]
    if moe:
        _, ne, _, fe = w1.shape
        in_specs += [pl.BlockSpec((d, LANES), fix),
                     pl.BlockSpec((None, None, d, fe), lambda i, e: (li, e, 0, 0)),
                     pl.BlockSpec((None, None, d, fe), lambda i, e: (li, e, 0, 0)),
                     pl.BlockSpec((None, None, fe, d), lambda i, e: (li, e, 0, 0))]
        args += [router, w1, w3, w2]
        scratch += [pltpu.VMEM((tm, LANES), F32)]
    else:
        ne = 2
        fe = w1.shape[2] // ne
        in_specs += [pl.BlockSpec((None, d, fe), lambda i, e: (li, 0, e)),
                     pl.BlockSpec((None, d, fe), lambda i, e: (li, 0, e)),
                     pl.BlockSpec((None, fe, d), lambda i, e: (li, e, 0))]
        args += [w1, w3, w2]
    return pl.pallas_call(
        functools.partial(_ffn_kernel, moe=moe),
        grid=(t // tm, ne),
        in_specs=in_specs,
        out_specs=pl.BlockSpec((tm, d), row),
        out_shape=jax.ShapeDtypeStruct((t, d), F32),
        scratch_shapes=scratch,
        compiler_params=_params(2),
        name="ffn_moe" if moe else "ffn_dense",
    )(*args)


def _dense_ffn_kernel(h_ref, g_ref, sh_ref, sc_ref, gate_ref, w1_ref, w3_ref, w2_ref, out_ref, *, parts):
    rows = h_ref.shape[0] // parts
    for p in range(parts):
        sl = slice(p * rows, (p + 1) * rows)
        hp = h_ref[sl, :]
        xb = _norm_mod(hp, g_ref[...], sh_ref[...], sc_ref[...]).astype(BF16)
        h1 = jnp.dot(xb, w1_ref[...], preferred_element_type=F32)
        h3 = jnp.dot(xb, w3_ref[...], preferred_element_type=F32)
        a = (h1 * _sigmoid(h1) * h3).astype(BF16)
        y = jnp.dot(a, w2_ref[...], preferred_element_type=F32)
        out_ref[sl, :] = hp + gate_ref[...] * y


def _dense_ffn(h, g, sh, sc, gate, w1, w3, w2, *, li, tm, seq, parts):
    t, d = h.shape
    f = w1.shape[2]
    wfix = lambda i: (li, 0, 0)
    tpb = seq // tm
    row = lambda i: (i, 0)
    fix = lambda i: (0, 0)
    mod = lambda i: (i // tpb, 0, 0)
    once = pl.Buffered(1)
    return pl.pallas_call(
        functools.partial(_dense_ffn_kernel, parts=parts),
        grid=(t // tm,),
        in_specs=[pl.BlockSpec((tm, d), row),
                  pl.BlockSpec((1, d), fix),
                  pl.BlockSpec((None, 1, d), mod),
                  pl.BlockSpec((None, 1, d), mod),
                  pl.BlockSpec((None, 1, d), mod),
                  pl.BlockSpec((None, d, f), wfix, pipeline_mode=once),
                  pl.BlockSpec((None, d, f), wfix, pipeline_mode=once),
                  pl.BlockSpec((None, f, d), wfix, pipeline_mode=once)],
        out_specs=pl.BlockSpec((tm, d), row),
        out_shape=jax.ShapeDtypeStruct((t, d), F32),
        compiler_params=_params(1),
        name="ffn_dense",
    )(h, g, sh, sc, gate, w1, w3, w2)


def _top2(logits):
    lane = lax.broadcasted_iota(jnp.int32, logits.shape, 1)
    lg = jnp.where(lane < N_EXPERTS, logits, NEG_INF)
    m1 = jnp.max(lg, axis=-1, keepdims=True)
    i1 = jnp.min(jnp.where(lg == m1, lane, LANES), axis=-1, keepdims=True)
    lg2 = jnp.where(lane == i1, NEG_INF, lg)
    m2 = jnp.max(lg2, axis=-1, keepdims=True)
    i2 = jnp.min(jnp.where(lg2 == m2, lane, LANES), axis=-1, keepdims=True)
    e2 = jnp.exp(m2 - m1)
    return i1, i2, 1.0 / (1.0 + e2), e2 / (1.0 + e2)


def _router_kernel(h_ref, g_ref, sh_ref, sc_ref, r_ref, tri_ref, xn_ref, route_ref, rt_ref, cnt_ref,
                   base_ref):
    tm = h_ref.shape[0]

    @pl.when(pl.program_id(0) == 0)
    def _():
        base_ref[...] = jnp.zeros_like(base_ref)

    xn = _norm_mod(h_ref[...], g_ref[...], sh_ref[...], sc_ref[...])
    xn_ref[...] = xn.reshape(xn_ref.shape)
    i1, i2, g1, g2 = _top2(jnp.dot(xn, r_ref[...], preferred_element_type=F32))
    lane = lax.broadcasted_iota(jnp.int32, (tm, LANES), 1)
    hit1 = lane == i1
    hit2 = lane == i2
    chosen = jnp.where(jnp.logical_or(hit1, hit2), 1.0, 0.0)
    before = base_ref[...] + jnp.dot(tri_ref[...], chosen.astype(BF16), preferred_element_type=F32)
    r1 = jnp.sum(jnp.where(hit1, before, 0.0), axis=-1, keepdims=True)
    r2 = jnp.sum(jnp.where(hit2, before, 0.0), axis=-1, keepdims=True)
    base_ref[...] += jnp.sum(chosen, axis=0, keepdims=True)
    cnt_ref[...] = base_ref[...]
    fields = (i1.astype(F32), i2.astype(F32), g1, g2, r1, r2)
    route = jnp.zeros((tm, LANES), F32)
    for k, f in enumerate(fields):
        route = jnp.where(lane == k, f, route)
    route_ref[...] = route
    rt_ref[...] = route.T[0:8, :]


def _router(h, g, sh, sc, router, *, tm, seq):
    t, d = h.shape
    tpb = seq // tm
    row = lambda i: (i, 0)
    fix = lambda i: (0, 0)
    mod = lambda i: (i // tpb, 0, 0)
    ids = jnp.arange(tm)
    tri = (ids[None, :] < ids[:, None]).astype(BF16)
    return pl.pallas_call(
        _router_kernel,
        grid=(t // tm,),
        in_specs=[pl.BlockSpec((tm, d), row), pl.BlockSpec((1, d), fix),
                  pl.BlockSpec((None, 1, d), mod), pl.BlockSpec((None, 1, d), mod),
                  pl.BlockSpec((d, LANES), fix), pl.BlockSpec((tm, tm), fix)],
        out_specs=[pl.BlockSpec((tm, d // LANES, LANES), lambda i: (i, 0, 0)),
                   pl.BlockSpec((tm, LANES), row),
                   pl.BlockSpec((None, 8, tm), lambda i: (i, 0, 0)),
                   pl.BlockSpec((1, LANES), fix)],
        out_shape=[jax.ShapeDtypeStruct((t, d // LANES, LANES), F32),
                   jax.ShapeDtypeStruct((t, LANES), F32),
                   jax.ShapeDtypeStruct((t // tm, 8, tm), F32),
                   jax.ShapeDtypeStruct((1, LANES), F32)],
        scratch_shapes=[pltpu.VMEM((1, LANES), F32)],
        compiler_params=_params(1),
        name="moe_router",
    )(h, g, sh, sc, router, tri)


def _route_plan(t, counts, tr):
    counts = counts[0, 0:N_EXPERTS].astype(jnp.int32)
    tiles = (counts + tr - 1) // tr
    tile_end = jnp.cumsum(tiles)
    tile_start = tile_end - tiles
    experts = jnp.arange(N_EXPERTS, dtype=jnp.int32)

    nt = 2 * t // tr + N_EXPERTS
    tid = jnp.arange(nt, dtype=jnp.int32)
    tile_expert = jnp.minimum(jnp.sum((tid[:, None] >= tile_end[None, :]).astype(jnp.int32), axis=1),
                              N_EXPERTS - 1)
    in_tile = tid - jnp.sum(jnp.where(tile_expert[:, None] == experts[None, :], tile_start[None, :], 0), axis=1)
    own = jnp.sum(jnp.where(tile_expert[:, None] == experts[None, :], counts[None, :], 0), axis=1)
    n_valid = jnp.where(tid < tile_end[-1], jnp.clip(own - in_tile * tr, 0, tr), 0)
    tail = tile_end[-1] + experts
    pad_tiles = jnp.concatenate([jnp.where(tiles > 0, tile_end - 1, -1), jnp.where(tail < nt, tail, -1)])
    return tile_expert, n_valid, tile_start * tr, pad_tiles


def _table_kernel(start_ref, rt_ref, tab_ref, *, nt):
    tm = rt_ref.shape[1]

    def position(e, r):
        start = jnp.zeros_like(r)
        for k in range(N_EXPERTS):
            start = jnp.where(e == float(k), start_ref[k].astype(F32), start)
        return (start + r).astype(jnp.int32)

    live = pl.program_id(0) < nt
    rt = rt_ref[...]
    for c in range(2):
        pos = position(rt[c:c + 1, :], rt[4 + c:5 + c, :])
        tab_ref[:, c * tm:(c + 1) * tm] = jnp.where(live, pos, 0)


def _position_table(route_t, row_start):
    nt, fields, tm = route_t.shape
    grid_spec = pltpu.PrefetchScalarGridSpec(
        num_scalar_prefetch=1,
        grid=(nt + 2,),
        in_specs=[pl.BlockSpec((None, fields, tm), lambda j, st: (jnp.minimum(j, nt - 1), 0, 0))],
        out_specs=pl.BlockSpec((None, 1, 2 * tm), lambda j, st: (j, 0, 0)))
    return pl.pallas_call(
        functools.partial(_table_kernel, nt=nt),
        grid_spec=grid_spec,
        out_shape=jax.ShapeDtypeStruct((nt + 2, 1, 2 * tm), jnp.int32),
        compiler_params=_params(1),
        name="moe_table",
    )(row_start, route_t)


def _row_copies(idx_smem, s_idx, tm, make):
    base = s_idx * (2 * tm)

    def body(r, c):
        make(r, idx_smem[base + r], idx_smem[base + tm + r])
        return c

    lax.fori_loop(0, tm, body, 0, unroll=8)


def _dispatch_kernel(zt_ref, idx_hbm, xn_hbm, xg_hbm, idx_smem, zbuf, xbuf,
                     sem_d, sem_i, sem_z, sem_in, *, tm, nt, tr):
    j = pl.program_id(0)
    slot = j % 2
    other = 1 - slot
    cur = j % 3
    nxt = (j + 1) % 3

    def in_copy(tile, s):
        return pltpu.make_async_copy(xn_hbm.at[pl.ds(tile * tm, tm)], xbuf.at[s], sem_in.at[s])

    @pl.when(j == 0)
    def _():
        zbuf[...] = jnp.zeros_like(zbuf)
        for k in range(zt_ref.shape[0]):
            fill = pltpu.make_async_copy(zbuf, xg_hbm.at[pl.ds(jnp.maximum(zt_ref[k], 0) * tr, tr)], sem_z)
            pl.when(zt_ref[k] >= 0)(fill.start)
        for k in range(zt_ref.shape[0]):
            fill = pltpu.make_async_copy(zbuf, xg_hbm.at[pl.ds(0, tr)], sem_z)
            pl.when(zt_ref[k] >= 0)(fill.wait)

    def idx_copy(row, s):
        return pltpu.make_async_copy(idx_hbm.at[row, 0], idx_smem.at[pl.ds(s * 2 * tm, 2 * tm)], sem_i.at[s])

    def wait_rows(s):
        for _ in range(2):
            pltpu.make_async_copy(xbuf.at[s], xg_hbm.at[pl.ds(0, tm)], sem_d.at[s]).wait()

    @pl.when(j == 0)
    def _():
        idx_copy(0, 0).start()
        in_copy(0, 0).start()

    @pl.when(j >= 2)
    def _():
        wait_rows(nxt)

    @pl.when(j + 1 < nt)
    def _():
        in_copy(j + 1, nxt).start()

    idx_copy(j, slot).wait()
    idx_copy(j + 1, other).start()
    in_copy(j, cur).wait()

    def make(r, p1, p2):
        src = xbuf.at[cur, r]
        pltpu.make_async_copy(src, xg_hbm.at[p1], sem_d.at[cur]).start(priority=0)
        pltpu.make_async_copy(src, xg_hbm.at[p2], sem_d.at[cur]).start(priority=1)

    _row_copies(idx_smem, slot, tm, make)

    @pl.when(j == nt - 1)
    def _():
        if nt > 1:
            wait_rows((nt - 2) % 3)
        wait_rows((nt - 1) % 3)
        idx_copy(j + 1, other).wait()


def _dispatch(xn3, table, pad_tiles, *, tm, tr, n_rows):
    t = xn3.shape[0]
    nt = t // tm
    any_spec = pl.BlockSpec(memory_space=pl.ANY)
    grid_spec = pltpu.PrefetchScalarGridSpec(
        num_scalar_prefetch=1,
        grid=(nt,),
        in_specs=[any_spec, any_spec],
        out_specs=any_spec,
        scratch_shapes=[pltpu.SMEM((4 * tm,), jnp.int32),
                        pltpu.VMEM((tr,) + xn3.shape[1:], F32),
                        pltpu.VMEM((3, tm) + xn3.shape[1:], F32),
                        pltpu.SemaphoreType.DMA((3,)), pltpu.SemaphoreType.DMA((2,)),
                        pltpu.SemaphoreType.DMA, pltpu.SemaphoreType.DMA((3,))])
    return pl.pallas_call(
        functools.partial(_dispatch_kernel, tm=tm, nt=nt, tr=tr),
        grid_spec=grid_spec,
        out_shape=jax.ShapeDtypeStruct((n_rows,) + xn3.shape[1:], F32),
        compiler_params=pltpu.CompilerParams(dimension_semantics=("arbitrary",),
                                             vmem_limit_bytes=VMEM_LIMIT,
                                             disable_bounds_checks=True),
        name="moe_dispatch",
    )(pad_tiles, table, xn3)


def _expert_kernel(te_ref, nv_ref, x_ref, w1_ref, w3_ref, w2_ref, y_ref):
    tr = x_ref.shape[0]
    nv = nv_ref[pl.program_id(0)]

    @pl.when(nv > 0)
    def _():
        half = tr // 2
        for p in range(2):
            x = x_ref[p * half:(p + 1) * half].reshape(half, D_MODEL).astype(BF16)
            h1 = jnp.dot(x, w1_ref[...], preferred_element_type=F32)
            h3 = jnp.dot(x, w3_ref[...], preferred_element_type=F32)
            a = (h1 * _sigmoid(h1) * h3).astype(BF16)
            y = jnp.dot(a, w2_ref[...], preferred_element_type=F32)
            y_ref[p * half:(p + 1) * half] = y.reshape((half,) + y_ref.shape[1:])

    @pl.when(nv == 0)
    def _():
        y_ref[...] = jnp.zeros_like(y_ref)


def _experts(xg3, tile_expert, n_valid, w1, w3, w2, *, li, tr):
    n_rows, sl, ln = xg3.shape
    d = sl * ln
    fe = w1.shape[3]
    rows = lambda j, te, nv: (j, 0, 0)
    wsel = lambda j, te, nv: (li, te[j], 0, 0)
    grid_spec = pltpu.PrefetchScalarGridSpec(
        num_scalar_prefetch=2,
        grid=(n_rows // tr,),
        in_specs=[pl.BlockSpec((tr, sl, ln), rows),
                  pl.BlockSpec((None, None, d, fe), wsel),
                  pl.BlockSpec((None, None, d, fe), wsel),
                  pl.BlockSpec((None, None, fe, d), wsel)],
        out_specs=pl.BlockSpec((tr, sl, ln), rows))
    return pl.pallas_call(
        _expert_kernel,
        grid_spec=grid_spec,
        out_shape=jax.ShapeDtypeStruct(xg3.shape, F32),
        compiler_params=_params(1),
        name="moe_experts",
    )(tile_expert, n_valid, xg3, w1, w3, w2)


def _combine_kernel(idx_hbm, h_ref, gate_ref, route_ref, yg_hbm, out_ref,
                    y1buf, y2buf, idx_smem, sem_y, sem_i, *, tm, nt):
    j = pl.program_id(0)
    slot = j % 2
    other = 1 - slot

    def idx_copy(row, s):
        return pltpu.make_async_copy(idx_hbm.at[row, 0], idx_smem.at[pl.ds(s * 2 * tm, 2 * tm)], sem_i.at[s])

    def fetch(s_idx, s_buf):
        def make(r, p1, p2):
            pltpu.make_async_copy(yg_hbm.at[p1], y1buf.at[s_buf, r], sem_y.at[s_buf]).start(priority=0)
            pltpu.make_async_copy(yg_hbm.at[p2], y2buf.at[s_buf, r], sem_y.at[s_buf]).start(priority=1)

        _row_copies(idx_smem, s_idx, tm, make)

    def wait_rows(s):
        pltpu.make_async_copy(yg_hbm.at[pl.ds(0, tm)], y1buf.at[s], sem_y.at[s]).wait()
        pltpu.make_async_copy(yg_hbm.at[pl.ds(0, tm)], y2buf.at[s], sem_y.at[s]).wait()

    @pl.when(j == 0)
    def _():
        first = idx_copy(0, 0)
        first.start()
        first.wait()
        fetch(0, 0)
        idx_copy(1, 1).start()

    idx_copy(j + 1, other).wait()
    fetch(other, other)
    idx_copy(j + 2, slot).start()
    wait_rows(slot)
    rt = route_ref[...]
    y1 = y1buf[slot].reshape(tm, D_MODEL)
    y2 = y2buf[slot].reshape(tm, D_MODEL)
    out_ref[...] = h_ref[...] + gate_ref[...] * (rt[:, 2:3] * y1 + rt[:, 3:4] * y2)

    @pl.when(j == nt - 1)
    def _():
        wait_rows(other)
        idx_copy(j + 2, slot).wait()


def _combine(h, gate, route, table, yg3, *, tm, seq):
    t, d = h.shape
    tpb = seq // tm
    nt = t // tm
    sl, ln = yg3.shape[1:]
    row = lambda i: (i, 0)
    any_spec = pl.BlockSpec(memory_space=pl.ANY)
    return pl.pallas_call(
        functools.partial(_combine_kernel, tm=tm, nt=nt),
        grid=(nt,),
        in_specs=[any_spec,
                  pl.BlockSpec((tm, d), row),
                  pl.BlockSpec((None, 1, d), lambda i: (i // tpb, 0, 0)),
                  pl.BlockSpec((tm, LANES), row),
                  any_spec],
        out_specs=pl.BlockSpec((tm, d), row),
        out_shape=jax.ShapeDtypeStruct((t, d), F32),
        scratch_shapes=[pltpu.VMEM((2, tm, sl, ln), F32), pltpu.VMEM((2, tm, sl, ln), F32),
                        pltpu.SMEM((4 * tm,), jnp.int32),
                        pltpu.SemaphoreType.DMA((2,)), pltpu.SemaphoreType.DMA((2,))],
        compiler_params=pltpu.CompilerParams(dimension_semantics=("arbitrary",),
                                             vmem_limit_bytes=VMEM_LIMIT,
                                             disable_bounds_checks=True),
        name="moe_combine",
    )(table, h, gate, route, yg3)


def _moe(h, g, sh, sc, gate, w1, w3, w2, router, *, li, seq, tr, tm):
    t = h.shape[0]
    xn3, route, route_t, counts = _router(h, g, sh, sc, router, tm=tm, seq=seq)
    tile_expert, n_valid, row_start, pad_tiles = _route_plan(t, counts, tr)
    table = _position_table(route_t, row_start)
    xg3 = _dispatch(xn3, table, pad_tiles, tm=tm, tr=tr, n_rows=2 * t + N_EXPERTS * tr)
    yg3 = _experts(xg3, tile_expert, n_valid, w1, w3, w2, li=li, tr=tr)
    return _combine(h, gate, route, table, yg3, tm=tm, seq=seq)


def _rope_tables(seq):
    rows = seq // GRID_W
    row, col = jnp.meshgrid(jnp.arange(rows, dtype=F32), jnp.arange(GRID_W, dtype=F32), indexing='ij')
    n_freq = HEAD_DIM // 4
    inv_freq = ROPE_THETA ** (-jnp.arange(n_freq, dtype=F32) / n_freq)
    ang_r = row.reshape(-1, 1) * inv_freq
    ang_c = col.reshape(-1, 1) * inv_freq
    ang = jnp.concatenate([ang_r, ang_r, ang_c, ang_c], axis=-1)
    cos, sin = jnp.cos(ang), jnp.sin(ang)
    first = (jnp.arange(HEAD_DIM) % (2 * n_freq)) < n_freq
    sin_a = jnp.where(first, -sin, 0.0)
    sin_b = jnp.where(first, 0.0, sin)
    rep = LANES // HEAD_DIM
    return tuple(jnp.tile(t, (1, rep)) for t in (cos, sin_a, sin_b))


def kernel(x, c, ctx, c_ctx, w_ada, b_ada, norm1_g, norm2_g, w_in, conv_w, q_norm_g, k_norm_g,
           attn_sink, out_norm_conv_g, out_norm_attn_g, w_out, ffn_w1, ffn_w3, ffn_w2,
           moe_router, moe_w1, moe_w3, moe_w2):
    b, s, d = x.shape
    lc = ctx.shape[1]
    depth = w_ada.shape[0]
    assert d == D_MODEL and s % 512 == 0 and lc % 256 == 0 and b + 1 <= 8

    c8 = jnp.zeros((8, d), F32).at[0:b].set(c).at[b].set(c_ctx)
    mod = _modulation(c8, w_ada, b_ada)

    tables = _rope_tables(s)
    ids = jnp.arange(256)
    gm = (ids[:, None] // HEAD_DIM == ids[None, :] // HEAD_DIM).astype(BF16)
    scale = HEAD_DIM ** -0.5

    h = x.reshape(b * s, d)
    hc = ctx.reshape(b * lc, d)
    w_in_b, w_out_b = w_in.astype(BF16), w_out.astype(BF16)
    dense_w = [w.astype(BF16) for w in (ffn_w1, ffn_w3, ffn_w2)]
    moe_w = [w.astype(BF16) for w in (moe_w1, moe_w3, moe_w2)]
    for layer in range(depth):
        last = layer == depth - 1
        m = mod[layer]
        lat = [m[0:b, k * d:(k + 1) * d].reshape(b, 1, d) for k in range(6)]
        cx = [jnp.broadcast_to(m[b:b + 1, k * d:(k + 1) * d].reshape(1, 1, d), (b, 1, d)) for k in range(6)]
        g1n = norm1_g[layer].reshape(1, d)
        g2n = norm2_g[layer].reshape(1, d)
        gq = (jnp.tile(q_norm_g[layer], N_HEADS) * scale).reshape(1, D_ATTN)
        gk = jnp.tile(k_norm_g[layer], N_KV).reshape(1, LANES)
        gc = out_norm_conv_g[layer].reshape(1, D_CONV)
        ga = out_norm_attn_g[layer].reshape(1, D_ATTN)
        sink = attn_sink[layer]
        cw = conv_w[layer]

        cu, q, kv = _inproj(h, g1n, lat[0], lat[1], w_in_b, gq, gk, gm, tables, li=layer, tm=512, seq=s)
        cuc, qc, kvc = _inproj(hc, g1n, cx[0], cx[1], w_in_b, gq, gk, gm, None, li=layer, tm=lc, seq=lc)
        h = _mixer(h, cu, q, kv, kvc, sink, cw, gc, ga, w_out_b, lat[2],
                   li=layer, tq=512, seq=s, ctx_len=lc, windowed=True)
        if not last:
            hc = _mixer(hc, cuc, qc, None, kvc, sink, cw, gc, ga, w_out_b, cx[2],
                        li=layer, tq=lc, seq=lc, ctx_len=lc, windowed=False)

        i = layer // 2
        if layer % 2 == 0:
            w1, w3, w2 = dense_w
            router = None
            h = _dense_ffn(h, g2n, lat[3], lat[4], lat[5], w1, w3, w2, li=i, tm=512, seq=s, parts=2)
        else:
            w1, w3, w2 = moe_w
            router = jnp.zeros((d, LANES), F32).at[:, 0:N_EXPERTS].set(moe_router[i])
            h = _moe(h, g2n, lat[3], lat[4], lat[5], w1, w3, w2, router, li=i, seq=s, tr=512, tm=512)
        if not last:
            hc = _ffn(hc, g2n, cx[3], cx[4], cx[5], w1, w3, w2, router, li=i, tm=lc, seq=lc)
    return h.reshape(b, s, d)
```

```python
import functools

import jax
import jax.numpy as jnp
from jax import lax
from jax.experimental import pallas as pl
from jax.experimental.pallas import tpu as pltpu

D_MODEL = 1024
GRID_W = 64
HEAD_DIM = 64
D_CONV = 512
D_ATTN = 512
N_HEADS = 8
N_KV = 2
WINDOW = 128
ROPE_THETA = 10000.0
N_EXPERTS = 8
EPS = 1e-6
KV_OFF = 3 * D_CONV + D_ATTN
D_IN = KV_OFF + 2 * N_KV * HEAD_DIM
LANES = 128
VMEM_LIMIT = 48 * 1024 * 1024

F32 = jnp.float32
BF16 = jnp.bfloat16
NEG_INF = float("-inf")
LOG2E = 1.4426950408889634


def _params(n_axes):
    return pltpu.CompilerParams(dimension_semantics=("arbitrary",) * n_axes,
                                vmem_limit_bytes=VMEM_LIMIT)


def _sigmoid(x):
    return 1.0 / (1.0 + jnp.exp(-x))


def _mod_kernel(c_ref, w_ref, b_ref, o_ref):
    c = c_ref[...]
    s = c * _sigmoid(c)
    o_ref[...] = jnp.dot(s, w_ref[...], preferred_element_type=F32) + b_ref[...]


def _modulation(c8, w_ada, b_ada):
    depth, d, n = w_ada.shape
    tn = 1536
    return pl.pallas_call(
        _mod_kernel,
        grid=(depth, n // tn),
        in_specs=[pl.BlockSpec((8, d), lambda l, j: (0, 0)),
                  pl.BlockSpec((None, d, tn), lambda l, j: (l, 0, j)),
                  pl.BlockSpec((None, 1, tn), lambda l, j: (l, 0, j))],
        out_specs=pl.BlockSpec((None, 8, tn), lambda l, j: (l, 0, j)),
        out_shape=jax.ShapeDtypeStruct((depth, 8, n), F32),
        compiler_params=_params(2),
        name="adaln_mod",
    )(c8, w_ada, b_ada.reshape(depth, 1, n))


def _norm_mod(x, g, sh, sc):
    ms = jnp.mean(x * x, axis=-1, keepdims=True)
    return (x * lax.rsqrt(ms + EPS) * g) * (1.0 + sc) + sh


def _inproj_kernel(*refs, rope, parts):
    if rope:
        (h_ref, g_ref, sh_ref, sc_ref, w_ref, gq_ref, gk_ref, gm_ref,
         cos_ref, sa_ref, sb_ref, cu_ref, q_ref, kv_ref) = refs
    else:
        (h_ref, g_ref, sh_ref, sc_ref, w_ref, gq_ref, gk_ref, gm_ref,
         cu_ref, q_ref, kv_ref) = refs
    gm = gm_ref[...]
    gq = gq_ref[...]
    rows = h_ref.shape[0] // parts

    def head_norm(t, gain):
        w = t.shape[1]
        ss = jnp.dot((t * t).astype(BF16), gm[0:w, 0:w], preferred_element_type=F32)
        return t * lax.rsqrt(ss * (1.0 / HEAD_DIM) + EPS) * gain

    for p in range(parts):
        sl = slice(p * rows, (p + 1) * rows)

        def rot(t):
            if not rope:
                return t
            return (t * cos_ref[sl, :] + pltpu.roll(t, LANES - 16, 1) * sa_ref[sl, :]
                    + pltpu.roll(t, 16, 1) * sb_ref[sl, :])

        xn = _norm_mod(h_ref[sl, :], g_ref[...], sh_ref[...], sc_ref[...])
        y = jnp.dot(xn.astype(BF16), w_ref[...], preferred_element_type=F32)
        cu_ref[sl, 0:D_CONV] = y[:, 0:D_CONV].astype(BF16)
        cu_ref[sl, D_CONV:2 * D_CONV] = (y[:, D_CONV:2 * D_CONV] * y[:, 2 * D_CONV:3 * D_CONV]).astype(BF16)
        for j in range(D_ATTN // 256):
            lo = 3 * D_CONV + 256 * j
            qn = head_norm(y[:, lo:lo + 256], gq[:, 256 * j:256 * j + 256])
            for c in range(2):
                q_ref[sl, 256 * j + LANES * c:256 * j + LANES * (c + 1)] = rot(
                    qn[:, LANES * c:LANES * (c + 1)]).astype(BF16)
        k = rot(head_norm(y[:, KV_OFF:KV_OFF + LANES], gk_ref[...]))
        v = y[:, KV_OFF + LANES:KV_OFF + 2 * LANES]
        lo_half = lax.broadcasted_iota(jnp.int32, k.shape, 1) < HEAD_DIM
        for c, t in enumerate((k, v)):
            sw = pltpu.roll(t, HEAD_DIM, 1)
            kv_ref[sl, 2 * c * LANES:(2 * c + 1) * LANES] = jnp.where(lo_half, t, sw).astype(BF16)
            kv_ref[sl, (2 * c + 1) * LANES:(2 * c + 2) * LANES] = jnp.where(lo_half, sw, t).astype(BF16)


def _inproj(h, g, sh, sc, w, gq, gk, gm, tables, *, li, tm, seq):
    t, d = h.shape
    tpb = seq // tm
    rope = tables is not None
    row = lambda i: (i, 0)
    fix = lambda i: (0, 0)
    mod = lambda i: (i // tpb, 0, 0)
    in_specs = [pl.BlockSpec((tm, d), row),
                pl.BlockSpec((1, d), fix),
                pl.BlockSpec((None, 1, d), mod),
                pl.BlockSpec((None, 1, d), mod),
                pl.BlockSpec((None, d, D_IN), lambda i: (li, 0, 0)),
                pl.BlockSpec((1, D_ATTN), fix),
                pl.BlockSpec((1, LANES), fix),
                pl.BlockSpec((256, 256), fix)]
    args = [h, g, sh, sc, w, gq, gk, gm]
    if rope:
        in_specs += [pl.BlockSpec((tm, LANES), lambda i: (i % tpb, 0))] * 3
        args += list(tables)
    return pl.pallas_call(
        functools.partial(_inproj_kernel, rope=rope, parts=2),
        grid=(t // tm,),
        in_specs=in_specs,
        out_specs=[pl.BlockSpec((tm, 2 * D_CONV), row),
                   pl.BlockSpec((tm, D_ATTN), row),
                   pl.BlockSpec((tm, 4 * LANES), row)],
        out_shape=[jax.ShapeDtypeStruct((t, 2 * D_CONV), BF16),
                   jax.ShapeDtypeStruct((t, D_ATTN), BF16),
                   jax.ShapeDtypeStruct((t, 4 * LANES), BF16)],
        compiler_params=_params(1),
        name="inproj_rope" if rope else "inproj_ctx",
    )(*args)


_NT = (((1,), (1,)), ((), ()))


def _mixer_kernel(*refs, tq, windowed):
    if windowed:
        (sink_ref, h_ref, cu_ref, cup_ref, cun_ref, q_ref, kvp_ref, kv_ref, kvn_ref, kvc_ref,
         cw_ref, gc_ref, ga_ref, wo_ref, g1_ref, out_ref, kw_ref, ya_ref) = refs
    else:
        (sink_ref, h_ref, cu_ref, q_ref, kvc_ref,
         cw_ref, gc_ref, ga_ref, wo_ref, g1_ref, out_ref, ya_ref) = refs
    i = pl.program_id(1)
    nt = pl.num_programs(1)
    nsub = tq // WINDOW

    cu = cu_ref[...]
    bg = cu[:, 0:D_CONV].astype(F32)
    u = cu[:, D_CONV:2 * D_CONV].astype(F32)
    rows = lax.broadcasted_iota(jnp.int32, (tq, 1), 0)
    if windowed:
        up_row = cup_ref[:, D_CONV:2 * D_CONV].astype(F32)[15:16, :]
        un_row = cun_ref[:, D_CONV:2 * D_CONV].astype(F32)[0:1, :]
        up_row = jnp.where(i > 0, up_row, 0.0)
        un_row = jnp.where(i < nt - 1, un_row, 0.0)
    else:
        up_row = jnp.zeros((1, D_CONV), F32)
        un_row = jnp.zeros((1, D_CONV), F32)
    u_prev = jnp.where(rows == 0, up_row, pltpu.roll(u, 1, 0))
    u_next = jnp.where(rows == tq - 1, un_row, pltpu.roll(u, tq - 1, 0))
    cw = cw_ref[...]
    yc = bg * (cw[0:1, :] * u_prev + cw[1:2, :] * u + cw[2:3, :] * u_next)
    yc = yc * lax.rsqrt(jnp.mean(yc * yc, axis=-1, keepdims=True) + EPS) * gc_ref[...]

    if windowed:
        kw_ref[0:WINDOW, :] = kvp_ref[...]
        kw_ref[WINDOW:WINDOW + tq, :] = kv_ref[...]
        kw_ref[WINDOW + tq:2 * WINDOW + tq, :] = kvn_ref[...]
    lane_lo = lax.broadcasted_iota(jnp.int32, (WINDOW, LANES), 1) < HEAD_DIM
    kvc = kvc_ref[...]
    gqa = N_HEADS // N_KV
    head_of_row = lax.broadcasted_iota(jnp.int32, (gqa * WINDOW, 1), 0) // WINDOW
    ones_c = jnp.ones((kvc.shape[0], LANES), BF16)
    if windowed:
        kk = lax.broadcasted_iota(jnp.int32, (WINDOW, WINDOW), 1)
        qq = lax.broadcasted_iota(jnp.int32, (WINDOW, WINDOW), 0)
        keep_before = kk >= qq
        keep_after = kk <= qq
        ones_w = jnp.ones((3 * WINDOW, LANES), BF16)

    def sub_block(s):
        r0 = s * WINDOW
        if windowed:
            kwin = kw_ref[pl.ds(r0, 3 * WINDOW), :]
            before, after = keep_before, keep_after
            if s == 0:
                before = before & (i > 0)
            if s == nsub - 1:
                after = after & (i < nt - 1)
        for grp in range(N_KV):
            rows, sk = [], jnp.zeros((gqa * WINDOW, 1), F32)
            for hh in range(gqa):
                head = gqa * grp + hh
                qp = q_ref[pl.ds(r0, WINDOW), LANES * (head // 2):LANES * (head // 2 + 1)]
                zero = jnp.zeros_like(qp)
                rows.append(jnp.where(lane_lo, zero, qp) if head % 2 else jnp.where(lane_lo, qp, zero))
                sk = jnp.where(head_of_row == hh, sink_ref[head] * LOG2E, sk)
            q4 = jnp.concatenate(rows, axis=0)
            s_c = lax.dot_general(q4, kvc[:, LANES * grp:LANES * (grp + 1)], _NT, preferred_element_type=F32)
            m = jnp.maximum(jnp.max(s_c, axis=-1, keepdims=True), sk)
            if windowed:
                s_w = lax.dot_general(q4, kwin[:, LANES * grp:LANES * (grp + 1)], _NT,
                                      preferred_element_type=F32).reshape(gqa, WINDOW, 3 * WINDOW)
                s_w = jnp.concatenate(
                    [jnp.where(before[None], s_w[:, :, 0:WINDOW], NEG_INF),
                     s_w[:, :, WINDOW:2 * WINDOW],
                     jnp.where(after[None], s_w[:, :, 2 * WINDOW:3 * WINDOW], NEG_INF)],
                    axis=-1).reshape(gqa * WINDOW, 3 * WINDOW)
                m = jnp.maximum(m, jnp.max(s_w, axis=-1, keepdims=True))
            v_c = jnp.concatenate([kvc[:, LANES * (2 + grp):LANES * (3 + grp)], ones_c], axis=1)
            o = jnp.dot(jnp.exp2(s_c - m).astype(BF16), v_c, preferred_element_type=F32)
            if windowed:
                v_w = jnp.concatenate([kwin[:, LANES * (2 + grp):LANES * (3 + grp)], ones_w], axis=1)
                o = o + jnp.dot(jnp.exp2(s_w - m).astype(BF16), v_w, preferred_element_type=F32)
            o = o[:, 0:LANES] / (o[:, LANES:2 * LANES] + jnp.exp2(sk - m))
            for pr in range(gqa // 2):
                pair = (gqa // 2) * grp + pr
                even = o[2 * pr * WINDOW:(2 * pr + 1) * WINDOW]
                odd = o[(2 * pr + 1) * WINDOW:(2 * pr + 2) * WINDOW]
                ya_ref[pl.ds(r0, WINDOW), LANES * pair:LANES * (pair + 1)] = jnp.where(lane_lo, even, odd)

    part = 2 * WINDOW
    ycb = yc.astype(BF16)
    for p in range(tq // part):
        sub_block(2 * p)
        sub_block(2 * p + 1)
        sl = slice(p * part, (p + 1) * part)
        ya = ya_ref[sl, :]
        ya = ya * lax.rsqrt(jnp.mean(ya * ya, axis=-1, keepdims=True) + EPS) * ga_ref[...]
        y = (jnp.dot(ycb[sl], wo_ref[0:D_CONV, :], preferred_element_type=F32)
             + jnp.dot(ya.astype(BF16), wo_ref[D_CONV:2 * D_CONV, :], preferred_element_type=F32))
        out_ref[sl, :] = h_ref[sl, :] + g1_ref[...] * y


def _mixer(h, cu, q, kv, kvc, sink, cw, gc, ga, wo, g1, *, li, tq, seq, ctx_len, windowed):
    t, d = h.shape
    nt = seq // tq
    nb = t // seq
    row = lambda b, i: (b * nt + i, 0)
    fix = lambda b, i: (0, 0)
    smem = pl.BlockSpec(memory_space=pltpu.SMEM)
    tail = [pl.BlockSpec((3, D_CONV), fix),
            pl.BlockSpec((1, D_CONV), fix),
            pl.BlockSpec((1, D_ATTN), fix),
            pl.BlockSpec((None, d, d), lambda b, i: (li, 0, 0)),
            pl.BlockSpec((None, 1, d), lambda b, i: (b, 0, 0))]
    ctx_spec = pl.BlockSpec((ctx_len, 4 * LANES), lambda b, i: (b, 0))
    if windowed:
        r16 = tq // 16
        n16 = t // 16
        rw = tq // WINDOW
        nw = t // WINDOW
        in_specs = [smem,
                    pl.BlockSpec((tq, d), row),
                    pl.BlockSpec((tq, 2 * D_CONV), row),
                    pl.BlockSpec((16, 2 * D_CONV), lambda b, i: (jnp.maximum((b * nt + i) * r16 - 1, 0), 0)),
                    pl.BlockSpec((16, 2 * D_CONV), lambda b, i: (jnp.minimum((b * nt + i + 1) * r16, n16 - 1), 0)),
                    pl.BlockSpec((tq, D_ATTN), row),
                    pl.BlockSpec((WINDOW, 4 * LANES), lambda b, i: (jnp.maximum((b * nt + i) * rw - 1, 0), 0)),
                    pl.BlockSpec((tq, 4 * LANES), row),
                    pl.BlockSpec((WINDOW, 4 * LANES), lambda b, i: (jnp.minimum((b * nt + i + 1) * rw, nw - 1), 0)),
                    ctx_spec] + tail
        args = [sink, h, cu, cu, cu, q, kv, kv, kv, kvc, cw, gc, ga, wo, g1]
        scratch = [pltpu.VMEM((tq + 2 * WINDOW, 4 * LANES), BF16), pltpu.VMEM((tq, D_ATTN), F32)]
    else:
        in_specs = [smem,
                    pl.BlockSpec((tq, d), row),
                    pl.BlockSpec((tq, 2 * D_CONV), row),
                    pl.BlockSpec((tq, D_ATTN), row),
                    ctx_spec] + tail
        args = [sink, h, cu, q, kvc, cw, gc, ga, wo, g1]
        scratch = [pltpu.VMEM((tq, D_ATTN), F32)]
    return pl.pallas_call(
        functools.partial(_mixer_kernel, tq=tq, windowed=windowed),
        grid=(nb, nt),
        in_specs=in_specs,
        out_specs=pl.BlockSpec((tq, d), row),
        out_shape=jax.ShapeDtypeStruct((t, d), F32),
        scratch_shapes=scratch,
        compiler_params=_params(2),
        name="mixer_win" if windowed else "mixer_ctx",
    )(*args)


def _ffn_kernel(*refs, moe):
    if moe:
        (h_ref, g_ref, sh_ref, sc_ref, gate_ref, r_ref, w13_ref, w2_ref,
         out_ref, xn_ref, acc_ref, comb_ref) = refs
    else:
        (h_ref, g_ref, sh_ref, sc_ref, gate_ref, w1_ref, w3_ref, w2_ref,
         out_ref, xn_ref, acc_ref) = refs
    e = pl.program_id(1)
    tm = h_ref.shape[0]

    @pl.when(e == 0)
    def _():
        xn = _norm_mod(h_ref[...], g_ref[...], sh_ref[...], sc_ref[...])
        xn_ref[...] = xn.astype(BF16)
        acc_ref[...] = jnp.zeros_like(acc_ref)
        if moe:
            i1, i2, g1, g2 = _top2(jnp.dot(xn, r_ref[...], preferred_element_type=F32))
            lane = lax.broadcasted_iota(jnp.int32, (tm, LANES), 1)
            comb_ref[...] = jnp.where(lane == i1, g1, 0.0) + jnp.where(lane == i2, g2, 0.0)

    xb = xn_ref[...]
    if moe:
        h13 = jnp.dot(xb, w13_ref[...], preferred_element_type=F32)
        fe = h13.shape[1] // 2
        h1, h3 = h13[:, 0:fe], h13[:, fe:2 * fe]
    else:
        h1 = jnp.dot(xb, w1_ref[...], preferred_element_type=F32)
        h3 = jnp.dot(xb, w3_ref[...], preferred_element_type=F32)
    a = h1 * _sigmoid(h1) * h3
    if moe:
        lane = lax.broadcasted_iota(jnp.int32, (tm, LANES), 1)
        a = a * jnp.sum(jnp.where(lane == e, comb_ref[...], 0.0), axis=-1, keepdims=True)
    acc_ref[...] += jnp.dot(a.astype(BF16), w2_ref[...], preferred_element_type=F32)

    @pl.when(e == pl.num_programs(1) - 1)
    def _():
        out_ref[...] = h_ref[...] + gate_ref[...] * acc_ref[...]


def _ffn(h, g, sh, sc, gate, w1, w3, w2, router, *, li, tm, seq):
    t, d = h.shape
    tpb = seq // tm
    moe = router is not None
    row = lambda i, e: (i, 0)
    fix = lambda i, e: (0, 0)
    mod = lambda i, e: (i // tpb, 0, 0)
    in_specs = [pl.BlockSpec((tm, d), row),
                pl.BlockSpec((1, d), fix),
                pl.BlockSpec((None, 1, d), mod),
                pl.BlockSpec((None, 1, d), mod),
                pl.BlockSpec((None, 1, d), mod)]
    args = [h, g, sh, sc, gate]
    scratch = [pltpu.VMEM((tm, d), BF16), pltpu.VMEM((tm, d), F32)]
    if moe:
        _, ne, _, fe2 = w1.shape
        in_specs += [pl.BlockSpec((d, LANES), fix),
                     pl.BlockSpec((None, None, d, fe2), lambda i, e: (li, e, 0, 0)),
                     pl.BlockSpec((None, None, fe2 // 2, d), lambda i, e: (li, e, 0, 0))]
        args += [router, w1, w2]
        scratch += [pltpu.VMEM((tm, LANES), F32)]
    else:
        ne = 2
        fe = w1.shape[2] // ne
        in_specs += [pl.BlockSpec((None, d, fe), lambda i, e: (li, 0, e)),
                     pl.BlockSpec((None, d, fe), lambda i, e: (li, 0, e)),
                     pl.BlockSpec((None, fe, d), lambda i, e: (li, e, 0))]
        args += [w1, w3, w2]
    return pl.pallas_call(
        functools.partial(_ffn_kernel, moe=moe),
        grid=(t // tm, ne),
        in_specs=in_specs,
        out_specs=pl.BlockSpec((tm, d), row),
        out_shape=jax.ShapeDtypeStruct((t, d), F32),
        scratch_shapes=scratch,
        compiler_params=_params(2),
        name="ffn_moe" if moe else "ffn_dense",
    )(*args)


def _dense_ffn_kernel(h_ref, g_ref, sh_ref, sc_ref, gate_ref, w1_ref, w3_ref, w2_ref, out_ref, *, parts):
    rows = h_ref.shape[0] // parts
    for p in range(parts):
        sl = slice(p * rows, (p + 1) * rows)
        hp = h_ref[sl, :]
        xb = _norm_mod(hp, g_ref[...], sh_ref[...], sc_ref[...]).astype(BF16)
        h1 = jnp.dot(xb, w1_ref[...], preferred_element_type=F32)
        h3 = jnp.dot(xb, w3_ref[...], preferred_element_type=F32)
        a = (h1 * _sigmoid(h1) * h3).astype(BF16)
        y = jnp.dot(a, w2_ref[...], preferred_element_type=F32)
        out_ref[sl, :] = hp + gate_ref[...] * y


def _dense_ffn(h, g, sh, sc, gate, w1, w3, w2, *, li, tm, seq, parts):
    t, d = h.shape
    f = w1.shape[2]
    wfix = lambda i: (li, 0, 0)
    tpb = seq // tm
    row = lambda i: (i, 0)
    fix = lambda i: (0, 0)
    mod = lambda i: (i // tpb, 0, 0)
    once = pl.Buffered(1)
    return pl.pallas_call(
        functools.partial(_dense_ffn_kernel, parts=parts),
        grid=(t // tm,),
        in_specs=[pl.BlockSpec((tm, d), row),
                  pl.BlockSpec((1, d), fix),
                  pl.BlockSpec((None, 1, d), mod),
                  pl.BlockSpec((None, 1, d), mod),
                  pl.BlockSpec((None, 1, d), mod),
                  pl.BlockSpec((None, d, f), wfix, pipeline_mode=once),
                  pl.BlockSpec((None, d, f), wfix, pipeline_mode=once),
                  pl.BlockSpec((None, f, d), wfix, pipeline_mode=once)],
        out_specs=pl.BlockSpec((tm, d), row),
        out_shape=jax.ShapeDtypeStruct((t, d), F32),
        compiler_params=_params(1),
        name="ffn_dense",
    )(h, g, sh, sc, gate, w1, w3, w2)


def _top2(logits):
    lane = lax.broadcasted_iota(jnp.int32, logits.shape, 1)
    lg = jnp.where(lane < N_EXPERTS, logits, NEG_INF)
    m1 = jnp.max(lg, axis=-1, keepdims=True)
    i1 = jnp.min(jnp.where(lg == m1, lane, LANES), axis=-1, keepdims=True)
    lg2 = jnp.where(lane == i1, NEG_INF, lg)
    m2 = jnp.max(lg2, axis=-1, keepdims=True)
    i2 = jnp.min(jnp.where(lg2 == m2, lane, LANES), axis=-1, keepdims=True)
    e2 = jnp.exp(m2 - m1)
    return i1, i2, 1.0 / (1.0 + e2), e2 / (1.0 + e2)


def _router_kernel(h_ref, g_ref, sh_ref, sc_ref, r_ref, tri_ref, xn_ref, route_ref, rt_ref, cnt_ref,
                   base_ref):
    tm = h_ref.shape[0]

    @pl.when(pl.program_id(0) == 0)
    def _():
        base_ref[...] = jnp.zeros_like(base_ref)

    xn = _norm_mod(h_ref[...], g_ref[...], sh_ref[...], sc_ref[...])
    xn_ref[...] = xn.reshape(xn_ref.shape)
    i1, i2, g1, g2 = _top2(jnp.dot(xn, r_ref[...], preferred_element_type=F32))
    lane = lax.broadcasted_iota(jnp.int32, (tm, LANES), 1)
    hit1 = lane == i1
    hit2 = lane == i2
    chosen = jnp.where(jnp.logical_or(hit1, hit2), 1.0, 0.0)
    before = base_ref[...] + jnp.dot(tri_ref[...], chosen.astype(BF16), preferred_element_type=F32)
    r1 = jnp.sum(jnp.where(hit1, before, 0.0), axis=-1, keepdims=True)
    r2 = jnp.sum(jnp.where(hit2, before, 0.0), axis=-1, keepdims=True)
    base_ref[...] += jnp.sum(chosen, axis=0, keepdims=True)
    cnt_ref[...] = base_ref[...]
    fields = (i1.astype(F32), i2.astype(F32), g1, g2, r1, r2)
    route = jnp.zeros((tm, LANES), F32)
    for k, f in enumerate(fields):
        route = jnp.where(lane == k, f, route)
    route_ref[...] = route
    rt_ref[...] = route.T[0:8, :]


def _router(h, g, sh, sc, router, *, tm, seq):
    t, d = h.shape
    tpb = seq // tm
    row = lambda i: (i, 0)
    fix = lambda i: (0, 0)
    mod = lambda i: (i // tpb, 0, 0)
    ids = jnp.arange(tm)
    tri = (ids[None, :] < ids[:, None]).astype(BF16)
    return pl.pallas_call(
        _router_kernel,
        grid=(t // tm,),
        in_specs=[pl.BlockSpec((tm, d), row), pl.BlockSpec((1, d), fix),
                  pl.BlockSpec((None, 1, d), mod), pl.BlockSpec((None, 1, d), mod),
                  pl.BlockSpec((d, LANES), fix), pl.BlockSpec((tm, tm), fix)],
        out_specs=[pl.BlockSpec((tm, d // LANES, LANES), lambda i: (i, 0, 0)),
                   pl.BlockSpec((tm, LANES), row),
                   pl.BlockSpec((None, 8, tm), lambda i: (i, 0, 0)),
                   pl.BlockSpec((1, LANES), fix)],
        out_shape=[jax.ShapeDtypeStruct((t, d // LANES, LANES), F32),
                   jax.ShapeDtypeStruct((t, LANES), F32),
                   jax.ShapeDtypeStruct((t // tm, 8, tm), F32),
                   jax.ShapeDtypeStruct((1, LANES), F32)],
        scratch_shapes=[pltpu.VMEM((1, LANES), F32)],
        compiler_params=_params(1),
        name="moe_router",
    )(h, g, sh, sc, router, tri)


def _route_plan(t, counts, tr):
    counts = counts[0, 0:N_EXPERTS].astype(jnp.int32)
    tiles = (counts + tr - 1) // tr
    tile_end = jnp.cumsum(tiles)
    tile_start = tile_end - tiles
    experts = jnp.arange(N_EXPERTS, dtype=jnp.int32)

    nt = 2 * t // tr + N_EXPERTS
    tid = jnp.arange(nt, dtype=jnp.int32)
    tile_expert = jnp.minimum(jnp.sum((tid[:, None] >= tile_end[None, :]).astype(jnp.int32), axis=1),
                              N_EXPERTS - 1)
    in_tile = tid - jnp.sum(jnp.where(tile_expert[:, None] == experts[None, :], tile_start[None, :], 0), axis=1)
    own = jnp.sum(jnp.where(tile_expert[:, None] == experts[None, :], counts[None, :], 0), axis=1)
    n_valid = jnp.where(tid < tile_end[-1], jnp.clip(own - in_tile * tr, 0, tr), 0)
    tail = tile_end[-1] + experts
    pad_tiles = jnp.concatenate([jnp.where(tiles > 0, tile_end - 1, -1), jnp.where(tail < nt, tail, -1)])
    return tile_expert, n_valid, tile_start * tr, pad_tiles


def _table_kernel(start_ref, rt_ref, tab_ref, *, nt):
    tm = rt_ref.shape[1]

    def position(e, r):
        start = jnp.zeros_like(r)
        for k in range(N_EXPERTS):
            start = jnp.where(e == float(k), start_ref[k].astype(F32), start)
        return (start + r).astype(jnp.int32)

    live = pl.program_id(0) < nt
    rt = rt_ref[...]
    for c in range(2):
        pos = position(rt[c:c + 1, :], rt[4 + c:5 + c, :])
        tab_ref[:, c * tm:(c + 1) * tm] = jnp.where(live, pos, 0)


def _position_table(route_t, row_start):
    nt, fields, tm = route_t.shape
    grid_spec = pltpu.PrefetchScalarGridSpec(
        num_scalar_prefetch=1,
        grid=(nt + 2,),
        in_specs=[pl.BlockSpec((None, fields, tm), lambda j, st: (jnp.minimum(j, nt - 1), 0, 0))],
        out_specs=pl.BlockSpec((None, 1, 2 * tm), lambda j, st: (j, 0, 0)))
    return pl.pallas_call(
        functools.partial(_table_kernel, nt=nt),
        grid_spec=grid_spec,
        out_shape=jax.ShapeDtypeStruct((nt + 2, 1, 2 * tm), jnp.int32),
        compiler_params=_params(1),
        name="moe_table",
    )(row_start, route_t)


def _row_copies(idx_smem, s_idx, tm, make):
    base = s_idx * (2 * tm)

    def body(r, c):
        make(r, idx_smem[base + r], idx_smem[base + tm + r])
        return c

    lax.fori_loop(0, tm, body, 0, unroll=8)


def _dispatch_kernel(zt_ref, idx_hbm, xn_hbm, xg_hbm, idx_smem, zbuf, xbuf,
                     sem_d, sem_i, sem_z, sem_in, *, tm, nt, tr):
    j = pl.program_id(0)
    slot = j % 2
    other = 1 - slot
    cur = j % 3
    nxt = (j + 1) % 3

    def in_copy(tile, s):
        return pltpu.make_async_copy(xn_hbm.at[pl.ds(tile * tm, tm)], xbuf.at[s], sem_in.at[s])

    @pl.when(j == 0)
    def _():
        zbuf[...] = jnp.zeros_like(zbuf)
        for k in range(zt_ref.shape[0]):
            fill = pltpu.make_async_copy(zbuf, xg_hbm.at[pl.ds(jnp.maximum(zt_ref[k], 0) * tr, tr)], sem_z)
            pl.when(zt_ref[k] >= 0)(fill.start)
        for k in range(zt_ref.shape[0]):
            fill = pltpu.make_async_copy(zbuf, xg_hbm.at[pl.ds(0, tr)], sem_z)
            pl.when(zt_ref[k] >= 0)(fill.wait)

    def idx_copy(row, s):
        return pltpu.make_async_copy(idx_hbm.at[row, 0], idx_smem.at[pl.ds(s * 2 * tm, 2 * tm)], sem_i.at[s])

    def wait_rows(s):
        for _ in range(2):
            pltpu.make_async_copy(xbuf.at[s], xg_hbm.at[pl.ds(0, tm)], sem_d.at[s]).wait()

    @pl.when(j == 0)
    def _():
        idx_copy(0, 0).start()
        in_copy(0, 0).start()

    @pl.when(j >= 2)
    def _():
        wait_rows(nxt)

    @pl.when(j + 1 < nt)
    def _():
        in_copy(j + 1, nxt).start()

    idx_copy(j, slot).wait()
    idx_copy(j + 1, other).start()
    in_copy(j, cur).wait()

    def make(r, p1, p2):
        src = xbuf.at[cur, r]
        pltpu.make_async_copy(src, xg_hbm.at[p1], sem_d.at[cur]).start(priority=0)
        pltpu.make_async_copy(src, xg_hbm.at[p2], sem_d.at[cur]).start(priority=1)

    _row_copies(idx_smem, slot, tm, make)

    @pl.when(j == nt - 1)
    def _():
        if nt > 1:
            wait_rows((nt - 2) % 3)
        wait_rows((nt - 1) % 3)
        idx_copy(j + 1, other).wait()


def _dispatch(xn3, table, pad_tiles, *, tm, tr, n_rows):
    t = xn3.shape[0]
    nt = t // tm
    any_spec = pl.BlockSpec(memory_space=pl.ANY)
    grid_spec = pltpu.PrefetchScalarGridSpec(
        num_scalar_prefetch=1,
        grid=(nt,),
        in_specs=[any_spec, any_spec],
        out_specs=any_spec,
        scratch_shapes=[pltpu.SMEM((4 * tm,), jnp.int32),
                        pltpu.VMEM((tr,) + xn3.shape[1:], F32),
                        pltpu.VMEM((3, tm) + xn3.shape[1:], F32),
                        pltpu.SemaphoreType.DMA((3,)), pltpu.SemaphoreType.DMA((2,)),
                        pltpu.SemaphoreType.DMA, pltpu.SemaphoreType.DMA((3,))])
    return pl.pallas_call(
        functools.partial(_dispatch_kernel, tm=tm, nt=nt, tr=tr),
        grid_spec=grid_spec,
        out_shape=jax.ShapeDtypeStruct((n_rows,) + xn3.shape[1:], F32),
        compiler_params=pltpu.CompilerParams(dimension_semantics=("arbitrary",),
                                             vmem_limit_bytes=VMEM_LIMIT,
                                             disable_bounds_checks=True),
        name="moe_dispatch",
    )(pad_tiles, table, xn3)


def _expert_kernel(te_ref, nv_ref, x_ref, w13_ref, w2_ref, y_ref):
    tr = x_ref.shape[0]
    nv = nv_ref[pl.program_id(0)]

    @pl.when(nv > 0)
    def _():
        half = tr // 2
        for p in range(2):
            x = x_ref[p * half:(p + 1) * half].reshape(half, D_MODEL).astype(BF16)
            h13 = jnp.dot(x, w13_ref[...], preferred_element_type=F32)
            fe = h13.shape[1] // 2
            h1, h3 = h13[:, 0:fe], h13[:, fe:2 * fe]
            a = (h1 * _sigmoid(h1) * h3).astype(BF16)
            y = jnp.dot(a, w2_ref[...], preferred_element_type=F32)
            y_ref[p * half:(p + 1) * half] = y.reshape((half,) + y_ref.shape[1:])

    @pl.when(nv == 0)
    def _():
        y_ref[...] = jnp.zeros_like(y_ref)


def _experts(xg3, tile_expert, n_valid, w13, w2, *, li, tr):
    n_rows, sl, ln = xg3.shape
    d = sl * ln
    fe = w2.shape[2]
    rows = lambda j, te, nv: (j, 0, 0)
    wsel = lambda j, te, nv: (li, te[j], 0, 0)
    grid_spec = pltpu.PrefetchScalarGridSpec(
        num_scalar_prefetch=2,
        grid=(n_rows // tr,),
        in_specs=[pl.BlockSpec((tr, sl, ln), rows),
                  pl.BlockSpec((None, None, d, 2 * fe), wsel),
                  pl.BlockSpec((None, None, fe, d), wsel)],
        out_specs=pl.BlockSpec((tr, sl, ln), rows))
    return pl.pallas_call(
        _expert_kernel,
        grid_spec=grid_spec,
        out_shape=jax.ShapeDtypeStruct(xg3.shape, F32),
        compiler_params=_params(1),
        name="moe_experts",
    )(tile_expert, n_valid, xg3, w13, w2)


def _combine_kernel(idx_hbm, h_ref, gate_ref, route_ref, yg_hbm, out_ref,
                    y1buf, y2buf, idx_smem, sem_y, sem_i, *, tm, nt):
    j = pl.program_id(0)
    slot = j % 2
    other = 1 - slot

    def idx_copy(row, s):
        return pltpu.make_async_copy(idx_hbm.at[row, 0], idx_smem.at[pl.ds(s * 2 * tm, 2 * tm)], sem_i.at[s])

    def fetch(s_idx, s_buf):
        def make(r, p1, p2):
            pltpu.make_async_copy(yg_hbm.at[p1], y1buf.at[s_buf, r], sem_y.at[s_buf]).start(priority=0)
            pltpu.make_async_copy(yg_hbm.at[p2], y2buf.at[s_buf, r], sem_y.at[s_buf]).start(priority=1)

        _row_copies(idx_smem, s_idx, tm, make)

    def wait_rows(s):
        pltpu.make_async_copy(yg_hbm.at[pl.ds(0, tm)], y1buf.at[s], sem_y.at[s]).wait()
        pltpu.make_async_copy(yg_hbm.at[pl.ds(0, tm)], y2buf.at[s], sem_y.at[s]).wait()

    @pl.when(j == 0)
    def _():
        first = idx_copy(0, 0)
        first.start()
        first.wait()
        fetch(0, 0)
        idx_copy(1, 1).start()

    idx_copy(j + 1, other).wait()
    fetch(other, other)
    idx_copy(j + 2, slot).start()
    wait_rows(slot)
    rt = route_ref[...]
    y1 = y1buf[slot].reshape(tm, D_MODEL)
    y2 = y2buf[slot].reshape(tm, D_MODEL)
    out_ref[...] = h_ref[...] + gate_ref[...] * (rt[:, 2:3] * y1 + rt[:, 3:4] * y2)

    @pl.when(j == nt - 1)
    def _():
        wait_rows(other)
        idx_copy(j + 2, slot).wait()


def _combine(h, gate, route, table, yg3, *, tm, seq):
    t, d = h.shape
    tpb = seq // tm
    nt = t // tm
    sl, ln = yg3.shape[1:]
    row = lambda i: (i, 0)
    any_spec = pl.BlockSpec(memory_space=pl.ANY)
    return pl.pallas_call(
        functools.partial(_combine_kernel, tm=tm, nt=nt),
        grid=(nt,),
        in_specs=[any_spec,
                  pl.BlockSpec((tm, d), row),
                  pl.BlockSpec((None, 1, d), lambda i: (i // tpb, 0, 0)),
                  pl.BlockSpec((tm, LANES), row),
                  any_spec],
        out_specs=pl.BlockSpec((tm, d), row),
        out_shape=jax.ShapeDtypeStruct((t, d), F32),
        scratch_shapes=[pltpu.VMEM((2, tm, sl, ln), F32), pltpu.VMEM((2, tm, sl, ln), F32),
                        pltpu.SMEM((4 * tm,), jnp.int32),
                        pltpu.SemaphoreType.DMA((2,)), pltpu.SemaphoreType.DMA((2,))],
        compiler_params=pltpu.CompilerParams(dimension_semantics=("arbitrary",),
                                             vmem_limit_bytes=VMEM_LIMIT,
                                             disable_bounds_checks=True),
        name="moe_combine",
    )(table, h, gate, route, yg3)


def _moe(h, g, sh, sc, gate, w13, w2, router, *, li, seq, tr, tm):
    t = h.shape[0]
    xn3, route, route_t, counts = _router(h, g, sh, sc, router, tm=tm, seq=seq)
    tile_expert, n_valid, row_start, pad_tiles = _route_plan(t, counts, tr)
    table = _position_table(route_t, row_start)
    xg3 = _dispatch(xn3, table, pad_tiles, tm=tm, tr=tr, n_rows=2 * t + N_EXPERTS * tr)
    yg3 = _experts(xg3, tile_expert, n_valid, w13, w2, li=li, tr=tr)
    return _combine(h, gate, route, table, yg3, tm=tm, seq=seq)


def _rope_tables(seq):
    rows = seq // GRID_W
    row, col = jnp.meshgrid(jnp.arange(rows, dtype=F32), jnp.arange(GRID_W, dtype=F32), indexing='ij')
    n_freq = HEAD_DIM // 4
    inv_freq = ROPE_THETA ** (-jnp.arange(n_freq, dtype=F32) / n_freq)
    ang_r = row.reshape(-1, 1) * inv_freq
    ang_c = col.reshape(-1, 1) * inv_freq
    ang = jnp.concatenate([ang_r, ang_r, ang_c, ang_c], axis=-1)
    cos, sin = jnp.cos(ang), jnp.sin(ang)
    first = (jnp.arange(HEAD_DIM) % (2 * n_freq)) < n_freq
    sin_a = jnp.where(first, -sin, 0.0)
    sin_b = jnp.where(first, 0.0, sin)
    rep = LANES // HEAD_DIM
    return tuple(jnp.tile(t, (1, rep)) for t in (cos, sin_a, sin_b))


def kernel(x, c, ctx, c_ctx, w_ada, b_ada, norm1_g, norm2_g, w_in, conv_w, q_norm_g, k_norm_g,
           attn_sink, out_norm_conv_g, out_norm_attn_g, w_out, ffn_w1, ffn_w3, ffn_w2,
           moe_router, moe_w1, moe_w3, moe_w2):
    b, s, d = x.shape
    lc = ctx.shape[1]
    depth = w_ada.shape[0]
    assert d == D_MODEL and s % 512 == 0 and lc % 256 == 0 and b + 1 <= 8

    c8 = jnp.zeros((8, d), F32).at[0:b].set(c).at[b].set(c_ctx)
    mod = _modulation(c8, w_ada, b_ada)

    tables = _rope_tables(s)
    ids = jnp.arange(256)
    gm = (ids[:, None] // HEAD_DIM == ids[None, :] // HEAD_DIM).astype(BF16)
    scale = HEAD_DIM ** -0.5 * LOG2E

    h = x.reshape(b * s, d)
    hc = ctx.reshape(b * lc, d)
    w_in_b, w_out_b = w_in.astype(BF16), w_out.astype(BF16)
    dense_w = [w.astype(BF16) for w in (ffn_w1, ffn_w3, ffn_w2)]
    moe_w13 = jnp.concatenate([moe_w1.astype(BF16), moe_w3.astype(BF16)], axis=-1)
    moe_w2b = moe_w2.astype(BF16)
    for layer in range(depth):
        last = layer == depth - 1
        m = mod[layer]
        lat = [m[0:b, k * d:(k + 1) * d].reshape(b, 1, d) for k in range(6)]
        cx = [jnp.broadcast_to(m[b:b + 1, k * d:(k + 1) * d].reshape(1, 1, d), (b, 1, d)) for k in range(6)]
        g1n = norm1_g[layer].reshape(1, d)
        g2n = norm2_g[layer].reshape(1, d)
        gq = (jnp.tile(q_norm_g[layer], N_HEADS) * scale).reshape(1, D_ATTN)
        gk = jnp.tile(k_norm_g[layer], N_KV).reshape(1, LANES)
        gc = out_norm_conv_g[layer].reshape(1, D_CONV)
        ga = out_norm_attn_g[layer].reshape(1, D_ATTN)
        sink = attn_sink[layer]
        cw = conv_w[layer]

        cu, q, kv = _inproj(h, g1n, lat[0], lat[1], w_in_b, gq, gk, gm, tables, li=layer, tm=512, seq=s)
        cuc, qc, kvc = _inproj(hc, g1n, cx[0], cx[1], w_in_b, gq, gk, gm, None, li=layer, tm=lc, seq=lc)
        h = _mixer(h, cu, q, kv, kvc, sink, cw, gc, ga, w_out_b, lat[2],
                   li=layer, tq=512, seq=s, ctx_len=lc, windowed=True)
        if not last:
            hc = _mixer(hc, cuc, qc, None, kvc, sink, cw, gc, ga, w_out_b, cx[2],
                        li=layer, tq=lc, seq=lc, ctx_len=lc, windowed=False)

        i = layer // 2
        if layer % 2 == 0:
            w1, w3, w2 = dense_w
            router = None
            h = _dense_ffn(h, g2n, lat[3], lat[4], lat[5], w1, w3, w2, li=i, tm=512, seq=s, parts=2)
        else:
            w1, w3, w2 = moe_w13, None, moe_w2b
            router = jnp.zeros((d, LANES), F32).at[:, 0:N_EXPERTS].set(moe_router[i])
            h = _moe(h, g2n, lat[3], lat[4], lat[5], w1, w2, router, li=i, seq=s, tr=512, tm=512)
        if not last:
            hc = _ffn(hc, g2n, cx[3], cx[4], cx[5], w1, w3, w2, router, li=i, tm=lc, seq=lc)
    return h.reshape(b, s, d)
```

```python
import functools

import jax
import jax.numpy as jnp
from jax import lax
from jax.experimental import pallas as pl
from jax.experimental.pallas import tpu as pltpu

D_MODEL = 1024
GRID_W = 64
HEAD_DIM = 64
D_CONV = 512
D_ATTN = 512
N_HEADS = 8
N_KV = 2
WINDOW = 128
ROPE_THETA = 10000.0
N_EXPERTS = 8
EPS = 1e-6
KV_OFF = 3 * D_CONV + D_ATTN
D_IN = KV_OFF + 2 * N_KV * HEAD_DIM
LANES = 128
VMEM_LIMIT = 48 * 1024 * 1024

F32 = jnp.float32
BF16 = jnp.bfloat16
NEG_INF = float("-inf")
LOG2E = 1.4426950408889634


def _params(n_axes):
    return pltpu.CompilerParams(dimension_semantics=("arbitrary",) * n_axes,
                                vmem_limit_bytes=VMEM_LIMIT)


def _sigmoid(x):
    return 1.0 / (1.0 + jnp.exp(-x))


def _mod_kernel(c_ref, w_ref, b_ref, o_ref):
    c = c_ref[...]
    s = c * _sigmoid(c)
    o_ref[...] = jnp.dot(s, w_ref[...], preferred_element_type=F32) + b_ref[...]


def _modulation(c8, w_ada, b_ada):
    depth, d, n = w_ada.shape
    tn = 1536
    return pl.pallas_call(
        _mod_kernel,
        grid=(depth, n // tn),
        in_specs=[pl.BlockSpec((8, d), lambda l, j: (0, 0)),
                  pl.BlockSpec((None, d, tn), lambda l, j: (l, 0, j)),
                  pl.BlockSpec((None, 1, tn), lambda l, j: (l, 0, j))],
        out_specs=pl.BlockSpec((None, 8, tn), lambda l, j: (l, 0, j)),
        out_shape=jax.ShapeDtypeStruct((depth, 8, n), F32),
        compiler_params=_params(2),
        name="adaln_mod",
    )(c8, w_ada, b_ada.reshape(depth, 1, n))


def _norm_mod(x, g, sh, sc):
    ms = jnp.mean(x * x, axis=-1, keepdims=True)
    return (x * lax.rsqrt(ms + EPS) * g) * (1.0 + sc) + sh


def _inproj_kernel(*refs, rope, parts):
    if rope:
        (h_ref, g_ref, sh_ref, sc_ref, w_ref, gq_ref, gk_ref, gm_ref,
         cos_ref, sa_ref, sb_ref, cu_ref, q_ref, kv_ref) = refs
    else:
        (h_ref, g_ref, sh_ref, sc_ref, w_ref, gq_ref, gk_ref, gm_ref,
         cu_ref, q_ref, kv_ref) = refs
    gm = gm_ref[...]
    gq = gq_ref[...]
    rows = h_ref.shape[0] // parts

    def head_norm(t, gain):
        w = t.shape[1]
        ss = jnp.dot((t * t).astype(BF16), gm[0:w, 0:w], preferred_element_type=F32)
        return t * lax.rsqrt(ss * (1.0 / HEAD_DIM) + EPS) * gain

    for p in range(parts):
        sl = slice(p * rows, (p + 1) * rows)

        def rot(t):
            if not rope:
                return t
            return (t * cos_ref[sl, :] + pltpu.roll(t, LANES - 16, 1) * sa_ref[sl, :]
                    + pltpu.roll(t, 16, 1) * sb_ref[sl, :])

        xn = _norm_mod(h_ref[sl, :], g_ref[...], sh_ref[...], sc_ref[...])
        y = jnp.dot(xn.astype(BF16), w_ref[...], preferred_element_type=F32)
        cu_ref[sl, 0:D_CONV] = y[:, 0:D_CONV].astype(BF16)
        cu_ref[sl, D_CONV:2 * D_CONV] = (y[:, D_CONV:2 * D_CONV] * y[:, 2 * D_CONV:3 * D_CONV]).astype(BF16)
        for j in range(D_ATTN // 256):
            lo = 3 * D_CONV + 256 * j
            qn = head_norm(y[:, lo:lo + 256], gq[:, 256 * j:256 * j + 256])
            for c in range(2):
                q_ref[sl, 256 * j + LANES * c:256 * j + LANES * (c + 1)] = rot(
                    qn[:, LANES * c:LANES * (c + 1)]).astype(BF16)
        k = rot(head_norm(y[:, KV_OFF:KV_OFF + LANES], gk_ref[...]))
        v = y[:, KV_OFF + LANES:KV_OFF + 2 * LANES]
        lo_half = lax.broadcasted_iota(jnp.int32, k.shape, 1) < HEAD_DIM
        for c, t in enumerate((k, v)):
            sw = pltpu.roll(t, HEAD_DIM, 1)
            kv_ref[sl, 2 * c * LANES:(2 * c + 1) * LANES] = jnp.where(lo_half, t, sw).astype(BF16)
            kv_ref[sl, (2 * c + 1) * LANES:(2 * c + 2) * LANES] = jnp.where(lo_half, sw, t).astype(BF16)


def _inproj(h, g, sh, sc, w, gq, gk, gm, tables, *, li, tm, seq):
    t, d = h.shape
    tpb = seq // tm
    rope = tables is not None
    row = lambda i: (i, 0)
    fix = lambda i: (0, 0)
    mod = lambda i: (i // tpb, 0, 0)
    in_specs = [pl.BlockSpec((tm, d), row),
                pl.BlockSpec((1, d), fix),
                pl.BlockSpec((None, 1, d), mod),
                pl.BlockSpec((None, 1, d), mod),
                pl.BlockSpec((None, d, D_IN), lambda i: (li, 0, 0)),
                pl.BlockSpec((1, D_ATTN), fix),
                pl.BlockSpec((1, LANES), fix),
                pl.BlockSpec((256, 256), fix)]
    args = [h, g, sh, sc, w, gq, gk, gm]
    if rope:
        in_specs += [pl.BlockSpec((tm, LANES), lambda i: (i % tpb, 0))] * 3
        args += list(tables)
    return pl.pallas_call(
        functools.partial(_inproj_kernel, rope=rope, parts=max(tm // 256, 1)),
        grid=(t // tm,),
        in_specs=in_specs,
        out_specs=[pl.BlockSpec((tm, 2 * D_CONV), row),
                   pl.BlockSpec((tm, D_ATTN), row),
                   pl.BlockSpec((tm, 4 * LANES), row)],
        out_shape=[jax.ShapeDtypeStruct((t, 2 * D_CONV), BF16),
                   jax.ShapeDtypeStruct((t, D_ATTN), BF16),
                   jax.ShapeDtypeStruct((t, 4 * LANES), BF16)],
        compiler_params=_params(1),
        name="inproj_rope" if rope else "inproj_ctx",
    )(*args)


_NT = (((1,), (1,)), ((), ()))


def _mixer_kernel(*refs, tq, windowed):
    if windowed:
        (sink_ref, h_ref, cu_ref, cup_ref, cun_ref, q_ref, kvp_ref, kv_ref, kvn_ref, kvc_ref,
         cw_ref, gc_ref, ga_ref, wo_ref, g1_ref, out_ref, kw_ref, ya_ref) = refs
    else:
        (sink_ref, h_ref, cu_ref, q_ref, kvc_ref,
         cw_ref, gc_ref, ga_ref, wo_ref, g1_ref, out_ref, ya_ref) = refs
    i = pl.program_id(1)
    nt = pl.num_programs(1)
    nsub = tq // WINDOW

    cu = cu_ref[...]
    bg = cu[:, 0:D_CONV].astype(F32)
    u = cu[:, D_CONV:2 * D_CONV].astype(F32)
    rows = lax.broadcasted_iota(jnp.int32, (tq, 1), 0)
    if windowed:
        up_row = cup_ref[:, D_CONV:2 * D_CONV].astype(F32)[15:16, :]
        un_row = cun_ref[:, D_CONV:2 * D_CONV].astype(F32)[0:1, :]
        up_row = jnp.where(i > 0, up_row, 0.0)
        un_row = jnp.where(i < nt - 1, un_row, 0.0)
    else:
        up_row = jnp.zeros((1, D_CONV), F32)
        un_row = jnp.zeros((1, D_CONV), F32)
    u_prev = jnp.where(rows == 0, up_row, pltpu.roll(u, 1, 0))
    u_next = jnp.where(rows == tq - 1, un_row, pltpu.roll(u, tq - 1, 0))
    cw = cw_ref[...]
    yc = bg * (cw[0:1, :] * u_prev + cw[1:2, :] * u + cw[2:3, :] * u_next)
    yc = yc * lax.rsqrt(jnp.mean(yc * yc, axis=-1, keepdims=True) + EPS) * gc_ref[...]

    if windowed:
        kw_ref[0:WINDOW, :] = kvp_ref[...]
        kw_ref[WINDOW:WINDOW + tq, :] = kv_ref[...]
        kw_ref[WINDOW + tq:2 * WINDOW + tq, :] = kvn_ref[...]
    lane_lo = lax.broadcasted_iota(jnp.int32, (WINDOW, LANES), 1) < HEAD_DIM
    kvc = kvc_ref[...]
    gqa = N_HEADS // N_KV
    head_of_row = lax.broadcasted_iota(jnp.int32, (gqa * WINDOW, 1), 0) // WINDOW
    ones_c = jnp.ones((kvc.shape[0], LANES), BF16)
    if windowed:
        kk = lax.broadcasted_iota(jnp.int32, (WINDOW, WINDOW), 1)
        qq = lax.broadcasted_iota(jnp.int32, (WINDOW, WINDOW), 0)
        keep_before = kk >= qq
        keep_after = kk <= qq
        ones_w = jnp.ones((3 * WINDOW, LANES), BF16)

    def sub_block(s):
        r0 = s * WINDOW
        if windowed:
            kwin = kw_ref[pl.ds(r0, 3 * WINDOW), :]
            before, after = keep_before, keep_after
            if s == 0:
                before = before & (i > 0)
            if s == nsub - 1:
                after = after & (i < nt - 1)
        for grp in range(N_KV):
            rows, sk = [], jnp.zeros((gqa * WINDOW, 1), F32)
            for hh in range(gqa):
                head = gqa * grp + hh
                qp = q_ref[pl.ds(r0, WINDOW), LANES * (head // 2):LANES * (head // 2 + 1)]
                zero = jnp.zeros_like(qp)
                rows.append(jnp.where(lane_lo, zero, qp) if head % 2 else jnp.where(lane_lo, qp, zero))
                sk = jnp.where(head_of_row == hh, sink_ref[head] * LOG2E, sk)
            q4 = jnp.concatenate(rows, axis=0)
            s_c = lax.dot_general(q4, kvc[:, LANES * grp:LANES * (grp + 1)], _NT, preferred_element_type=F32)
            m = jnp.maximum(jnp.max(s_c, axis=-1, keepdims=True), sk)
            if windowed:
                s_w = lax.dot_general(q4, kwin[:, LANES * grp:LANES * (grp + 1)], _NT,
                                      preferred_element_type=F32).reshape(gqa, WINDOW, 3 * WINDOW)
                s_w = jnp.concatenate(
                    [jnp.where(before[None], s_w[:, :, 0:WINDOW], NEG_INF),
                     s_w[:, :, WINDOW:2 * WINDOW],
                     jnp.where(after[None], s_w[:, :, 2 * WINDOW:3 * WINDOW], NEG_INF)],
                    axis=-1).reshape(gqa * WINDOW, 3 * WINDOW)
                m = jnp.maximum(m, jnp.max(s_w, axis=-1, keepdims=True))
            v_c = jnp.concatenate([kvc[:, LANES * (2 + grp):LANES * (3 + grp)], ones_c], axis=1)
            o = jnp.dot(jnp.exp2(s_c - m).astype(BF16), v_c, preferred_element_type=F32)
            if windowed:
                v_w = jnp.concatenate([kwin[:, LANES * (2 + grp):LANES * (3 + grp)], ones_w], axis=1)
                o = o + jnp.dot(jnp.exp2(s_w - m).astype(BF16), v_w, preferred_element_type=F32)
            o = o[:, 0:LANES] / (o[:, LANES:2 * LANES] + jnp.exp2(sk - m))
            for pr in range(gqa // 2):
                pair = (gqa // 2) * grp + pr
                even = o[2 * pr * WINDOW:(2 * pr + 1) * WINDOW]
                odd = o[(2 * pr + 1) * WINDOW:(2 * pr + 2) * WINDOW]
                ya_ref[pl.ds(r0, WINDOW), LANES * pair:LANES * (pair + 1)] = jnp.where(lane_lo, even, odd)

    part = 2 * WINDOW
    ycb = yc.astype(BF16)
    for p in range(tq // part):
        sub_block(2 * p)
        sub_block(2 * p + 1)
        sl = slice(p * part, (p + 1) * part)
        ya = ya_ref[sl, :]
        ya = ya * lax.rsqrt(jnp.mean(ya * ya, axis=-1, keepdims=True) + EPS) * ga_ref[...]
        y = (jnp.dot(ycb[sl], wo_ref[0:D_CONV, :], preferred_element_type=F32)
             + jnp.dot(ya.astype(BF16), wo_ref[D_CONV:2 * D_CONV, :], preferred_element_type=F32))
        out_ref[sl, :] = h_ref[sl, :] + g1_ref[...] * y


def _mixer(h, cu, q, kv, kvc, sink, cw, gc, ga, wo, g1, *, li, tq, seq, ctx_len, windowed):
    t, d = h.shape
    nt = seq // tq
    nb = t // seq
    row = lambda b, i: (b * nt + i, 0)
    fix = lambda b, i: (0, 0)
    smem = pl.BlockSpec(memory_space=pltpu.SMEM)
    tail = [pl.BlockSpec((3, D_CONV), fix),
            pl.BlockSpec((1, D_CONV), fix),
            pl.BlockSpec((1, D_ATTN), fix),
            pl.BlockSpec((None, d, d), lambda b, i: (li, 0, 0)),
            pl.BlockSpec((None, 1, d), lambda b, i: (b, 0, 0))]
    ctx_spec = pl.BlockSpec((ctx_len, 4 * LANES), lambda b, i: (b, 0))
    if windowed:
        r16 = tq // 16
        n16 = t // 16
        rw = tq // WINDOW
        nw = t // WINDOW
        in_specs = [smem,
                    pl.BlockSpec((tq, d), row),
                    pl.BlockSpec((tq, 2 * D_CONV), row),
                    pl.BlockSpec((16, 2 * D_CONV), lambda b, i: (jnp.maximum((b * nt + i) * r16 - 1, 0), 0)),
                    pl.BlockSpec((16, 2 * D_CONV), lambda b, i: (jnp.minimum((b * nt + i + 1) * r16, n16 - 1), 0)),
                    pl.BlockSpec((tq, D_ATTN), row),
                    pl.BlockSpec((WINDOW, 4 * LANES), lambda b, i: (jnp.maximum((b * nt + i) * rw - 1, 0), 0)),
                    pl.BlockSpec((tq, 4 * LANES), row),
                    pl.BlockSpec((WINDOW, 4 * LANES), lambda b, i: (jnp.minimum((b * nt + i + 1) * rw, nw - 1), 0)),
                    ctx_spec] + tail
        args = [sink, h, cu, cu, cu, q, kv, kv, kv, kvc, cw, gc, ga, wo, g1]
        scratch = [pltpu.VMEM((tq + 2 * WINDOW, 4 * LANES), BF16), pltpu.VMEM((tq, D_ATTN), F32)]
    else:
        in_specs = [smem,
                    pl.BlockSpec((tq, d), row),
                    pl.BlockSpec((tq, 2 * D_CONV), row),
                    pl.BlockSpec((tq, D_ATTN), row),
                    ctx_spec] + tail
        args = [sink, h, cu, q, kvc, cw, gc, ga, wo, g1]
        scratch = [pltpu.VMEM((tq, D_ATTN), F32)]
    return pl.pallas_call(
        functools.partial(_mixer_kernel, tq=tq, windowed=windowed),
        grid=(nb, nt),
        in_specs=in_specs,
        out_specs=pl.BlockSpec((tq, d), row),
        out_shape=jax.ShapeDtypeStruct((t, d), F32),
        scratch_shapes=scratch,
        compiler_params=_params(2),
        name="mixer_win" if windowed else "mixer_ctx",
    )(*args)


def _dense_ffn_kernel(h_ref, g_ref, sh_ref, sc_ref, gate_ref, w1_ref, w3_ref, w2_ref, out_ref, *, parts):
    rows = h_ref.shape[0] // parts
    for p in range(parts):
        sl = slice(p * rows, (p + 1) * rows)
        hp = h_ref[sl, :]
        xb = _norm_mod(hp, g_ref[...], sh_ref[...], sc_ref[...]).astype(BF16)
        h1 = jnp.dot(xb, w1_ref[...], preferred_element_type=F32)
        h3 = jnp.dot(xb, w3_ref[...], preferred_element_type=F32)
        a = (h1 * _sigmoid(h1) * h3).astype(BF16)
        y = jnp.dot(a, w2_ref[...], preferred_element_type=F32)
        out_ref[sl, :] = hp + gate_ref[...] * y


def _dense_ffn(h, g, sh, sc, gate, w1, w3, w2, *, li, tm, seq, parts):
    t, d = h.shape
    f = w1.shape[2]
    wfix = lambda i: (li, 0, 0)
    tpb = seq // tm
    row = lambda i: (i, 0)
    fix = lambda i: (0, 0)
    mod = lambda i: (i // tpb, 0, 0)
    once = pl.Buffered(1)
    return pl.pallas_call(
        functools.partial(_dense_ffn_kernel, parts=parts),
        grid=(t // tm,),
        in_specs=[pl.BlockSpec((tm, d), row),
                  pl.BlockSpec((1, d), fix),
                  pl.BlockSpec((None, 1, d), mod),
                  pl.BlockSpec((None, 1, d), mod),
                  pl.BlockSpec((None, 1, d), mod),
                  pl.BlockSpec((None, d, f), wfix, pipeline_mode=once),
                  pl.BlockSpec((None, d, f), wfix, pipeline_mode=once),
                  pl.BlockSpec((None, f, d), wfix, pipeline_mode=once)],
        out_specs=pl.BlockSpec((tm, d), row),
        out_shape=jax.ShapeDtypeStruct((t, d), F32),
        compiler_params=_params(1),
        name="ffn_dense",
    )(h, g, sh, sc, gate, w1, w3, w2)


def _top2(logits):
    lane = lax.broadcasted_iota(jnp.int32, logits.shape, 1)
    lg = jnp.where(lane < N_EXPERTS, logits, NEG_INF)
    m1 = jnp.max(lg, axis=-1, keepdims=True)
    i1 = jnp.min(jnp.where(lg == m1, lane, LANES), axis=-1, keepdims=True)
    lg2 = jnp.where(lane == i1, NEG_INF, lg)
    m2 = jnp.max(lg2, axis=-1, keepdims=True)
    i2 = jnp.min(jnp.where(lg2 == m2, lane, LANES), axis=-1, keepdims=True)
    e2 = jnp.exp(m2 - m1)
    return i1, i2, 1.0 / (1.0 + e2), e2 / (1.0 + e2)


def _router_kernel(h_ref, g_ref, sh_ref, sc_ref, r_ref, tri_ref, xn_ref, route_ref, rt_ref, cnt_ref,
                   base_ref):
    tm = h_ref.shape[0]

    @pl.when(pl.program_id(0) == 0)
    def _():
        base_ref[...] = jnp.zeros_like(base_ref)

    xn = _norm_mod(h_ref[...], g_ref[...], sh_ref[...], sc_ref[...])
    xn_ref[...] = xn.reshape(xn_ref.shape)
    i1, i2, g1, g2 = _top2(jnp.dot(xn, r_ref[...], preferred_element_type=F32))
    lane = lax.broadcasted_iota(jnp.int32, (tm, LANES), 1)
    hit1 = lane == i1
    hit2 = lane == i2
    chosen = jnp.where(jnp.logical_or(hit1, hit2), 1.0, 0.0)
    before = base_ref[...] + jnp.dot(tri_ref[...], chosen.astype(BF16), preferred_element_type=F32)
    r1 = jnp.sum(jnp.where(hit1, before, 0.0), axis=-1, keepdims=True)
    r2 = jnp.sum(jnp.where(hit2, before, 0.0), axis=-1, keepdims=True)
    base_ref[...] += jnp.sum(chosen, axis=0, keepdims=True)
    cnt_ref[...] = base_ref[...]
    fields = (i1.astype(F32), i2.astype(F32), g1, g2, r1, r2)
    route = jnp.zeros((tm, LANES), F32)
    for k, f in enumerate(fields):
        route = jnp.where(lane == k, f, route)
    route_ref[...] = route
    rt_ref[...] = route.T[0:8, :]


def _router(h, g, sh, sc, router, *, tm, seq):
    t, d = h.shape
    tpb = seq // tm
    row = lambda i: (i, 0)
    fix = lambda i: (0, 0)
    mod = lambda i: (i // tpb, 0, 0)
    ids = jnp.arange(tm)
    tri = (ids[None, :] < ids[:, None]).astype(BF16)
    return pl.pallas_call(
        _router_kernel,
        grid=(t // tm,),
        in_specs=[pl.BlockSpec((tm, d), row), pl.BlockSpec((1, d), fix),
                  pl.BlockSpec((None, 1, d), mod), pl.BlockSpec((None, 1, d), mod),
                  pl.BlockSpec((d, LANES), fix), pl.BlockSpec((tm, tm), fix)],
        out_specs=[pl.BlockSpec((tm, d // LANES, LANES), lambda i: (i, 0, 0)),
                   pl.BlockSpec((tm, LANES), row),
                   pl.BlockSpec((None, 8, tm), lambda i: (i, 0, 0)),
                   pl.BlockSpec((1, LANES), fix)],
        out_shape=[jax.ShapeDtypeStruct((t, d // LANES, LANES), F32),
                   jax.ShapeDtypeStruct((t, LANES), F32),
                   jax.ShapeDtypeStruct((t // tm, 8, tm), F32),
                   jax.ShapeDtypeStruct((1, LANES), F32)],
        scratch_shapes=[pltpu.VMEM((1, LANES), F32)],
        compiler_params=_params(1),
        name="moe_router",
    )(h, g, sh, sc, router, tri)


def _route_plan(t, counts, tr):
    counts = counts[0, 0:N_EXPERTS].astype(jnp.int32)
    tiles = (counts + tr - 1) // tr
    tile_end = jnp.cumsum(tiles)
    tile_start = tile_end - tiles
    experts = jnp.arange(N_EXPERTS, dtype=jnp.int32)

    nt = 2 * t // tr + N_EXPERTS
    tid = jnp.arange(nt, dtype=jnp.int32)
    tile_expert = jnp.minimum(jnp.sum((tid[:, None] >= tile_end[None, :]).astype(jnp.int32), axis=1),
                              N_EXPERTS - 1)
    in_tile = tid - jnp.sum(jnp.where(tile_expert[:, None] == experts[None, :], tile_start[None, :], 0), axis=1)
    own = jnp.sum(jnp.where(tile_expert[:, None] == experts[None, :], counts[None, :], 0), axis=1)
    n_valid = jnp.where(tid < tile_end[-1], jnp.clip(own - in_tile * tr, 0, tr), 0)
    tail = tile_end[-1] + experts
    pad_tiles = jnp.concatenate([jnp.where(tiles > 0, tile_end - 1, -1), jnp.where(tail < nt, tail, -1)])
    return tile_expert, n_valid, tile_start * tr, pad_tiles


def _table_kernel(start_ref, rt_ref, tab_ref, *, nt):
    tm = rt_ref.shape[1]

    def position(e, r):
        start = jnp.zeros_like(r)
        for k in range(N_EXPERTS):
            start = jnp.where(e == float(k), start_ref[k].astype(F32), start)
        return (start + r).astype(jnp.int32)

    live = pl.program_id(0) < nt
    rt = rt_ref[...]
    for c in range(2):
        pos = position(rt[c:c + 1, :], rt[4 + c:5 + c, :])
        tab_ref[:, c * tm:(c + 1) * tm] = jnp.where(live, pos, 0)


def _position_table(route_t, row_start):
    nt, fields, tm = route_t.shape
    grid_spec = pltpu.PrefetchScalarGridSpec(
        num_scalar_prefetch=1,
        grid=(nt + 2,),
        in_specs=[pl.BlockSpec((None, fields, tm), lambda j, st: (jnp.minimum(j, nt - 1), 0, 0))],
        out_specs=pl.BlockSpec((None, 1, 2 * tm), lambda j, st: (j, 0, 0)))
    return pl.pallas_call(
        functools.partial(_table_kernel, nt=nt),
        grid_spec=grid_spec,
        out_shape=jax.ShapeDtypeStruct((nt + 2, 1, 2 * tm), jnp.int32),
        compiler_params=_params(1),
        name="moe_table",
    )(row_start, route_t)


def _row_copies(idx_smem, s_idx, tm, make):
    base = s_idx * (2 * tm)

    def body(r, c):
        make(r, idx_smem[base + r], idx_smem[base + tm + r])
        return c

    lax.fori_loop(0, tm, body, 0, unroll=8)


def _dispatch_kernel(zt_ref, idx_hbm, xn_hbm, xg_hbm, idx_smem, zbuf, xbuf,
                     sem_d, sem_i, sem_z, sem_in, *, tm, nt, tr):
    j = pl.program_id(0)
    slot = j % 2
    other = 1 - slot
    cur = j % 3
    nxt = (j + 1) % 3

    def in_copy(tile, s):
        return pltpu.make_async_copy(xn_hbm.at[pl.ds(tile * tm, tm)], xbuf.at[s], sem_in.at[s])

    @pl.when(j == 0)
    def _():
        zbuf[...] = jnp.zeros_like(zbuf)
        for k in range(zt_ref.shape[0]):
            fill = pltpu.make_async_copy(zbuf, xg_hbm.at[pl.ds(jnp.maximum(zt_ref[k], 0) * tr, tr)], sem_z)
            pl.when(zt_ref[k] >= 0)(fill.start)
        for k in range(zt_ref.shape[0]):
            fill = pltpu.make_async_copy(zbuf, xg_hbm.at[pl.ds(0, tr)], sem_z)
            pl.when(zt_ref[k] >= 0)(fill.wait)

    def idx_copy(row, s):
        return pltpu.make_async_copy(idx_hbm.at[row, 0], idx_smem.at[pl.ds(s * 2 * tm, 2 * tm)], sem_i.at[s])

    def wait_rows(s):
        for _ in range(2):
            pltpu.make_async_copy(xbuf.at[s], xg_hbm.at[pl.ds(0, tm)], sem_d.at[s]).wait()

    @pl.when(j == 0)
    def _():
        idx_copy(0, 0).start()
        in_copy(0, 0).start()

    @pl.when(j >= 2)
    def _():
        wait_rows(nxt)

    @pl.when(j + 1 < nt)
    def _():
        in_copy(j + 1, nxt).start()

    idx_copy(j, slot).wait()
    idx_copy(j + 1, other).start()
    in_copy(j, cur).wait()

    def make(r, p1, p2):
        src = xbuf.at[cur, r]
        pltpu.make_async_copy(src, xg_hbm.at[p1], sem_d.at[cur]).start(priority=0)
        pltpu.make_async_copy(src, xg_hbm.at[p2], sem_d.at[cur]).start(priority=1)

    _row_copies(idx_smem, slot, tm, make)

    @pl.when(j == nt - 1)
    def _():
        if nt > 1:
            wait_rows((nt - 2) % 3)
        wait_rows((nt - 1) % 3)
        idx_copy(j + 1, other).wait()


def _dispatch(xn3, table, pad_tiles, *, tm, tr, n_rows):
    t = xn3.shape[0]
    nt = t // tm
    any_spec = pl.BlockSpec(memory_space=pl.ANY)
    grid_spec = pltpu.PrefetchScalarGridSpec(
        num_scalar_prefetch=1,
        grid=(nt,),
        in_specs=[any_spec, any_spec],
        out_specs=any_spec,
        scratch_shapes=[pltpu.SMEM((4 * tm,), jnp.int32),
                        pltpu.VMEM((tr,) + xn3.shape[1:], F32),
                        pltpu.VMEM((3, tm) + xn3.shape[1:], F32),
                        pltpu.SemaphoreType.DMA((3,)), pltpu.SemaphoreType.DMA((2,)),
                        pltpu.SemaphoreType.DMA, pltpu.SemaphoreType.DMA((3,))])
    return pl.pallas_call(
        functools.partial(_dispatch_kernel, tm=tm, nt=nt, tr=tr),
        grid_spec=grid_spec,
        out_shape=jax.ShapeDtypeStruct((n_rows,) + xn3.shape[1:], F32),
        compiler_params=pltpu.CompilerParams(dimension_semantics=("arbitrary",),
                                             vmem_limit_bytes=VMEM_LIMIT,
                                             disable_bounds_checks=True),
        name="moe_dispatch",
    )(pad_tiles, table, xn3)


def _expert_kernel(te_ref, nv_ref, x_ref, w13_ref, w2_ref, y_ref):
    tr = x_ref.shape[0]
    nv = nv_ref[pl.program_id(0)]

    @pl.when(nv > 0)
    def _():
        half = tr // 2
        for p in range(2):
            x = x_ref[p * half:(p + 1) * half].reshape(half, D_MODEL).astype(BF16)
            h13 = jnp.dot(x, w13_ref[...], preferred_element_type=F32)
            fe = h13.shape[1] // 2
            h1, h3 = h13[:, 0:fe], h13[:, fe:2 * fe]
            a = (h1 * _sigmoid(h1) * h3).astype(BF16)
            y = jnp.dot(a, w2_ref[...], preferred_element_type=F32)
            y_ref[p * half:(p + 1) * half] = y.reshape((half,) + y_ref.shape[1:])

    @pl.when(nv == 0)
    def _():
        y_ref[...] = jnp.zeros_like(y_ref)


def _experts(xg3, tile_expert, n_valid, w13, w2, *, li, tr):
    n_rows, sl, ln = xg3.shape
    d = sl * ln
    fe = w2.shape[2]
    rows = lambda j, te, nv: (j, 0, 0)
    wsel = lambda j, te, nv: (li, te[j], 0, 0)
    grid_spec = pltpu.PrefetchScalarGridSpec(
        num_scalar_prefetch=2,
        grid=(n_rows // tr,),
        in_specs=[pl.BlockSpec((tr, sl, ln), rows),
                  pl.BlockSpec((None, None, d, 2 * fe), wsel),
                  pl.BlockSpec((None, None, fe, d), wsel)],
        out_specs=pl.BlockSpec((tr, sl, ln), rows))
    return pl.pallas_call(
        _expert_kernel,
        grid_spec=grid_spec,
        out_shape=jax.ShapeDtypeStruct(xg3.shape, F32),
        compiler_params=_params(1),
        name="moe_experts",
    )(tile_expert, n_valid, xg3, w13, w2)


def _combine_kernel(idx_hbm, h_ref, gate_ref, route_ref, yg_hbm, out_ref,
                    y1buf, y2buf, idx_smem, sem_y, sem_i, *, tm, nt):
    j = pl.program_id(0)
    slot = j % 2
    other = 1 - slot

    def idx_copy(row, s):
        return pltpu.make_async_copy(idx_hbm.at[row, 0], idx_smem.at[pl.ds(s * 2 * tm, 2 * tm)], sem_i.at[s])

    def fetch(s_idx, s_buf):
        def make(r, p1, p2):
            pltpu.make_async_copy(yg_hbm.at[p1], y1buf.at[s_buf, r], sem_y.at[s_buf]).start(priority=0)
            pltpu.make_async_copy(yg_hbm.at[p2], y2buf.at[s_buf, r], sem_y.at[s_buf]).start(priority=1)

        _row_copies(idx_smem, s_idx, tm, make)

    def wait_rows(s):
        pltpu.make_async_copy(yg_hbm.at[pl.ds(0, tm)], y1buf.at[s], sem_y.at[s]).wait()
        pltpu.make_async_copy(yg_hbm.at[pl.ds(0, tm)], y2buf.at[s], sem_y.at[s]).wait()

    @pl.when(j == 0)
    def _():
        first = idx_copy(0, 0)
        first.start()
        first.wait()
        fetch(0, 0)
        idx_copy(1, 1).start()

    idx_copy(j + 1, other).wait()
    fetch(other, other)
    idx_copy(j + 2, slot).start()
    wait_rows(slot)
    rt = route_ref[...]
    y1 = y1buf[slot].reshape(tm, D_MODEL)
    y2 = y2buf[slot].reshape(tm, D_MODEL)
    out_ref[...] = h_ref[...] + gate_ref[...] * (rt[:, 2:3] * y1 + rt[:, 3:4] * y2)

    @pl.when(j == nt - 1)
    def _():
        wait_rows(other)
        idx_copy(j + 2, slot).wait()


def _combine(h, gate, route, table, yg3, *, tm, seq):
    t, d = h.shape
    tpb = seq // tm
    nt = t // tm
    sl, ln = yg3.shape[1:]
    row = lambda i: (i, 0)
    any_spec = pl.BlockSpec(memory_space=pl.ANY)
    return pl.pallas_call(
        functools.partial(_combine_kernel, tm=tm, nt=nt),
        grid=(nt,),
        in_specs=[any_spec,
                  pl.BlockSpec((tm, d), row),
                  pl.BlockSpec((None, 1, d), lambda i: (i // tpb, 0, 0)),
                  pl.BlockSpec((tm, LANES), row),
                  any_spec],
        out_specs=pl.BlockSpec((tm, d), row),
        out_shape=jax.ShapeDtypeStruct((t, d), F32),
        scratch_shapes=[pltpu.VMEM((2, tm, sl, ln), F32), pltpu.VMEM((2, tm, sl, ln), F32),
                        pltpu.SMEM((4 * tm,), jnp.int32),
                        pltpu.SemaphoreType.DMA((2,)), pltpu.SemaphoreType.DMA((2,))],
        compiler_params=pltpu.CompilerParams(dimension_semantics=("arbitrary",),
                                             vmem_limit_bytes=VMEM_LIMIT,
                                             disable_bounds_checks=True),
        name="moe_combine",
    )(table, h, gate, route, yg3)


def _moe(h, g, sh, sc, gate, w13, w2, router, *, li, seq, tr, tm):
    t = h.shape[0]
    xn3, route, route_t, counts = _router(h, g, sh, sc, router, tm=tm, seq=seq)
    tile_expert, n_valid, row_start, pad_tiles = _route_plan(t, counts, tr)
    table = _position_table(route_t, row_start)
    xg3 = _dispatch(xn3, table, pad_tiles, tm=tm, tr=tr, n_rows=2 * t + N_EXPERTS * tr)
    yg3 = _experts(xg3, tile_expert, n_valid, w13, w2, li=li, tr=tr)
    return _combine(h, gate, route, table, yg3, tm=tm, seq=seq)


def _rope_tables(seq):
    rows = seq // GRID_W
    row, col = jnp.meshgrid(jnp.arange(rows, dtype=F32), jnp.arange(GRID_W, dtype=F32), indexing='ij')
    n_freq = HEAD_DIM // 4
    inv_freq = ROPE_THETA ** (-jnp.arange(n_freq, dtype=F32) / n_freq)
    ang_r = row.reshape(-1, 1) * inv_freq
    ang_c = col.reshape(-1, 1) * inv_freq
    ang = jnp.concatenate([ang_r, ang_r, ang_c, ang_c], axis=-1)
    cos, sin = jnp.cos(ang), jnp.sin(ang)
    first = (jnp.arange(HEAD_DIM) % (2 * n_freq)) < n_freq
    sin_a = jnp.where(first, -sin, 0.0)
    sin_b = jnp.where(first, 0.0, sin)
    rep = LANES // HEAD_DIM
    return tuple(jnp.tile(t, (1, rep)) for t in (cos, sin_a, sin_b))


def kernel(x, c, ctx, c_ctx, w_ada, b_ada, norm1_g, norm2_g, w_in, conv_w, q_norm_g, k_norm_g,
           attn_sink, out_norm_conv_g, out_norm_attn_g, w_out, ffn_w1, ffn_w3, ffn_w2,
           moe_router, moe_w1, moe_w3, moe_w2):
    b, s, d = x.shape
    lc = ctx.shape[1]
    depth = w_ada.shape[0]
    assert d == D_MODEL and s % 512 == 0 and lc % 256 == 0 and b + 1 <= 8

    c8 = jnp.zeros((8, d), F32).at[0:b].set(c).at[b].set(c_ctx)
    mod = _modulation(c8, w_ada, b_ada)

    tables = _rope_tables(s)
    ids = jnp.arange(256)
    gm = (ids[:, None] // HEAD_DIM == ids[None, :] // HEAD_DIM).astype(BF16)
    scale = HEAD_DIM ** -0.5 * LOG2E

    h = x.reshape(b * s, d)
    hc = ctx.reshape(b * lc, d)
    w_in_b, w_out_b = w_in.astype(BF16), w_out.astype(BF16)
    dense_w = [w.astype(BF16) for w in (ffn_w1, ffn_w3, ffn_w2)]
    moe_w13 = jnp.concatenate([moe_w1, moe_w3], axis=-1).astype(BF16)
    moe_w2b = moe_w2.astype(BF16)
    for layer in range(depth):
        last = layer == depth - 1
        m = mod[layer]
        lat = [m[0:b, k * d:(k + 1) * d].reshape(b, 1, d) for k in range(6)]
        cx = [jnp.broadcast_to(m[b:b + 1, k * d:(k + 1) * d].reshape(1, 1, d), (b, 1, d)) for k in range(6)]
        g1n = norm1_g[layer].reshape(1, d)
        g2n = norm2_g[layer].reshape(1, d)
        gq = (jnp.tile(q_norm_g[layer], N_HEADS) * scale).reshape(1, D_ATTN)
        gk = jnp.tile(k_norm_g[layer], N_KV).reshape(1, LANES)
        gc = out_norm_conv_g[layer].reshape(1, D_CONV)
        ga = out_norm_attn_g[layer].reshape(1, D_ATTN)
        sink = attn_sink[layer]
        cw = conv_w[layer]

        cu, q, kv = _inproj(h, g1n, lat[0], lat[1], w_in_b, gq, gk, gm, tables, li=layer, tm=1024, seq=s)
        cuc, qc, kvc = _inproj(hc, g1n, cx[0], cx[1], w_in_b, gq, gk, gm, None, li=layer, tm=lc, seq=lc)
        h = _mixer(h, cu, q, kv, kvc, sink, cw, gc, ga, w_out_b, lat[2],
                   li=layer, tq=512, seq=s, ctx_len=lc, windowed=True)
        if not last:
            hc = _mixer(hc, cuc, qc, None, kvc, sink, cw, gc, ga, w_out_b, cx[2],
                        li=layer, tq=lc, seq=lc, ctx_len=lc, windowed=False)

        i = layer // 2
        if layer % 2 == 0:
            w1, w3, w2 = dense_w
            router = None
            h = _dense_ffn(h, g2n, lat[3], lat[4], lat[5], w1, w3, w2, li=i, tm=512, seq=s, parts=2)
        else:
            w1, w3, w2 = moe_w13, None, moe_w2b
            router = jnp.zeros((d, LANES), F32).at[:, 0:N_EXPERTS].set(moe_router[i])
            h = _moe(h, g2n, lat[3], lat[4], lat[5], w1, w2, router, li=i, seq=s, tr=512, tm=512)
        if not last:
            if router is None:
                hc = _dense_ffn(hc, g2n, cx[3], cx[4], cx[5], w1, w3, w2, li=i, tm=lc, seq=lc, parts=2)
            else:
                hc = _moe(hc, g2n, cx[3], cx[4], cx[5], w1, w2, router, li=i, seq=lc, tr=256, tm=lc)
    return h.reshape(b, s, d)
```

```python
import functools

import jax
import jax.numpy as jnp
from jax import lax
from jax.experimental import pallas as pl
from jax.experimental.pallas import tpu as pltpu

D_MODEL = 1024
GRID_W = 64
HEAD_DIM = 64
D_CONV = 512
D_ATTN = 512
N_HEADS = 8
N_KV = 2
WINDOW = 128
ROPE_THETA = 10000.0
N_EXPERTS = 8
EPS = 1e-6
KV_OFF = 3 * D_CONV + D_ATTN
D_IN = KV_OFF + 2 * N_KV * HEAD_DIM
LANES = 128
VMEM_LIMIT = 48 * 1024 * 1024

F32 = jnp.float32
BF16 = jnp.bfloat16
NEG_INF = float("-inf")
LOG2E = 1.4426950408889634


def _params(n_axes):
    return pltpu.CompilerParams(dimension_semantics=("arbitrary",) * n_axes,
                                vmem_limit_bytes=VMEM_LIMIT)


def _sigmoid(x):
    return 1.0 / (1.0 + jnp.exp(-x))


def _mod_kernel(c_ref, w_ref, b_ref, o_ref):
    c = c_ref[...]
    s = c * _sigmoid(c)
    o_ref[...] = jnp.dot(s, w_ref[...], preferred_element_type=F32) + b_ref[...]


def _modulation(c8, w_ada, b_ada):
    depth, d, n = w_ada.shape
    tn = 1536
    return pl.pallas_call(
        _mod_kernel,
        grid=(depth, n // tn),
        in_specs=[pl.BlockSpec((8, d), lambda l, j: (0, 0)),
                  pl.BlockSpec((None, d, tn), lambda l, j: (l, 0, j)),
                  pl.BlockSpec((None, 1, tn), lambda l, j: (l, 0, j))],
        out_specs=pl.BlockSpec((None, 8, tn), lambda l, j: (l, 0, j)),
        out_shape=jax.ShapeDtypeStruct((depth, 8, n), F32),
        compiler_params=_params(2),
        name="adaln_mod",
    )(c8, w_ada, b_ada.reshape(depth, 1, n))


def _norm_mod(x, g, sh, sc):
    ms = jnp.mean(x * x, axis=-1, keepdims=True)
    return (x * lax.rsqrt(ms + EPS) * g) * (1.0 + sc) + sh


def _inproj_kernel(*refs, rope, parts):
    if rope:
        (h_ref, g_ref, sh_ref, sc_ref, w_ref, gq_ref, gk_ref, gm_ref,
         cos_ref, sa_ref, sb_ref, cu_ref, q_ref, kv_ref) = refs
    else:
        (h_ref, g_ref, sh_ref, sc_ref, w_ref, gq_ref, gk_ref, gm_ref,
         cu_ref, q_ref, kv_ref) = refs
    gm = gm_ref[...]
    gq = gq_ref[...]
    rows = h_ref.shape[0] // parts

    def head_norm(t, gain):
        w = t.shape[1]
        ss = jnp.dot((t * t).astype(BF16), gm[0:w, 0:w], preferred_element_type=F32)
        return t * lax.rsqrt(ss * (1.0 / HEAD_DIM) + EPS) * gain

    for p in range(parts):
        sl = slice(p * rows, (p + 1) * rows)

        def rot(t):
            if not rope:
                return t
            return (t * cos_ref[sl, :] + pltpu.roll(t, LANES - 16, 1) * sa_ref[sl, :]
                    + pltpu.roll(t, 16, 1) * sb_ref[sl, :])

        xn = _norm_mod(h_ref[sl, :], g_ref[...], sh_ref[...], sc_ref[...])
        y = jnp.dot(xn.astype(BF16), w_ref[...], preferred_element_type=F32)
        cu_ref[sl, 0:D_CONV] = y[:, 0:D_CONV].astype(BF16)
        cu_ref[sl, D_CONV:2 * D_CONV] = (y[:, D_CONV:2 * D_CONV] * y[:, 2 * D_CONV:3 * D_CONV]).astype(BF16)
        for j in range(D_ATTN // 256):
            lo = 3 * D_CONV + 256 * j
            qn = head_norm(y[:, lo:lo + 256], gq[:, 256 * j:256 * j + 256])
            for c in range(2):
                q_ref[sl, 256 * j + LANES * c:256 * j + LANES * (c + 1)] = rot(
                    qn[:, LANES * c:LANES * (c + 1)]).astype(BF16)
        k = rot(head_norm(y[:, KV_OFF:KV_OFF + LANES], gk_ref[...]))
        v = y[:, KV_OFF + LANES:KV_OFF + 2 * LANES]
        lo_half = lax.broadcasted_iota(jnp.int32, k.shape, 1) < HEAD_DIM
        for c, t in enumerate((k, v)):
            sw = pltpu.roll(t, HEAD_DIM, 1)
            kv_ref[sl, 2 * c * LANES:(2 * c + 1) * LANES] = jnp.where(lo_half, t, sw).astype(BF16)
            kv_ref[sl, (2 * c + 1) * LANES:(2 * c + 2) * LANES] = jnp.where(lo_half, sw, t).astype(BF16)


def _inproj(h, g, sh, sc, w, gq, gk, gm, tables, *, li, tm, seq):
    t, d = h.shape
    tpb = seq // tm
    rope = tables is not None
    row = lambda i: (i, 0)
    fix = lambda i: (0, 0)
    mod = lambda i: (i // tpb, 0, 0)
    in_specs = [pl.BlockSpec((tm, d), row),
                pl.BlockSpec((1, d), fix),
                pl.BlockSpec((None, 1, d), mod),
                pl.BlockSpec((None, 1, d), mod),
                pl.BlockSpec((None, d, D_IN), lambda i: (li, 0, 0)),
                pl.BlockSpec((1, D_ATTN), fix),
                pl.BlockSpec((1, LANES), fix),
                pl.BlockSpec((256, 256), fix)]
    args = [h, g, sh, sc, w, gq, gk, gm]
    if rope:
        in_specs += [pl.BlockSpec((tm, LANES), lambda i: (i % tpb, 0))] * 3
        args += list(tables)
    return pl.pallas_call(
        functools.partial(_inproj_kernel, rope=rope, parts=max(tm // 256, 1)),
        grid=(t // tm,),
        in_specs=in_specs,
        out_specs=[pl.BlockSpec((tm, 2 * D_CONV), row),
                   pl.BlockSpec((tm, D_ATTN), row),
                   pl.BlockSpec((tm, 4 * LANES), row)],
        out_shape=[jax.ShapeDtypeStruct((t, 2 * D_CONV), BF16),
                   jax.ShapeDtypeStruct((t, D_ATTN), BF16),
                   jax.ShapeDtypeStruct((t, 4 * LANES), BF16)],
        compiler_params=_params(1),
        name="inproj_rope" if rope else "inproj_ctx",
    )(*args)


_NT = (((1,), (1,)), ((), ()))


def _mixer_kernel(*refs, tq, windowed):
    if windowed:
        (sink_ref, h_ref, cu_ref, cup_ref, cun_ref, q_ref, kvp_ref, kv_ref, kvn_ref, kvc_ref,
         cw_ref, gc_ref, ga_ref, wo_ref, g1_ref, out_ref, kw_ref, ya_ref) = refs
    else:
        (sink_ref, h_ref, cu_ref, q_ref, kvc_ref,
         cw_ref, gc_ref, ga_ref, wo_ref, g1_ref, out_ref, ya_ref) = refs
    i = pl.program_id(1)
    nt = pl.num_programs(1)
    nsub = tq // WINDOW

    cu = cu_ref[...]
    bg = cu[:, 0:D_CONV].astype(F32)
    u = cu[:, D_CONV:2 * D_CONV].astype(F32)
    rows = lax.broadcasted_iota(jnp.int32, (tq, 1), 0)
    if windowed:
        up_row = cup_ref[:, D_CONV:2 * D_CONV].astype(F32)[15:16, :]
        un_row = cun_ref[:, D_CONV:2 * D_CONV].astype(F32)[0:1, :]
        up_row = jnp.where(i > 0, up_row, 0.0)
        un_row = jnp.where(i < nt - 1, un_row, 0.0)
    else:
        up_row = jnp.zeros((1, D_CONV), F32)
        un_row = jnp.zeros((1, D_CONV), F32)
    u_prev = jnp.where(rows == 0, up_row, pltpu.roll(u, 1, 0))
    u_next = jnp.where(rows == tq - 1, un_row, pltpu.roll(u, tq - 1, 0))
    cw = cw_ref[...]
    yc = bg * (cw[0:1, :] * u_prev + cw[1:2, :] * u + cw[2:3, :] * u_next)
    yc = yc * lax.rsqrt(jnp.mean(yc * yc, axis=-1, keepdims=True) + EPS) * gc_ref[...]

    if windowed:
        kw_ref[0:WINDOW, :] = kvp_ref[...]
        kw_ref[WINDOW:WINDOW + tq, :] = kv_ref[...]
        kw_ref[WINDOW + tq:2 * WINDOW + tq, :] = kvn_ref[...]
    lane_lo = lax.broadcasted_iota(jnp.int32, (WINDOW, LANES), 1) < HEAD_DIM
    kvc = kvc_ref[...]
    gqa = N_HEADS // N_KV
    head_of_row = lax.broadcasted_iota(jnp.int32, (gqa * WINDOW, 1), 0) // WINDOW
    ones_c = jnp.ones((kvc.shape[0], LANES), BF16)
    if windowed:
        kk = lax.broadcasted_iota(jnp.int32, (WINDOW, WINDOW), 1)
        qq = lax.broadcasted_iota(jnp.int32, (WINDOW, WINDOW), 0)
        keep_before = kk >= qq
        keep_after = kk <= qq
        ones_w = jnp.ones((3 * WINDOW, LANES), BF16)

    def sub_block(s):
        r0 = s * WINDOW
        if windowed:
            kwin = kw_ref[pl.ds(r0, 3 * WINDOW), :]
            before, after = keep_before, keep_after
            if s == 0:
                before = before & (i > 0)
            if s == nsub - 1:
                after = after & (i < nt - 1)
        for grp in range(N_KV):
            rows, sk = [], jnp.zeros((gqa * WINDOW, 1), F32)
            for hh in range(gqa):
                head = gqa * grp + hh
                qp = q_ref[pl.ds(r0, WINDOW), LANES * (head // 2):LANES * (head // 2 + 1)]
                zero = jnp.zeros_like(qp)
                rows.append(jnp.where(lane_lo, zero, qp) if head % 2 else jnp.where(lane_lo, qp, zero))
                sk = jnp.where(head_of_row == hh, sink_ref[head] * LOG2E, sk)
            q4 = jnp.concatenate(rows, axis=0)
            s_c = lax.dot_general(q4, kvc[:, LANES * grp:LANES * (grp + 1)], _NT, preferred_element_type=F32)
            m = jnp.maximum(jnp.max(s_c, axis=-1, keepdims=True), sk)
            if windowed:
                s_w = lax.dot_general(q4, kwin[:, LANES * grp:LANES * (grp + 1)], _NT,
                                      preferred_element_type=F32).reshape(gqa, WINDOW, 3 * WINDOW)
                s_w = jnp.concatenate(
                    [jnp.where(before[None], s_w[:, :, 0:WINDOW], NEG_INF),
                     s_w[:, :, WINDOW:2 * WINDOW],
                     jnp.where(after[None], s_w[:, :, 2 * WINDOW:3 * WINDOW], NEG_INF)],
                    axis=-1).reshape(gqa * WINDOW, 3 * WINDOW)
                m = jnp.maximum(m, jnp.max(s_w, axis=-1, keepdims=True))
            v_c = jnp.concatenate([kvc[:, LANES * (2 + grp):LANES * (3 + grp)], ones_c], axis=1)
            o = jnp.dot(jnp.exp2(s_c - m).astype(BF16), v_c, preferred_element_type=F32)
            if windowed:
                v_w = jnp.concatenate([kwin[:, LANES * (2 + grp):LANES * (3 + grp)], ones_w], axis=1)
                o = o + jnp.dot(jnp.exp2(s_w - m).astype(BF16), v_w, preferred_element_type=F32)
            o = o[:, 0:LANES] / (o[:, LANES:2 * LANES] + jnp.exp2(sk - m))
            for pr in range(gqa // 2):
                pair = (gqa // 2) * grp + pr
                even = o[2 * pr * WINDOW:(2 * pr + 1) * WINDOW]
                odd = o[(2 * pr + 1) * WINDOW:(2 * pr + 2) * WINDOW]
                ya_ref[pl.ds(r0, WINDOW), LANES * pair:LANES * (pair + 1)] = jnp.where(lane_lo, even, odd)

    part = 2 * WINDOW
    ycb = yc.astype(BF16)
    for p in range(tq // part):
        sub_block(2 * p)
        sub_block(2 * p + 1)
        sl = slice(p * part, (p + 1) * part)
        ya = ya_ref[sl, :]
        ya = ya * lax.rsqrt(jnp.mean(ya * ya, axis=-1, keepdims=True) + EPS) * ga_ref[...]
        y = (jnp.dot(ycb[sl], wo_ref[0:D_CONV, :], preferred_element_type=F32)
             + jnp.dot(ya.astype(BF16), wo_ref[D_CONV:2 * D_CONV, :], preferred_element_type=F32))
        out_ref[sl, :] = h_ref[sl, :] + g1_ref[...] * y


def _mixer(h, cu, q, kv, kvc, sink, cw, gc, ga, wo, g1, *, li, tq, seq, ctx_len, windowed):
    t, d = h.shape
    nt = seq // tq
    nb = t // seq
    row = lambda b, i: (b * nt + i, 0)
    fix = lambda b, i: (0, 0)
    smem = pl.BlockSpec(memory_space=pltpu.SMEM)
    tail = [pl.BlockSpec((3, D_CONV), fix),
            pl.BlockSpec((1, D_CONV), fix),
            pl.BlockSpec((1, D_ATTN), fix),
            pl.BlockSpec((None, d, d), lambda b, i: (li, 0, 0)),
            pl.BlockSpec((None, 1, d), lambda b, i: (b, 0, 0))]
    ctx_spec = pl.BlockSpec((ctx_len, 4 * LANES), lambda b, i: (b, 0))
    if windowed:
        r16 = tq // 16
        n16 = t // 16
        rw = tq // WINDOW
        nw = t // WINDOW
        in_specs = [smem,
                    pl.BlockSpec((tq, d), row),
                    pl.BlockSpec((tq, 2 * D_CONV), row),
                    pl.BlockSpec((16, 2 * D_CONV), lambda b, i: (jnp.maximum((b * nt + i) * r16 - 1, 0), 0)),
                    pl.BlockSpec((16, 2 * D_CONV), lambda b, i: (jnp.minimum((b * nt + i + 1) * r16, n16 - 1), 0)),
                    pl.BlockSpec((tq, D_ATTN), row),
                    pl.BlockSpec((WINDOW, 4 * LANES), lambda b, i: (jnp.maximum((b * nt + i) * rw - 1, 0), 0)),
                    pl.BlockSpec((tq, 4 * LANES), row),
                    pl.BlockSpec((WINDOW, 4 * LANES), lambda b, i: (jnp.minimum((b * nt + i + 1) * rw, nw - 1), 0)),
                    ctx_spec] + tail
        args = [sink, h, cu, cu, cu, q, kv, kv, kv, kvc, cw, gc, ga, wo, g1]
        scratch = [pltpu.VMEM((tq + 2 * WINDOW, 4 * LANES), BF16), pltpu.VMEM((tq, D_ATTN), F32)]
    else:
        in_specs = [smem,
                    pl.BlockSpec((tq, d), row),
                    pl.BlockSpec((tq, 2 * D_CONV), row),
                    pl.BlockSpec((tq, D_ATTN), row),
                    ctx_spec] + tail
        args = [sink, h, cu, q, kvc, cw, gc, ga, wo, g1]
        scratch = [pltpu.VMEM((tq, D_ATTN), F32)]
    return pl.pallas_call(
        functools.partial(_mixer_kernel, tq=tq, windowed=windowed),
        grid=(nb, nt),
        in_specs=in_specs,
        out_specs=pl.BlockSpec((tq, d), row),
        out_shape=jax.ShapeDtypeStruct((t, d), F32),
        scratch_shapes=scratch,
        compiler_params=_params(2),
        name="mixer_win" if windowed else "mixer_ctx",
    )(*args)


def _dense_ffn_kernel(h_ref, g_ref, sh_ref, sc_ref, gate_ref, w1_ref, w3_ref, w2_ref, out_ref, *, parts):
    rows = h_ref.shape[0] // parts
    for p in range(parts):
        sl = slice(p * rows, (p + 1) * rows)
        hp = h_ref[sl, :]
        xb = _norm_mod(hp, g_ref[...], sh_ref[...], sc_ref[...]).astype(BF16)
        h1 = jnp.dot(xb, w1_ref[...], preferred_element_type=F32)
        h3 = jnp.dot(xb, w3_ref[...], preferred_element_type=F32)
        a = (h1 * _sigmoid(h1) * h3).astype(BF16)
        y = jnp.dot(a, w2_ref[...], preferred_element_type=F32)
        out_ref[sl, :] = hp + gate_ref[...] * y


def _dense_ffn(h, g, sh, sc, gate, w1, w3, w2, *, li, tm, seq, parts):
    t, d = h.shape
    f = w1.shape[2]
    wfix = lambda i: (li, 0, 0)
    tpb = seq // tm
    row = lambda i: (i, 0)
    fix = lambda i: (0, 0)
    mod = lambda i: (i // tpb, 0, 0)
    once = pl.Buffered(1)
    return pl.pallas_call(
        functools.partial(_dense_ffn_kernel, parts=parts),
        grid=(t // tm,),
        in_specs=[pl.BlockSpec((tm, d), row),
                  pl.BlockSpec((1, d), fix),
                  pl.BlockSpec((None, 1, d), mod),
                  pl.BlockSpec((None, 1, d), mod),
                  pl.BlockSpec((None, 1, d), mod),
                  pl.BlockSpec((None, d, f), wfix, pipeline_mode=once),
                  pl.BlockSpec((None, d, f), wfix, pipeline_mode=once),
                  pl.BlockSpec((None, f, d), wfix, pipeline_mode=once)],
        out_specs=pl.BlockSpec((tm, d), row),
        out_shape=jax.ShapeDtypeStruct((t, d), F32),
        compiler_params=_params(1),
        name="ffn_dense",
    )(h, g, sh, sc, gate, w1, w3, w2)


def _pack_w13_kernel(w1_ref, w3_ref, o_ref):
    fe = w1_ref.shape[1]
    o_ref[:, 0:fe] = w1_ref[...].astype(BF16)
    o_ref[:, fe:2 * fe] = w3_ref[...].astype(BF16)


def _pack_w13(w1, w3):
    nl, ne, d, fe = w1.shape
    rows = d // 2
    spec = pl.BlockSpec((None, None, rows, fe), lambda l, e, r: (l, e, r, 0))
    return pl.pallas_call(
        _pack_w13_kernel,
        grid=(nl, ne, d // rows),
        in_specs=[spec, spec],
        out_specs=pl.BlockSpec((None, None, rows, 2 * fe), lambda l, e, r: (l, e, r, 0)),
        out_shape=jax.ShapeDtypeStruct((nl, ne, d, 2 * fe), BF16),
        compiler_params=_params(3),
        name="pack_w13",
    )(w1, w3)


def _top2(logits):
    lane = lax.broadcasted_iota(jnp.int32, logits.shape, 1)
    lg = jnp.where(lane < N_EXPERTS, logits, NEG_INF)
    m1 = jnp.max(lg, axis=-1, keepdims=True)
    i1 = jnp.min(jnp.where(lg == m1, lane, LANES), axis=-1, keepdims=True)
    lg2 = jnp.where(lane == i1, NEG_INF, lg)
    m2 = jnp.max(lg2, axis=-1, keepdims=True)
    i2 = jnp.min(jnp.where(lg2 == m2, lane, LANES), axis=-1, keepdims=True)
    e2 = jnp.exp(m2 - m1)
    return i1, i2, 1.0 / (1.0 + e2), e2 / (1.0 + e2)


def _router_kernel(h_ref, g_ref, sh_ref, sc_ref, r_ref, tri_ref, xn_ref, route_ref, rt_ref, cnt_ref,
                   base_ref):
    tm = h_ref.shape[0]

    @pl.when(pl.program_id(0) == 0)
    def _():
        base_ref[...] = jnp.zeros_like(base_ref)

    xn = _norm_mod(h_ref[...], g_ref[...], sh_ref[...], sc_ref[...])
    xn_ref[...] = xn.reshape(xn_ref.shape)
    i1, i2, g1, g2 = _top2(jnp.dot(xn, r_ref[...], preferred_element_type=F32))
    lane = lax.broadcasted_iota(jnp.int32, (tm, LANES), 1)
    hit1 = lane == i1
    hit2 = lane == i2
    chosen = jnp.where(jnp.logical_or(hit1, hit2), 1.0, 0.0)
    before = base_ref[...] + jnp.dot(tri_ref[...], chosen.astype(BF16), preferred_element_type=F32)
    r1 = jnp.sum(jnp.where(hit1, before, 0.0), axis=-1, keepdims=True)
    r2 = jnp.sum(jnp.where(hit2, before, 0.0), axis=-1, keepdims=True)
    base_ref[...] += jnp.sum(chosen, axis=0, keepdims=True)
    cnt_ref[...] = base_ref[...]
    fields = (i1.astype(F32), i2.astype(F32), g1, g2, r1, r2)
    route = jnp.zeros((tm, LANES), F32)
    for k, f in enumerate(fields):
        route = jnp.where(lane == k, f, route)
    route_ref[...] = route
    rt_ref[...] = route.T[0:8, :]


def _router(h, g, sh, sc, router, *, tm, seq):
    t, d = h.shape
    tpb = seq // tm
    row = lambda i: (i, 0)
    fix = lambda i: (0, 0)
    mod = lambda i: (i // tpb, 0, 0)
    ids = jnp.arange(tm)
    tri = (ids[None, :] < ids[:, None]).astype(BF16)
    return pl.pallas_call(
        _router_kernel,
        grid=(t // tm,),
        in_specs=[pl.BlockSpec((tm, d), row), pl.BlockSpec((1, d), fix),
                  pl.BlockSpec((None, 1, d), mod), pl.BlockSpec((None, 1, d), mod),
                  pl.BlockSpec((d, LANES), fix), pl.BlockSpec((tm, tm), fix)],
        out_specs=[pl.BlockSpec((tm, d // LANES, LANES), lambda i: (i, 0, 0)),
                   pl.BlockSpec((tm, LANES), row),
                   pl.BlockSpec((None, 8, tm), lambda i: (i, 0, 0)),
                   pl.BlockSpec((1, LANES), fix)],
        out_shape=[jax.ShapeDtypeStruct((t, d // LANES, LANES), F32),
                   jax.ShapeDtypeStruct((t, LANES), F32),
                   jax.ShapeDtypeStruct((t // tm, 8, tm), F32),
                   jax.ShapeDtypeStruct((1, LANES), F32)],
        scratch_shapes=[pltpu.VMEM((1, LANES), F32)],
        compiler_params=_params(1),
        name="moe_router",
    )(h, g, sh, sc, router, tri)


def _route_plan(t, counts, tr):
    counts = counts[0, 0:N_EXPERTS].astype(jnp.int32)
    tiles = (counts + tr - 1) // tr
    tile_end = jnp.cumsum(tiles)
    tile_start = tile_end - tiles
    experts = jnp.arange(N_EXPERTS, dtype=jnp.int32)

    nt = 2 * t // tr + N_EXPERTS
    tid = jnp.arange(nt, dtype=jnp.int32)
    tile_expert = jnp.minimum(jnp.sum((tid[:, None] >= tile_end[None, :]).astype(jnp.int32), axis=1),
                              N_EXPERTS - 1)
    in_tile = tid - jnp.sum(jnp.where(tile_expert[:, None] == experts[None, :], tile_start[None, :], 0), axis=1)
    own = jnp.sum(jnp.where(tile_expert[:, None] == experts[None, :], counts[None, :], 0), axis=1)
    n_valid = jnp.where(tid < tile_end[-1], jnp.clip(own - in_tile * tr, 0, tr), 0)
    tail = tile_end[-1] + experts
    pad_tiles = jnp.concatenate([jnp.where(tiles > 0, tile_end - 1, -1), jnp.where(tail < nt, tail, -1)])
    return tile_expert, n_valid, tile_start * tr, pad_tiles


def _table_kernel(start_ref, rt_ref, tab_ref, *, nt):
    tm = rt_ref.shape[1]

    def position(e, r):
        start = jnp.zeros_like(r)
        for k in range(N_EXPERTS):
            start = jnp.where(e == float(k), start_ref[k].astype(F32), start)
        return (start + r).astype(jnp.int32)

    live = pl.program_id(0) < nt
    rt = rt_ref[...]
    for c in range(2):
        pos = position(rt[c:c + 1, :], rt[4 + c:5 + c, :])
        tab_ref[:, c * tm:(c + 1) * tm] = jnp.where(live, pos, 0)


def _position_table(route_t, row_start):
    nt, fields, tm = route_t.shape
    grid_spec = pltpu.PrefetchScalarGridSpec(
        num_scalar_prefetch=1,
        grid=(nt + 2,),
        in_specs=[pl.BlockSpec((None, fields, tm), lambda j, st: (jnp.minimum(j, nt - 1), 0, 0))],
        out_specs=pl.BlockSpec((None, 1, 2 * tm), lambda j, st: (j, 0, 0)))
    return pl.pallas_call(
        functools.partial(_table_kernel, nt=nt),
        grid_spec=grid_spec,
        out_shape=jax.ShapeDtypeStruct((nt + 2, 1, 2 * tm), jnp.int32),
        compiler_params=_params(1),
        name="moe_table",
    )(row_start, route_t)


def _row_copies(idx_smem, s_idx, tm, make):
    base = s_idx * (2 * tm)

    def body(r, c):
        make(r, idx_smem[base + r], idx_smem[base + tm + r])
        return c

    lax.fori_loop(0, tm, body, 0, unroll=8)


def _dispatch_kernel(zt_ref, idx_hbm, xn_hbm, xg_hbm, idx_smem, zbuf, xbuf,
                     sem_d, sem_i, sem_z, sem_in, *, tm, nt, tr):
    j = pl.program_id(0)
    slot = j % 2
    other = 1 - slot
    cur = j % 3
    nxt = (j + 1) % 3

    def in_copy(tile, s):
        return pltpu.make_async_copy(xn_hbm.at[pl.ds(tile * tm, tm)], xbuf.at[s], sem_in.at[s])

    @pl.when(j == 0)
    def _():
        zbuf[...] = jnp.zeros_like(zbuf)
        for k in range(zt_ref.shape[0]):
            fill = pltpu.make_async_copy(zbuf, xg_hbm.at[pl.ds(jnp.maximum(zt_ref[k], 0) * tr, tr)], sem_z)
            pl.when(zt_ref[k] >= 0)(fill.start)
        for k in range(zt_ref.shape[0]):
            fill = pltpu.make_async_copy(zbuf, xg_hbm.at[pl.ds(0, tr)], sem_z)
            pl.when(zt_ref[k] >= 0)(fill.wait)

    def idx_copy(row, s):
        return pltpu.make_async_copy(idx_hbm.at[row, 0], idx_smem.at[pl.ds(s * 2 * tm, 2 * tm)], sem_i.at[s])

    def wait_rows(s):
        for _ in range(2):
            pltpu.make_async_copy(xbuf.at[s], xg_hbm.at[pl.ds(0, tm)], sem_d.at[s]).wait()

    @pl.when(j == 0)
    def _():
        idx_copy(0, 0).start()
        in_copy(0, 0).start()

    @pl.when(j >= 2)
    def _():
        wait_rows(nxt)

    @pl.when(j + 1 < nt)
    def _():
        in_copy(j + 1, nxt).start()

    idx_copy(j, slot).wait()
    idx_copy(j + 1, other).start()
    in_copy(j, cur).wait()

    def make(r, p1, p2):
        src = xbuf.at[cur, r]
        pltpu.make_async_copy(src, xg_hbm.at[p1], sem_d.at[cur]).start(priority=0)
        pltpu.make_async_copy(src, xg_hbm.at[p2], sem_d.at[cur]).start(priority=1)

    _row_copies(idx_smem, slot, tm, make)

    @pl.when(j == nt - 1)
    def _():
        if nt > 1:
            wait_rows((nt - 2) % 3)
        wait_rows((nt - 1) % 3)
        idx_copy(j + 1, other).wait()


def _dispatch(xn3, table, pad_tiles, *, tm, tr, n_rows):
    t = xn3.shape[0]
    nt = t // tm
    any_spec = pl.BlockSpec(memory_space=pl.ANY)
    grid_spec = pltpu.PrefetchScalarGridSpec(
        num_scalar_prefetch=1,
        grid=(nt,),
        in_specs=[any_spec, any_spec],
        out_specs=any_spec,
        scratch_shapes=[pltpu.SMEM((4 * tm,), jnp.int32),
                        pltpu.VMEM((tr,) + xn3.shape[1:], F32),
                        pltpu.VMEM((3, tm) + xn3.shape[1:], F32),
                        pltpu.SemaphoreType.DMA((3,)), pltpu.SemaphoreType.DMA((2,)),
                        pltpu.SemaphoreType.DMA, pltpu.SemaphoreType.DMA((3,))])
    return pl.pallas_call(
        functools.partial(_dispatch_kernel, tm=tm, nt=nt, tr=tr),
        grid_spec=grid_spec,
        out_shape=jax.ShapeDtypeStruct((n_rows,) + xn3.shape[1:], F32),
        compiler_params=pltpu.CompilerParams(dimension_semantics=("arbitrary",),
                                             vmem_limit_bytes=VMEM_LIMIT,
                                             disable_bounds_checks=True),
        name="moe_dispatch",
    )(pad_tiles, table, xn3)


def _expert_kernel(te_ref, nv_ref, x_ref, w13_ref, w2_ref, y_ref):
    tr = x_ref.shape[0]
    nv = nv_ref[pl.program_id(0)]

    @pl.when(nv > 0)
    def _():
        half = tr // 2
        for p in range(2):
            x = x_ref[p * half:(p + 1) * half].reshape(half, D_MODEL).astype(BF16)
            h13 = jnp.dot(x, w13_ref[...], preferred_element_type=F32)
            fe = h13.shape[1] // 2
            h1, h3 = h13[:, 0:fe], h13[:, fe:2 * fe]
            a = (h1 * _sigmoid(h1) * h3).astype(BF16)
            y = jnp.dot(a, w2_ref[...], preferred_element_type=F32)
            y_ref[p * half:(p + 1) * half] = y.reshape((half,) + y_ref.shape[1:])

    @pl.when(nv == 0)
    def _():
        y_ref[...] = jnp.zeros_like(y_ref)


def _experts(xg3, tile_expert, n_valid, w13, w2, *, li, tr):
    n_rows, sl, ln = xg3.shape
    d = sl * ln
    fe = w2.shape[2]
    rows = lambda j, te, nv: (j, 0, 0)
    wsel = lambda j, te, nv: (li, te[j], 0, 0)
    grid_spec = pltpu.PrefetchScalarGridSpec(
        num_scalar_prefetch=2,
        grid=(n_rows // tr,),
        in_specs=[pl.BlockSpec((tr, sl, ln), rows),
                  pl.BlockSpec((None, None, d, 2 * fe), wsel),
                  pl.BlockSpec((None, None, fe, d), wsel)],
        out_specs=pl.BlockSpec((tr, sl, ln), rows))
    return pl.pallas_call(
        _expert_kernel,
        grid_spec=grid_spec,
        out_shape=jax.ShapeDtypeStruct(xg3.shape, F32),
        compiler_params=_params(1),
        name="moe_experts",
    )(tile_expert, n_valid, xg3, w13, w2)


def _combine_kernel(idx_hbm, h_ref, gate_ref, route_ref, yg_hbm, out_ref,
                    y1buf, y2buf, idx_smem, sem_y, sem_i, *, tm, nt):
    j = pl.program_id(0)
    slot = j % 2
    other = 1 - slot

    def idx_copy(row, s):
        return pltpu.make_async_copy(idx_hbm.at[row, 0], idx_smem.at[pl.ds(s * 2 * tm, 2 * tm)], sem_i.at[s])

    def fetch(s_idx, s_buf):
        def make(r, p1, p2):
            pltpu.make_async_copy(yg_hbm.at[p1], y1buf.at[s_buf, r], sem_y.at[s_buf]).start(priority=0)
            pltpu.make_async_copy(yg_hbm.at[p2], y2buf.at[s_buf, r], sem_y.at[s_buf]).start(priority=1)

        _row_copies(idx_smem, s_idx, tm, make)

    def wait_rows(s):
        pltpu.make_async_copy(yg_hbm.at[pl.ds(0, tm)], y1buf.at[s], sem_y.at[s]).wait()
        pltpu.make_async_copy(yg_hbm.at[pl.ds(0, tm)], y2buf.at[s], sem_y.at[s]).wait()

    @pl.when(j == 0)
    def _():
        first = idx_copy(0, 0)
        first.start()
        first.wait()
        fetch(0, 0)
        idx_copy(1, 1).start()

    idx_copy(j + 1, other).wait()
    fetch(other, other)
    idx_copy(j + 2, slot).start()
    wait_rows(slot)
    rt = route_ref[...]
    y1 = y1buf[slot].reshape(tm, D_MODEL)
    y2 = y2buf[slot].reshape(tm, D_MODEL)
    out_ref[...] = h_ref[...] + gate_ref[...] * (rt[:, 2:3] * y1 + rt[:, 3:4] * y2)

    @pl.when(j == nt - 1)
    def _():
        wait_rows(other)
        idx_copy(j + 2, slot).wait()


def _combine(h, gate, route, table, yg3, *, tm, seq):
    t, d = h.shape
    tpb = seq // tm
    nt = t // tm
    sl, ln = yg3.shape[1:]
    row = lambda i: (i, 0)
    any_spec = pl.BlockSpec(memory_space=pl.ANY)
    return pl.pallas_call(
        functools.partial(_combine_kernel, tm=tm, nt=nt),
        grid=(nt,),
        in_specs=[any_spec,
                  pl.BlockSpec((tm, d), row),
                  pl.BlockSpec((None, 1, d), lambda i: (i // tpb, 0, 0)),
                  pl.BlockSpec((tm, LANES), row),
                  any_spec],
        out_specs=pl.BlockSpec((tm, d), row),
        out_shape=jax.ShapeDtypeStruct((t, d), F32),
        scratch_shapes=[pltpu.VMEM((2, tm, sl, ln), F32), pltpu.VMEM((2, tm, sl, ln), F32),
                        pltpu.SMEM((4 * tm,), jnp.int32),
                        pltpu.SemaphoreType.DMA((2,)), pltpu.SemaphoreType.DMA((2,))],
        compiler_params=pltpu.CompilerParams(dimension_semantics=("arbitrary",),
                                             vmem_limit_bytes=VMEM_LIMIT,
                                             disable_bounds_checks=True),
        name="moe_combine",
    )(table, h, gate, route, yg3)


def _moe(h, g, sh, sc, gate, w13, w2, router, *, li, seq, tr, tm):
    t = h.shape[0]
    xn3, route, route_t, counts = _router(h, g, sh, sc, router, tm=tm, seq=seq)
    tile_expert, n_valid, row_start, pad_tiles = _route_plan(t, counts, tr)
    table = _position_table(route_t, row_start)
    xg3 = _dispatch(xn3, table, pad_tiles, tm=tm, tr=tr, n_rows=2 * t + N_EXPERTS * tr)
    yg3 = _experts(xg3, tile_expert, n_valid, w13, w2, li=li, tr=tr)
    return _combine(h, gate, route, table, yg3, tm=tm, seq=seq)


def _rope_tables(seq):
    rows = seq // GRID_W
    row, col = jnp.meshgrid(jnp.arange(rows, dtype=F32), jnp.arange(GRID_W, dtype=F32), indexing='ij')
    n_freq = HEAD_DIM // 4
    inv_freq = ROPE_THETA ** (-jnp.arange(n_freq, dtype=F32) / n_freq)
    ang_r = row.reshape(-1, 1) * inv_freq
    ang_c = col.reshape(-1, 1) * inv_freq
    ang = jnp.concatenate([ang_r, ang_r, ang_c, ang_c], axis=-1)
    cos, sin = jnp.cos(ang), jnp.sin(ang)
    first = (jnp.arange(HEAD_DIM) % (2 * n_freq)) < n_freq
    sin_a = jnp.where(first, -sin, 0.0)
    sin_b = jnp.where(first, 0.0, sin)
    rep = LANES // HEAD_DIM
    return tuple(jnp.tile(t, (1, rep)) for t in (cos, sin_a, sin_b))


def kernel(x, c, ctx, c_ctx, w_ada, b_ada, norm1_g, norm2_g, w_in, conv_w, q_norm_g, k_norm_g,
           attn_sink, out_norm_conv_g, out_norm_attn_g, w_out, ffn_w1, ffn_w3, ffn_w2,
           moe_router, moe_w1, moe_w3, moe_w2):
    b, s, d = x.shape
    lc = ctx.shape[1]
    depth = w_ada.shape[0]
    assert d == D_MODEL and s % 512 == 0 and lc % 256 == 0 and b + 1 <= 8

    c8 = jnp.zeros((8, d), F32).at[0:b].set(c).at[b].set(c_ctx)
    mod = _modulation(c8, w_ada, b_ada)

    tables = _rope_tables(s)
    ids = jnp.arange(256)
    gm = (ids[:, None] // HEAD_DIM == ids[None, :] // HEAD_DIM).astype(BF16)
    scale = HEAD_DIM ** -0.5 * LOG2E

    h = x.reshape(b * s, d)
    hc = ctx.reshape(b * lc, d)
    w_in_b, w_out_b = w_in.astype(BF16), w_out.astype(BF16)
    dense_w = [w.astype(BF16) for w in (ffn_w1, ffn_w3, ffn_w2)]
    moe_w13 = _pack_w13(moe_w1, moe_w3)
    moe_w2b = moe_w2.astype(BF16)
    for layer in range(depth):
        last = layer == depth - 1
        m = mod[layer]
        lat = [m[0:b, k * d:(k + 1) * d].reshape(b, 1, d) for k in range(6)]
        cx = [jnp.broadcast_to(m[b:b + 1, k * d:(k + 1) * d].reshape(1, 1, d), (b, 1, d)) for k in range(6)]
        g1n = norm1_g[layer].reshape(1, d)
        g2n = norm2_g[layer].reshape(1, d)
        gq = (jnp.tile(q_norm_g[layer], N_HEADS) * scale).reshape(1, D_ATTN)
        gk = jnp.tile(k_norm_g[layer], N_KV).reshape(1, LANES)
        gc = out_norm_conv_g[layer].reshape(1, D_CONV)
        ga = out_norm_attn_g[layer].reshape(1, D_ATTN)
        sink = attn_sink[layer]
        cw = conv_w[layer]

        cu, q, kv = _inproj(h, g1n, lat[0], lat[1], w_in_b, gq, gk, gm, tables, li=layer, tm=1024, seq=s)
        cuc, qc, kvc = _inproj(hc, g1n, cx[0], cx[1], w_in_b, gq, gk, gm, None, li=layer, tm=lc, seq=lc)
        h = _mixer(h, cu, q, kv, kvc, sink, cw, gc, ga, w_out_b, lat[2],
                   li=layer, tq=512, seq=s, ctx_len=lc, windowed=True)
        if not last:
            hc = _mixer(hc, cuc, qc, None, kvc, sink, cw, gc, ga, w_out_b, cx[2],
                        li=layer, tq=lc, seq=lc, ctx_len=lc, windowed=False)

        i = layer // 2
        if layer % 2 == 0:
            w1, w3, w2 = dense_w
            router = None
            h = _dense_ffn(h, g2n, lat[3], lat[4], lat[5], w1, w3, w2, li=i, tm=512, seq=s, parts=2)
        else:
            w1, w3, w2 = moe_w13, None, moe_w2b
            router = jnp.zeros((d, LANES), F32).at[:, 0:N_EXPERTS].set(moe_router[i])
            h = _moe(h, g2n, lat[3], lat[4], lat[5], w1, w2, router, li=i, seq=s, tr=512, tm=512)
        if not last:
            if router is None:
                hc = _dense_ffn(hc, g2n, cx[3], cx[4], cx[5], w1, w3, w2, li=i, tm=lc, seq=lc, parts=2)
            else:
                hc = _moe(hc, g2n, cx[3], cx[4], cx[5], w1, w2, router, li=i, seq=lc, tr=256, tm=lc)
    return h.reshape(b, s, d)
```

```python
import functools

import jax
import jax.numpy as jnp
from jax import lax
from jax.experimental import pallas as pl
from jax.experimental.pallas import tpu as pltpu

D_MODEL = 1024
GRID_W = 64
HEAD_DIM = 64
D_CONV = 512
D_ATTN = 512
N_HEADS = 8
N_KV = 2
WINDOW = 128
ROPE_THETA = 10000.0
N_EXPERTS = 8
EPS = 1e-6
KV_OFF = 3 * D_CONV + D_ATTN
D_IN = KV_OFF + 2 * N_KV * HEAD_DIM
LANES = 128
VMEM_LIMIT = 48 * 1024 * 1024

F32 = jnp.float32
BF16 = jnp.bfloat16
NEG_INF = float("-inf")
LOG2E = 1.4426950408889634


def _params(n_axes):
    return pltpu.CompilerParams(dimension_semantics=("arbitrary",) * n_axes,
                                vmem_limit_bytes=VMEM_LIMIT)


def _sigmoid(x):
    return 1.0 / (1.0 + jnp.exp(-x))


def _mod_kernel(c_ref, w_ref, b_ref, o_ref):
    c = c_ref[...]
    s = c * _sigmoid(c)
    o_ref[...] = jnp.dot(s, w_ref[...], preferred_element_type=F32) + b_ref[...]


def _modulation(c8, w_ada, b_ada):
    depth, d, n = w_ada.shape
    tn = 1536
    return pl.pallas_call(
        _mod_kernel,
        grid=(depth, n // tn),
        in_specs=[pl.BlockSpec((8, d), lambda l, j: (0, 0)),
                  pl.BlockSpec((None, d, tn), lambda l, j: (l, 0, j)),
                  pl.BlockSpec((None, 1, tn), lambda l, j: (l, 0, j))],
        out_specs=pl.BlockSpec((None, 8, tn), lambda l, j: (l, 0, j)),
        out_shape=jax.ShapeDtypeStruct((depth, 8, n), F32),
        compiler_params=_params(2),
        name="adaln_mod",
    )(c8, w_ada, b_ada.reshape(depth, 1, n))


def _norm_mod(x, g, sh, sc):
    ms = jnp.mean(x * x, axis=-1, keepdims=True)
    return (x * lax.rsqrt(ms + EPS) * g) * (1.0 + sc) + sh


def _inproj_kernel(*refs, rope, parts):
    if rope:
        (h_ref, g_ref, sh_ref, sc_ref, w_ref, gq_ref, gk_ref, gm_ref,
         cos_ref, sa_ref, sb_ref, cu_ref, q_ref, kv_ref) = refs
    else:
        (h_ref, g_ref, sh_ref, sc_ref, w_ref, gq_ref, gk_ref, gm_ref,
         cu_ref, q_ref, kv_ref) = refs
    gm = gm_ref[...]
    gq = gq_ref[...]
    rows = h_ref.shape[0] // parts

    def head_norm(t, gain):
        w = t.shape[1]
        ss = jnp.dot((t * t).astype(BF16), gm[0:w, 0:w], preferred_element_type=F32)
        return t * lax.rsqrt(ss * (1.0 / HEAD_DIM) + EPS) * gain

    for p in range(parts):
        sl = slice(p * rows, (p + 1) * rows)

        def rot(t):
            if not rope:
                return t
            return (t * cos_ref[sl, :] + pltpu.roll(t, LANES - 16, 1) * sa_ref[sl, :]
                    + pltpu.roll(t, 16, 1) * sb_ref[sl, :])

        xn = _norm_mod(h_ref[sl, :], g_ref[...], sh_ref[...], sc_ref[...])
        y = jnp.dot(xn.astype(BF16), w_ref[...], preferred_element_type=F32)
        cu_ref[sl, 0:D_CONV] = y[:, 0:D_CONV].astype(BF16)
        cu_ref[sl, D_CONV:2 * D_CONV] = (y[:, D_CONV:2 * D_CONV] * y[:, 2 * D_CONV:3 * D_CONV]).astype(BF16)
        for j in range(D_ATTN // 256):
            lo = 3 * D_CONV + 256 * j
            qn = head_norm(y[:, lo:lo + 256], gq[:, 256 * j:256 * j + 256])
            for c in range(2):
                q_ref[sl, 256 * j + LANES * c:256 * j + LANES * (c + 1)] = rot(
                    qn[:, LANES * c:LANES * (c + 1)]).astype(BF16)
        k = rot(head_norm(y[:, KV_OFF:KV_OFF + LANES], gk_ref[...]))
        v = y[:, KV_OFF + LANES:KV_OFF + 2 * LANES]
        lo_half = lax.broadcasted_iota(jnp.int32, k.shape, 1) < HEAD_DIM
        for c, t in enumerate((k, v)):
            sw = pltpu.roll(t, HEAD_DIM, 1)
            kv_ref[sl, 2 * c * LANES:(2 * c + 1) * LANES] = jnp.where(lo_half, t, sw).astype(BF16)
            kv_ref[sl, (2 * c + 1) * LANES:(2 * c + 2) * LANES] = jnp.where(lo_half, sw, t).astype(BF16)


def _inproj(h, g, sh, sc, w, gq, gk, gm, tables, *, li, tm, seq):
    t, d = h.shape
    tpb = seq // tm
    rope = tables is not None
    row = lambda i: (i, 0)
    fix = lambda i: (0, 0)
    mod = lambda i: (i // tpb, 0, 0)
    in_specs = [pl.BlockSpec((tm, d), row),
                pl.BlockSpec((1, d), fix),
                pl.BlockSpec((None, 1, d), mod),
                pl.BlockSpec((None, 1, d), mod),
                pl.BlockSpec((None, d, D_IN), lambda i: (li, 0, 0)),
                pl.BlockSpec((1, D_ATTN), fix),
                pl.BlockSpec((1, LANES), fix),
                pl.BlockSpec((256, 256), fix)]
    args = [h, g, sh, sc, w, gq, gk, gm]
    if rope:
        in_specs += [pl.BlockSpec((tm, LANES), lambda i: (i % tpb, 0))] * 3
        args += list(tables)
    return pl.pallas_call(
        functools.partial(_inproj_kernel, rope=rope, parts=max(tm // 256, 1)),
        grid=(t // tm,),
        in_specs=in_specs,
        out_specs=[pl.BlockSpec((tm, 2 * D_CONV), row),
                   pl.BlockSpec((tm, D_ATTN), row),
                   pl.BlockSpec((tm, 4 * LANES), row)],
        out_shape=[jax.ShapeDtypeStruct((t, 2 * D_CONV), BF16),
                   jax.ShapeDtypeStruct((t, D_ATTN), BF16),
                   jax.ShapeDtypeStruct((t, 4 * LANES), BF16)],
        compiler_params=_params(1),
        name="inproj_rope" if rope else "inproj_ctx",
    )(*args)


_NT = (((1,), (1,)), ((), ()))


def _mixer_kernel(*refs, tq, windowed):
    if windowed:
        (sink_ref, h_ref, cu_ref, cup_ref, cun_ref, q_ref, kvp_ref, kv_ref, kvn_ref, kvc_ref,
         cw_ref, gc_ref, ga_ref, wo_ref, g1_ref, out_ref, kw_ref, ya_ref) = refs
    else:
        (sink_ref, h_ref, cu_ref, q_ref, kvc_ref,
         cw_ref, gc_ref, ga_ref, wo_ref, g1_ref, out_ref, ya_ref) = refs
    i = pl.program_id(1)
    nt = pl.num_programs(1)
    nsub = tq // WINDOW

    cu = cu_ref[...]
    bg = cu[:, 0:D_CONV].astype(F32)
    u = cu[:, D_CONV:2 * D_CONV].astype(F32)
    rows = lax.broadcasted_iota(jnp.int32, (tq, 1), 0)
    if windowed:
        up_row = cup_ref[:, D_CONV:2 * D_CONV].astype(F32)[15:16, :]
        un_row = cun_ref[:, D_CONV:2 * D_CONV].astype(F32)[0:1, :]
        up_row = jnp.where(i > 0, up_row, 0.0)
        un_row = jnp.where(i < nt - 1, un_row, 0.0)
    else:
        up_row = jnp.zeros((1, D_CONV), F32)
        un_row = jnp.zeros((1, D_CONV), F32)
    u_prev = jnp.where(rows == 0, up_row, pltpu.roll(u, 1, 0))
    u_next = jnp.where(rows == tq - 1, un_row, pltpu.roll(u, tq - 1, 0))
    cw = cw_ref[...]
    yc = bg * (cw[0:1, :] * u_prev + cw[1:2, :] * u + cw[2:3, :] * u_next)
    yc = yc * lax.rsqrt(jnp.mean(yc * yc, axis=-1, keepdims=True) + EPS) * gc_ref[...]

    if windowed:
        kw_ref[0:WINDOW, :] = kvp_ref[...]
        kw_ref[WINDOW:WINDOW + tq, :] = kv_ref[...]
        kw_ref[WINDOW + tq:2 * WINDOW + tq, :] = kvn_ref[...]
    lane_lo = lax.broadcasted_iota(jnp.int32, (WINDOW, LANES), 1) < HEAD_DIM
    kvc = kvc_ref[...]
    gqa = N_HEADS // N_KV
    head_of_row = lax.broadcasted_iota(jnp.int32, (gqa * WINDOW, 1), 0) // WINDOW
    ones_c = jnp.ones((kvc.shape[0], LANES), BF16)
    if windowed:
        kk = lax.broadcasted_iota(jnp.int32, (WINDOW, WINDOW), 1)
        qq = lax.broadcasted_iota(jnp.int32, (WINDOW, WINDOW), 0)
        keep_before = kk >= qq
        keep_after = kk <= qq
        ones_w = jnp.ones((3 * WINDOW, LANES), BF16)

    def sub_block(s):
        r0 = s * WINDOW
        if windowed:
            kwin = kw_ref[pl.ds(r0, 3 * WINDOW), :]
            before, after = keep_before, keep_after
            if s == 0:
                before = before & (i > 0)
            if s == nsub - 1:
                after = after & (i < nt - 1)
        for grp in range(N_KV):
            rows, sk = [], jnp.zeros((gqa * WINDOW, 1), F32)
            for hh in range(gqa):
                head = gqa * grp + hh
                qp = q_ref[pl.ds(r0, WINDOW), LANES * (head // 2):LANES * (head // 2 + 1)]
                zero = jnp.zeros_like(qp)
                rows.append(jnp.where(lane_lo, zero, qp) if head % 2 else jnp.where(lane_lo, qp, zero))
                sk = jnp.where(head_of_row == hh, sink_ref[head] * LOG2E, sk)
            q4 = jnp.concatenate(rows, axis=0)
            s_c = lax.dot_general(q4, kvc[:, LANES * grp:LANES * (grp + 1)], _NT, preferred_element_type=F32)
            m = jnp.maximum(jnp.max(s_c, axis=-1, keepdims=True), sk)
            if windowed:
                s_w = lax.dot_general(q4, kwin[:, LANES * grp:LANES * (grp + 1)], _NT,
                                      preferred_element_type=F32).reshape(gqa, WINDOW, 3 * WINDOW)
                s_w = jnp.concatenate(
                    [jnp.where(before[None], s_w[:, :, 0:WINDOW], NEG_INF),
                     s_w[:, :, WINDOW:2 * WINDOW],
                     jnp.where(after[None], s_w[:, :, 2 * WINDOW:3 * WINDOW], NEG_INF)],
                    axis=-1).reshape(gqa * WINDOW, 3 * WINDOW)
                m = jnp.maximum(m, jnp.max(s_w, axis=-1, keepdims=True))
            v_c = jnp.concatenate([kvc[:, LANES * (2 + grp):LANES * (3 + grp)], ones_c], axis=1)
            o = jnp.dot(jnp.exp2(s_c - m).astype(BF16), v_c, preferred_element_type=F32)
            if windowed:
                v_w = jnp.concatenate([kwin[:, LANES * (2 + grp):LANES * (3 + grp)], ones_w], axis=1)
                o = o + jnp.dot(jnp.exp2(s_w - m).astype(BF16), v_w, preferred_element_type=F32)
            o = o[:, 0:LANES] / (o[:, LANES:2 * LANES] + jnp.exp2(sk - m))
            for pr in range(gqa // 2):
                pair = (gqa // 2) * grp + pr
                even = o[2 * pr * WINDOW:(2 * pr + 1) * WINDOW]
                odd = o[(2 * pr + 1) * WINDOW:(2 * pr + 2) * WINDOW]
                ya_ref[pl.ds(r0, WINDOW), LANES * pair:LANES * (pair + 1)] = jnp.where(lane_lo, even, odd)

    part = 2 * WINDOW
    ycb = yc.astype(BF16)
    for p in range(tq // part):
        sub_block(2 * p)
        sub_block(2 * p + 1)
        sl = slice(p * part, (p + 1) * part)
        ya = ya_ref[sl, :]
        ya = ya * lax.rsqrt(jnp.mean(ya * ya, axis=-1, keepdims=True) + EPS) * ga_ref[...]
        y = (jnp.dot(ycb[sl], wo_ref[0:D_CONV, :], preferred_element_type=F32)
             + jnp.dot(ya.astype(BF16), wo_ref[D_CONV:2 * D_CONV, :], preferred_element_type=F32))
        out_ref[sl, :] = h_ref[sl, :] + g1_ref[...] * y


def _mixer(h, cu, q, kv, kvc, sink, cw, gc, ga, wo, g1, *, li, tq, seq, ctx_len, windowed):
    t, d = h.shape
    nt = seq // tq
    nb = t // seq
    row = lambda b, i: (b * nt + i, 0)
    fix = lambda b, i: (0, 0)
    smem = pl.BlockSpec(memory_space=pltpu.SMEM)
    tail = [pl.BlockSpec((3, D_CONV), fix),
            pl.BlockSpec((1, D_CONV), fix),
            pl.BlockSpec((1, D_ATTN), fix),
            pl.BlockSpec((None, d, d), lambda b, i: (li, 0, 0)),
            pl.BlockSpec((None, 1, d), lambda b, i: (b, 0, 0))]
    ctx_spec = pl.BlockSpec((ctx_len, 4 * LANES), lambda b, i: (b, 0))
    if windowed:
        r16 = tq // 16
        n16 = t // 16
        rw = tq // WINDOW
        nw = t // WINDOW
        in_specs = [smem,
                    pl.BlockSpec((tq, d), row),
                    pl.BlockSpec((tq, 2 * D_CONV), row),
                    pl.BlockSpec((16, 2 * D_CONV), lambda b, i: (jnp.maximum((b * nt + i) * r16 - 1, 0), 0)),
                    pl.BlockSpec((16, 2 * D_CONV), lambda b, i: (jnp.minimum((b * nt + i + 1) * r16, n16 - 1), 0)),
                    pl.BlockSpec((tq, D_ATTN), row),
                    pl.BlockSpec((WINDOW, 4 * LANES), lambda b, i: (jnp.maximum((b * nt + i) * rw - 1, 0), 0)),
                    pl.BlockSpec((tq, 4 * LANES), row),
                    pl.BlockSpec((WINDOW, 4 * LANES), lambda b, i: (jnp.minimum((b * nt + i + 1) * rw, nw - 1), 0)),
                    ctx_spec] + tail
        args = [sink, h, cu, cu, cu, q, kv, kv, kv, kvc, cw, gc, ga, wo, g1]
        scratch = [pltpu.VMEM((tq + 2 * WINDOW, 4 * LANES), BF16), pltpu.VMEM((tq, D_ATTN), F32)]
    else:
        in_specs = [smem,
                    pl.BlockSpec((tq, d), row),
                    pl.BlockSpec((tq, 2 * D_CONV), row),
                    pl.BlockSpec((tq, D_ATTN), row),
                    ctx_spec] + tail
        args = [sink, h, cu, q, kvc, cw, gc, ga, wo, g1]
        scratch = [pltpu.VMEM((tq, D_ATTN), F32)]
    return pl.pallas_call(
        functools.partial(_mixer_kernel, tq=tq, windowed=windowed),
        grid=(nb, nt),
        in_specs=in_specs,
        out_specs=pl.BlockSpec((tq, d), row),
        out_shape=jax.ShapeDtypeStruct((t, d), F32),
        scratch_shapes=scratch,
        compiler_params=_params(2),
        name="mixer_win" if windowed else "mixer_ctx",
    )(*args)


def _dense_ffn_kernel(h_ref, g_ref, sh_ref, sc_ref, gate_ref, w1_ref, w3_ref, w2_ref, out_ref, *, parts):
    rows = h_ref.shape[0] // parts
    for p in range(parts):
        sl = slice(p * rows, (p + 1) * rows)
        hp = h_ref[sl, :]
        xb = _norm_mod(hp, g_ref[...], sh_ref[...], sc_ref[...]).astype(BF16)
        h1 = jnp.dot(xb, w1_ref[...], preferred_element_type=F32)
        h3 = jnp.dot(xb, w3_ref[...], preferred_element_type=F32)
        a = (h1 * _sigmoid(h1) * h3).astype(BF16)
        y = jnp.dot(a, w2_ref[...], preferred_element_type=F32)
        out_ref[sl, :] = hp + gate_ref[...] * y


def _dense_ffn(h, g, sh, sc, gate, w1, w3, w2, *, li, tm, seq, parts):
    t, d = h.shape
    f = w1.shape[2]
    wfix = lambda i: (li, 0, 0)
    tpb = seq // tm
    row = lambda i: (i, 0)
    fix = lambda i: (0, 0)
    mod = lambda i: (i // tpb, 0, 0)
    once = pl.Buffered(1)
    return pl.pallas_call(
        functools.partial(_dense_ffn_kernel, parts=parts),
        grid=(t // tm,),
        in_specs=[pl.BlockSpec((tm, d), row),
                  pl.BlockSpec((1, d), fix),
                  pl.BlockSpec((None, 1, d), mod),
                  pl.BlockSpec((None, 1, d), mod),
                  pl.BlockSpec((None, 1, d), mod),
                  pl.BlockSpec((None, d, f), wfix, pipeline_mode=once),
                  pl.BlockSpec((None, d, f), wfix, pipeline_mode=once),
                  pl.BlockSpec((None, f, d), wfix, pipeline_mode=once)],
        out_specs=pl.BlockSpec((tm, d), row),
        out_shape=jax.ShapeDtypeStruct((t, d), F32),
        compiler_params=_params(1),
        name="ffn_dense",
    )(h, g, sh, sc, gate, w1, w3, w2)


def _pack_w13_kernel(w1_ref, w3_ref, o_ref):
    fe = w1_ref.shape[1]
    o_ref[:, 0:fe] = w1_ref[...].astype(BF16)
    o_ref[:, fe:2 * fe] = w3_ref[...].astype(BF16)


def _pack_w13(w1, w3):
    nl, ne, d, fe = w1.shape
    rows = d // 2
    spec = pl.BlockSpec((None, None, rows, fe), lambda l, e, r: (l, e, r, 0))
    return pl.pallas_call(
        _pack_w13_kernel,
        grid=(nl, ne, d // rows),
        in_specs=[spec, spec],
        out_specs=pl.BlockSpec((None, None, rows, 2 * fe), lambda l, e, r: (l, e, r, 0)),
        out_shape=jax.ShapeDtypeStruct((nl, ne, d, 2 * fe), BF16),
        compiler_params=_params(3),
        name="pack_w13",
    )(w1, w3)


def _top2(logits):
    lane = lax.broadcasted_iota(jnp.int32, logits.shape, 1)
    lg = jnp.where(lane < N_EXPERTS, logits, NEG_INF)
    m1 = jnp.max(lg, axis=-1, keepdims=True)
    i1 = jnp.min(jnp.where(lg == m1, lane, LANES), axis=-1, keepdims=True)
    lg2 = jnp.where(lane == i1, NEG_INF, lg)
    m2 = jnp.max(lg2, axis=-1, keepdims=True)
    i2 = jnp.min(jnp.where(lg2 == m2, lane, LANES), axis=-1, keepdims=True)
    e2 = jnp.exp(m2 - m1)
    return i1, i2, 1.0 / (1.0 + e2), e2 / (1.0 + e2)


def _router_kernel(h_ref, g_ref, sh_ref, sc_ref, r_ref, tri_ref, xn_ref, route_ref, rt_ref, cnt_ref,
                   base_ref):
    tm = h_ref.shape[0]

    @pl.when(pl.program_id(0) == 0)
    def _():
        base_ref[...] = jnp.zeros_like(base_ref)

    xn = _norm_mod(h_ref[...], g_ref[...], sh_ref[...], sc_ref[...])
    xn_ref[...] = xn.reshape(xn_ref.shape)
    i1, i2, g1, g2 = _top2(jnp.dot(xn, r_ref[...], preferred_element_type=F32))
    lane = lax.broadcasted_iota(jnp.int32, (tm, LANES), 1)
    hit1 = lane == i1
    hit2 = lane == i2
    chosen = jnp.where(jnp.logical_or(hit1, hit2), 1.0, 0.0)
    before = base_ref[...] + jnp.dot(tri_ref[...], chosen.astype(BF16), preferred_element_type=F32)
    r1 = jnp.sum(jnp.where(hit1, before, 0.0), axis=-1, keepdims=True)
    r2 = jnp.sum(jnp.where(hit2, before, 0.0), axis=-1, keepdims=True)
    base_ref[...] += jnp.sum(chosen, axis=0, keepdims=True)
    cnt_ref[...] = base_ref[...]
    fields = (i1.astype(F32), i2.astype(F32), g1, g2, r1, r2)
    route = jnp.zeros((tm, LANES), F32)
    for k, f in enumerate(fields):
        route = jnp.where(lane == k, f, route)
    route_ref[...] = route
    rt_ref[...] = route.T[0:8, :]


def _router(h, g, sh, sc, router, *, tm, seq):
    t, d = h.shape
    tpb = seq // tm
    row = lambda i: (i, 0)
    fix = lambda i: (0, 0)
    mod = lambda i: (i // tpb, 0, 0)
    ids = jnp.arange(tm)
    tri = (ids[None, :] < ids[:, None]).astype(BF16)
    return pl.pallas_call(
        _router_kernel,
        grid=(t // tm,),
        in_specs=[pl.BlockSpec((tm, d), row), pl.BlockSpec((1, d), fix),
                  pl.BlockSpec((None, 1, d), mod), pl.BlockSpec((None, 1, d), mod),
                  pl.BlockSpec((d, LANES), fix), pl.BlockSpec((tm, tm), fix)],
        out_specs=[pl.BlockSpec((tm, d // LANES, LANES), lambda i: (i, 0, 0)),
                   pl.BlockSpec((tm, LANES), row),
                   pl.BlockSpec((None, 8, tm), lambda i: (i, 0, 0)),
                   pl.BlockSpec((1, LANES), fix)],
        out_shape=[jax.ShapeDtypeStruct((t, d // LANES, LANES), F32),
                   jax.ShapeDtypeStruct((t, LANES), F32),
                   jax.ShapeDtypeStruct((t // tm, 8, tm), F32),
                   jax.ShapeDtypeStruct((1, LANES), F32)],
        scratch_shapes=[pltpu.VMEM((1, LANES), F32)],
        compiler_params=_params(1),
        name="moe_router",
    )(h, g, sh, sc, router, tri)


def _route_plan(t, counts, tr):
    counts = counts[0, 0:N_EXPERTS].astype(jnp.int32)
    tiles = (counts + tr - 1) // tr
    tile_end = jnp.cumsum(tiles)
    tile_start = tile_end - tiles
    experts = jnp.arange(N_EXPERTS, dtype=jnp.int32)

    nt = 2 * t // tr + N_EXPERTS
    tid = jnp.arange(nt, dtype=jnp.int32)
    tile_expert = jnp.minimum(jnp.sum((tid[:, None] >= tile_end[None, :]).astype(jnp.int32), axis=1),
                              N_EXPERTS - 1)
    in_tile = tid - jnp.sum(jnp.where(tile_expert[:, None] == experts[None, :], tile_start[None, :], 0), axis=1)
    own = jnp.sum(jnp.where(tile_expert[:, None] == experts[None, :], counts[None, :], 0), axis=1)
    n_valid = jnp.where(tid < tile_end[-1], jnp.clip(own - in_tile * tr, 0, tr), 0)
    tail = tile_end[-1] + experts
    pad_tiles = jnp.concatenate([jnp.where(tiles > 0, tile_end - 1, -1), jnp.where(tail < nt, tail, -1)])
    return tile_expert, n_valid, tile_start * tr, pad_tiles


def _table_kernel(start_ref, rt_ref, tab_ref, *, nt):
    tm = rt_ref.shape[1]

    def position(e, r):
        start = jnp.zeros_like(r)
        for k in range(N_EXPERTS):
            start = jnp.where(e == float(k), start_ref[k].astype(F32), start)
        return (start + r).astype(jnp.int32)

    live = pl.program_id(0) < nt
    rt = rt_ref[...]
    for c in range(2):
        pos = position(rt[c:c + 1, :], rt[4 + c:5 + c, :])
        tab_ref[:, c * tm:(c + 1) * tm] = jnp.where(live, pos, 0)


def _position_table(route_t, row_start):
    nt, fields, tm = route_t.shape
    grid_spec = pltpu.PrefetchScalarGridSpec(
        num_scalar_prefetch=1,
        grid=(nt + 2,),
        in_specs=[pl.BlockSpec((None, fields, tm), lambda j, st: (jnp.minimum(j, nt - 1), 0, 0))],
        out_specs=pl.BlockSpec((None, 1, 2 * tm), lambda j, st: (j, 0, 0)))
    return pl.pallas_call(
        functools.partial(_table_kernel, nt=nt),
        grid_spec=grid_spec,
        out_shape=jax.ShapeDtypeStruct((nt + 2, 1, 2 * tm), jnp.int32),
        compiler_params=_params(1),
        name="moe_table",
    )(row_start, route_t)


def _row_copies(idx_smem, s_idx, tm, make):
    base = s_idx * (2 * tm)

    def body(r, c):
        make(r, idx_smem[base + r], idx_smem[base + tm + r])
        return c

    lax.fori_loop(0, tm, body, 0, unroll=8)


def _dispatch_kernel(zt_ref, idx_hbm, xn_hbm, xg_hbm, idx_smem, zbuf, xbuf,
                     sem_d, sem_i, sem_z, sem_in, *, tm, nt, tr):
    j = pl.program_id(0)
    slot = j % 2
    other = 1 - slot
    cur = j % 3
    nxt = (j + 1) % 3

    def in_copy(tile, s):
        return pltpu.make_async_copy(xn_hbm.at[pl.ds(tile * tm, tm)], xbuf.at[s], sem_in.at[s])

    @pl.when(j == 0)
    def _():
        zbuf[...] = jnp.zeros_like(zbuf)
        for k in range(zt_ref.shape[0]):
            fill = pltpu.make_async_copy(zbuf, xg_hbm.at[pl.ds(jnp.maximum(zt_ref[k], 0) * tr, tr)], sem_z)
            pl.when(zt_ref[k] >= 0)(fill.start)
        for k in range(zt_ref.shape[0]):
            fill = pltpu.make_async_copy(zbuf, xg_hbm.at[pl.ds(0, tr)], sem_z)
            pl.when(zt_ref[k] >= 0)(fill.wait)

    def idx_copy(row, s):
        return pltpu.make_async_copy(idx_hbm.at[row, 0], idx_smem.at[pl.ds(s * 2 * tm, 2 * tm)], sem_i.at[s])

    def wait_rows(s):
        for _ in range(2):
            pltpu.make_async_copy(xbuf.at[s], xg_hbm.at[pl.ds(0, tm)], sem_d.at[s]).wait()

    @pl.when(j == 0)
    def _():
        idx_copy(0, 0).start()
        in_copy(0, 0).start()

    @pl.when(j >= 2)
    def _():
        wait_rows(nxt)

    @pl.when(j + 1 < nt)
    def _():
        in_copy(j + 1, nxt).start()

    idx_copy(j, slot).wait()
    idx_copy(j + 1, other).start()
    in_copy(j, cur).wait()

    def make(r, p1, p2):
        src = xbuf.at[cur, r]
        pltpu.make_async_copy(src, xg_hbm.at[p1], sem_d.at[cur]).start(priority=0)
        pltpu.make_async_copy(src, xg_hbm.at[p2], sem_d.at[cur]).start(priority=1)

    _row_copies(idx_smem, slot, tm, make)

    @pl.when(j == nt - 1)
    def _():
        if nt > 1:
            wait_rows((nt - 2) % 3)
        wait_rows((nt - 1) % 3)
        idx_copy(j + 1, other).wait()


def _dispatch(xn3, table, pad_tiles, *, tm, tr, n_rows):
    t = xn3.shape[0]
    nt = t // tm
    any_spec = pl.BlockSpec(memory_space=pl.ANY)
    grid_spec = pltpu.PrefetchScalarGridSpec(
        num_scalar_prefetch=1,
        grid=(nt,),
        in_specs=[any_spec, any_spec],
        out_specs=any_spec,
        scratch_shapes=[pltpu.SMEM((4 * tm,), jnp.int32),
                        pltpu.VMEM((tr,) + xn3.shape[1:], F32),
                        pltpu.VMEM((3, tm) + xn3.shape[1:], F32),
                        pltpu.SemaphoreType.DMA((3,)), pltpu.SemaphoreType.DMA((2,)),
                        pltpu.SemaphoreType.DMA, pltpu.SemaphoreType.DMA((3,))])
    return pl.pallas_call(
        functools.partial(_dispatch_kernel, tm=tm, nt=nt, tr=tr),
        grid_spec=grid_spec,
        out_shape=jax.ShapeDtypeStruct((n_rows,) + xn3.shape[1:], F32),
        compiler_params=pltpu.CompilerParams(dimension_semantics=("arbitrary",),
                                             vmem_limit_bytes=VMEM_LIMIT,
                                             disable_bounds_checks=True),
        name="moe_dispatch",
    )(pad_tiles, table, xn3)


def _expert_kernel(te_ref, nv_ref, x_ref, w13_ref, w2_ref, y_ref):
    tr = x_ref.shape[0]
    nv = nv_ref[pl.program_id(0)]

    @pl.when(nv > 0)
    def _():
        half = tr // 2
        for p in range(2):
            x = x_ref[p * half:(p + 1) * half].reshape(half, D_MODEL).astype(BF16)
            h13 = jnp.dot(x, w13_ref[...], preferred_element_type=F32)
            fe = h13.shape[1] // 2
            h1, h3 = h13[:, 0:fe], h13[:, fe:2 * fe]
            a = (h1 * _sigmoid(h1) * h3).astype(BF16)
            y = jnp.dot(a, w2_ref[...], preferred_element_type=F32)
            y_ref[p * half:(p + 1) * half] = y.reshape((half,) + y_ref.shape[1:])

    @pl.when(nv == 0)
    def _():
        y_ref[...] = jnp.zeros_like(y_ref)


def _experts(xg3, tile_expert, n_valid, w13, w2, *, li, tr):
    n_rows, sl, ln = xg3.shape
    d = sl * ln
    fe = w2.shape[2]
    rows = lambda j, te, nv: (j, 0, 0)
    wsel = lambda j, te, nv: (li, te[j], 0, 0)
    grid_spec = pltpu.PrefetchScalarGridSpec(
        num_scalar_prefetch=2,
        grid=(n_rows // tr,),
        in_specs=[pl.BlockSpec((tr, sl, ln), rows),
                  pl.BlockSpec((None, None, d, 2 * fe), wsel),
                  pl.BlockSpec((None, None, fe, d), wsel)],
        out_specs=pl.BlockSpec((tr, sl, ln), rows))
    return pl.pallas_call(
        _expert_kernel,
        grid_spec=grid_spec,
        out_shape=jax.ShapeDtypeStruct(xg3.shape, F32),
        compiler_params=_params(1),
        name="moe_experts",
    )(tile_expert, n_valid, xg3, w13, w2)


def _combine_kernel(idx_hbm, h_ref, gate_ref, route_ref, yg_hbm, out_ref,
                    y1buf, y2buf, idx_smem, sem_y, sem_i, *, tm, nt):
    j = pl.program_id(0)
    slot = j % 2
    other = 1 - slot

    def idx_copy(row, s):
        return pltpu.make_async_copy(idx_hbm.at[row, 0], idx_smem.at[pl.ds(s * 2 * tm, 2 * tm)], sem_i.at[s])

    def fetch(s_idx, s_buf):
        def make(r, p1, p2):
            pltpu.make_async_copy(yg_hbm.at[p1], y1buf.at[s_buf, r], sem_y.at[s_buf]).start(priority=0)
            pltpu.make_async_copy(yg_hbm.at[p2], y2buf.at[s_buf, r], sem_y.at[s_buf]).start(priority=1)

        _row_copies(idx_smem, s_idx, tm, make)

    def wait_rows(s):
        pltpu.make_async_copy(yg_hbm.at[pl.ds(0, tm)], y1buf.at[s], sem_y.at[s]).wait()
        pltpu.make_async_copy(yg_hbm.at[pl.ds(0, tm)], y2buf.at[s], sem_y.at[s]).wait()

    @pl.when(j == 0)
    def _():
        first = idx_copy(0, 0)
        first.start()
        first.wait()
        fetch(0, 0)
        idx_copy(1, 1).start()

    idx_copy(j + 1, other).wait()
    fetch(other, other)
    idx_copy(j + 2, slot).start()
    wait_rows(slot)
    rt = route_ref[...]
    y1 = y1buf[slot].reshape(tm, D_MODEL)
    y2 = y2buf[slot].reshape(tm, D_MODEL)
    out_ref[...] = h_ref[...] + gate_ref[...] * (rt[:, 2:3] * y1 + rt[:, 3:4] * y2)

    @pl.when(j == nt - 1)
    def _():
        wait_rows(other)
        idx_copy(j + 2, slot).wait()


def _combine(h, gate, route, table, yg3, *, tm, seq):
    t, d = h.shape
    tpb = seq // tm
    nt = t // tm
    sl, ln = yg3.shape[1:]
    row = lambda i: (i, 0)
    any_spec = pl.BlockSpec(memory_space=pl.ANY)
    return pl.pallas_call(
        functools.partial(_combine_kernel, tm=tm, nt=nt),
        grid=(nt,),
        in_specs=[any_spec,
                  pl.BlockSpec((tm, d), row),
                  pl.BlockSpec((None, 1, d), lambda i: (i // tpb, 0, 0)),
                  pl.BlockSpec((tm, LANES), row),
                  any_spec],
        out_specs=pl.BlockSpec((tm, d), row),
        out_shape=jax.ShapeDtypeStruct((t, d), F32),
        scratch_shapes=[pltpu.VMEM((2, tm, sl, ln), F32), pltpu.VMEM((2, tm, sl, ln), F32),
                        pltpu.SMEM((4 * tm,), jnp.int32),
                        pltpu.SemaphoreType.DMA((2,)), pltpu.SemaphoreType.DMA((2,))],
        compiler_params=pltpu.CompilerParams(dimension_semantics=("arbitrary",),
                                             vmem_limit_bytes=VMEM_LIMIT,
                                             disable_bounds_checks=True),
        name="moe_combine",
    )(table, h, gate, route, yg3)


def _moe(h, g, sh, sc, gate, w13, w2, router, *, li, seq, tr, tm):
    t = h.shape[0]
    xn3, route, route_t, counts = _router(h, g, sh, sc, router, tm=tm, seq=seq)
    tile_expert, n_valid, row_start, pad_tiles = _route_plan(t, counts, tr)
    table = _position_table(route_t, row_start)
    xg3 = _dispatch(xn3, table, pad_tiles, tm=tm, tr=tr, n_rows=2 * t + N_EXPERTS * tr)
    yg3 = _experts(xg3, tile_expert, n_valid, w13, w2, li=li, tr=tr)
    return _combine(h, gate, route, table, yg3, tm=tm, seq=seq)


def _rope_tables(seq):
    rows = seq // GRID_W
    row, col = jnp.meshgrid(jnp.arange(rows, dtype=F32), jnp.arange(GRID_W, dtype=F32), indexing='ij')
    n_freq = HEAD_DIM // 4
    inv_freq = ROPE_THETA ** (-jnp.arange(n_freq, dtype=F32) / n_freq)
    ang_r = row.reshape(-1, 1) * inv_freq
    ang_c = col.reshape(-1, 1) * inv_freq
    ang = jnp.concatenate([ang_r, ang_r, ang_c, ang_c], axis=-1)
    cos, sin = jnp.cos(ang), jnp.sin(ang)
    first = (jnp.arange(HEAD_DIM) % (2 * n_freq)) < n_freq
    sin_a = jnp.where(first, -sin, 0.0)
    sin_b = jnp.where(first, 0.0, sin)
    rep = LANES // HEAD_DIM
    return tuple(jnp.tile(t, (1, rep)) for t in (cos, sin_a, sin_b))


def kernel(x, c, ctx, c_ctx, w_ada, b_ada, norm1_g, norm2_g, w_in, conv_w, q_norm_g, k_norm_g,
           attn_sink, out_norm_conv_g, out_norm_attn_g, w_out, ffn_w1, ffn_w3, ffn_w2,
           moe_router, moe_w1, moe_w3, moe_w2):
    b, s, d = x.shape
    lc = ctx.shape[1]
    depth = w_ada.shape[0]
    assert d == D_MODEL and s % 512 == 0 and lc % 256 == 0 and b + 1 <= 8

    c8 = jnp.zeros((8, d), F32).at[0:b].set(c).at[b].set(c_ctx)
    mod = _modulation(c8, w_ada, b_ada)

    tables = _rope_tables(s)
    ids = jnp.arange(256)
    gm = (ids[:, None] // HEAD_DIM == ids[None, :] // HEAD_DIM).astype(BF16)
    scale = HEAD_DIM ** -0.5 * LOG2E

    h = x.reshape(b * s, d)
    hc = ctx.reshape(b * lc, d)
    w_in_b, w_out_b = w_in.astype(BF16), w_out.astype(BF16)
    dense_w = [w.astype(BF16) for w in (ffn_w1, ffn_w3, ffn_w2)]
    moe_w13 = _pack_w13(moe_w1, moe_w3)
    moe_w2b = moe_w2.astype(BF16)
    for layer in range(depth):
        last = layer == depth - 1
        m = mod[layer]
        lat = [m[0:b, k * d:(k + 1) * d].reshape(b, 1, d) for k in range(6)]
        cx = [jnp.broadcast_to(m[b:b + 1, k * d:(k + 1) * d].reshape(1, 1, d), (b, 1, d)) for k in range(6)]
        g1n = norm1_g[layer].reshape(1, d)
        g2n = norm2_g[layer].reshape(1, d)
        gq = (jnp.tile(q_norm_g[layer], N_HEADS) * scale).reshape(1, D_ATTN)
        gk = jnp.tile(k_norm_g[layer], N_KV).reshape(1, LANES)
        gc = out_norm_conv_g[layer].reshape(1, D_CONV)
        ga = out_norm_attn_g[layer].reshape(1, D_ATTN)
        sink = attn_sink[layer]
        cw = conv_w[layer]

        cu, q, kv = _inproj(h, g1n, lat[0], lat[1], w_in_b, gq, gk, gm, tables, li=layer, tm=1024, seq=s)
        cuc, qc, kvc = _inproj(hc, g1n, cx[0], cx[1], w_in_b, gq, gk, gm, None, li=layer, tm=lc, seq=lc)
        h = _mixer(h, cu, q, kv, kvc, sink, cw, gc, ga, w_out_b, lat[2],
                   li=layer, tq=1024, seq=s, ctx_len=lc, windowed=True)
        if not last:
            hc = _mixer(hc, cuc, qc, None, kvc, sink, cw, gc, ga, w_out_b, cx[2],
                        li=layer, tq=lc, seq=lc, ctx_len=lc, windowed=False)

        i = layer // 2
        if layer % 2 == 0:
            w1, w3, w2 = dense_w
            router = None
            h = _dense_ffn(h, g2n, lat[3], lat[4], lat[5], w1, w3, w2, li=i, tm=512, seq=s, parts=2)
        else:
            w1, w3, w2 = moe_w13, None, moe_w2b
            router = jnp.zeros((d, LANES), F32).at[:, 0:N_EXPERTS].set(moe_router[i])
            h = _moe(h, g2n, lat[3], lat[4], lat[5], w1, w2, router, li=i, seq=s, tr=512, tm=512)
        if not last:
            if router is None:
                hc = _dense_ffn(hc, g2n, cx[3], cx[4], cx[5], w1, w3, w2, li=i, tm=lc, seq=lc, parts=2)
            else:
                hc = _moe(hc, g2n, cx[3], cx[4], cx[5], w1, w2, router, li=i, seq=lc, tr=256, tm=lc)
    return h.reshape(b, s, d)
```

```python
import functools

import jax
import jax.numpy as jnp
from jax import lax
from jax.experimental import pallas as pl
from jax.experimental.pallas import tpu as pltpu

D_MODEL = 1024
GRID_W = 64
HEAD_DIM = 64
D_CONV = 512
D_ATTN = 512
N_HEADS = 8
N_KV = 2
WINDOW = 128
ROPE_THETA = 10000.0
N_EXPERTS = 8
EPS = 1e-6
KV_OFF = 3 * D_CONV + D_ATTN
D_IN = KV_OFF + 2 * N_KV * HEAD_DIM
LANES = 128
VMEM_LIMIT = 48 * 1024 * 1024
BIG_TILE = 1024
MOE_TILE = 512
PART_ROWS = 256

F32 = jnp.float32
BF16 = jnp.bfloat16
NEG_INF = float("-inf")
LOG2E = 1.4426950408889634


def _params(n_axes):
    return pltpu.CompilerParams(dimension_semantics=("arbitrary",) * n_axes,
                                vmem_limit_bytes=VMEM_LIMIT)


def _sigmoid(x):
    return 1.0 / (1.0 + jnp.exp(-x))


def _mod_kernel(c_ref, w_ref, b_ref, o_ref):
    c = c_ref[...]
    s = c * _sigmoid(c)
    o_ref[...] = jnp.dot(s, w_ref[...], preferred_element_type=F32) + b_ref[...]


def _modulation(c8, w_ada, b_ada):
    depth, d, n = w_ada.shape
    tn = 1536
    return pl.pallas_call(
        _mod_kernel,
        grid=(depth, n // tn),
        in_specs=[pl.BlockSpec((8, d), lambda l, j: (0, 0)),
                  pl.BlockSpec((None, d, tn), lambda l, j: (l, 0, j)),
                  pl.BlockSpec((None, 1, tn), lambda l, j: (l, 0, j))],
        out_specs=pl.BlockSpec((None, 8, tn), lambda l, j: (l, 0, j)),
        out_shape=jax.ShapeDtypeStruct((depth, 8, n), F32),
        compiler_params=_params(2),
        name="adaln_mod",
    )(c8, w_ada, b_ada.reshape(depth, 1, n))


def _norm_mod(x, g, sh, sc):
    ms = jnp.mean(x * x, axis=-1, keepdims=True)
    return (x * lax.rsqrt(ms + EPS) * g) * (1.0 + sc) + sh


def _inproj_kernel(*refs, rope, parts):
    if rope:
        (h_ref, g_ref, sh_ref, sc_ref, w_ref, gq_ref, gk_ref, gm_ref,
         cos_ref, sa_ref, sb_ref, cu_ref, q_ref, kv_ref) = refs
    else:
        (h_ref, g_ref, sh_ref, sc_ref, w_ref, gq_ref, gk_ref, gm_ref,
         cu_ref, q_ref, kv_ref) = refs
    gm = gm_ref[...]
    gq = gq_ref[...]
    rows = h_ref.shape[0] // parts

    def head_norm(t, gain):
        w = t.shape[1]
        ss = jnp.dot((t * t).astype(BF16), gm[0:w, 0:w], preferred_element_type=F32)
        return t * lax.rsqrt(ss * (1.0 / HEAD_DIM) + EPS) * gain

    for p in range(parts):
        sl = slice(p * rows, (p + 1) * rows)

        def rot(t):
            if not rope:
                return t
            return (t * cos_ref[sl, :] + pltpu.roll(t, LANES - 16, 1) * sa_ref[sl, :]
                    + pltpu.roll(t, 16, 1) * sb_ref[sl, :])

        xn = _norm_mod(h_ref[sl, :], g_ref[...], sh_ref[...], sc_ref[...])
        y = jnp.dot(xn.astype(BF16), w_ref[...], preferred_element_type=F32)
        cu_ref[sl, 0:D_CONV] = y[:, 0:D_CONV].astype(BF16)
        cu_ref[sl, D_CONV:2 * D_CONV] = (y[:, D_CONV:2 * D_CONV] * y[:, 2 * D_CONV:3 * D_CONV]).astype(BF16)
        for j in range(D_ATTN // 256):
            lo = 3 * D_CONV + 256 * j
            qn = head_norm(y[:, lo:lo + 256], gq[:, 256 * j:256 * j + 256])
            for c in range(2):
                q_ref[sl, 256 * j + LANES * c:256 * j + LANES * (c + 1)] = rot(
                    qn[:, LANES * c:LANES * (c + 1)]).astype(BF16)
        k = rot(head_norm(y[:, KV_OFF:KV_OFF + LANES], gk_ref[...]))
        v = y[:, KV_OFF + LANES:KV_OFF + 2 * LANES]
        lo_half = lax.broadcasted_iota(jnp.int32, k.shape, 1) < HEAD_DIM
        for c, t in enumerate((k, v)):
            sw = pltpu.roll(t, HEAD_DIM, 1)
            kv_ref[sl, 2 * c * LANES:(2 * c + 1) * LANES] = jnp.where(lo_half, t, sw).astype(BF16)
            kv_ref[sl, (2 * c + 1) * LANES:(2 * c + 2) * LANES] = jnp.where(lo_half, sw, t).astype(BF16)


def _inproj(h, g, sh, sc, w, gq, gk, gm, tables, *, li, tm, seq):
    t, d = h.shape
    tpb = seq // tm
    rope = tables is not None
    row = lambda i: (i, 0)
    fix = lambda i: (0, 0)
    mod = lambda i: (i // tpb, 0, 0)
    in_specs = [pl.BlockSpec((tm, d), row),
                pl.BlockSpec((1, d), fix),
                pl.BlockSpec((None, 1, d), mod),
                pl.BlockSpec((None, 1, d), mod),
                pl.BlockSpec((None, d, D_IN), lambda i: (li, 0, 0)),
                pl.BlockSpec((1, D_ATTN), fix),
                pl.BlockSpec((1, LANES), fix),
                pl.BlockSpec((256, 256), fix)]
    args = [h, g, sh, sc, w, gq, gk, gm]
    if rope:
        in_specs += [pl.BlockSpec((tm, LANES), lambda i: (i % tpb, 0))] * 3
        args += list(tables)
    return pl.pallas_call(
        functools.partial(_inproj_kernel, rope=rope, parts=max(tm // PART_ROWS, 1)),
        grid=(t // tm,),
        in_specs=in_specs,
        out_specs=[pl.BlockSpec((tm, 2 * D_CONV), row),
                   pl.BlockSpec((tm, D_ATTN), row),
                   pl.BlockSpec((tm, 4 * LANES), row)],
        out_shape=[jax.ShapeDtypeStruct((t, 2 * D_CONV), BF16),
                   jax.ShapeDtypeStruct((t, D_ATTN), BF16),
                   jax.ShapeDtypeStruct((t, 4 * LANES), BF16)],
        compiler_params=_params(1),
        name="inproj_rope" if rope else "inproj_ctx",
    )(*args)


_NT = (((1,), (1,)), ((), ()))


def _mixer_kernel(*refs, tq, windowed):
    if windowed:
        (sink_ref, h_ref, cu_ref, cup_ref, cun_ref, q_ref, kvp_ref, kv_ref, kvn_ref, kvc_ref,
         cw_ref, gc_ref, ga_ref, wo_ref, g1_ref, out_ref, kw_ref, ya_ref) = refs
    else:
        (sink_ref, h_ref, cu_ref, q_ref, kvc_ref,
         cw_ref, gc_ref, ga_ref, wo_ref, g1_ref, out_ref, ya_ref) = refs
    i = pl.program_id(1)
    nt = pl.num_programs(1)
    nsub = tq // WINDOW

    cu = cu_ref[...]
    bg = cu[:, 0:D_CONV].astype(F32)
    u = cu[:, D_CONV:2 * D_CONV].astype(F32)
    rows = lax.broadcasted_iota(jnp.int32, (tq, 1), 0)
    if windowed:
        up_row = cup_ref[:, D_CONV:2 * D_CONV].astype(F32)[15:16, :]
        un_row = cun_ref[:, D_CONV:2 * D_CONV].astype(F32)[0:1, :]
        up_row = jnp.where(i > 0, up_row, 0.0)
        un_row = jnp.where(i < nt - 1, un_row, 0.0)
    else:
        up_row = jnp.zeros((1, D_CONV), F32)
        un_row = jnp.zeros((1, D_CONV), F32)
    u_prev = jnp.where(rows == 0, up_row, pltpu.roll(u, 1, 0))
    u_next = jnp.where(rows == tq - 1, un_row, pltpu.roll(u, tq - 1, 0))
    cw = cw_ref[...]
    yc = bg * (cw[0:1, :] * u_prev + cw[1:2, :] * u + cw[2:3, :] * u_next)
    yc = yc * lax.rsqrt(jnp.mean(yc * yc, axis=-1, keepdims=True) + EPS) * gc_ref[...]

    if windowed:
        kw_ref[0:WINDOW, :] = kvp_ref[...]
        kw_ref[WINDOW:WINDOW + tq, :] = kv_ref[...]
        kw_ref[WINDOW + tq:2 * WINDOW + tq, :] = kvn_ref[...]
    lane_lo = lax.broadcasted_iota(jnp.int32, (WINDOW, LANES), 1) < HEAD_DIM
    kvc = kvc_ref[...]
    gqa = N_HEADS // N_KV
    head_of_row = lax.broadcasted_iota(jnp.int32, (gqa * WINDOW, 1), 0) // WINDOW
    ones_c = jnp.ones((kvc.shape[0], LANES), BF16)
    if windowed:
        kk = lax.broadcasted_iota(jnp.int32, (WINDOW, WINDOW), 1)
        qq = lax.broadcasted_iota(jnp.int32, (WINDOW, WINDOW), 0)
        keep_before = kk >= qq
        keep_after = kk <= qq
        ones_w = jnp.ones((3 * WINDOW, LANES), BF16)

    def sub_block(s):
        r0 = s * WINDOW
        if windowed:
            kwin = kw_ref[pl.ds(r0, 3 * WINDOW), :]
            before, after = keep_before, keep_after
            if s == 0:
                before = before & (i > 0)
            if s == nsub - 1:
                after = after & (i < nt - 1)
        for grp in range(N_KV):
            rows, sk = [], jnp.zeros((gqa * WINDOW, 1), F32)
            for hh in range(gqa):
                head = gqa * grp + hh
                qp = q_ref[pl.ds(r0, WINDOW), LANES * (head // 2):LANES * (head // 2 + 1)]
                zero = jnp.zeros_like(qp)
                rows.append(jnp.where(lane_lo, zero, qp) if head % 2 else jnp.where(lane_lo, qp, zero))
                sk = jnp.where(head_of_row == hh, sink_ref[head] * LOG2E, sk)
            q4 = jnp.concatenate(rows, axis=0)
            s_c = lax.dot_general(q4, kvc[:, LANES * grp:LANES * (grp + 1)], _NT, preferred_element_type=F32)
            m = jnp.maximum(jnp.max(s_c, axis=-1, keepdims=True), sk)
            if windowed:
                s_w = lax.dot_general(q4, kwin[:, LANES * grp:LANES * (grp + 1)], _NT,
                                      preferred_element_type=F32).reshape(gqa, WINDOW, 3 * WINDOW)
                s_w = jnp.concatenate(
                    [jnp.where(before[None], s_w[:, :, 0:WINDOW], NEG_INF),
                     s_w[:, :, WINDOW:2 * WINDOW],
                     jnp.where(after[None], s_w[:, :, 2 * WINDOW:3 * WINDOW], NEG_INF)],
                    axis=-1).reshape(gqa * WINDOW, 3 * WINDOW)
                m = jnp.maximum(m, jnp.max(s_w, axis=-1, keepdims=True))
            v_c = jnp.concatenate([kvc[:, LANES * (2 + grp):LANES * (3 + grp)], ones_c], axis=1)
            o = jnp.dot(jnp.exp2(s_c - m).astype(BF16), v_c, preferred_element_type=F32)
            if windowed:
                v_w = jnp.concatenate([kwin[:, LANES * (2 + grp):LANES * (3 + grp)], ones_w], axis=1)
                o = o + jnp.dot(jnp.exp2(s_w - m).astype(BF16), v_w, preferred_element_type=F32)
            o = o[:, 0:LANES] / (o[:, LANES:2 * LANES] + jnp.exp2(sk - m))
            for pr in range(gqa // 2):
                pair = (gqa // 2) * grp + pr
                even = o[2 * pr * WINDOW:(2 * pr + 1) * WINDOW]
                odd = o[(2 * pr + 1) * WINDOW:(2 * pr + 2) * WINDOW]
                ya_ref[pl.ds(r0, WINDOW), LANES * pair:LANES * (pair + 1)] = jnp.where(lane_lo, even, odd)

    part = 2 * WINDOW
    ycb = yc.astype(BF16)
    for p in range(tq // part):
        sub_block(2 * p)
        sub_block(2 * p + 1)
        sl = slice(p * part, (p + 1) * part)
        ya = ya_ref[sl, :]
        ya = ya * lax.rsqrt(jnp.mean(ya * ya, axis=-1, keepdims=True) + EPS) * ga_ref[...]
        y = (jnp.dot(ycb[sl], wo_ref[0:D_CONV, :], preferred_element_type=F32)
             + jnp.dot(ya.astype(BF16), wo_ref[D_CONV:2 * D_CONV, :], preferred_element_type=F32))
        out_ref[sl, :] = h_ref[sl, :] + g1_ref[...] * y


def _mixer(h, cu, q, kv, kvc, sink, cw, gc, ga, wo, g1, *, li, tq, seq, ctx_len, windowed):
    t, d = h.shape
    nt = seq // tq
    nb = t // seq
    row = lambda b, i: (b * nt + i, 0)
    fix = lambda b, i: (0, 0)
    smem = pl.BlockSpec(memory_space=pltpu.SMEM)
    tail = [pl.BlockSpec((3, D_CONV), fix),
            pl.BlockSpec((1, D_CONV), fix),
            pl.BlockSpec((1, D_ATTN), fix),
            pl.BlockSpec((None, d, d), lambda b, i: (li, 0, 0)),
            pl.BlockSpec((None, 1, d), lambda b, i: (b, 0, 0))]
    ctx_spec = pl.BlockSpec((ctx_len, 4 * LANES), lambda b, i: (b, 0))
    if windowed:
        r16 = tq // 16
        n16 = t // 16
        rw = tq // WINDOW
        nw = t // WINDOW
        in_specs = [smem,
                    pl.BlockSpec((tq, d), row),
                    pl.BlockSpec((tq, 2 * D_CONV), row),
                    pl.BlockSpec((16, 2 * D_CONV), lambda b, i: (jnp.maximum((b * nt + i) * r16 - 1, 0), 0)),
                    pl.BlockSpec((16, 2 * D_CONV), lambda b, i: (jnp.minimum((b * nt + i + 1) * r16, n16 - 1), 0)),
                    pl.BlockSpec((tq, D_ATTN), row),
                    pl.BlockSpec((WINDOW, 4 * LANES), lambda b, i: (jnp.maximum((b * nt + i) * rw - 1, 0), 0)),
                    pl.BlockSpec((tq, 4 * LANES), row),
                    pl.BlockSpec((WINDOW, 4 * LANES), lambda b, i: (jnp.minimum((b * nt + i + 1) * rw, nw - 1), 0)),
                    ctx_spec] + tail
        args = [sink, h, cu, cu, cu, q, kv, kv, kv, kvc, cw, gc, ga, wo, g1]
        scratch = [pltpu.VMEM((tq + 2 * WINDOW, 4 * LANES), BF16), pltpu.VMEM((tq, D_ATTN), F32)]
    else:
        in_specs = [smem,
                    pl.BlockSpec((tq, d), row),
                    pl.BlockSpec((tq, 2 * D_CONV), row),
                    pl.BlockSpec((tq, D_ATTN), row),
                    ctx_spec] + tail
        args = [sink, h, cu, q, kvc, cw, gc, ga, wo, g1]
        scratch = [pltpu.VMEM((tq, D_ATTN), F32)]
    return pl.pallas_call(
        functools.partial(_mixer_kernel, tq=tq, windowed=windowed),
        grid=(nb, nt),
        in_specs=in_specs,
        out_specs=pl.BlockSpec((tq, d), row),
        out_shape=jax.ShapeDtypeStruct((t, d), F32),
        scratch_shapes=scratch,
        compiler_params=_params(2),
        name="mixer_win" if windowed else "mixer_ctx",
    )(*args)


def _dense_ffn_kernel(h_ref, g_ref, sh_ref, sc_ref, gate_ref, w1_ref, w3_ref, w2_ref, out_ref, *, parts):
    rows = h_ref.shape[0] // parts
    for p in range(parts):
        sl = slice(p * rows, (p + 1) * rows)
        hp = h_ref[sl, :]
        xb = _norm_mod(hp, g_ref[...], sh_ref[...], sc_ref[...]).astype(BF16)
        h1 = jnp.dot(xb, w1_ref[...], preferred_element_type=F32)
        h3 = jnp.dot(xb, w3_ref[...], preferred_element_type=F32)
        a = (h1 * _sigmoid(h1) * h3).astype(BF16)
        y = jnp.dot(a, w2_ref[...], preferred_element_type=F32)
        out_ref[sl, :] = hp + gate_ref[...] * y


def _dense_ffn(h, g, sh, sc, gate, w1, w3, w2, *, li, tm, seq):
    t, d = h.shape
    f = w1.shape[2]
    wfix = lambda i: (li, 0, 0)
    tpb = seq // tm
    row = lambda i: (i, 0)
    fix = lambda i: (0, 0)
    mod = lambda i: (i // tpb, 0, 0)
    once = pl.Buffered(1)
    return pl.pallas_call(
        functools.partial(_dense_ffn_kernel, parts=max(tm // PART_ROWS, 1)),
        grid=(t // tm,),
        in_specs=[pl.BlockSpec((tm, d), row),
                  pl.BlockSpec((1, d), fix),
                  pl.BlockSpec((None, 1, d), mod),
                  pl.BlockSpec((None, 1, d), mod),
                  pl.BlockSpec((None, 1, d), mod),
                  pl.BlockSpec((None, d, f), wfix, pipeline_mode=once),
                  pl.BlockSpec((None, d, f), wfix, pipeline_mode=once),
                  pl.BlockSpec((None, f, d), wfix, pipeline_mode=once)],
        out_specs=pl.BlockSpec((tm, d), row),
        out_shape=jax.ShapeDtypeStruct((t, d), F32),
        compiler_params=_params(1),
        name="ffn_dense",
    )(h, g, sh, sc, gate, w1, w3, w2)


def _pack_w13_kernel(w1_ref, w3_ref, o_ref):
    fe = w1_ref.shape[1]
    o_ref[:, 0:fe] = w1_ref[...].astype(BF16)
    o_ref[:, fe:2 * fe] = w3_ref[...].astype(BF16)


def _pack_w13(w1, w3):
    nl, ne, d, fe = w1.shape
    rows = d // 2
    spec = pl.BlockSpec((None, None, rows, fe), lambda l, e, r: (l, e, r, 0))
    return pl.pallas_call(
        _pack_w13_kernel,
        grid=(nl, ne, d // rows),
        in_specs=[spec, spec],
        out_specs=pl.BlockSpec((None, None, rows, 2 * fe), lambda l, e, r: (l, e, r, 0)),
        out_shape=jax.ShapeDtypeStruct((nl, ne, d, 2 * fe), BF16),
        compiler_params=_params(3),
        name="pack_w13",
    )(w1, w3)


def _top2(logits):
    lane = lax.broadcasted_iota(jnp.int32, logits.shape, 1)
    lg = jnp.where(lane < N_EXPERTS, logits, NEG_INF)
    m1 = jnp.max(lg, axis=-1, keepdims=True)
    i1 = jnp.min(jnp.where(lg == m1, lane, LANES), axis=-1, keepdims=True)
    lg2 = jnp.where(lane == i1, NEG_INF, lg)
    m2 = jnp.max(lg2, axis=-1, keepdims=True)
    i2 = jnp.min(jnp.where(lg2 == m2, lane, LANES), axis=-1, keepdims=True)
    e2 = jnp.exp(m2 - m1)
    return i1, i2, 1.0 / (1.0 + e2), e2 / (1.0 + e2)


def _router_kernel(h_ref, g_ref, sh_ref, sc_ref, r_ref, tri_ref, xn_ref, route_ref, rt_ref, cnt_ref,
                   base_ref):
    tm = h_ref.shape[0]

    @pl.when(pl.program_id(0) == 0)
    def _():
        base_ref[...] = jnp.zeros_like(base_ref)

    xn = _norm_mod(h_ref[...], g_ref[...], sh_ref[...], sc_ref[...])
    xn_ref[...] = xn.reshape(xn_ref.shape)
    i1, i2, g1, g2 = _top2(jnp.dot(xn, r_ref[...], preferred_element_type=F32))
    lane = lax.broadcasted_iota(jnp.int32, (tm, LANES), 1)
    hit1 = lane == i1
    hit2 = lane == i2
    chosen = jnp.where(jnp.logical_or(hit1, hit2), 1.0, 0.0)
    before = base_ref[...] + jnp.dot(tri_ref[...], chosen.astype(BF16), preferred_element_type=F32)
    r1 = jnp.sum(jnp.where(hit1, before, 0.0), axis=-1, keepdims=True)
    r2 = jnp.sum(jnp.where(hit2, before, 0.0), axis=-1, keepdims=True)
    base_ref[...] += jnp.sum(chosen, axis=0, keepdims=True)
    cnt_ref[...] = base_ref[...]
    fields = (i1.astype(F32), i2.astype(F32), g1, g2, r1, r2)
    route = jnp.zeros((tm, LANES), F32)
    for k, f in enumerate(fields):
        route = jnp.where(lane == k, f, route)
    route_ref[...] = route
    rt_ref[...] = route.T[0:8, :]


def _router(h, g, sh, sc, router, *, tm, seq):
    t, d = h.shape
    tpb = seq // tm
    row = lambda i: (i, 0)
    fix = lambda i: (0, 0)
    mod = lambda i: (i // tpb, 0, 0)
    ids = jnp.arange(tm)
    tri = (ids[None, :] < ids[:, None]).astype(BF16)
    return pl.pallas_call(
        _router_kernel,
        grid=(t // tm,),
        in_specs=[pl.BlockSpec((tm, d), row), pl.BlockSpec((1, d), fix),
                  pl.BlockSpec((None, 1, d), mod), pl.BlockSpec((None, 1, d), mod),
                  pl.BlockSpec((d, LANES), fix), pl.BlockSpec((tm, tm), fix)],
        out_specs=[pl.BlockSpec((tm, d // LANES, LANES), lambda i: (i, 0, 0)),
                   pl.BlockSpec((tm, LANES), row),
                   pl.BlockSpec((None, 8, tm), lambda i: (i, 0, 0)),
                   pl.BlockSpec((1, LANES), fix)],
        out_shape=[jax.ShapeDtypeStruct((t, d // LANES, LANES), F32),
                   jax.ShapeDtypeStruct((t, LANES), F32),
                   jax.ShapeDtypeStruct((t // tm, 8, tm), F32),
                   jax.ShapeDtypeStruct((1, LANES), F32)],
        scratch_shapes=[pltpu.VMEM((1, LANES), F32)],
        compiler_params=_params(1),
        name="moe_router",
    )(h, g, sh, sc, router, tri)


def _route_plan(t, counts, tr):
    counts = counts[0, 0:N_EXPERTS].astype(jnp.int32)
    tiles = (counts + tr - 1) // tr
    tile_end = jnp.cumsum(tiles)
    tile_start = tile_end - tiles
    experts = jnp.arange(N_EXPERTS, dtype=jnp.int32)

    nt = 2 * t // tr + N_EXPERTS
    tid = jnp.arange(nt, dtype=jnp.int32)
    tile_expert = jnp.minimum(jnp.sum((tid[:, None] >= tile_end[None, :]).astype(jnp.int32), axis=1),
                              N_EXPERTS - 1)
    in_tile = tid - jnp.sum(jnp.where(tile_expert[:, None] == experts[None, :], tile_start[None, :], 0), axis=1)
    own = jnp.sum(jnp.where(tile_expert[:, None] == experts[None, :], counts[None, :], 0), axis=1)
    n_valid = jnp.where(tid < tile_end[-1], jnp.clip(own - in_tile * tr, 0, tr), 0)
    tail = tile_end[-1] + experts
    pad_tiles = jnp.concatenate([jnp.where(tiles > 0, tile_end - 1, -1), jnp.where(tail < nt, tail, -1)])
    return tile_expert, n_valid, tile_start * tr, pad_tiles


def _table_kernel(start_ref, rt_ref, tab_ref, *, nt):
    tm = rt_ref.shape[1]

    def position(e, r):
        start = jnp.zeros_like(r)
        for k in range(N_EXPERTS):
            start = jnp.where(e == float(k), start_ref[k].astype(F32), start)
        return (start + r).astype(jnp.int32)

    live = pl.program_id(0) < nt
    rt = rt_ref[...]
    for c in range(2):
        pos = position(rt[c:c + 1, :], rt[4 + c:5 + c, :])
        tab_ref[:, c * tm:(c + 1) * tm] = jnp.where(live, pos, 0)


def _position_table(route_t, row_start):
    nt, fields, tm = route_t.shape
    grid_spec = pltpu.PrefetchScalarGridSpec(
        num_scalar_prefetch=1,
        grid=(nt + 2,),
        in_specs=[pl.BlockSpec((None, fields, tm), lambda j, st: (jnp.minimum(j, nt - 1), 0, 0))],
        out_specs=pl.BlockSpec((None, 1, 2 * tm), lambda j, st: (j, 0, 0)))
    return pl.pallas_call(
        functools.partial(_table_kernel, nt=nt),
        grid_spec=grid_spec,
        out_shape=jax.ShapeDtypeStruct((nt + 2, 1, 2 * tm), jnp.int32),
        compiler_params=_params(1),
        name="moe_table",
    )(row_start, route_t)


def _row_copies(idx_smem, s_idx, tm, make):
    base = s_idx * (2 * tm)

    def body(r, c):
        make(r, idx_smem[base + r], idx_smem[base + tm + r])
        return c

    lax.fori_loop(0, tm, body, 0, unroll=8)


def _dispatch_kernel(zt_ref, idx_hbm, xn_hbm, xg_hbm, idx_smem, zbuf, xbuf,
                     sem_d, sem_i, sem_z, sem_in, *, tm, nt, tr):
    j = pl.program_id(0)
    slot = j % 2
    other = 1 - slot
    cur = j % 3
    nxt = (j + 1) % 3

    def in_copy(tile, s):
        return pltpu.make_async_copy(xn_hbm.at[pl.ds(tile * tm, tm)], xbuf.at[s], sem_in.at[s])

    @pl.when(j == 0)
    def _():
        zbuf[...] = jnp.zeros_like(zbuf)
        for k in range(zt_ref.shape[0]):
            fill = pltpu.make_async_copy(zbuf, xg_hbm.at[pl.ds(jnp.maximum(zt_ref[k], 0) * tr, tr)], sem_z)
            pl.when(zt_ref[k] >= 0)(fill.start)
        for k in range(zt_ref.shape[0]):
            fill = pltpu.make_async_copy(zbuf, xg_hbm.at[pl.ds(0, tr)], sem_z)
            pl.when(zt_ref[k] >= 0)(fill.wait)

    def idx_copy(row, s):
        return pltpu.make_async_copy(idx_hbm.at[row, 0], idx_smem.at[pl.ds(s * 2 * tm, 2 * tm)], sem_i.at[s])

    def wait_rows(s):
        for _ in range(2):
            pltpu.make_async_copy(xbuf.at[s], xg_hbm.at[pl.ds(0, tm)], sem_d.at[s]).wait()

    @pl.when(j == 0)
    def _():
        idx_copy(0, 0).start()
        in_copy(0, 0).start()

    @pl.when(j >= 2)
    def _():
        wait_rows(nxt)

    @pl.when(j + 1 < nt)
    def _():
        in_copy(j + 1, nxt).start()

    idx_copy(j, slot).wait()
    idx_copy(j + 1, other).start()
    in_copy(j, cur).wait()

    def make(r, p1, p2):
        src = xbuf.at[cur, r]
        pltpu.make_async_copy(src, xg_hbm.at[p1], sem_d.at[cur]).start(priority=0)
        pltpu.make_async_copy(src, xg_hbm.at[p2], sem_d.at[cur]).start(priority=1)

    _row_copies(idx_smem, slot, tm, make)

    @pl.when(j == nt - 1)
    def _():
        if nt > 1:
            wait_rows((nt - 2) % 3)
        wait_rows((nt - 1) % 3)
        idx_copy(j + 1, other).wait()


def _dispatch(xn3, table, pad_tiles, *, tm, tr, n_rows):
    t = xn3.shape[0]
    nt = t // tm
    any_spec = pl.BlockSpec(memory_space=pl.ANY)
    grid_spec = pltpu.PrefetchScalarGridSpec(
        num_scalar_prefetch=1,
        grid=(nt,),
        in_specs=[any_spec, any_spec],
        out_specs=any_spec,
        scratch_shapes=[pltpu.SMEM((4 * tm,), jnp.int32),
                        pltpu.VMEM((tr,) + xn3.shape[1:], F32),
                        pltpu.VMEM((3, tm) + xn3.shape[1:], F32),
                        pltpu.SemaphoreType.DMA((3,)), pltpu.SemaphoreType.DMA((2,)),
                        pltpu.SemaphoreType.DMA, pltpu.SemaphoreType.DMA((3,))])
    return pl.pallas_call(
        functools.partial(_dispatch_kernel, tm=tm, nt=nt, tr=tr),
        grid_spec=grid_spec,
        out_shape=jax.ShapeDtypeStruct((n_rows,) + xn3.shape[1:], F32),
        compiler_params=pltpu.CompilerParams(dimension_semantics=("arbitrary",),
                                             vmem_limit_bytes=VMEM_LIMIT,
                                             disable_bounds_checks=True),
        name="moe_dispatch",
    )(pad_tiles, table, xn3)


def _expert_kernel(te_ref, nv_ref, x_ref, w13_ref, w2_ref, y_ref):
    tr = x_ref.shape[0]
    nv = nv_ref[pl.program_id(0)]

    @pl.when(nv > 0)
    def _():
        half = tr // 2
        for p in range(2):
            x = x_ref[p * half:(p + 1) * half].reshape(half, D_MODEL).astype(BF16)
            h13 = jnp.dot(x, w13_ref[...], preferred_element_type=F32)
            fe = h13.shape[1] // 2
            h1, h3 = h13[:, 0:fe], h13[:, fe:2 * fe]
            a = (h1 * _sigmoid(h1) * h3).astype(BF16)
            y = jnp.dot(a, w2_ref[...], preferred_element_type=F32)
            y_ref[p * half:(p + 1) * half] = y.reshape((half,) + y_ref.shape[1:])

    @pl.when(nv == 0)
    def _():
        y_ref[...] = jnp.zeros_like(y_ref)


def _experts(xg3, tile_expert, n_valid, w13, w2, *, li, tr):
    n_rows, sl, ln = xg3.shape
    d = sl * ln
    fe = w2.shape[2]
    rows = lambda j, te, nv: (j, 0, 0)
    wsel = lambda j, te, nv: (li, te[j], 0, 0)
    grid_spec = pltpu.PrefetchScalarGridSpec(
        num_scalar_prefetch=2,
        grid=(n_rows // tr,),
        in_specs=[pl.BlockSpec((tr, sl, ln), rows),
                  pl.BlockSpec((None, None, d, 2 * fe), wsel),
                  pl.BlockSpec((None, None, fe, d), wsel)],
        out_specs=pl.BlockSpec((tr, sl, ln), rows))
    return pl.pallas_call(
        _expert_kernel,
        grid_spec=grid_spec,
        out_shape=jax.ShapeDtypeStruct(xg3.shape, F32),
        compiler_params=_params(1),
        name="moe_experts",
    )(tile_expert, n_valid, xg3, w13, w2)


def _combine_kernel(idx_hbm, h_ref, gate_ref, route_ref, yg_hbm, out_ref,
                    y1buf, y2buf, idx_smem, sem_y, sem_i, *, tm, nt):
    j = pl.program_id(0)
    slot = j % 2
    other = 1 - slot

    def idx_copy(row, s):
        return pltpu.make_async_copy(idx_hbm.at[row, 0], idx_smem.at[pl.ds(s * 2 * tm, 2 * tm)], sem_i.at[s])

    def fetch(s_idx, s_buf):
        def make(r, p1, p2):
            pltpu.make_async_copy(yg_hbm.at[p1], y1buf.at[s_buf, r], sem_y.at[s_buf]).start(priority=0)
            pltpu.make_async_copy(yg_hbm.at[p2], y2buf.at[s_buf, r], sem_y.at[s_buf]).start(priority=1)

        _row_copies(idx_smem, s_idx, tm, make)

    def wait_rows(s):
        pltpu.make_async_copy(yg_hbm.at[pl.ds(0, tm)], y1buf.at[s], sem_y.at[s]).wait()
        pltpu.make_async_copy(yg_hbm.at[pl.ds(0, tm)], y2buf.at[s], sem_y.at[s]).wait()

    @pl.when(j == 0)
    def _():
        first = idx_copy(0, 0)
        first.start()
        first.wait()
        fetch(0, 0)
        idx_copy(1, 1).start()

    idx_copy(j + 1, other).wait()
    fetch(other, other)
    idx_copy(j + 2, slot).start()
    wait_rows(slot)
    rt = route_ref[...]
    y1 = y1buf[slot].reshape(tm, D_MODEL)
    y2 = y2buf[slot].reshape(tm, D_MODEL)
    out_ref[...] = h_ref[...] + gate_ref[...] * (rt[:, 2:3] * y1 + rt[:, 3:4] * y2)

    @pl.when(j == nt - 1)
    def _():
        wait_rows(other)
        idx_copy(j + 2, slot).wait()


def _combine(h, gate, route, table, yg3, *, tm, seq):
    t, d = h.shape
    tpb = seq // tm
    nt = t // tm
    sl, ln = yg3.shape[1:]
    row = lambda i: (i, 0)
    any_spec = pl.BlockSpec(memory_space=pl.ANY)
    return pl.pallas_call(
        functools.partial(_combine_kernel, tm=tm, nt=nt),
        grid=(nt,),
        in_specs=[any_spec,
                  pl.BlockSpec((tm, d), row),
                  pl.BlockSpec((None, 1, d), lambda i: (i // tpb, 0, 0)),
                  pl.BlockSpec((tm, LANES), row),
                  any_spec],
        out_specs=pl.BlockSpec((tm, d), row),
        out_shape=jax.ShapeDtypeStruct((t, d), F32),
        scratch_shapes=[pltpu.VMEM((2, tm, sl, ln), F32), pltpu.VMEM((2, tm, sl, ln), F32),
                        pltpu.SMEM((4 * tm,), jnp.int32),
                        pltpu.SemaphoreType.DMA((2,)), pltpu.SemaphoreType.DMA((2,))],
        compiler_params=pltpu.CompilerParams(dimension_semantics=("arbitrary",),
                                             vmem_limit_bytes=VMEM_LIMIT,
                                             disable_bounds_checks=True),
        name="moe_combine",
    )(table, h, gate, route, yg3)


def _moe(h, g, sh, sc, gate, w13, w2, router, *, li, seq, tr, tm):
    t = h.shape[0]
    xn3, route, route_t, counts = _router(h, g, sh, sc, router, tm=tm, seq=seq)
    tile_expert, n_valid, row_start, pad_tiles = _route_plan(t, counts, tr)
    table = _position_table(route_t, row_start)
    xg3 = _dispatch(xn3, table, pad_tiles, tm=tm, tr=tr, n_rows=2 * t + N_EXPERTS * tr)
    yg3 = _experts(xg3, tile_expert, n_valid, w13, w2, li=li, tr=tr)
    return _combine(h, gate, route, table, yg3, tm=tm, seq=seq)


def _rope_tables(seq):
    rows = seq // GRID_W
    row, col = jnp.meshgrid(jnp.arange(rows, dtype=F32), jnp.arange(GRID_W, dtype=F32), indexing='ij')
    n_freq = HEAD_DIM // 4
    inv_freq = ROPE_THETA ** (-jnp.arange(n_freq, dtype=F32) / n_freq)
    ang_r = row.reshape(-1, 1) * inv_freq
    ang_c = col.reshape(-1, 1) * inv_freq
    ang = jnp.concatenate([ang_r, ang_r, ang_c, ang_c], axis=-1)
    cos, sin = jnp.cos(ang), jnp.sin(ang)
    first = (jnp.arange(HEAD_DIM) % (2 * n_freq)) < n_freq
    sin_a = jnp.where(first, -sin, 0.0)
    sin_b = jnp.where(first, 0.0, sin)
    rep = LANES // HEAD_DIM
    return tuple(jnp.tile(t, (1, rep)) for t in (cos, sin_a, sin_b))


def kernel(x, c, ctx, c_ctx, w_ada, b_ada, norm1_g, norm2_g, w_in, conv_w, q_norm_g, k_norm_g,
           attn_sink, out_norm_conv_g, out_norm_attn_g, w_out, ffn_w1, ffn_w3, ffn_w2,
           moe_router, moe_w1, moe_w3, moe_w2):
    b, s, d = x.shape
    lc = ctx.shape[1]
    depth = w_ada.shape[0]
    assert d == D_MODEL and s % BIG_TILE == 0 and lc % PART_ROWS == 0 and b + 1 <= 8

    c8 = jnp.zeros((8, d), F32).at[0:b].set(c).at[b].set(c_ctx)
    mod = _modulation(c8, w_ada, b_ada)

    tables = _rope_tables(s)
    ids = jnp.arange(256)
    gm = (ids[:, None] // HEAD_DIM == ids[None, :] // HEAD_DIM).astype(BF16)
    scale = HEAD_DIM ** -0.5 * LOG2E

    h = x.reshape(b * s, d)
    hc = ctx.reshape(b * lc, d)
    w_in_b, w_out_b = w_in.astype(BF16), w_out.astype(BF16)
    dense_w = [w.astype(BF16) for w in (ffn_w1, ffn_w3, ffn_w2)]
    moe_w13 = _pack_w13(moe_w1, moe_w3)
    moe_w2b = moe_w2.astype(BF16)
    for layer in range(depth):
        last = layer == depth - 1
        m = mod[layer]
        lat = [m[0:b, k * d:(k + 1) * d].reshape(b, 1, d) for k in range(6)]
        cx = [jnp.broadcast_to(m[b:b + 1, k * d:(k + 1) * d].reshape(1, 1, d), (b, 1, d)) for k in range(6)]
        g1n = norm1_g[layer].reshape(1, d)
        g2n = norm2_g[layer].reshape(1, d)
        gq = (jnp.tile(q_norm_g[layer], N_HEADS) * scale).reshape(1, D_ATTN)
        gk = jnp.tile(k_norm_g[layer], N_KV).reshape(1, LANES)
        gc = out_norm_conv_g[layer].reshape(1, D_CONV)
        ga = out_norm_attn_g[layer].reshape(1, D_ATTN)
        sink = attn_sink[layer]
        cw = conv_w[layer]

        cu, q, kv = _inproj(h, g1n, lat[0], lat[1], w_in_b, gq, gk, gm, tables, li=layer, tm=BIG_TILE, seq=s)
        cuc, qc, kvc = _inproj(hc, g1n, cx[0], cx[1], w_in_b, gq, gk, gm, None, li=layer, tm=lc, seq=lc)
        h = _mixer(h, cu, q, kv, kvc, sink, cw, gc, ga, w_out_b, lat[2],
                   li=layer, tq=BIG_TILE, seq=s, ctx_len=lc, windowed=True)
        if not last:
            hc = _mixer(hc, cuc, qc, None, kvc, sink, cw, gc, ga, w_out_b, cx[2],
                        li=layer, tq=lc, seq=lc, ctx_len=lc, windowed=False)

        i = layer // 2
        if layer % 2 == 0:
            h = _dense_ffn(h, g2n, lat[3], lat[4], lat[5], *dense_w, li=i, tm=BIG_TILE, seq=s)
            if not last:
                hc = _dense_ffn(hc, g2n, cx[3], cx[4], cx[5], *dense_w, li=i, tm=lc, seq=lc)
        else:
            router = jnp.zeros((d, LANES), F32).at[:, 0:N_EXPERTS].set(moe_router[i])
            h = _moe(h, g2n, lat[3], lat[4], lat[5], moe_w13, moe_w2b, router,
                     li=i, seq=s, tr=MOE_TILE, tm=MOE_TILE)
            if not last:
                hc = _moe(hc, g2n, cx[3], cx[4], cx[5], moe_w13, moe_w2b, router,
                          li=i, seq=lc, tr=lc, tm=lc)
    return h.reshape(b, s, d)
```

```python
import functools

import jax
import jax.numpy as jnp
from jax import lax
from jax.experimental import pallas as pl
from jax.experimental.pallas import tpu as pltpu

D_MODEL = 1024
GRID_W = 64
HEAD_DIM = 64
D_CONV = 512
D_ATTN = 512
N_HEADS = 8
N_KV = 2
WINDOW = 128
ROPE_THETA = 10000.0
N_EXPERTS = 8
EPS = 1e-6
KV_OFF = 3 * D_CONV + D_ATTN
D_IN = KV_OFF + 2 * N_KV * HEAD_DIM
LANES = 128
VMEM_LIMIT = 48 * 1024 * 1024
BIG_TILE = 1024
MOE_TILE = 512
PART_ROWS = 256

F32 = jnp.float32
BF16 = jnp.bfloat16
NEG_INF = float("-inf")
LOG2E = 1.4426950408889634


def _params(n_axes):
    return pltpu.CompilerParams(dimension_semantics=("arbitrary",) * n_axes,
                                vmem_limit_bytes=VMEM_LIMIT)


def _sigmoid(x):
    return 1.0 / (1.0 + jnp.exp(-x))


def _mod_kernel(c_ref, w_ref, b_ref, o_ref):
    c = c_ref[...]
    s = c * _sigmoid(c)
    o_ref[...] = jnp.dot(s, w_ref[...], preferred_element_type=F32) + b_ref[...]


def _modulation(c8, w_ada, b_ada):
    depth, d, n = w_ada.shape
    tn = 1536
    return pl.pallas_call(
        _mod_kernel,
        grid=(depth, n // tn),
        in_specs=[pl.BlockSpec((8, d), lambda l, j: (0, 0)),
                  pl.BlockSpec((None, d, tn), lambda l, j: (l, 0, j)),
                  pl.BlockSpec((None, 1, tn), lambda l, j: (l, 0, j))],
        out_specs=pl.BlockSpec((None, 8, tn), lambda l, j: (l, 0, j)),
        out_shape=jax.ShapeDtypeStruct((depth, 8, n), F32),
        compiler_params=_params(2),
        name="adaln_mod",
    )(c8, w_ada, b_ada.reshape(depth, 1, n))


def _norm_mod(x, g, sh, sc):
    ms = jnp.mean(x * x, axis=-1, keepdims=True)
    return (x * lax.rsqrt(ms + EPS) * g) * (1.0 + sc) + sh


def _inproj_kernel(*refs, rope, parts):
    if rope:
        (h_ref, g_ref, sh_ref, sc_ref, w_ref, gq_ref, gk_ref, gm_ref,
         cos_ref, sa_ref, sb_ref, cu_ref, q_ref, kv_ref) = refs
    else:
        (h_ref, g_ref, sh_ref, sc_ref, w_ref, gq_ref, gk_ref, gm_ref,
         cu_ref, q_ref, kv_ref) = refs
    gm = gm_ref[...]
    gq = gq_ref[...]
    rows = h_ref.shape[0] // parts

    def head_norm(t, gain):
        w = t.shape[1]
        ss = jnp.dot((t * t).astype(BF16), gm[0:w, 0:w], preferred_element_type=F32)
        return t * lax.rsqrt(ss * (1.0 / HEAD_DIM) + EPS) * gain

    for p in range(parts):
        sl = slice(p * rows, (p + 1) * rows)

        def rot(t):
            if not rope:
                return t
            return (t * cos_ref[sl, :] + pltpu.roll(t, LANES - 16, 1) * sa_ref[sl, :]
                    + pltpu.roll(t, 16, 1) * sb_ref[sl, :])

        xn = _norm_mod(h_ref[sl, :], g_ref[...], sh_ref[...], sc_ref[...])
        y = jnp.dot(xn.astype(BF16), w_ref[...], preferred_element_type=F32)
        cu_ref[sl, 0:D_CONV] = y[:, 0:D_CONV].astype(BF16)
        cu_ref[sl, D_CONV:2 * D_CONV] = (y[:, D_CONV:2 * D_CONV] * y[:, 2 * D_CONV:3 * D_CONV]).astype(BF16)
        for j in range(D_ATTN // 256):
            lo = 3 * D_CONV + 256 * j
            qn = head_norm(y[:, lo:lo + 256], gq[:, 256 * j:256 * j + 256])
            for c in range(2):
                q_ref[sl, 256 * j + LANES * c:256 * j + LANES * (c + 1)] = rot(
                    qn[:, LANES * c:LANES * (c + 1)]).astype(BF16)
        k = rot(head_norm(y[:, KV_OFF:KV_OFF + LANES], gk_ref[...]))
        v = y[:, KV_OFF + LANES:KV_OFF + 2 * LANES]
        lo_half = lax.broadcasted_iota(jnp.int32, k.shape, 1) < HEAD_DIM
        for c, t in enumerate((k, v)):
            sw = pltpu.roll(t, HEAD_DIM, 1)
            kv_ref[sl, 2 * c * LANES:(2 * c + 1) * LANES] = jnp.where(lo_half, t, sw).astype(BF16)
            kv_ref[sl, (2 * c + 1) * LANES:(2 * c + 2) * LANES] = jnp.where(lo_half, sw, t).astype(BF16)


def _inproj(h, g, sh, sc, w, gq, gk, gm, tables, *, li, tm, seq):
    t, d = h.shape
    tpb = seq // tm
    rope = tables is not None
    row = lambda i: (i, 0)
    fix = lambda i: (0, 0)
    mod = lambda i: (i // tpb, 0, 0)
    in_specs = [pl.BlockSpec((tm, d), row),
                pl.BlockSpec((1, d), fix),
                pl.BlockSpec((None, 1, d), mod),
                pl.BlockSpec((None, 1, d), mod),
                pl.BlockSpec((None, d, D_IN), lambda i: (li, 0, 0)),
                pl.BlockSpec((1, D_ATTN), fix),
                pl.BlockSpec((1, LANES), fix),
                pl.BlockSpec((256, 256), fix)]
    args = [h, g, sh, sc, w, gq, gk, gm]
    if rope:
        in_specs += [pl.BlockSpec((tm, LANES), lambda i: (i % tpb, 0))] * 3
        args += list(tables)
    return pl.pallas_call(
        functools.partial(_inproj_kernel, rope=rope, parts=max(tm // PART_ROWS, 1)),
        grid=(t // tm,),
        in_specs=in_specs,
        out_specs=[pl.BlockSpec((tm, 2 * D_CONV), row),
                   pl.BlockSpec((tm, D_ATTN), row),
                   pl.BlockSpec((tm, 4 * LANES), row)],
        out_shape=[jax.ShapeDtypeStruct((t, 2 * D_CONV), BF16),
                   jax.ShapeDtypeStruct((t, D_ATTN), BF16),
                   jax.ShapeDtypeStruct((t, 4 * LANES), BF16)],
        compiler_params=_params(1),
        name="inproj_rope" if rope else "inproj_ctx",
    )(*args)


_NT = (((1,), (1,)), ((), ()))


def _mixer_kernel(*refs, tq, windowed):
    if windowed:
        (sink_ref, h_ref, cu_ref, cup_ref, cun_ref, q_ref, kvp_ref, kv_ref, kvn_ref, kvc_ref,
         cw_ref, gc_ref, ga_ref, wo_ref, g1_ref, out_ref, kw_ref, ya_ref) = refs
    else:
        (sink_ref, h_ref, cu_ref, q_ref, kvc_ref,
         cw_ref, gc_ref, ga_ref, wo_ref, g1_ref, out_ref, ya_ref) = refs
    i = pl.program_id(1)
    nt = pl.num_programs(1)
    nsub = tq // WINDOW

    cu = cu_ref[...]
    bg = cu[:, 0:D_CONV].astype(F32)
    u = cu[:, D_CONV:2 * D_CONV].astype(F32)
    rows = lax.broadcasted_iota(jnp.int32, (tq, 1), 0)
    if windowed:
        up_row = cup_ref[:, D_CONV:2 * D_CONV].astype(F32)[15:16, :]
        un_row = cun_ref[:, D_CONV:2 * D_CONV].astype(F32)[0:1, :]
        up_row = jnp.where(i > 0, up_row, 0.0)
        un_row = jnp.where(i < nt - 1, un_row, 0.0)
    else:
        up_row = jnp.zeros((1, D_CONV), F32)
        un_row = jnp.zeros((1, D_CONV), F32)
    u_prev = jnp.where(rows == 0, up_row, pltpu.roll(u, 1, 0))
    u_next = jnp.where(rows == tq - 1, un_row, pltpu.roll(u, tq - 1, 0))
    cw = cw_ref[...]
    yc = bg * (cw[0:1, :] * u_prev + cw[1:2, :] * u + cw[2:3, :] * u_next)
    yc = yc * lax.rsqrt(jnp.mean(yc * yc, axis=-1, keepdims=True) + EPS) * gc_ref[...]

    if windowed:
        kw_ref[0:WINDOW, :] = kvp_ref[...]
        kw_ref[WINDOW:WINDOW + tq, :] = kv_ref[...]
        kw_ref[WINDOW + tq:2 * WINDOW + tq, :] = kvn_ref[...]
    lane_lo = lax.broadcasted_iota(jnp.int32, (WINDOW, LANES), 1) < HEAD_DIM
    kvc = kvc_ref[...]
    gqa = N_HEADS // N_KV
    head_of_row = lax.broadcasted_iota(jnp.int32, (N_HEADS * WINDOW, 1), 0) // WINDOW
    ones_c = jnp.ones((kvc.shape[0], LANES), BF16)
    if windowed:
        kk = lax.broadcasted_iota(jnp.int32, (WINDOW, WINDOW), 1)
        qq = lax.broadcasted_iota(jnp.int32, (WINDOW, WINDOW), 0)
        keep_before = kk >= qq
        keep_after = kk <= qq
        ones_w = jnp.ones((3 * WINDOW, LANES), BF16)

    def sub_block(s):
        r0 = s * WINDOW
        if windowed:
            kwin = kw_ref[pl.ds(r0, 3 * WINDOW), :]
            before, after = keep_before, keep_after
            if s == 0:
                before = before & (i > 0)
            if s == nsub - 1:
                after = after & (i < nt - 1)
        rows, sk = [], jnp.zeros((N_HEADS * WINDOW, 1), F32)
        for head in range(N_HEADS):
            qp = q_ref[pl.ds(r0, WINDOW), LANES * (head // 2):LANES * (head // 2 + 1)]
            zero = jnp.zeros_like(qp)
            qh = jnp.where(lane_lo, zero, qp) if head % 2 else jnp.where(lane_lo, qp, zero)
            rows.append(jnp.concatenate([qh, zero] if head < gqa else [zero, qh], axis=1))
            sk = jnp.where(head_of_row == head, sink_ref[head] * LOG2E, sk)
        q8 = jnp.concatenate(rows, axis=0)
        s_c = lax.dot_general(q8, kvc[:, 0:2 * LANES], _NT, preferred_element_type=F32)
        m = jnp.maximum(jnp.max(s_c, axis=-1, keepdims=True), sk)
        if windowed:
            s_w = lax.dot_general(q8, kwin[:, 0:2 * LANES], _NT,
                                  preferred_element_type=F32).reshape(N_HEADS, WINDOW, 3 * WINDOW)
            s_w = jnp.concatenate(
                [jnp.where(before[None], s_w[:, :, 0:WINDOW], NEG_INF),
                 s_w[:, :, WINDOW:2 * WINDOW],
                 jnp.where(after[None], s_w[:, :, 2 * WINDOW:3 * WINDOW], NEG_INF)],
                axis=-1).reshape(N_HEADS * WINDOW, 3 * WINDOW)
            m = jnp.maximum(m, jnp.max(s_w, axis=-1, keepdims=True))
            p_w = jnp.exp2(s_w - m).astype(BF16)
        p_c = jnp.exp2(s_c - m).astype(BF16)
        tail = jnp.exp2(sk - m)
        for grp in range(N_KV):
            gr = slice(grp * gqa * WINDOW, (grp + 1) * gqa * WINDOW)
            v_c = jnp.concatenate([kvc[:, LANES * (2 + grp):LANES * (3 + grp)], ones_c], axis=1)
            o = jnp.dot(p_c[gr], v_c, preferred_element_type=F32)
            if windowed:
                v_w = jnp.concatenate([kwin[:, LANES * (2 + grp):LANES * (3 + grp)], ones_w], axis=1)
                o = o + jnp.dot(p_w[gr], v_w, preferred_element_type=F32)
            o = o[:, 0:LANES] / (o[:, LANES:2 * LANES] + tail[gr])
            for pr in range(gqa // 2):
                pair = (gqa // 2) * grp + pr
                even = o[2 * pr * WINDOW:(2 * pr + 1) * WINDOW]
                odd = o[(2 * pr + 1) * WINDOW:(2 * pr + 2) * WINDOW]
                ya_ref[pl.ds(r0, WINDOW), LANES * pair:LANES * (pair + 1)] = jnp.where(lane_lo, even, odd)

    part = 2 * WINDOW
    ycb = yc.astype(BF16)
    for p in range(tq // part):
        sub_block(2 * p)
        sub_block(2 * p + 1)
        sl = slice(p * part, (p + 1) * part)
        ya = ya_ref[sl, :]
        ya = ya * lax.rsqrt(jnp.mean(ya * ya, axis=-1, keepdims=True) + EPS) * ga_ref[...]
        y = (jnp.dot(ycb[sl], wo_ref[0:D_CONV, :], preferred_element_type=F32)
             + jnp.dot(ya.astype(BF16), wo_ref[D_CONV:2 * D_CONV, :], preferred_element_type=F32))
        out_ref[sl, :] = h_ref[sl, :] + g1_ref[...] * y


def _mixer(h, cu, q, kv, kvc, sink, cw, gc, ga, wo, g1, *, li, tq, seq, ctx_len, windowed):
    t, d = h.shape
    nt = seq // tq
    nb = t // seq
    row = lambda b, i: (b * nt + i, 0)
    fix = lambda b, i: (0, 0)
    smem = pl.BlockSpec(memory_space=pltpu.SMEM)
    tail = [pl.BlockSpec((3, D_CONV), fix),
            pl.BlockSpec((1, D_CONV), fix),
            pl.BlockSpec((1, D_ATTN), fix),
            pl.BlockSpec((None, d, d), lambda b, i: (li, 0, 0)),
            pl.BlockSpec((None, 1, d), lambda b, i: (b, 0, 0))]
    ctx_spec = pl.BlockSpec((ctx_len, 4 * LANES), lambda b, i: (b, 0))
    if windowed:
        r16 = tq // 16
        n16 = t // 16
        rw = tq // WINDOW
        nw = t // WINDOW
        in_specs = [smem,
                    pl.BlockSpec((tq, d), row),
                    pl.BlockSpec((tq, 2 * D_CONV), row),
                    pl.BlockSpec((16, 2 * D_CONV), lambda b, i: (jnp.maximum((b * nt + i) * r16 - 1, 0), 0)),
                    pl.BlockSpec((16, 2 * D_CONV), lambda b, i: (jnp.minimum((b * nt + i + 1) * r16, n16 - 1), 0)),
                    pl.BlockSpec((tq, D_ATTN), row),
                    pl.BlockSpec((WINDOW, 4 * LANES), lambda b, i: (jnp.maximum((b * nt + i) * rw - 1, 0), 0)),
                    pl.BlockSpec((tq, 4 * LANES), row),
                    pl.BlockSpec((WINDOW, 4 * LANES), lambda b, i: (jnp.minimum((b * nt + i + 1) * rw, nw - 1), 0)),
                    ctx_spec] + tail
        args = [sink, h, cu, cu, cu, q, kv, kv, kv, kvc, cw, gc, ga, wo, g1]
        scratch = [pltpu.VMEM((tq + 2 * WINDOW, 4 * LANES), BF16), pltpu.VMEM((tq, D_ATTN), F32)]
    else:
        in_specs = [smem,
                    pl.BlockSpec((tq, d), row),
                    pl.BlockSpec((tq, 2 * D_CONV), row),
                    pl.BlockSpec((tq, D_ATTN), row),
                    ctx_spec] + tail
        args = [sink, h, cu, q, kvc, cw, gc, ga, wo, g1]
        scratch = [pltpu.VMEM((tq, D_ATTN), F32)]
    return pl.pallas_call(
        functools.partial(_mixer_kernel, tq=tq, windowed=windowed),
        grid=(nb, nt),
        in_specs=in_specs,
        out_specs=pl.BlockSpec((tq, d), row),
        out_shape=jax.ShapeDtypeStruct((t, d), F32),
        scratch_shapes=scratch,
        compiler_params=_params(2),
        name="mixer_win" if windowed else "mixer_ctx",
    )(*args)


def _dense_ffn_kernel(h_ref, g_ref, sh_ref, sc_ref, gate_ref, w1_ref, w3_ref, w2_ref, out_ref, *, parts):
    rows = h_ref.shape[0] // parts
    for p in range(parts):
        sl = slice(p * rows, (p + 1) * rows)
        hp = h_ref[sl, :]
        xb = _norm_mod(hp, g_ref[...], sh_ref[...], sc_ref[...]).astype(BF16)
        h1 = jnp.dot(xb, w1_ref[...], preferred_element_type=F32)
        h3 = jnp.dot(xb, w3_ref[...], preferred_element_type=F32)
        a = (h1 * _sigmoid(h1) * h3).astype(BF16)
        y = jnp.dot(a, w2_ref[...], preferred_element_type=F32)
        out_ref[sl, :] = hp + gate_ref[...] * y


def _dense_ffn(h, g, sh, sc, gate, w1, w3, w2, *, li, tm, seq):
    t, d = h.shape
    f = w1.shape[2]
    wfix = lambda i: (li, 0, 0)
    tpb = seq // tm
    row = lambda i: (i, 0)
    fix = lambda i: (0, 0)
    mod = lambda i: (i // tpb, 0, 0)
    once = pl.Buffered(1)
    return pl.pallas_call(
        functools.partial(_dense_ffn_kernel, parts=max(tm // PART_ROWS, 1)),
        grid=(t // tm,),
        in_specs=[pl.BlockSpec((tm, d), row),
                  pl.BlockSpec((1, d), fix),
                  pl.BlockSpec((None, 1, d), mod),
                  pl.BlockSpec((None, 1, d), mod),
                  pl.BlockSpec((None, 1, d), mod),
                  pl.BlockSpec((None, d, f), wfix, pipeline_mode=once),
                  pl.BlockSpec((None, d, f), wfix, pipeline_mode=once),
                  pl.BlockSpec((None, f, d), wfix, pipeline_mode=once)],
        out_specs=pl.BlockSpec((tm, d), row),
        out_shape=jax.ShapeDtypeStruct((t, d), F32),
        compiler_params=_params(1),
        name="ffn_dense",
    )(h, g, sh, sc, gate, w1, w3, w2)


def _pack_w13_kernel(w1_ref, w3_ref, o_ref):
    fe = w1_ref.shape[1]
    o_ref[:, 0:fe] = w1_ref[...].astype(BF16)
    o_ref[:, fe:2 * fe] = w3_ref[...].astype(BF16)


def _pack_w13(w1, w3):
    nl, ne, d, fe = w1.shape
    rows = d // 2
    spec = pl.BlockSpec((None, None, rows, fe), lambda l, e, r: (l, e, r, 0))
    return pl.pallas_call(
        _pack_w13_kernel,
        grid=(nl, ne, d // rows),
        in_specs=[spec, spec],
        out_specs=pl.BlockSpec((None, None, rows, 2 * fe), lambda l, e, r: (l, e, r, 0)),
        out_shape=jax.ShapeDtypeStruct((nl, ne, d, 2 * fe), BF16),
        compiler_params=_params(3),
        name="pack_w13",
    )(w1, w3)


def _top2(logits):
    lane = lax.broadcasted_iota(jnp.int32, logits.shape, 1)
    lg = jnp.where(lane < N_EXPERTS, logits, NEG_INF)
    m1 = jnp.max(lg, axis=-1, keepdims=True)
    i1 = jnp.min(jnp.where(lg == m1, lane, LANES), axis=-1, keepdims=True)
    lg2 = jnp.where(lane == i1, NEG_INF, lg)
    m2 = jnp.max(lg2, axis=-1, keepdims=True)
    i2 = jnp.min(jnp.where(lg2 == m2, lane, LANES), axis=-1, keepdims=True)
    e2 = jnp.exp(m2 - m1)
    return i1, i2, 1.0 / (1.0 + e2), e2 / (1.0 + e2)


def _router_kernel(h_ref, g_ref, sh_ref, sc_ref, r_ref, tri_ref, xn_ref, route_ref, rt_ref, cnt_ref,
                   base_ref):
    tm = h_ref.shape[0]

    @pl.when(pl.program_id(0) == 0)
    def _():
        base_ref[...] = jnp.zeros_like(base_ref)

    xn = _norm_mod(h_ref[...], g_ref[...], sh_ref[...], sc_ref[...])
    xn_ref[...] = xn.reshape(xn_ref.shape)
    i1, i2, g1, g2 = _top2(jnp.dot(xn, r_ref[...], preferred_element_type=F32))
    lane = lax.broadcasted_iota(jnp.int32, (tm, LANES), 1)
    hit1 = lane == i1
    hit2 = lane == i2
    chosen = jnp.where(jnp.logical_or(hit1, hit2), 1.0, 0.0)
    before = base_ref[...] + jnp.dot(tri_ref[...], chosen.astype(BF16), preferred_element_type=F32)
    r1 = jnp.sum(jnp.where(hit1, before, 0.0), axis=-1, keepdims=True)
    r2 = jnp.sum(jnp.where(hit2, before, 0.0), axis=-1, keepdims=True)
    base_ref[...] += jnp.sum(chosen, axis=0, keepdims=True)
    cnt_ref[...] = base_ref[...]
    fields = (i1.astype(F32), i2.astype(F32), g1, g2, r1, r2)
    route = jnp.zeros((tm, LANES), F32)
    for k, f in enumerate(fields):
        route = jnp.where(lane == k, f, route)
    route_ref[...] = route
    rt_ref[...] = route.T[0:8, :]


def _router(h, g, sh, sc, router, *, tm, seq):
    t, d = h.shape
    tpb = seq // tm
    row = lambda i: (i, 0)
    fix = lambda i: (0, 0)
    mod = lambda i: (i // tpb, 0, 0)
    ids = jnp.arange(tm)
    tri = (ids[None, :] < ids[:, None]).astype(BF16)
    return pl.pallas_call(
        _router_kernel,
        grid=(t // tm,),
        in_specs=[pl.BlockSpec((tm, d), row), pl.BlockSpec((1, d), fix),
                  pl.BlockSpec((None, 1, d), mod), pl.BlockSpec((None, 1, d), mod),
                  pl.BlockSpec((d, LANES), fix), pl.BlockSpec((tm, tm), fix)],
        out_specs=[pl.BlockSpec((tm, d // LANES, LANES), lambda i: (i, 0, 0)),
                   pl.BlockSpec((tm, LANES), row),
                   pl.BlockSpec((None, 8, tm), lambda i: (i, 0, 0)),
                   pl.BlockSpec((1, LANES), fix)],
        out_shape=[jax.ShapeDtypeStruct((t, d // LANES, LANES), F32),
                   jax.ShapeDtypeStruct((t, LANES), F32),
                   jax.ShapeDtypeStruct((t // tm, 8, tm), F32),
                   jax.ShapeDtypeStruct((1, LANES), F32)],
        scratch_shapes=[pltpu.VMEM((1, LANES), F32)],
        compiler_params=_params(1),
        name="moe_router",
    )(h, g, sh, sc, router, tri)


def _route_plan(t, counts, tr):
    counts = counts[0, 0:N_EXPERTS].astype(jnp.int32)
    tiles = (counts + tr - 1) // tr
    tile_end = jnp.cumsum(tiles)
    tile_start = tile_end - tiles
    experts = jnp.arange(N_EXPERTS, dtype=jnp.int32)

    nt = 2 * t // tr + N_EXPERTS
    tid = jnp.arange(nt, dtype=jnp.int32)
    tile_expert = jnp.minimum(jnp.sum((tid[:, None] >= tile_end[None, :]).astype(jnp.int32), axis=1),
                              N_EXPERTS - 1)
    in_tile = tid - jnp.sum(jnp.where(tile_expert[:, None] == experts[None, :], tile_start[None, :], 0), axis=1)
    own = jnp.sum(jnp.where(tile_expert[:, None] == experts[None, :], counts[None, :], 0), axis=1)
    n_valid = jnp.where(tid < tile_end[-1], jnp.clip(own - in_tile * tr, 0, tr), 0)
    tail = tile_end[-1] + experts
    pad_tiles = jnp.concatenate([jnp.where(tiles > 0, tile_end - 1, -1), jnp.where(tail < nt, tail, -1)])
    return tile_expert, n_valid, tile_start * tr, pad_tiles


def _table_kernel(start_ref, rt_ref, tab_ref, *, nt):
    tm = rt_ref.shape[1]

    def position(e, r):
        start = jnp.zeros_like(r)
        for k in range(N_EXPERTS):
            start = jnp.where(e == float(k), start_ref[k].astype(F32), start)
        return (start + r).astype(jnp.int32)

    live = pl.program_id(0) < nt
    rt = rt_ref[...]
    for c in range(2):
        pos = position(rt[c:c + 1, :], rt[4 + c:5 + c, :])
        tab_ref[:, c * tm:(c + 1) * tm] = jnp.where(live, pos, 0)


def _position_table(route_t, row_start):
    nt, fields, tm = route_t.shape
    grid_spec = pltpu.PrefetchScalarGridSpec(
        num_scalar_prefetch=1,
        grid=(nt + 2,),
        in_specs=[pl.BlockSpec((None, fields, tm), lambda j, st: (jnp.minimum(j, nt - 1), 0, 0))],
        out_specs=pl.BlockSpec((None, 1, 2 * tm), lambda j, st: (j, 0, 0)))
    return pl.pallas_call(
        functools.partial(_table_kernel, nt=nt),
        grid_spec=grid_spec,
        out_shape=jax.ShapeDtypeStruct((nt + 2, 1, 2 * tm), jnp.int32),
        compiler_params=_params(1),
        name="moe_table",
    )(row_start, route_t)


def _row_copies(idx_smem, s_idx, tm, make):
    base = s_idx * (2 * tm)

    def body(r, c):
        make(r, idx_smem[base + r], idx_smem[base + tm + r])
        return c

    lax.fori_loop(0, tm, body, 0, unroll=8)


def _dispatch_kernel(zt_ref, idx_hbm, xn_hbm, xg_hbm, idx_smem, zbuf, xbuf,
                     sem_d, sem_i, sem_z, sem_in, *, tm, nt, tr):
    j = pl.program_id(0)
    slot = j % 2
    other = 1 - slot
    cur = j % 3
    nxt = (j + 1) % 3

    def in_copy(tile, s):
        return pltpu.make_async_copy(xn_hbm.at[pl.ds(tile * tm, tm)], xbuf.at[s], sem_in.at[s])

    @pl.when(j == 0)
    def _():
        zbuf[...] = jnp.zeros_like(zbuf)
        for k in range(zt_ref.shape[0]):
            fill = pltpu.make_async_copy(zbuf, xg_hbm.at[pl.ds(jnp.maximum(zt_ref[k], 0) * tr, tr)], sem_z)
            pl.when(zt_ref[k] >= 0)(fill.start)
        for k in range(zt_ref.shape[0]):
            fill = pltpu.make_async_copy(zbuf, xg_hbm.at[pl.ds(0, tr)], sem_z)
            pl.when(zt_ref[k] >= 0)(fill.wait)

    def idx_copy(row, s):
        return pltpu.make_async_copy(idx_hbm.at[row, 0], idx_smem.at[pl.ds(s * 2 * tm, 2 * tm)], sem_i.at[s])

    def wait_rows(s):
        for _ in range(2):
            pltpu.make_async_copy(xbuf.at[s], xg_hbm.at[pl.ds(0, tm)], sem_d.at[s]).wait()

    @pl.when(j == 0)
    def _():
        idx_copy(0, 0).start()
        in_copy(0, 0).start()

    @pl.when(j >= 2)
    def _():
        wait_rows(nxt)

    @pl.when(j + 1 < nt)
    def _():
        in_copy(j + 1, nxt).start()

    idx_copy(j, slot).wait()
    idx_copy(j + 1, other).start()
    in_copy(j, cur).wait()

    def make(r, p1, p2):
        src = xbuf.at[cur, r]
        pltpu.make_async_copy(src, xg_hbm.at[p1], sem_d.at[cur]).start(priority=0)
        pltpu.make_async_copy(src, xg_hbm.at[p2], sem_d.at[cur]).start(priority=1)

    _row_copies(idx_smem, slot, tm, make)

    @pl.when(j == nt - 1)
    def _():
        if nt > 1:
            wait_rows((nt - 2) % 3)
        wait_rows((nt - 1) % 3)
        idx_copy(j + 1, other).wait()


def _dispatch(xn3, table, pad_tiles, *, tm, tr, n_rows):
    t = xn3.shape[0]
    nt = t // tm
    any_spec = pl.BlockSpec(memory_space=pl.ANY)
    grid_spec = pltpu.PrefetchScalarGridSpec(
        num_scalar_prefetch=1,
        grid=(nt,),
        in_specs=[any_spec, any_spec],
        out_specs=any_spec,
        scratch_shapes=[pltpu.SMEM((4 * tm,), jnp.int32),
                        pltpu.VMEM((tr,) + xn3.shape[1:], F32),
                        pltpu.VMEM((3, tm) + xn3.shape[1:], F32),
                        pltpu.SemaphoreType.DMA((3,)), pltpu.SemaphoreType.DMA((2,)),
                        pltpu.SemaphoreType.DMA, pltpu.SemaphoreType.DMA((3,))])
    return pl.pallas_call(
        functools.partial(_dispatch_kernel, tm=tm, nt=nt, tr=tr),
        grid_spec=grid_spec,
        out_shape=jax.ShapeDtypeStruct((n_rows,) + xn3.shape[1:], F32),
        compiler_params=pltpu.CompilerParams(dimension_semantics=("arbitrary",),
                                             vmem_limit_bytes=VMEM_LIMIT,
                                             disable_bounds_checks=True),
        name="moe_dispatch",
    )(pad_tiles, table, xn3)


def _expert_kernel(te_ref, nv_ref, x_ref, w13_ref, w2_ref, y_ref):
    tr = x_ref.shape[0]
    nv = nv_ref[pl.program_id(0)]

    @pl.when(nv > 0)
    def _():
        half = tr // 2
        for p in range(2):
            x = x_ref[p * half:(p + 1) * half].reshape(half, D_MODEL).astype(BF16)
            h13 = jnp.dot(x, w13_ref[...], preferred_element_type=F32)
            fe = h13.shape[1] // 2
            h1, h3 = h13[:, 0:fe], h13[:, fe:2 * fe]
            a = (h1 * _sigmoid(h1) * h3).astype(BF16)
            y = jnp.dot(a, w2_ref[...], preferred_element_type=F32)
            y_ref[p * half:(p + 1) * half] = y.reshape((half,) + y_ref.shape[1:])

    @pl.when(nv == 0)
    def _():
        y_ref[...] = jnp.zeros_like(y_ref)


def _experts(xg3, tile_expert, n_valid, w13, w2, *, li, tr):
    n_rows, sl, ln = xg3.shape
    d = sl * ln
    fe = w2.shape[2]
    rows = lambda j, te, nv: (j, 0, 0)
    wsel = lambda j, te, nv: (li, te[j], 0, 0)
    grid_spec = pltpu.PrefetchScalarGridSpec(
        num_scalar_prefetch=2,
        grid=(n_rows // tr,),
        in_specs=[pl.BlockSpec((tr, sl, ln), rows),
                  pl.BlockSpec((None, None, d, 2 * fe), wsel),
                  pl.BlockSpec((None, None, fe, d), wsel)],
        out_specs=pl.BlockSpec((tr, sl, ln), rows))
    return pl.pallas_call(
        _expert_kernel,
        grid_spec=grid_spec,
        out_shape=jax.ShapeDtypeStruct(xg3.shape, F32),
        compiler_params=_params(1),
        name="moe_experts",
    )(tile_expert, n_valid, xg3, w13, w2)


def _combine_kernel(idx_hbm, h_ref, gate_ref, route_ref, yg_hbm, out_ref,
                    y1buf, y2buf, idx_smem, sem_y, sem_i, *, tm, nt):
    j = pl.program_id(0)
    slot = j % 2
    other = 1 - slot

    def idx_copy(row, s):
        return pltpu.make_async_copy(idx_hbm.at[row, 0], idx_smem.at[pl.ds(s * 2 * tm, 2 * tm)], sem_i.at[s])

    def fetch(s_idx, s_buf):
        def make(r, p1, p2):
            pltpu.make_async_copy(yg_hbm.at[p1], y1buf.at[s_buf, r], sem_y.at[s_buf]).start(priority=0)
            pltpu.make_async_copy(yg_hbm.at[p2], y2buf.at[s_buf, r], sem_y.at[s_buf]).start(priority=1)

        _row_copies(idx_smem, s_idx, tm, make)

    def wait_rows(s):
        pltpu.make_async_copy(yg_hbm.at[pl.ds(0, tm)], y1buf.at[s], sem_y.at[s]).wait()
        pltpu.make_async_copy(yg_hbm.at[pl.ds(0, tm)], y2buf.at[s], sem_y.at[s]).wait()

    @pl.when(j == 0)
    def _():
        first = idx_copy(0, 0)
        first.start()
        first.wait()
        fetch(0, 0)
        idx_copy(1, 1).start()

    idx_copy(j + 1, other).wait()
    fetch(other, other)
    idx_copy(j + 2, slot).start()
    wait_rows(slot)
    rt = route_ref[...]
    y1 = y1buf[slot].reshape(tm, D_MODEL)
    y2 = y2buf[slot].reshape(tm, D_MODEL)
    out_ref[...] = h_ref[...] + gate_ref[...] * (rt[:, 2:3] * y1 + rt[:, 3:4] * y2)

    @pl.when(j == nt - 1)
    def _():
        wait_rows(other)
        idx_copy(j + 2, slot).wait()


def _combine(h, gate, route, table, yg3, *, tm, seq):
    t, d = h.shape
    tpb = seq // tm
    nt = t // tm
    sl, ln = yg3.shape[1:]
    row = lambda i: (i, 0)
    any_spec = pl.BlockSpec(memory_space=pl.ANY)
    return pl.pallas_call(
        functools.partial(_combine_kernel, tm=tm, nt=nt),
        grid=(nt,),
        in_specs=[any_spec,
                  pl.BlockSpec((tm, d), row),
                  pl.BlockSpec((None, 1, d), lambda i: (i // tpb, 0, 0)),
                  pl.BlockSpec((tm, LANES), row),
                  any_spec],
        out_specs=pl.BlockSpec((tm, d), row),
        out_shape=jax.ShapeDtypeStruct((t, d), F32),
        scratch_shapes=[pltpu.VMEM((2, tm, sl, ln), F32), pltpu.VMEM((2, tm, sl, ln), F32),
                        pltpu.SMEM((4 * tm,), jnp.int32),
                        pltpu.SemaphoreType.DMA((2,)), pltpu.SemaphoreType.DMA((2,))],
        compiler_params=pltpu.CompilerParams(dimension_semantics=("arbitrary",),
                                             vmem_limit_bytes=VMEM_LIMIT,
                                             disable_bounds_checks=True),
        name="moe_combine",
    )(table, h, gate, route, yg3)


def _moe(h, g, sh, sc, gate, w13, w2, router, *, li, seq, tr, tm):
    t = h.shape[0]
    xn3, route, route_t, counts = _router(h, g, sh, sc, router, tm=tm, seq=seq)
    tile_expert, n_valid, row_start, pad_tiles = _route_plan(t, counts, tr)
    table = _position_table(route_t, row_start)
    xg3 = _dispatch(xn3, table, pad_tiles, tm=tm, tr=tr, n_rows=2 * t + N_EXPERTS * tr)
    yg3 = _experts(xg3, tile_expert, n_valid, w13, w2, li=li, tr=tr)
    return _combine(h, gate, route, table, yg3, tm=tm, seq=seq)


def _rope_tables(seq):
    rows = seq // GRID_W
    row, col = jnp.meshgrid(jnp.arange(rows, dtype=F32), jnp.arange(GRID_W, dtype=F32), indexing='ij')
    n_freq = HEAD_DIM // 4
    inv_freq = ROPE_THETA ** (-jnp.arange(n_freq, dtype=F32) / n_freq)
    ang_r = row.reshape(-1, 1) * inv_freq
    ang_c = col.reshape(-1, 1) * inv_freq
    ang = jnp.concatenate([ang_r, ang_r, ang_c, ang_c], axis=-1)
    cos, sin = jnp.cos(ang), jnp.sin(ang)
    first = (jnp.arange(HEAD_DIM) % (2 * n_freq)) < n_freq
    sin_a = jnp.where(first, -sin, 0.0)
    sin_b = jnp.where(first, 0.0, sin)
    rep = LANES // HEAD_DIM
    return tuple(jnp.tile(t, (1, rep)) for t in (cos, sin_a, sin_b))


def kernel(x, c, ctx, c_ctx, w_ada, b_ada, norm1_g, norm2_g, w_in, conv_w, q_norm_g, k_norm_g,
           attn_sink, out_norm_conv_g, out_norm_attn_g, w_out, ffn_w1, ffn_w3, ffn_w2,
           moe_router, moe_w1, moe_w3, moe_w2):
    b, s, d = x.shape
    lc = ctx.shape[1]
    depth = w_ada.shape[0]
    assert d == D_MODEL and s % BIG_TILE == 0 and lc % PART_ROWS == 0 and b + 1 <= 8

    c8 = jnp.zeros((8, d), F32).at[0:b].set(c).at[b].set(c_ctx)
    mod = _modulation(c8, w_ada, b_ada)

    tables = _rope_tables(s)
    ids = jnp.arange(256)
    gm = (ids[:, None] // HEAD_DIM == ids[None, :] // HEAD_DIM).astype(BF16)
    scale = HEAD_DIM ** -0.5 * LOG2E

    h = x.reshape(b * s, d)
    hc = ctx.reshape(b * lc, d)
    w_in_b, w_out_b = w_in.astype(BF16), w_out.astype(BF16)
    dense_w = [w.astype(BF16) for w in (ffn_w1, ffn_w3, ffn_w2)]
    moe_w13 = _pack_w13(moe_w1, moe_w3)
    moe_w2b = moe_w2.astype(BF16)
    for layer in range(depth):
        last = layer == depth - 1
        m = mod[layer]
        lat = [m[0:b, k * d:(k + 1) * d].reshape(b, 1, d) for k in range(6)]
        cx = [jnp.broadcast_to(m[b:b + 1, k * d:(k + 1) * d].reshape(1, 1, d), (b, 1, d)) for k in range(6)]
        g1n = norm1_g[layer].reshape(1, d)
        g2n = norm2_g[layer].reshape(1, d)
        gq = (jnp.tile(q_norm_g[layer], N_HEADS) * scale).reshape(1, D_ATTN)
        gk = jnp.tile(k_norm_g[layer], N_KV).reshape(1, LANES)
        gc = out_norm_conv_g[layer].reshape(1, D_CONV)
        ga = out_norm_attn_g[layer].reshape(1, D_ATTN)
        sink = attn_sink[layer]
        cw = conv_w[layer]

        cu, q, kv = _inproj(h, g1n, lat[0], lat[1], w_in_b, gq, gk, gm, tables, li=layer, tm=BIG_TILE, seq=s)
        cuc, qc, kvc = _inproj(hc, g1n, cx[0], cx[1], w_in_b, gq, gk, gm, None, li=layer, tm=lc, seq=lc)
        h = _mixer(h, cu, q, kv, kvc, sink, cw, gc, ga, w_out_b, lat[2],
                   li=layer, tq=BIG_TILE, seq=s, ctx_len=lc, windowed=True)
        if not last:
            hc = _mixer(hc, cuc, qc, None, kvc, sink, cw, gc, ga, w_out_b, cx[2],
                        li=layer, tq=lc, seq=lc, ctx_len=lc, windowed=False)

        i = layer // 2
        if layer % 2 == 0:
            h = _dense_ffn(h, g2n, lat[3], lat[4], lat[5], *dense_w, li=i, tm=BIG_TILE, seq=s)
            if not last:
                hc = _dense_ffn(hc, g2n, cx[3], cx[4], cx[5], *dense_w, li=i, tm=lc, seq=lc)
        else:
            router = jnp.zeros((d, LANES), F32).at[:, 0:N_EXPERTS].set(moe_router[i])
            h = _moe(h, g2n, lat[3], lat[4], lat[5], moe_w13, moe_w2b, router,
                     li=i, seq=s, tr=MOE_TILE, tm=MOE_TILE)
            if not last:
                hc = _moe(hc, g2n, cx[3], cx[4], cx[5], moe_w13, moe_w2b, router,
                          li=i, seq=lc, tr=lc, tm=lc)
    return h.reshape(b, s, d)
```

```python
import functools

import jax
import jax.numpy as jnp
from jax import lax
from jax.experimental import pallas as pl
from jax.experimental.pallas import tpu as pltpu

D_MODEL = 1024
GRID_W = 64
HEAD_DIM = 64
D_CONV = 512
D_ATTN = 512
N_HEADS = 8
N_KV = 2
WINDOW = 128
ROPE_THETA = 10000.0
N_EXPERTS = 8
EPS = 1e-6
KV_OFF = 3 * D_CONV + D_ATTN
D_IN = KV_OFF + 2 * N_KV * HEAD_DIM
LANES = 128
VMEM_LIMIT = 48 * 1024 * 1024
BIG_TILE = 1024
MOE_TILE = 512
PART_ROWS = 256

F32 = jnp.float32
BF16 = jnp.bfloat16
NEG_INF = float("-inf")
LOG2E = 1.4426950408889634


def _params(n_axes):
    return pltpu.CompilerParams(dimension_semantics=("arbitrary",) * n_axes,
                                vmem_limit_bytes=VMEM_LIMIT)


def _sigmoid(x):
    return 1.0 / (1.0 + jnp.exp(-x))


def _mod_kernel(c_ref, w_ref, b_ref, o_ref):
    c = c_ref[...]
    s = c * _sigmoid(c)
    o_ref[...] = jnp.dot(s, w_ref[...], preferred_element_type=F32) + b_ref[...]


def _modulation(c8, w_ada, b_ada):
    depth, d, n = w_ada.shape
    tn = 1536
    return pl.pallas_call(
        _mod_kernel,
        grid=(depth, n // tn),
        in_specs=[pl.BlockSpec((8, d), lambda l, j: (0, 0)),
                  pl.BlockSpec((None, d, tn), lambda l, j: (l, 0, j)),
                  pl.BlockSpec((None, 1, tn), lambda l, j: (l, 0, j))],
        out_specs=pl.BlockSpec((None, 8, tn), lambda l, j: (l, 0, j)),
        out_shape=jax.ShapeDtypeStruct((depth, 8, n), F32),
        compiler_params=_params(2),
        name="adaln_mod",
    )(c8, w_ada, b_ada.reshape(depth, 1, n))


def _norm_mod(x, g, sh, sc):
    ms = jnp.mean(x * x, axis=-1, keepdims=True)
    return (x * lax.rsqrt(ms + EPS) * g) * (1.0 + sc) + sh


def _inproj_kernel(*refs, rope, parts):
    if rope:
        (h_ref, g_ref, sh_ref, sc_ref, w_ref, gq_ref, gk_ref, gm_ref,
         cos_ref, sa_ref, sb_ref, cu_ref, q_ref, kv_ref) = refs
    else:
        (h_ref, g_ref, sh_ref, sc_ref, w_ref, gq_ref, gk_ref, gm_ref,
         cu_ref, q_ref, kv_ref) = refs
    gm = gm_ref[...]
    gq = gq_ref[...]
    rows = h_ref.shape[0] // parts

    def head_norm(t, gain):
        w = t.shape[1]
        ss = jnp.dot((t * t).astype(BF16), gm[0:w, 0:w], preferred_element_type=F32)
        return t * lax.rsqrt(ss * (1.0 / HEAD_DIM) + EPS) * gain

    for p in range(parts):
        sl = slice(p * rows, (p + 1) * rows)

        def rot(t):
            if not rope:
                return t
            return (t * cos_ref[sl, :] + pltpu.roll(t, LANES - 16, 1) * sa_ref[sl, :]
                    + pltpu.roll(t, 16, 1) * sb_ref[sl, :])

        xn = _norm_mod(h_ref[sl, :], g_ref[...], sh_ref[...], sc_ref[...])
        y = jnp.dot(xn.astype(BF16), w_ref[...], preferred_element_type=F32)
        cu_ref[sl, 0:D_CONV] = y[:, 0:D_CONV].astype(BF16)
        cu_ref[sl, D_CONV:2 * D_CONV] = (y[:, D_CONV:2 * D_CONV] * y[:, 2 * D_CONV:3 * D_CONV]).astype(BF16)
        for j in range(D_ATTN // 256):
            lo = 3 * D_CONV + 256 * j
            qn = head_norm(y[:, lo:lo + 256], gq[:, 256 * j:256 * j + 256])
            for c in range(2):
                q_ref[sl, 256 * j + LANES * c:256 * j + LANES * (c + 1)] = rot(
                    qn[:, LANES * c:LANES * (c + 1)]).astype(BF16)
        k = rot(head_norm(y[:, KV_OFF:KV_OFF + LANES], gk_ref[...]))
        v = y[:, KV_OFF + LANES:KV_OFF + 2 * LANES]
        lo_half = lax.broadcasted_iota(jnp.int32, k.shape, 1) < HEAD_DIM
        for c, t in enumerate((k, v)):
            sw = pltpu.roll(t, HEAD_DIM, 1)
            kv_ref[sl, 2 * c * LANES:(2 * c + 1) * LANES] = jnp.where(lo_half, t, sw).astype(BF16)
            kv_ref[sl, (2 * c + 1) * LANES:(2 * c + 2) * LANES] = jnp.where(lo_half, sw, t).astype(BF16)


def _inproj(h, g, sh, sc, w, gq, gk, gm, tables, *, li, tm, seq):
    t, d = h.shape
    tpb = seq // tm
    rope = tables is not None
    row = lambda i: (i, 0)
    fix = lambda i: (0, 0)
    mod = lambda i: (i // tpb, 0, 0)
    in_specs = [pl.BlockSpec((tm, d), row),
                pl.BlockSpec((1, d), fix),
                pl.BlockSpec((None, 1, d), mod),
                pl.BlockSpec((None, 1, d), mod),
                pl.BlockSpec((None, d, D_IN), lambda i: (li, 0, 0)),
                pl.BlockSpec((1, D_ATTN), fix),
                pl.BlockSpec((1, LANES), fix),
                pl.BlockSpec((256, 256), fix)]
    args = [h, g, sh, sc, w, gq, gk, gm]
    if rope:
        in_specs += [pl.BlockSpec((tm, LANES), lambda i: (i % tpb, 0))] * 3
        args += list(tables)
    return pl.pallas_call(
        functools.partial(_inproj_kernel, rope=rope, parts=max(tm // PART_ROWS, 1)),
        grid=(t // tm,),
        in_specs=in_specs,
        out_specs=[pl.BlockSpec((tm, 2 * D_CONV), row),
                   pl.BlockSpec((tm, D_ATTN), row),
                   pl.BlockSpec((tm, 4 * LANES), row)],
        out_shape=[jax.ShapeDtypeStruct((t, 2 * D_CONV), BF16),
                   jax.ShapeDtypeStruct((t, D_ATTN), BF16),
                   jax.ShapeDtypeStruct((t, 4 * LANES), BF16)],
        compiler_params=_params(1),
        name="inproj_rope" if rope else "inproj_ctx",
    )(*args)


_NT = (((1,), (1,)), ((), ()))


def _mixer_kernel(*refs, tq, windowed):
    if windowed:
        (sink_ref, h_ref, cu_ref, cup_ref, cun_ref, q_ref, kvp_ref, kv_ref, kvn_ref, kvc_ref,
         cw_ref, gc_ref, ga_ref, wo_ref, g1_ref, out_ref, kw_ref, ya_ref) = refs
    else:
        (sink_ref, h_ref, cu_ref, q_ref, kvc_ref,
         cw_ref, gc_ref, ga_ref, wo_ref, g1_ref, out_ref, ya_ref) = refs
    i = pl.program_id(1)
    nt = pl.num_programs(1)
    nsub = tq // WINDOW

    cu = cu_ref[...]
    bg = cu[:, 0:D_CONV].astype(F32)
    u = cu[:, D_CONV:2 * D_CONV].astype(F32)
    rows = lax.broadcasted_iota(jnp.int32, (tq, 1), 0)
    if windowed:
        up_row = cup_ref[:, D_CONV:2 * D_CONV].astype(F32)[15:16, :]
        un_row = cun_ref[:, D_CONV:2 * D_CONV].astype(F32)[0:1, :]
        up_row = jnp.where(i > 0, up_row, 0.0)
        un_row = jnp.where(i < nt - 1, un_row, 0.0)
    else:
        up_row = jnp.zeros((1, D_CONV), F32)
        un_row = jnp.zeros((1, D_CONV), F32)
    u_prev = jnp.where(rows == 0, up_row, pltpu.roll(u, 1, 0))
    u_next = jnp.where(rows == tq - 1, un_row, pltpu.roll(u, tq - 1, 0))
    cw = cw_ref[...]
    yc = bg * (cw[0:1, :] * u_prev + cw[1:2, :] * u + cw[2:3, :] * u_next)
    yc = yc * lax.rsqrt(jnp.mean(yc * yc, axis=-1, keepdims=True) + EPS) * gc_ref[...]

    if windowed:
        kw_ref[0:WINDOW, :] = kvp_ref[...]
        kw_ref[WINDOW:WINDOW + tq, :] = kv_ref[...]
        kw_ref[WINDOW + tq:2 * WINDOW + tq, :] = kvn_ref[...]
    lane_lo = lax.broadcasted_iota(jnp.int32, (WINDOW, LANES), 1) < HEAD_DIM
    kvc = kvc_ref[...]
    gqa = N_HEADS // N_KV
    head_of_row = lax.broadcasted_iota(jnp.int32, (N_HEADS * WINDOW, 1), 0) // WINDOW
    ones_c = jnp.ones((kvc.shape[0], LANES), BF16)
    if windowed:
        kk = lax.broadcasted_iota(jnp.int32, (WINDOW, WINDOW), 1)
        qq = lax.broadcasted_iota(jnp.int32, (WINDOW, WINDOW), 0)
        keep_before = kk >= qq
        keep_after = kk <= qq
        ones_w = jnp.ones((3 * WINDOW, LANES), BF16)

    def sub_block(s):
        r0 = s * WINDOW
        if windowed:
            kwin = kw_ref[pl.ds(r0, 3 * WINDOW), :]
            before, after = keep_before, keep_after
            if s == 0:
                before = before & (i > 0)
            if s == nsub - 1:
                after = after & (i < nt - 1)
        rows, sk = [], jnp.zeros((N_HEADS * WINDOW, 1), F32)
        for head in range(N_HEADS):
            qp = q_ref[pl.ds(r0, WINDOW), LANES * (head // 2):LANES * (head // 2 + 1)]
            zero = jnp.zeros_like(qp)
            qh = jnp.where(lane_lo, zero, qp) if head % 2 else jnp.where(lane_lo, qp, zero)
            rows.append(jnp.concatenate([qh, zero] if head < gqa else [zero, qh], axis=1))
            sk = jnp.where(head_of_row == head, sink_ref[head] * LOG2E, sk)
        q8 = jnp.concatenate(rows, axis=0)
        s_c = lax.dot_general(q8, kvc[:, 0:2 * LANES], _NT, preferred_element_type=F32)
        m = jnp.maximum(jnp.max(s_c, axis=-1, keepdims=True), sk)
        if windowed:
            s_w = lax.dot_general(q8, kwin[:, 0:2 * LANES], _NT,
                                  preferred_element_type=F32).reshape(N_HEADS, WINDOW, 3 * WINDOW)
            s_w = jnp.concatenate(
                [jnp.where(before[None], s_w[:, :, 0:WINDOW], NEG_INF),
                 s_w[:, :, WINDOW:2 * WINDOW],
                 jnp.where(after[None], s_w[:, :, 2 * WINDOW:3 * WINDOW], NEG_INF)],
                axis=-1).reshape(N_HEADS * WINDOW, 3 * WINDOW)
            m = jnp.maximum(m, jnp.max(s_w, axis=-1, keepdims=True))
            p_w = jnp.exp2(s_w - m).astype(BF16)
        p_c = jnp.exp2(s_c - m).astype(BF16)
        tail = jnp.exp2(sk - m)
        for grp in range(N_KV):
            gr = slice(grp * gqa * WINDOW, (grp + 1) * gqa * WINDOW)
            v_c = jnp.concatenate([kvc[:, LANES * (2 + grp):LANES * (3 + grp)], ones_c], axis=1)
            o = jnp.dot(p_c[gr], v_c, preferred_element_type=F32)
            if windowed:
                v_w = jnp.concatenate([kwin[:, LANES * (2 + grp):LANES * (3 + grp)], ones_w], axis=1)
                o = o + jnp.dot(p_w[gr], v_w, preferred_element_type=F32)
            o = o[:, 0:LANES] / (o[:, LANES:2 * LANES] + tail[gr])
            for pr in range(gqa // 2):
                pair = (gqa // 2) * grp + pr
                even = o[2 * pr * WINDOW:(2 * pr + 1) * WINDOW]
                odd = o[(2 * pr + 1) * WINDOW:(2 * pr + 2) * WINDOW]
                ya_ref[pl.ds(r0, WINDOW), LANES * pair:LANES * (pair + 1)] = jnp.where(lane_lo, even, odd)

    part = 2 * WINDOW
    ycb = yc.astype(BF16)
    for p in range(tq // part):
        sub_block(2 * p)
        sub_block(2 * p + 1)
        sl = slice(p * part, (p + 1) * part)
        ya = ya_ref[sl, :]
        ya = ya * lax.rsqrt(jnp.mean(ya * ya, axis=-1, keepdims=True) + EPS) * ga_ref[...]
        y = (jnp.dot(ycb[sl], wo_ref[0:D_CONV, :], preferred_element_type=F32)
             + jnp.dot(ya.astype(BF16), wo_ref[D_CONV:2 * D_CONV, :], preferred_element_type=F32))
        out_ref[sl, :] = h_ref[sl, :] + g1_ref[...] * y


def _mixer(h, cu, q, kv, kvc, sink, cw, gc, ga, wo, g1, *, li, tq, seq, ctx_len, windowed):
    t, d = h.shape
    nt = seq // tq
    nb = t // seq
    row = lambda b, i: (b * nt + i, 0)
    fix = lambda b, i: (0, 0)
    smem = pl.BlockSpec(memory_space=pltpu.SMEM)
    tail = [pl.BlockSpec((3, D_CONV), fix),
            pl.BlockSpec((1, D_CONV), fix),
            pl.BlockSpec((1, D_ATTN), fix),
            pl.BlockSpec((None, d, d), lambda b, i: (li, 0, 0)),
            pl.BlockSpec((None, 1, d), lambda b, i: (b, 0, 0))]
    ctx_spec = pl.BlockSpec((ctx_len, 4 * LANES), lambda b, i: (b, 0))
    if windowed:
        r16 = tq // 16
        n16 = t // 16
        rw = tq // WINDOW
        nw = t // WINDOW
        in_specs = [smem,
                    pl.BlockSpec((tq, d), row),
                    pl.BlockSpec((tq, 2 * D_CONV), row),
                    pl.BlockSpec((16, 2 * D_CONV), lambda b, i: (jnp.maximum((b * nt + i) * r16 - 1, 0), 0)),
                    pl.BlockSpec((16, 2 * D_CONV), lambda b, i: (jnp.minimum((b * nt + i + 1) * r16, n16 - 1), 0)),
                    pl.BlockSpec((tq, D_ATTN), row),
                    pl.BlockSpec((WINDOW, 4 * LANES), lambda b, i: (jnp.maximum((b * nt + i) * rw - 1, 0), 0)),
                    pl.BlockSpec((tq, 4 * LANES), row),
                    pl.BlockSpec((WINDOW, 4 * LANES), lambda b, i: (jnp.minimum((b * nt + i + 1) * rw, nw - 1), 0)),
                    ctx_spec] + tail
        args = [sink, h, cu, cu, cu, q, kv, kv, kv, kvc, cw, gc, ga, wo, g1]
        scratch = [pltpu.VMEM((tq + 2 * WINDOW, 4 * LANES), BF16), pltpu.VMEM((tq, D_ATTN), F32)]
    else:
        in_specs = [smem,
                    pl.BlockSpec((tq, d), row),
                    pl.BlockSpec((tq, 2 * D_CONV), row),
                    pl.BlockSpec((tq, D_ATTN), row),
                    ctx_spec] + tail
        args = [sink, h, cu, q, kvc, cw, gc, ga, wo, g1]
        scratch = [pltpu.VMEM((tq, D_ATTN), F32)]
    return pl.pallas_call(
        functools.partial(_mixer_kernel, tq=tq, windowed=windowed),
        grid=(nb, nt),
        in_specs=in_specs,
        out_specs=pl.BlockSpec((tq, d), row),
        out_shape=jax.ShapeDtypeStruct((t, d), F32),
        scratch_shapes=scratch,
        compiler_params=_params(2),
        name="mixer_win" if windowed else "mixer_ctx",
    )(*args)


def _dense_ffn_kernel(h_ref, g_ref, sh_ref, sc_ref, gate_ref, w1_ref, w3_ref, w2_ref, out_ref, *, parts):
    rows = h_ref.shape[0] // parts
    for p in range(parts):
        sl = slice(p * rows, (p + 1) * rows)
        hp = h_ref[sl, :]
        xb = _norm_mod(hp, g_ref[...], sh_ref[...], sc_ref[...]).astype(BF16)
        h1 = jnp.dot(xb, w1_ref[...], preferred_element_type=F32)
        h3 = jnp.dot(xb, w3_ref[...], preferred_element_type=F32)
        a = (h1 * _sigmoid(h1) * h3).astype(BF16)
        y = jnp.dot(a, w2_ref[...], preferred_element_type=F32)
        out_ref[sl, :] = hp + gate_ref[...] * y


def _dense_ffn(h, g, sh, sc, gate, w1, w3, w2, *, li, tm, seq):
    t, d = h.shape
    f = w1.shape[2]
    wfix = lambda i: (li, 0, 0)
    tpb = seq // tm
    row = lambda i: (i, 0)
    fix = lambda i: (0, 0)
    mod = lambda i: (i // tpb, 0, 0)
    once = pl.Buffered(1)
    return pl.pallas_call(
        functools.partial(_dense_ffn_kernel, parts=max(tm // PART_ROWS, 1)),
        grid=(t // tm,),
        in_specs=[pl.BlockSpec((tm, d), row),
                  pl.BlockSpec((1, d), fix),
                  pl.BlockSpec((None, 1, d), mod),
                  pl.BlockSpec((None, 1, d), mod),
                  pl.BlockSpec((None, 1, d), mod),
                  pl.BlockSpec((None, d, f), wfix, pipeline_mode=once),
                  pl.BlockSpec((None, d, f), wfix, pipeline_mode=once),
                  pl.BlockSpec((None, f, d), wfix, pipeline_mode=once)],
        out_specs=pl.BlockSpec((tm, d), row),
        out_shape=jax.ShapeDtypeStruct((t, d), F32),
        compiler_params=_params(1),
        name="ffn_dense",
    )(h, g, sh, sc, gate, w1, w3, w2)


def _pack_w13_kernel(w1_ref, w3_ref, o_ref):
    fe = w1_ref.shape[1]
    o_ref[:, 0:fe] = w1_ref[...].astype(BF16)
    o_ref[:, fe:2 * fe] = w3_ref[...].astype(BF16)


def _pack_w13(w1, w3):
    nl, ne, d, fe = w1.shape
    rows = d // 2
    spec = pl.BlockSpec((None, None, rows, fe), lambda l, e, r: (l, e, r, 0))
    return pl.pallas_call(
        _pack_w13_kernel,
        grid=(nl, ne, d // rows),
        in_specs=[spec, spec],
        out_specs=pl.BlockSpec((None, None, rows, 2 * fe), lambda l, e, r: (l, e, r, 0)),
        out_shape=jax.ShapeDtypeStruct((nl, ne, d, 2 * fe), BF16),
        compiler_params=_params(3),
        name="pack_w13",
    )(w1, w3)


def _top2(logits):
    lane = lax.broadcasted_iota(jnp.int32, logits.shape, 1)
    lg = jnp.where(lane < N_EXPERTS, logits, NEG_INF)
    m1 = jnp.max(lg, axis=-1, keepdims=True)
    i1 = jnp.min(jnp.where(lg == m1, lane, LANES), axis=-1, keepdims=True)
    lg2 = jnp.where(lane == i1, NEG_INF, lg)
    m2 = jnp.max(lg2, axis=-1, keepdims=True)
    i2 = jnp.min(jnp.where(lg2 == m2, lane, LANES), axis=-1, keepdims=True)
    e2 = jnp.exp(m2 - m1)
    return i1, i2, 1.0 / (1.0 + e2), e2 / (1.0 + e2)


def _router_kernel(h_ref, g_ref, sh_ref, sc_ref, r_ref, tri_ref, xn_ref, route_ref, rt_ref, cnt_ref,
                   base_ref):
    tm = h_ref.shape[0]

    @pl.when(pl.program_id(0) == 0)
    def _():
        base_ref[...] = jnp.zeros_like(base_ref)

    xn = _norm_mod(h_ref[...], g_ref[...], sh_ref[...], sc_ref[...])
    xn_ref[...] = xn.reshape(xn_ref.shape)
    i1, i2, g1, g2 = _top2(jnp.dot(xn, r_ref[...], preferred_element_type=F32))
    lane = lax.broadcasted_iota(jnp.int32, (tm, LANES), 1)
    hit1 = lane == i1
    hit2 = lane == i2
    chosen = jnp.where(jnp.logical_or(hit1, hit2), 1.0, 0.0)
    before = base_ref[...] + jnp.dot(tri_ref[...], chosen.astype(BF16), preferred_element_type=F32)
    r1 = jnp.sum(jnp.where(hit1, before, 0.0), axis=-1, keepdims=True)
    r2 = jnp.sum(jnp.where(hit2, before, 0.0), axis=-1, keepdims=True)
    base_ref[...] += jnp.sum(chosen, axis=0, keepdims=True)
    cnt_ref[...] = base_ref[...]
    fields = (i1.astype(F32), i2.astype(F32), g1, g2, r1, r2)
    route = jnp.zeros((tm, LANES), F32)
    for k, f in enumerate(fields):
        route = jnp.where(lane == k, f, route)
    route_ref[...] = route
    rt_ref[...] = route.T[0:8, :]


def _router(h, g, sh, sc, router, *, tm, seq):
    t, d = h.shape
    tpb = seq // tm
    row = lambda i: (i, 0)
    fix = lambda i: (0, 0)
    mod = lambda i: (i // tpb, 0, 0)
    ids = jnp.arange(tm)
    tri = (ids[None, :] < ids[:, None]).astype(BF16)
    return pl.pallas_call(
        _router_kernel,
        grid=(t // tm,),
        in_specs=[pl.BlockSpec((tm, d), row), pl.BlockSpec((1, d), fix),
                  pl.BlockSpec((None, 1, d), mod), pl.BlockSpec((None, 1, d), mod),
                  pl.BlockSpec((d, LANES), fix), pl.BlockSpec((tm, tm), fix)],
        out_specs=[pl.BlockSpec((tm, d // LANES, LANES), lambda i: (i, 0, 0)),
                   pl.BlockSpec((tm, LANES), row),
                   pl.BlockSpec((None, 8, tm), lambda i: (i, 0, 0)),
                   pl.BlockSpec((1, LANES), fix)],
        out_shape=[jax.ShapeDtypeStruct((t, d // LANES, LANES), F32),
                   jax.ShapeDtypeStruct((t, LANES), F32),
                   jax.ShapeDtypeStruct((t // tm, 8, tm), F32),
                   jax.ShapeDtypeStruct((1, LANES), F32)],
        scratch_shapes=[pltpu.VMEM((1, LANES), F32)],
        compiler_params=_params(1),
        name="moe_router",
    )(h, g, sh, sc, router, tri)


def _route_plan(t, counts, tr):
    counts = counts[0, 0:N_EXPERTS].astype(jnp.int32)
    tiles = (counts + tr - 1) // tr
    tile_end = jnp.cumsum(tiles)
    tile_start = tile_end - tiles
    experts = jnp.arange(N_EXPERTS, dtype=jnp.int32)

    nt = 2 * t // tr + N_EXPERTS
    tid = jnp.arange(nt, dtype=jnp.int32)
    tile_expert = jnp.minimum(jnp.sum((tid[:, None] >= tile_end[None, :]).astype(jnp.int32), axis=1),
                              N_EXPERTS - 1)
    in_tile = tid - jnp.sum(jnp.where(tile_expert[:, None] == experts[None, :], tile_start[None, :], 0), axis=1)
    own = jnp.sum(jnp.where(tile_expert[:, None] == experts[None, :], counts[None, :], 0), axis=1)
    n_valid = jnp.where(tid < tile_end[-1], jnp.clip(own - in_tile * tr, 0, tr), 0)
    tail = tile_end[-1] + experts
    pad_tiles = jnp.concatenate([jnp.where(tiles > 0, tile_end - 1, -1), jnp.where(tail < nt, tail, -1)])
    return tile_expert, n_valid, tile_start * tr, pad_tiles


def _table_kernel(start_ref, rt_ref, tab_ref, *, nt):
    tm = rt_ref.shape[1]

    def position(e, r):
        start = jnp.zeros_like(r)
        for k in range(N_EXPERTS):
            start = jnp.where(e == float(k), start_ref[k].astype(F32), start)
        return (start + r).astype(jnp.int32)

    live = pl.program_id(0) < nt
    rt = rt_ref[...]
    for c in range(2):
        pos = position(rt[c:c + 1, :], rt[4 + c:5 + c, :])
        tab_ref[:, c * tm:(c + 1) * tm] = jnp.where(live, pos, 0)


def _position_table(route_t, row_start):
    nt, fields, tm = route_t.shape
    grid_spec = pltpu.PrefetchScalarGridSpec(
        num_scalar_prefetch=1,
        grid=(nt + 2,),
        in_specs=[pl.BlockSpec((None, fields, tm), lambda j, st: (jnp.minimum(j, nt - 1), 0, 0))],
        out_specs=pl.BlockSpec((None, 1, 2 * tm), lambda j, st: (j, 0, 0)))
    return pl.pallas_call(
        functools.partial(_table_kernel, nt=nt),
        grid_spec=grid_spec,
        out_shape=jax.ShapeDtypeStruct((nt + 2, 1, 2 * tm), jnp.int32),
        compiler_params=_params(1),
        name="moe_table",
    )(row_start, route_t)


def _row_copies(idx_smem, s_idx, tm, make):
    base = s_idx * (2 * tm)

    def body(r, c):
        make(r, idx_smem[base + r], idx_smem[base + tm + r])
        return c

    lax.fori_loop(0, tm, body, 0, unroll=8)


def _dispatch_kernel(zt_ref, idx_hbm, xn_hbm, xg_hbm, idx_smem, zbuf, xbuf,
                     sem_d, sem_i, sem_z, sem_in, *, tm, nt, tr):
    j = pl.program_id(0)
    cur = j % 3
    nxt = (j + 1) % 3

    def in_copy(tile, s):
        return pltpu.make_async_copy(xn_hbm.at[pl.ds(tile * tm, tm)], xbuf.at[s], sem_in.at[s])

    @pl.when(j == 0)
    def _():
        zbuf[...] = jnp.zeros_like(zbuf)
        for k in range(zt_ref.shape[0]):
            fill = pltpu.make_async_copy(zbuf, xg_hbm.at[pl.ds(jnp.maximum(zt_ref[k], 0) * tr, tr)], sem_z)
            pl.when(zt_ref[k] >= 0)(fill.start)
        for k in range(zt_ref.shape[0]):
            fill = pltpu.make_async_copy(zbuf, xg_hbm.at[pl.ds(0, tr)], sem_z)
            pl.when(zt_ref[k] >= 0)(fill.wait)

    def idx_copy(row, s):
        return pltpu.make_async_copy(idx_hbm.at[row, 0], idx_smem.at[pl.ds(s * 2 * tm, 2 * tm)], sem_i.at[s])

    def wait_rows(s):
        for _ in range(2):
            pltpu.make_async_copy(xbuf.at[s], xg_hbm.at[pl.ds(0, tm)], sem_d.at[s]).wait()

    @pl.when(j == 0)
    def _():
        idx_copy(0, 0).start()
        in_copy(0, 0).start()

    @pl.when(j >= 2)
    def _():
        wait_rows(nxt)

    @pl.when(j + 1 < nt)
    def _():
        in_copy(j + 1, nxt).start()

    idx_copy(j, cur).wait()
    idx_copy(j + 1, nxt).start()
    in_copy(j, cur).wait()

    def issue(c):
        def make(r, p1, p2):
            src = xbuf.at[c, r]
            pltpu.make_async_copy(src, xg_hbm.at[p1], sem_d.at[c]).start(priority=0)
            pltpu.make_async_copy(src, xg_hbm.at[p2], sem_d.at[c]).start(priority=1)

        _row_copies(idx_smem, c, tm, make)

    for c in range(3):
        pl.when(cur == c)(functools.partial(issue, c))

    @pl.when(j == nt - 1)
    def _():
        if nt > 1:
            wait_rows((nt - 2) % 3)
        wait_rows((nt - 1) % 3)
        idx_copy(j + 1, nxt).wait()


def _dispatch(xn3, table, pad_tiles, *, tm, tr, n_rows):
    t = xn3.shape[0]
    nt = t // tm
    any_spec = pl.BlockSpec(memory_space=pl.ANY)
    grid_spec = pltpu.PrefetchScalarGridSpec(
        num_scalar_prefetch=1,
        grid=(nt,),
        in_specs=[any_spec, any_spec],
        out_specs=any_spec,
        scratch_shapes=[pltpu.SMEM((6 * tm,), jnp.int32),
                        pltpu.VMEM((tr,) + xn3.shape[1:], F32),
                        pltpu.VMEM((3, tm) + xn3.shape[1:], F32),
                        pltpu.SemaphoreType.DMA((3,)), pltpu.SemaphoreType.DMA((3,)),
                        pltpu.SemaphoreType.DMA, pltpu.SemaphoreType.DMA((3,))])
    return pl.pallas_call(
        functools.partial(_dispatch_kernel, tm=tm, nt=nt, tr=tr),
        grid_spec=grid_spec,
        out_shape=jax.ShapeDtypeStruct((n_rows,) + xn3.shape[1:], F32),
        compiler_params=pltpu.CompilerParams(dimension_semantics=("arbitrary",),
                                             vmem_limit_bytes=VMEM_LIMIT,
                                             disable_bounds_checks=True),
        name="moe_dispatch",
    )(pad_tiles, table, xn3)


def _expert_kernel(te_ref, nv_ref, x_ref, w13_ref, w2_ref, y_ref):
    tr = x_ref.shape[0]
    nv = nv_ref[pl.program_id(0)]

    @pl.when(nv > 0)
    def _():
        half = tr // 2
        for p in range(2):
            x = x_ref[p * half:(p + 1) * half].reshape(half, D_MODEL).astype(BF16)
            h13 = jnp.dot(x, w13_ref[...], preferred_element_type=F32)
            fe = h13.shape[1] // 2
            h1, h3 = h13[:, 0:fe], h13[:, fe:2 * fe]
            a = (h1 * _sigmoid(h1) * h3).astype(BF16)
            y = jnp.dot(a, w2_ref[...], preferred_element_type=F32)
            y_ref[p * half:(p + 1) * half] = y.reshape((half,) + y_ref.shape[1:])

    @pl.when(nv == 0)
    def _():
        y_ref[...] = jnp.zeros_like(y_ref)


def _experts(xg3, tile_expert, n_valid, w13, w2, *, li, tr):
    n_rows, sl, ln = xg3.shape
    d = sl * ln
    fe = w2.shape[2]
    rows = lambda j, te, nv: (j, 0, 0)
    wsel = lambda j, te, nv: (li, te[j], 0, 0)
    grid_spec = pltpu.PrefetchScalarGridSpec(
        num_scalar_prefetch=2,
        grid=(n_rows // tr,),
        in_specs=[pl.BlockSpec((tr, sl, ln), rows),
                  pl.BlockSpec((None, None, d, 2 * fe), wsel),
                  pl.BlockSpec((None, None, fe, d), wsel)],
        out_specs=pl.BlockSpec((tr, sl, ln), rows))
    return pl.pallas_call(
        _expert_kernel,
        grid_spec=grid_spec,
        out_shape=jax.ShapeDtypeStruct(xg3.shape, F32),
        compiler_params=_params(1),
        name="moe_experts",
    )(tile_expert, n_valid, xg3, w13, w2)


def _combine_kernel(idx_hbm, h_ref, gate_ref, route_ref, yg_hbm, out_ref,
                    y1buf, y2buf, idx_smem, sem_y, sem_i, *, tm, nt):
    j = pl.program_id(0)
    slot = j % 2
    other = 1 - slot

    def idx_copy(row, s):
        return pltpu.make_async_copy(idx_hbm.at[row, 0], idx_smem.at[pl.ds(s * 2 * tm, 2 * tm)], sem_i.at[s])

    def fetch(s_idx, s_buf):
        def make(r, p1, p2):
            pltpu.make_async_copy(yg_hbm.at[p1], y1buf.at[s_buf, r], sem_y.at[s_buf]).start(priority=0)
            pltpu.make_async_copy(yg_hbm.at[p2], y2buf.at[s_buf, r], sem_y.at[s_buf]).start(priority=1)

        _row_copies(idx_smem, s_idx, tm, make)

    def wait_rows(s):
        pltpu.make_async_copy(yg_hbm.at[pl.ds(0, tm)], y1buf.at[s], sem_y.at[s]).wait()
        pltpu.make_async_copy(yg_hbm.at[pl.ds(0, tm)], y2buf.at[s], sem_y.at[s]).wait()

    @pl.when(j == 0)
    def _():
        first = idx_copy(0, 0)
        first.start()
        first.wait()
        fetch(0, 0)
        idx_copy(1, 1).start()

    idx_copy(j + 1, other).wait()
    for c in range(2):
        pl.when(other == c)(functools.partial(fetch, c, c))
    idx_copy(j + 2, slot).start()
    wait_rows(slot)
    rt = route_ref[...]
    y1 = y1buf[slot].reshape(tm, D_MODEL)
    y2 = y2buf[slot].reshape(tm, D_MODEL)
    out_ref[...] = h_ref[...] + gate_ref[...] * (rt[:, 2:3] * y1 + rt[:, 3:4] * y2)

    @pl.when(j == nt - 1)
    def _():
        wait_rows(other)
        idx_copy(j + 2, slot).wait()


def _combine(h, gate, route, table, yg3, *, tm, seq):
    t, d = h.shape
    tpb = seq // tm
    nt = t // tm
    sl, ln = yg3.shape[1:]
    row = lambda i: (i, 0)
    any_spec = pl.BlockSpec(memory_space=pl.ANY)
    return pl.pallas_call(
        functools.partial(_combine_kernel, tm=tm, nt=nt),
        grid=(nt,),
        in_specs=[any_spec,
                  pl.BlockSpec((tm, d), row),
                  pl.BlockSpec((None, 1, d), lambda i: (i // tpb, 0, 0)),
                  pl.BlockSpec((tm, LANES), row),
                  any_spec],
        out_specs=pl.BlockSpec((tm, d), row),
        out_shape=jax.ShapeDtypeStruct((t, d), F32),
        scratch_shapes=[pltpu.VMEM((2, tm, sl, ln), F32), pltpu.VMEM((2, tm, sl, ln), F32),
                        pltpu.SMEM((4 * tm,), jnp.int32),
                        pltpu.SemaphoreType.DMA((2,)), pltpu.SemaphoreType.DMA((2,))],
        compiler_params=pltpu.CompilerParams(dimension_semantics=("arbitrary",),
                                             vmem_limit_bytes=VMEM_LIMIT,
                                             disable_bounds_checks=True),
        name="moe_combine",
    )(table, h, gate, route, yg3)


def _moe(h, g, sh, sc, gate, w13, w2, router, *, li, seq, tr, tm):
    t = h.shape[0]
    xn3, route, route_t, counts = _router(h, g, sh, sc, router, tm=tm, seq=seq)
    tile_expert, n_valid, row_start, pad_tiles = _route_plan(t, counts, tr)
    table = _position_table(route_t, row_start)
    xg3 = _dispatch(xn3, table, pad_tiles, tm=tm, tr=tr, n_rows=2 * t + N_EXPERTS * tr)
    yg3 = _experts(xg3, tile_expert, n_valid, w13, w2, li=li, tr=tr)
    return _combine(h, gate, route, table, yg3, tm=tm, seq=seq)


def _rope_tables(seq):
    rows = seq // GRID_W
    row, col = jnp.meshgrid(jnp.arange(rows, dtype=F32), jnp.arange(GRID_W, dtype=F32), indexing='ij')
    n_freq = HEAD_DIM // 4
    inv_freq = ROPE_THETA ** (-jnp.arange(n_freq, dtype=F32) / n_freq)
    ang_r = row.reshape(-1, 1) * inv_freq
    ang_c = col.reshape(-1, 1) * inv_freq
    ang = jnp.concatenate([ang_r, ang_r, ang_c, ang_c], axis=-1)
    cos, sin = jnp.cos(ang), jnp.sin(ang)
    first = (jnp.arange(HEAD_DIM) % (2 * n_freq)) < n_freq
    sin_a = jnp.where(first, -sin, 0.0)
    sin_b = jnp.where(first, 0.0, sin)
    rep = LANES // HEAD_DIM
    return tuple(jnp.tile(t, (1, rep)) for t in (cos, sin_a, sin_b))


def kernel(x, c, ctx, c_ctx, w_ada, b_ada, norm1_g, norm2_g, w_in, conv_w, q_norm_g, k_norm_g,
           attn_sink, out_norm_conv_g, out_norm_attn_g, w_out, ffn_w1, ffn_w3, ffn_w2,
           moe_router, moe_w1, moe_w3, moe_w2):
    b, s, d = x.shape
    lc = ctx.shape[1]
    depth = w_ada.shape[0]
    assert d == D_MODEL and s % BIG_TILE == 0 and lc % PART_ROWS == 0 and b + 1 <= 8

    c8 = jnp.zeros((8, d), F32).at[0:b].set(c).at[b].set(c_ctx)
    mod = _modulation(c8, w_ada, b_ada)

    tables = _rope_tables(s)
    ids = jnp.arange(256)
    gm = (ids[:, None] // HEAD_DIM == ids[None, :] // HEAD_DIM).astype(BF16)
    scale = HEAD_DIM ** -0.5 * LOG2E

    h = x.reshape(b * s, d)
    hc = ctx.reshape(b * lc, d)
    w_in_b, w_out_b = w_in.astype(BF16), w_out.astype(BF16)
    dense_w = [w.astype(BF16) for w in (ffn_w1, ffn_w3, ffn_w2)]
    moe_w13 = _pack_w13(moe_w1, moe_w3)
    moe_w2b = moe_w2.astype(BF16)
    for layer in range(depth):
        last = layer == depth - 1
        m = mod[layer]
        lat = [m[0:b, k * d:(k + 1) * d].reshape(b, 1, d) for k in range(6)]
        cx = [jnp.broadcast_to(m[b:b + 1, k * d:(k + 1) * d].reshape(1, 1, d), (b, 1, d)) for k in range(6)]
        g1n = norm1_g[layer].reshape(1, d)
        g2n = norm2_g[layer].reshape(1, d)
        gq = (jnp.tile(q_norm_g[layer], N_HEADS) * scale).reshape(1, D_ATTN)
        gk = jnp.tile(k_norm_g[layer], N_KV).reshape(1, LANES)
        gc = out_norm_conv_g[layer].reshape(1, D_CONV)
        ga = out_norm_attn_g[layer].reshape(1, D_ATTN)
        sink = attn_sink[layer]
        cw = conv_w[layer]

        cu, q, kv = _inproj(h, g1n, lat[0], lat[1], w_in_b, gq, gk, gm, tables, li=layer, tm=BIG_TILE, seq=s)
        cuc, qc, kvc = _inproj(hc, g1n, cx[0], cx[1], w_in_b, gq, gk, gm, None, li=layer, tm=lc, seq=lc)
        h = _mixer(h, cu, q, kv, kvc, sink, cw, gc, ga, w_out_b, lat[2],
                   li=layer, tq=BIG_TILE, seq=s, ctx_len=lc, windowed=True)
        if not last:
            hc = _mixer(hc, cuc, qc, None, kvc, sink, cw, gc, ga, w_out_b, cx[2],
                        li=layer, tq=lc, seq=lc, ctx_len=lc, windowed=False)

        i = layer // 2
        if layer % 2 == 0:
            h = _dense_ffn(h, g2n, lat[3], lat[4], lat[5], *dense_w, li=i, tm=BIG_TILE, seq=s)
            if not last:
                hc = _dense_ffn(hc, g2n, cx[3], cx[4], cx[5], *dense_w, li=i, tm=lc, seq=lc)
        else:
            router = jnp.zeros((d, LANES), F32).at[:, 0:N_EXPERTS].set(moe_router[i])
            h = _moe(h, g2n, lat[3], lat[4], lat[5], moe_w13, moe_w2b, router,
                     li=i, seq=s, tr=MOE_TILE, tm=MOE_TILE)
            if not last:
                hc = _moe(hc, g2n, cx[3], cx[4], cx[5], moe_w13, moe_w2b, router,
                          li=i, seq=lc, tr=lc, tm=lc)
    return h.reshape(b, s, d)
```

```python
import functools

import jax
import jax.numpy as jnp
from jax import lax
from jax.experimental import pallas as pl
from jax.experimental.pallas import tpu as pltpu

D_MODEL = 1024
GRID_W = 64
HEAD_DIM = 64
D_CONV = 512
D_ATTN = 512
N_HEADS = 8
N_KV = 2
WINDOW = 128
ROPE_THETA = 10000.0
N_EXPERTS = 8
EPS = 1e-6
KV_OFF = 3 * D_CONV + D_ATTN
D_IN = KV_OFF + 2 * N_KV * HEAD_DIM
LANES = 128
VMEM_LIMIT = 48 * 1024 * 1024
BIG_TILE = 1024
MOE_TILE = 512
PART_ROWS = 256

F32 = jnp.float32
BF16 = jnp.bfloat16
NEG_INF = float("-inf")
LOG2E = 1.4426950408889634


def _params(n_axes):
    return pltpu.CompilerParams(dimension_semantics=("arbitrary",) * n_axes,
                                vmem_limit_bytes=VMEM_LIMIT)


def _sigmoid(x):
    return 1.0 / (1.0 + jnp.exp(-x))


def _mod_kernel(c_ref, w_ref, b_ref, o_ref):
    c = c_ref[...]
    s = c * _sigmoid(c)
    o_ref[...] = jnp.dot(s, w_ref[...], preferred_element_type=F32) + b_ref[...]


def _modulation(c8, w_ada, b_ada):
    depth, d, n = w_ada.shape
    tn = 1536
    return pl.pallas_call(
        _mod_kernel,
        grid=(depth, n // tn),
        in_specs=[pl.BlockSpec((8, d), lambda l, j: (0, 0)),
                  pl.BlockSpec((None, d, tn), lambda l, j: (l, 0, j)),
                  pl.BlockSpec((None, 1, tn), lambda l, j: (l, 0, j))],
        out_specs=pl.BlockSpec((None, 8, tn), lambda l, j: (l, 0, j)),
        out_shape=jax.ShapeDtypeStruct((depth, 8, n), F32),
        compiler_params=_params(2),
        name="adaln_mod",
    )(c8, w_ada, b_ada.reshape(depth, 1, n))


def _norm_mod(x, g, sh, sc):
    ms = jnp.mean(x * x, axis=-1, keepdims=True)
    return (x * lax.rsqrt(ms + EPS) * g) * (1.0 + sc) + sh


def _inproj_kernel(*refs, rope, parts):
    if rope:
        (h_ref, g_ref, sh_ref, sc_ref, w_ref, gq_ref, gk_ref, gm_ref,
         cos_ref, sa_ref, sb_ref, cu_ref, q_ref, kv_ref) = refs
    else:
        (h_ref, g_ref, sh_ref, sc_ref, w_ref, gq_ref, gk_ref, gm_ref,
         cu_ref, q_ref, kv_ref) = refs
    gm = gm_ref[...]
    gq = gq_ref[...]
    rows = h_ref.shape[0] // parts

    def head_norm(t, gain):
        w = t.shape[1]
        ss = jnp.dot((t * t).astype(BF16), gm[0:w, 0:w], preferred_element_type=F32)
        return t * lax.rsqrt(ss * (1.0 / HEAD_DIM) + EPS) * gain

    for p in range(parts):
        sl = slice(p * rows, (p + 1) * rows)

        def rot(t):
            if not rope:
                return t
            return (t * cos_ref[sl, :] + pltpu.roll(t, LANES - 16, 1) * sa_ref[sl, :]
                    + pltpu.roll(t, 16, 1) * sb_ref[sl, :])

        xb = _norm_mod(h_ref[sl, :], g_ref[...], sh_ref[...], sc_ref[...]).astype(BF16)
        ya = jnp.dot(xb, w_ref[:, 3 * D_CONV:D_IN], preferred_element_type=F32)
        yc = jnp.dot(xb, w_ref[:, 0:3 * D_CONV], preferred_element_type=F32)
        cu_ref[sl, 0:D_CONV] = yc[:, 0:D_CONV].astype(BF16)
        cu_ref[sl, D_CONV:2 * D_CONV] = (yc[:, D_CONV:2 * D_CONV] * yc[:, 2 * D_CONV:3 * D_CONV]).astype(BF16)
        for j in range(D_ATTN // 256):
            qn = head_norm(ya[:, 256 * j:256 * j + 256], gq[:, 256 * j:256 * j + 256])
            for c in range(2):
                q_ref[sl, 256 * j + LANES * c:256 * j + LANES * (c + 1)] = rot(
                    qn[:, LANES * c:LANES * (c + 1)]).astype(BF16)
        k = rot(head_norm(ya[:, D_ATTN:D_ATTN + LANES], gk_ref[...]))
        v = ya[:, D_ATTN + LANES:D_ATTN + 2 * LANES]
        lo_half = lax.broadcasted_iota(jnp.int32, k.shape, 1) < HEAD_DIM
        for c, t in enumerate((k, v)):
            sw = pltpu.roll(t, HEAD_DIM, 1)
            kv_ref[sl, 2 * c * LANES:(2 * c + 1) * LANES] = jnp.where(lo_half, t, sw).astype(BF16)
            kv_ref[sl, (2 * c + 1) * LANES:(2 * c + 2) * LANES] = jnp.where(lo_half, sw, t).astype(BF16)


def _inproj(h, g, sh, sc, w, gq, gk, gm, tables, *, li, tm, seq):
    t, d = h.shape
    tpb = seq // tm
    rope = tables is not None
    row = lambda i: (i, 0)
    fix = lambda i: (0, 0)
    mod = lambda i: (i // tpb, 0, 0)
    in_specs = [pl.BlockSpec((tm, d), row),
                pl.BlockSpec((1, d), fix),
                pl.BlockSpec((None, 1, d), mod),
                pl.BlockSpec((None, 1, d), mod),
                pl.BlockSpec((None, d, D_IN), lambda i: (li, 0, 0)),
                pl.BlockSpec((1, D_ATTN), fix),
                pl.BlockSpec((1, LANES), fix),
                pl.BlockSpec((256, 256), fix)]
    args = [h, g, sh, sc, w, gq, gk, gm]
    if rope:
        in_specs += [pl.BlockSpec((tm, LANES), lambda i: (i % tpb, 0))] * 3
        args += list(tables)
    return pl.pallas_call(
        functools.partial(_inproj_kernel, rope=rope, parts=max(tm // PART_ROWS, 1)),
        grid=(t // tm,),
        in_specs=in_specs,
        out_specs=[pl.BlockSpec((tm, 2 * D_CONV), row),
                   pl.BlockSpec((tm, D_ATTN), row),
                   pl.BlockSpec((tm, 4 * LANES), row)],
        out_shape=[jax.ShapeDtypeStruct((t, 2 * D_CONV), BF16),
                   jax.ShapeDtypeStruct((t, D_ATTN), BF16),
                   jax.ShapeDtypeStruct((t, 4 * LANES), BF16)],
        compiler_params=_params(1),
        name="inproj_rope" if rope else "inproj_ctx",
    )(*args)


_NT = (((1,), (1,)), ((), ()))


def _mixer_kernel(*refs, tq, windowed):
    if windowed:
        (sink_ref, h_ref, cu_ref, cup_ref, cun_ref, q_ref, kvp_ref, kv_ref, kvn_ref, kvc_ref,
         cw_ref, gc_ref, ga_ref, wo_ref, g1_ref, out_ref, kw_ref, ya_ref) = refs
    else:
        (sink_ref, h_ref, cu_ref, q_ref, kvc_ref,
         cw_ref, gc_ref, ga_ref, wo_ref, g1_ref, out_ref, ya_ref) = refs
    i = pl.program_id(1)
    nt = pl.num_programs(1)
    nsub = tq // WINDOW

    cu = cu_ref[...]
    bg = cu[:, 0:D_CONV].astype(F32)
    u = cu[:, D_CONV:2 * D_CONV].astype(F32)
    rows = lax.broadcasted_iota(jnp.int32, (tq, 1), 0)
    if windowed:
        up_row = cup_ref[:, D_CONV:2 * D_CONV].astype(F32)[15:16, :]
        un_row = cun_ref[:, D_CONV:2 * D_CONV].astype(F32)[0:1, :]
        up_row = jnp.where(i > 0, up_row, 0.0)
        un_row = jnp.where(i < nt - 1, un_row, 0.0)
    else:
        up_row = jnp.zeros((1, D_CONV), F32)
        un_row = jnp.zeros((1, D_CONV), F32)
    u_prev = jnp.where(rows == 0, up_row, pltpu.roll(u, 1, 0))
    u_next = jnp.where(rows == tq - 1, un_row, pltpu.roll(u, tq - 1, 0))
    cw = cw_ref[...]
    yc = bg * (cw[0:1, :] * u_prev + cw[1:2, :] * u + cw[2:3, :] * u_next)
    yc = yc * lax.rsqrt(jnp.mean(yc * yc, axis=-1, keepdims=True) + EPS) * gc_ref[...]

    if windowed:
        kw_ref[0:WINDOW, :] = kvp_ref[...]
        kw_ref[WINDOW:WINDOW + tq, :] = kv_ref[...]
        kw_ref[WINDOW + tq:2 * WINDOW + tq, :] = kvn_ref[...]
    lane_lo = lax.broadcasted_iota(jnp.int32, (WINDOW, LANES), 1) < HEAD_DIM
    kvc = kvc_ref[...]
    gqa = N_HEADS // N_KV
    head_of_row = lax.broadcasted_iota(jnp.int32, (N_HEADS * WINDOW, 1), 0) // WINDOW
    ones_c = jnp.ones((kvc.shape[0], LANES), BF16)
    if windowed:
        kk = lax.broadcasted_iota(jnp.int32, (WINDOW, WINDOW), 1)
        qq = lax.broadcasted_iota(jnp.int32, (WINDOW, WINDOW), 0)
        keep_before = kk >= qq
        keep_after = kk <= qq
        ones_w = jnp.ones((3 * WINDOW, LANES), BF16)

    def sub_block(s):
        r0 = s * WINDOW
        if windowed:
            kwin = kw_ref[pl.ds(r0, 3 * WINDOW), :]
            before, after = keep_before, keep_after
            if s == 0:
                before = before & (i > 0)
            if s == nsub - 1:
                after = after & (i < nt - 1)
        rows, sk = [], jnp.zeros((N_HEADS * WINDOW, 1), F32)
        for head in range(N_HEADS):
            qp = q_ref[pl.ds(r0, WINDOW), LANES * (head // 2):LANES * (head // 2 + 1)]
            zero = jnp.zeros_like(qp)
            qh = jnp.where(lane_lo, zero, qp) if head % 2 else jnp.where(lane_lo, qp, zero)
            rows.append(jnp.concatenate([qh, zero] if head < gqa else [zero, qh], axis=1))
            sk = jnp.where(head_of_row == head, sink_ref[head] * LOG2E, sk)
        q8 = jnp.concatenate(rows, axis=0)
        s_c = lax.dot_general(q8, kvc[:, 0:2 * LANES], _NT, preferred_element_type=F32)
        m = jnp.maximum(jnp.max(s_c, axis=-1, keepdims=True), sk)
        if windowed:
            s_w = lax.dot_general(q8, kwin[:, 0:2 * LANES], _NT,
                                  preferred_element_type=F32).reshape(N_HEADS, WINDOW, 3 * WINDOW)
            s_w = jnp.concatenate(
                [jnp.where(before[None], s_w[:, :, 0:WINDOW], NEG_INF),
                 s_w[:, :, WINDOW:2 * WINDOW],
                 jnp.where(after[None], s_w[:, :, 2 * WINDOW:3 * WINDOW], NEG_INF)],
                axis=-1).reshape(N_HEADS * WINDOW, 3 * WINDOW)
            m = jnp.maximum(m, jnp.max(s_w, axis=-1, keepdims=True))
            p_w = jnp.exp2(s_w - m).astype(BF16)
        p_c = jnp.exp2(s_c - m).astype(BF16)
        tail = jnp.exp2(sk - m)
        for grp in range(N_KV):
            gr = slice(grp * gqa * WINDOW, (grp + 1) * gqa * WINDOW)
            v_c = jnp.concatenate([kvc[:, LANES * (2 + grp):LANES * (3 + grp)], ones_c], axis=1)
            o = jnp.dot(p_c[gr], v_c, preferred_element_type=F32)
            if windowed:
                v_w = jnp.concatenate([kwin[:, LANES * (2 + grp):LANES * (3 + grp)], ones_w], axis=1)
                o = o + jnp.dot(p_w[gr], v_w, preferred_element_type=F32)
            o = o[:, 0:LANES] / (o[:, LANES:2 * LANES] + tail[gr])
            for pr in range(gqa // 2):
                pair = (gqa // 2) * grp + pr
                even = o[2 * pr * WINDOW:(2 * pr + 1) * WINDOW]
                odd = o[(2 * pr + 1) * WINDOW:(2 * pr + 2) * WINDOW]
                ya_ref[pl.ds(r0, WINDOW), LANES * pair:LANES * (pair + 1)] = jnp.where(lane_lo, even, odd)

    part = 2 * WINDOW
    ycb = yc.astype(BF16)
    for p in range(tq // part):
        sub_block(2 * p)
        sub_block(2 * p + 1)
        sl = slice(p * part, (p + 1) * part)
        ya = ya_ref[sl, :]
        ya = ya * lax.rsqrt(jnp.mean(ya * ya, axis=-1, keepdims=True) + EPS) * ga_ref[...]
        y = (jnp.dot(ycb[sl], wo_ref[0:D_CONV, :], preferred_element_type=F32)
             + jnp.dot(ya.astype(BF16), wo_ref[D_CONV:2 * D_CONV, :], preferred_element_type=F32))
        out_ref[sl, :] = h_ref[sl, :] + g1_ref[...] * y


def _mixer(h, cu, q, kv, kvc, sink, cw, gc, ga, wo, g1, *, li, tq, seq, ctx_len, windowed):
    t, d = h.shape
    nt = seq // tq
    nb = t // seq
    row = lambda b, i: (b * nt + i, 0)
    fix = lambda b, i: (0, 0)
    smem = pl.BlockSpec(memory_space=pltpu.SMEM)
    tail = [pl.BlockSpec((3, D_CONV), fix),
            pl.BlockSpec((1, D_CONV), fix),
            pl.BlockSpec((1, D_ATTN), fix),
            pl.BlockSpec((None, d, d), lambda b, i: (li, 0, 0)),
            pl.BlockSpec((None, 1, d), lambda b, i: (b, 0, 0))]
    ctx_spec = pl.BlockSpec((ctx_len, 4 * LANES), lambda b, i: (b, 0))
    if windowed:
        r16 = tq // 16
        n16 = t // 16
        rw = tq // WINDOW
        nw = t // WINDOW
        in_specs = [smem,
                    pl.BlockSpec((tq, d), row),
                    pl.BlockSpec((tq, 2 * D_CONV), row),
                    pl.BlockSpec((16, 2 * D_CONV), lambda b, i: (jnp.maximum((b * nt + i) * r16 - 1, 0), 0)),
                    pl.BlockSpec((16, 2 * D_CONV), lambda b, i: (jnp.minimum((b * nt + i + 1) * r16, n16 - 1), 0)),
                    pl.BlockSpec((tq, D_ATTN), row),
                    pl.BlockSpec((WINDOW, 4 * LANES), lambda b, i: (jnp.maximum((b * nt + i) * rw - 1, 0), 0)),
                    pl.BlockSpec((tq, 4 * LANES), row),
                    pl.BlockSpec((WINDOW, 4 * LANES), lambda b, i: (jnp.minimum((b * nt + i + 1) * rw, nw - 1), 0)),
                    ctx_spec] + tail
        args = [sink, h, cu, cu, cu, q, kv, kv, kv, kvc, cw, gc, ga, wo, g1]
        scratch = [pltpu.VMEM((tq + 2 * WINDOW, 4 * LANES), BF16), pltpu.VMEM((tq, D_ATTN), F32)]
    else:
        in_specs = [smem,
                    pl.BlockSpec((tq, d), row),
                    pl.BlockSpec((tq, 2 * D_CONV), row),
                    pl.BlockSpec((tq, D_ATTN), row),
                    ctx_spec] + tail
        args = [sink, h, cu, q, kvc, cw, gc, ga, wo, g1]
        scratch = [pltpu.VMEM((tq, D_ATTN), F32)]
    return pl.pallas_call(
        functools.partial(_mixer_kernel, tq=tq, windowed=windowed),
        grid=(nb, nt),
        in_specs=in_specs,
        out_specs=pl.BlockSpec((tq, d), row),
        out_shape=jax.ShapeDtypeStruct((t, d), F32),
        scratch_shapes=scratch,
        compiler_params=_params(2),
        name="mixer_win" if windowed else "mixer_ctx",
    )(*args)


def _dense_ffn_kernel(h_ref, g_ref, sh_ref, sc_ref, gate_ref, w1_ref, w3_ref, w2_ref, out_ref, *, parts):
    rows = h_ref.shape[0] // parts
    for p in range(parts):
        sl = slice(p * rows, (p + 1) * rows)
        hp = h_ref[sl, :]
        xb = _norm_mod(hp, g_ref[...], sh_ref[...], sc_ref[...]).astype(BF16)
        h1 = jnp.dot(xb, w1_ref[...], preferred_element_type=F32)
        h3 = jnp.dot(xb, w3_ref[...], preferred_element_type=F32)
        a = (h1 * _sigmoid(h1) * h3).astype(BF16)
        y = jnp.dot(a, w2_ref[...], preferred_element_type=F32)
        out_ref[sl, :] = hp + gate_ref[...] * y


def _dense_ffn(h, g, sh, sc, gate, w1, w3, w2, *, li, tm, seq):
    t, d = h.shape
    f = w1.shape[2]
    wfix = lambda i: (li, 0, 0)
    tpb = seq // tm
    row = lambda i: (i, 0)
    fix = lambda i: (0, 0)
    mod = lambda i: (i // tpb, 0, 0)
    once = pl.Buffered(1)
    return pl.pallas_call(
        functools.partial(_dense_ffn_kernel, parts=max(tm // PART_ROWS, 1)),
        grid=(t // tm,),
        in_specs=[pl.BlockSpec((tm, d), row),
                  pl.BlockSpec((1, d), fix),
                  pl.BlockSpec((None, 1, d), mod),
                  pl.BlockSpec((None, 1, d), mod),
                  pl.BlockSpec((None, 1, d), mod),
                  pl.BlockSpec((None, d, f), wfix, pipeline_mode=once),
                  pl.BlockSpec((None, d, f), wfix, pipeline_mode=once),
                  pl.BlockSpec((None, f, d), wfix, pipeline_mode=once)],
        out_specs=pl.BlockSpec((tm, d), row),
        out_shape=jax.ShapeDtypeStruct((t, d), F32),
        compiler_params=_params(1),
        name="ffn_dense",
    )(h, g, sh, sc, gate, w1, w3, w2)


def _pack_w13_kernel(w1_ref, w3_ref, o_ref):
    fe = w1_ref.shape[1]
    o_ref[:, 0:fe] = w1_ref[...].astype(BF16)
    o_ref[:, fe:2 * fe] = w3_ref[...].astype(BF16)


def _pack_w13(w1, w3):
    nl, ne, d, fe = w1.shape
    rows = d // 2
    spec = pl.BlockSpec((None, None, rows, fe), lambda l, e, r: (l, e, r, 0))
    return pl.pallas_call(
        _pack_w13_kernel,
        grid=(nl, ne, d // rows),
        in_specs=[spec, spec],
        out_specs=pl.BlockSpec((None, None, rows, 2 * fe), lambda l, e, r: (l, e, r, 0)),
        out_shape=jax.ShapeDtypeStruct((nl, ne, d, 2 * fe), BF16),
        compiler_params=_params(3),
        name="pack_w13",
    )(w1, w3)


def _top2(logits):
    lane = lax.broadcasted_iota(jnp.int32, logits.shape, 1)
    lg = jnp.where(lane < N_EXPERTS, logits, NEG_INF)
    m1 = jnp.max(lg, axis=-1, keepdims=True)
    i1 = jnp.min(jnp.where(lg == m1, lane, LANES), axis=-1, keepdims=True)
    lg2 = jnp.where(lane == i1, NEG_INF, lg)
    m2 = jnp.max(lg2, axis=-1, keepdims=True)
    i2 = jnp.min(jnp.where(lg2 == m2, lane, LANES), axis=-1, keepdims=True)
    e2 = jnp.exp(m2 - m1)
    return i1, i2, 1.0 / (1.0 + e2), e2 / (1.0 + e2)


def _router_kernel(h_ref, g_ref, sh_ref, sc_ref, r_ref, tri_ref, xn_ref, route_ref, rt_ref, cnt_ref,
                   base_ref):
    tm = h_ref.shape[0]

    @pl.when(pl.program_id(0) == 0)
    def _():
        base_ref[...] = jnp.zeros_like(base_ref)

    xn = _norm_mod(h_ref[...], g_ref[...], sh_ref[...], sc_ref[...])
    xn_ref[...] = xn.reshape(xn_ref.shape)
    i1, i2, g1, g2 = _top2(jnp.dot(xn, r_ref[...], preferred_element_type=F32))
    lane = lax.broadcasted_iota(jnp.int32, (tm, LANES), 1)
    hit1 = lane == i1
    hit2 = lane == i2
    chosen = jnp.where(jnp.logical_or(hit1, hit2), 1.0, 0.0)
    before = base_ref[...] + jnp.dot(tri_ref[...], chosen.astype(BF16), preferred_element_type=F32)
    r1 = jnp.sum(jnp.where(hit1, before, 0.0), axis=-1, keepdims=True)
    r2 = jnp.sum(jnp.where(hit2, before, 0.0), axis=-1, keepdims=True)
    base_ref[...] += jnp.sum(chosen, axis=0, keepdims=True)
    cnt_ref[...] = base_ref[...]
    fields = (i1.astype(F32), i2.astype(F32), g1, g2, r1, r2)
    route = jnp.zeros((tm, LANES), F32)
    for k, f in enumerate(fields):
        route = jnp.where(lane == k, f, route)
    route_ref[...] = route
    rt_ref[...] = route.T[0:8, :]


def _router(h, g, sh, sc, router, *, tm, seq):
    t, d = h.shape
    tpb = seq // tm
    row = lambda i: (i, 0)
    fix = lambda i: (0, 0)
    mod = lambda i: (i // tpb, 0, 0)
    ids = jnp.arange(tm)
    tri = (ids[None, :] < ids[:, None]).astype(BF16)
    return pl.pallas_call(
        _router_kernel,
        grid=(t // tm,),
        in_specs=[pl.BlockSpec((tm, d), row), pl.BlockSpec((1, d), fix),
                  pl.BlockSpec((None, 1, d), mod), pl.BlockSpec((None, 1, d), mod),
                  pl.BlockSpec((d, LANES), fix), pl.BlockSpec((tm, tm), fix)],
        out_specs=[pl.BlockSpec((tm, d // LANES, LANES), lambda i: (i, 0, 0)),
                   pl.BlockSpec((tm, LANES), row),
                   pl.BlockSpec((None, 8, tm), lambda i: (i, 0, 0)),
                   pl.BlockSpec((1, LANES), fix)],
        out_shape=[jax.ShapeDtypeStruct((t, d // LANES, LANES), F32),
                   jax.ShapeDtypeStruct((t, LANES), F32),
                   jax.ShapeDtypeStruct((t // tm, 8, tm), F32),
                   jax.ShapeDtypeStruct((1, LANES), F32)],
        scratch_shapes=[pltpu.VMEM((1, LANES), F32)],
        compiler_params=_params(1),
        name="moe_router",
    )(h, g, sh, sc, router, tri)


def _route_plan(t, counts, tr):
    counts = counts[0, 0:N_EXPERTS].astype(jnp.int32)
    tiles = (counts + tr - 1) // tr
    tile_end = jnp.cumsum(tiles)
    tile_start = tile_end - tiles
    experts = jnp.arange(N_EXPERTS, dtype=jnp.int32)

    nt = 2 * t // tr + N_EXPERTS
    tid = jnp.arange(nt, dtype=jnp.int32)
    tile_expert = jnp.minimum(jnp.sum((tid[:, None] >= tile_end[None, :]).astype(jnp.int32), axis=1),
                              N_EXPERTS - 1)
    in_tile = tid - jnp.sum(jnp.where(tile_expert[:, None] == experts[None, :], tile_start[None, :], 0), axis=1)
    own = jnp.sum(jnp.where(tile_expert[:, None] == experts[None, :], counts[None, :], 0), axis=1)
    n_valid = jnp.where(tid < tile_end[-1], jnp.clip(own - in_tile * tr, 0, tr), 0)
    tail = tile_end[-1] + experts
    pad_tiles = jnp.concatenate([jnp.where(tiles > 0, tile_end - 1, -1), jnp.where(tail < nt, tail, -1)])
    return tile_expert, n_valid, tile_start * tr, pad_tiles


def _table_kernel(start_ref, rt_ref, tab_ref, *, nt):
    tm = rt_ref.shape[1]

    def position(e, r):
        start = jnp.zeros_like(r)
        for k in range(N_EXPERTS):
            start = jnp.where(e == float(k), start_ref[k].astype(F32), start)
        return (start + r).astype(jnp.int32)

    live = pl.program_id(0) < nt
    rt = rt_ref[...]
    for c in range(2):
        pos = position(rt[c:c + 1, :], rt[4 + c:5 + c, :])
        tab_ref[:, c * tm:(c + 1) * tm] = jnp.where(live, pos, 0)


def _position_table(route_t, row_start):
    nt, fields, tm = route_t.shape
    grid_spec = pltpu.PrefetchScalarGridSpec(
        num_scalar_prefetch=1,
        grid=(nt + 2,),
        in_specs=[pl.BlockSpec((None, fields, tm), lambda j, st: (jnp.minimum(j, nt - 1), 0, 0))],
        out_specs=pl.BlockSpec((None, 1, 2 * tm), lambda j, st: (j, 0, 0)))
    return pl.pallas_call(
        functools.partial(_table_kernel, nt=nt),
        grid_spec=grid_spec,
        out_shape=jax.ShapeDtypeStruct((nt + 2, 1, 2 * tm), jnp.int32),
        compiler_params=_params(1),
        name="moe_table",
    )(row_start, route_t)


def _row_copies(idx_smem, s_idx, tm, make):
    base = s_idx * (2 * tm)

    def body(r, c):
        make(r, idx_smem[base + r], idx_smem[base + tm + r])
        return c

    lax.fori_loop(0, tm, body, 0, unroll=8)


def _dispatch_kernel(zt_ref, idx_hbm, xn_hbm, xg_hbm, idx_smem, zbuf, xbuf,
                     sem_d, sem_i, sem_z, sem_in, *, tm, nt, tr):
    j = pl.program_id(0)
    slot = j % 2
    other = 1 - slot
    cur = j % 3
    nxt = (j + 1) % 3

    def in_copy(tile, s):
        return pltpu.make_async_copy(xn_hbm.at[pl.ds(tile * tm, tm)], xbuf.at[s], sem_in.at[s])

    @pl.when(j == 0)
    def _():
        zbuf[...] = jnp.zeros_like(zbuf)
        for k in range(zt_ref.shape[0]):
            fill = pltpu.make_async_copy(zbuf, xg_hbm.at[pl.ds(jnp.maximum(zt_ref[k], 0) * tr, tr)], sem_z)
            pl.when(zt_ref[k] >= 0)(fill.start)
        for k in range(zt_ref.shape[0]):
            fill = pltpu.make_async_copy(zbuf, xg_hbm.at[pl.ds(0, tr)], sem_z)
            pl.when(zt_ref[k] >= 0)(fill.wait)

    def idx_copy(row, s):
        return pltpu.make_async_copy(idx_hbm.at[row, 0], idx_smem.at[pl.ds(s * 2 * tm, 2 * tm)], sem_i.at[s])

    def wait_rows(s):
        for _ in range(2):
            pltpu.make_async_copy(xbuf.at[s], xg_hbm.at[pl.ds(0, tm)], sem_d.at[s]).wait()

    @pl.when(j == 0)
    def _():
        idx_copy(0, 0).start()
        in_copy(0, 0).start()

    @pl.when(j >= 2)
    def _():
        wait_rows(nxt)

    @pl.when(j + 1 < nt)
    def _():
        in_copy(j + 1, nxt).start()

    idx_copy(j, slot).wait()
    idx_copy(j + 1, other).start()
    in_copy(j, cur).wait()

    def make(r, p1, p2):
        src = xbuf.at[cur, r]
        pltpu.make_async_copy(src, xg_hbm.at[p1], sem_d.at[cur]).start(priority=0)
        pltpu.make_async_copy(src, xg_hbm.at[p2], sem_d.at[cur]).start(priority=1)

    _row_copies(idx_smem, slot, tm, make)

    @pl.when(j == nt - 1)
    def _():
        if nt > 1:
            wait_rows((nt - 2) % 3)
        wait_rows((nt - 1) % 3)
        idx_copy(j + 1, other).wait()


def _dispatch(xn3, table, pad_tiles, *, tm, tr, n_rows):
    t = xn3.shape[0]
    nt = t // tm
    any_spec = pl.BlockSpec(memory_space=pl.ANY)
    grid_spec = pltpu.PrefetchScalarGridSpec(
        num_scalar_prefetch=1,
        grid=(nt,),
        in_specs=[any_spec, any_spec],
        out_specs=any_spec,
        scratch_shapes=[pltpu.SMEM((4 * tm,), jnp.int32),
                        pltpu.VMEM((tr,) + xn3.shape[1:], F32),
                        pltpu.VMEM((3, tm) + xn3.shape[1:], F32),
                        pltpu.SemaphoreType.DMA((3,)), pltpu.SemaphoreType.DMA((2,)),
                        pltpu.SemaphoreType.DMA, pltpu.SemaphoreType.DMA((3,))])
    return pl.pallas_call(
        functools.partial(_dispatch_kernel, tm=tm, nt=nt, tr=tr),
        grid_spec=grid_spec,
        out_shape=jax.ShapeDtypeStruct((n_rows,) + xn3.shape[1:], F32),
        compiler_params=pltpu.CompilerParams(dimension_semantics=("arbitrary",),
                                             vmem_limit_bytes=VMEM_LIMIT,
                                             disable_bounds_checks=True),
        name="moe_dispatch",
    )(pad_tiles, table, xn3)


def _expert_kernel(te_ref, nv_ref, x_ref, w13_ref, w2_ref, y_ref):
    tr = x_ref.shape[0]
    nv = nv_ref[pl.program_id(0)]

    @pl.when(nv > 0)
    def _():
        half = tr // 2
        for p in range(2):
            x = x_ref[p * half:(p + 1) * half].reshape(half, D_MODEL).astype(BF16)
            h13 = jnp.dot(x, w13_ref[...], preferred_element_type=F32)
            fe = h13.shape[1] // 2
            h1, h3 = h13[:, 0:fe], h13[:, fe:2 * fe]
            a = (h1 * _sigmoid(h1) * h3).astype(BF16)
            y = jnp.dot(a, w2_ref[...], preferred_element_type=F32)
            y_ref[p * half:(p + 1) * half] = y.reshape((half,) + y_ref.shape[1:])

    @pl.when(nv == 0)
    def _():
        y_ref[...] = jnp.zeros_like(y_ref)


def _experts(xg3, tile_expert, n_valid, w13, w2, *, li, tr):
    n_rows, sl, ln = xg3.shape
    d = sl * ln
    fe = w2.shape[2]
    rows = lambda j, te, nv: (j, 0, 0)
    wsel = lambda j, te, nv: (li, te[j], 0, 0)
    grid_spec = pltpu.PrefetchScalarGridSpec(
        num_scalar_prefetch=2,
        grid=(n_rows // tr,),
        in_specs=[pl.BlockSpec((tr, sl, ln), rows),
                  pl.BlockSpec((None, None, d, 2 * fe), wsel),
                  pl.BlockSpec((None, None, fe, d), wsel)],
        out_specs=pl.BlockSpec((tr, sl, ln), rows))
    return pl.pallas_call(
        _expert_kernel,
        grid_spec=grid_spec,
        out_shape=jax.ShapeDtypeStruct(xg3.shape, F32),
        compiler_params=_params(1),
        name="moe_experts",
    )(tile_expert, n_valid, xg3, w13, w2)


def _combine_kernel(idx_hbm, h_ref, gate_ref, route_ref, yg_hbm, out_ref,
                    y1buf, y2buf, idx_smem, sem_y, sem_i, *, tm, nt):
    j = pl.program_id(0)
    slot = j % 2
    other = 1 - slot

    def idx_copy(row, s):
        return pltpu.make_async_copy(idx_hbm.at[row, 0], idx_smem.at[pl.ds(s * 2 * tm, 2 * tm)], sem_i.at[s])

    def fetch(s_idx, s_buf):
        def make(r, p1, p2):
            pltpu.make_async_copy(yg_hbm.at[p1], y1buf.at[s_buf, r], sem_y.at[s_buf]).start(priority=0)
            pltpu.make_async_copy(yg_hbm.at[p2], y2buf.at[s_buf, r], sem_y.at[s_buf]).start(priority=1)

        _row_copies(idx_smem, s_idx, tm, make)

    def wait_rows(s):
        pltpu.make_async_copy(yg_hbm.at[pl.ds(0, tm)], y1buf.at[s], sem_y.at[s]).wait()
        pltpu.make_async_copy(yg_hbm.at[pl.ds(0, tm)], y2buf.at[s], sem_y.at[s]).wait()

    @pl.when(j == 0)
    def _():
        first = idx_copy(0, 0)
        first.start()
        first.wait()
        fetch(0, 0)
        idx_copy(1, 1).start()

    idx_copy(j + 1, other).wait()
    fetch(other, other)
    idx_copy(j + 2, slot).start()
    wait_rows(slot)
    rt = route_ref[...]
    y1 = y1buf[slot].reshape(tm, D_MODEL)
    y2 = y2buf[slot].reshape(tm, D_MODEL)
    out_ref[...] = h_ref[...] + gate_ref[...] * (rt[:, 2:3] * y1 + rt[:, 3:4] * y2)

    @pl.when(j == nt - 1)
    def _():
        wait_rows(other)
        idx_copy(j + 2, slot).wait()


def _combine(h, gate, route, table, yg3, *, tm, seq):
    t, d = h.shape
    tpb = seq // tm
    nt = t // tm
    sl, ln = yg3.shape[1:]
    row = lambda i: (i, 0)
    any_spec = pl.BlockSpec(memory_space=pl.ANY)
    return pl.pallas_call(
        functools.partial(_combine_kernel, tm=tm, nt=nt),
        grid=(nt,),
        in_specs=[any_spec,
                  pl.BlockSpec((tm, d), row),
                  pl.BlockSpec((None, 1, d), lambda i: (i // tpb, 0, 0)),
                  pl.BlockSpec((tm, LANES), row),
                  any_spec],
        out_specs=pl.BlockSpec((tm, d), row),
        out_shape=jax.ShapeDtypeStruct((t, d), F32),
        scratch_shapes=[pltpu.VMEM((2, tm, sl, ln), F32), pltpu.VMEM((2, tm, sl, ln), F32),
                        pltpu.SMEM((4 * tm,), jnp.int32),
                        pltpu.SemaphoreType.DMA((2,)), pltpu.SemaphoreType.DMA((2,))],
        compiler_params=pltpu.CompilerParams(dimension_semantics=("arbitrary",),
                                             vmem_limit_bytes=VMEM_LIMIT,
                                             disable_bounds_checks=True),
        name="moe_combine",
    )(table, h, gate, route, yg3)


def _moe(h, g, sh, sc, gate, w13, w2, router, *, li, seq, tr, tm):
    t = h.shape[0]
    xn3, route, route_t, counts = _router(h, g, sh, sc, router, tm=tm, seq=seq)
    tile_expert, n_valid, row_start, pad_tiles = _route_plan(t, counts, tr)
    table = _position_table(route_t, row_start)
    xg3 = _dispatch(xn3, table, pad_tiles, tm=tm, tr=tr, n_rows=2 * t + N_EXPERTS * tr)
    yg3 = _experts(xg3, tile_expert, n_valid, w13, w2, li=li, tr=tr)
    return _combine(h, gate, route, table, yg3, tm=tm, seq=seq)


def _rope_tables(seq):
    rows = seq // GRID_W
    row, col = jnp.meshgrid(jnp.arange(rows, dtype=F32), jnp.arange(GRID_W, dtype=F32), indexing='ij')
    n_freq = HEAD_DIM // 4
    inv_freq = ROPE_THETA ** (-jnp.arange(n_freq, dtype=F32) / n_freq)
    ang_r = row.reshape(-1, 1) * inv_freq
    ang_c = col.reshape(-1, 1) * inv_freq
    ang = jnp.concatenate([ang_r, ang_r, ang_c, ang_c], axis=-1)
    cos, sin = jnp.cos(ang), jnp.sin(ang)
    first = (jnp.arange(HEAD_DIM) % (2 * n_freq)) < n_freq
    sin_a = jnp.where(first, -sin, 0.0)
    sin_b = jnp.where(first, 0.0, sin)
    rep = LANES // HEAD_DIM
    return tuple(jnp.tile(t, (1, rep)) for t in (cos, sin_a, sin_b))


def kernel(x, c, ctx, c_ctx, w_ada, b_ada, norm1_g, norm2_g, w_in, conv_w, q_norm_g, k_norm_g,
           attn_sink, out_norm_conv_g, out_norm_attn_g, w_out, ffn_w1, ffn_w3, ffn_w2,
           moe_router, moe_w1, moe_w3, moe_w2):
    b, s, d = x.shape
    lc = ctx.shape[1]
    depth = w_ada.shape[0]
    assert d == D_MODEL and s % BIG_TILE == 0 and lc % PART_ROWS == 0 and b + 1 <= 8

    c8 = jnp.zeros((8, d), F32).at[0:b].set(c).at[b].set(c_ctx)
    mod = _modulation(c8, w_ada, b_ada)

    tables = _rope_tables(s)
    ids = jnp.arange(256)
    gm = (ids[:, None] // HEAD_DIM == ids[None, :] // HEAD_DIM).astype(BF16)
    scale = HEAD_DIM ** -0.5 * LOG2E

    h = x.reshape(b * s, d)
    hc = ctx.reshape(b * lc, d)
    w_in_b, w_out_b = w_in.astype(BF16), w_out.astype(BF16)
    dense_w = [w.astype(BF16) for w in (ffn_w1, ffn_w3, ffn_w2)]
    moe_w13 = _pack_w13(moe_w1, moe_w3)
    moe_w2b = moe_w2.astype(BF16)
    for layer in range(depth):
        last = layer == depth - 1
        m = mod[layer]
        lat = [m[0:b, k * d:(k + 1) * d].reshape(b, 1, d) for k in range(6)]
        cx = [jnp.broadcast_to(m[b:b + 1, k * d:(k + 1) * d].reshape(1, 1, d), (b, 1, d)) for k in range(6)]
        g1n = norm1_g[layer].reshape(1, d)
        g2n = norm2_g[layer].reshape(1, d)
        gq = (jnp.tile(q_norm_g[layer], N_HEADS) * scale).reshape(1, D_ATTN)
        gk = jnp.tile(k_norm_g[layer], N_KV).reshape(1, LANES)
        gc = out_norm_conv_g[layer].reshape(1, D_CONV)
        ga = out_norm_attn_g[layer].reshape(1, D_ATTN)
        sink = attn_sink[layer]
        cw = conv_w[layer]

        cu, q, kv = _inproj(h, g1n, lat[0], lat[1], w_in_b, gq, gk, gm, tables, li=layer, tm=BIG_TILE, seq=s)
        cuc, qc, kvc = _inproj(hc, g1n, cx[0], cx[1], w_in_b, gq, gk, gm, None, li=layer, tm=lc, seq=lc)
        h = _mixer(h, cu, q, kv, kvc, sink, cw, gc, ga, w_out_b, lat[2],
                   li=layer, tq=BIG_TILE, seq=s, ctx_len=lc, windowed=True)
        if not last:
            hc = _mixer(hc, cuc, qc, None, kvc, sink, cw, gc, ga, w_out_b, cx[2],
                        li=layer, tq=lc, seq=lc, ctx_len=lc, windowed=False)

        i = layer // 2
        if layer % 2 == 0:
            h = _dense_ffn(h, g2n, lat[3], lat[4], lat[5], *dense_w, li=i, tm=BIG_TILE, seq=s)
            if not last:
                hc = _dense_ffn(hc, g2n, cx[3], cx[4], cx[5], *dense_w, li=i, tm=lc, seq=lc)
        else:
            router = jnp.zeros((d, LANES), F32).at[:, 0:N_EXPERTS].set(moe_router[i])
            h = _moe(h, g2n, lat[3], lat[4], lat[5], moe_w13, moe_w2b, router,
                     li=i, seq=s, tr=MOE_TILE, tm=MOE_TILE)
            if not last:
                hc = _moe(hc, g2n, cx[3], cx[4], cx[5], moe_w13, moe_w2b, router,
                          li=i, seq=lc, tr=lc, tm=lc)
    return h.reshape(b, s, d)
```

```python
import functools

import jax
import jax.numpy as jnp
from jax import lax
from jax.experimental import pallas as pl
from jax.experimental.pallas import tpu as pltpu

D_MODEL = 1024
GRID_W = 64
HEAD_DIM = 64
D_CONV = 512
D_ATTN = 512
N_HEADS = 8
N_KV = 2
WINDOW = 128
ROPE_THETA = 10000.0
N_EXPERTS = 8
EPS = 1e-6
KV_OFF = 3 * D_CONV + D_ATTN
D_IN = KV_OFF + 2 * N_KV * HEAD_DIM
LANES = 128
VMEM_LIMIT = 48 * 1024 * 1024
BIG_TILE = 1024
MOE_TILE = 512
PART_ROWS = 256

F32 = jnp.float32
BF16 = jnp.bfloat16
NEG_INF = float("-inf")
LOG2E = 1.4426950408889634


def _params(n_axes):
    return pltpu.CompilerParams(dimension_semantics=("arbitrary",) * n_axes,
                                vmem_limit_bytes=VMEM_LIMIT)


def _sigmoid(x):
    return 1.0 / (1.0 + jnp.exp(-x))


def _mod_kernel(c_ref, w_ref, b_ref, o_ref):
    c = c_ref[...]
    s = c * _sigmoid(c)
    o_ref[...] = jnp.dot(s, w_ref[...], preferred_element_type=F32) + b_ref[...]


def _modulation(c8, w_ada, b_ada):
    depth, d, n = w_ada.shape
    tn = 1536
    return pl.pallas_call(
        _mod_kernel,
        grid=(depth, n // tn),
        in_specs=[pl.BlockSpec((8, d), lambda l, j: (0, 0)),
                  pl.BlockSpec((None, d, tn), lambda l, j: (l, 0, j)),
                  pl.BlockSpec((None, 1, tn), lambda l, j: (l, 0, j))],
        out_specs=pl.BlockSpec((None, 8, tn), lambda l, j: (l, 0, j)),
        out_shape=jax.ShapeDtypeStruct((depth, 8, n), F32),
        compiler_params=_params(2),
        name="adaln_mod",
    )(c8, w_ada, b_ada.reshape(depth, 1, n))


def _norm_mod(x, g, sh, sc):
    ms = jnp.mean(x * x, axis=-1, keepdims=True)
    return (x * lax.rsqrt(ms + EPS) * g) * (1.0 + sc) + sh


def _inproj_kernel(*refs, rope, parts):
    if rope:
        (h_ref, g_ref, sh_ref, sc_ref, w_ref, gq_ref, gk_ref, gm_ref,
         cos_ref, sa_ref, sb_ref, cu_ref, q_ref, kv_ref) = refs
    else:
        (h_ref, g_ref, sh_ref, sc_ref, w_ref, gq_ref, gk_ref, gm_ref,
         cu_ref, q_ref, kv_ref) = refs
    gm = gm_ref[...]
    gq = gq_ref[...]
    rows = h_ref.shape[0] // parts

    def head_norm(t, gain):
        w = t.shape[1]
        ss = jnp.dot((t * t).astype(BF16), gm[0:w, 0:w], preferred_element_type=F32)
        return t * lax.rsqrt(ss * (1.0 / HEAD_DIM) + EPS) * gain

    for p in range(parts):
        sl = slice(p * rows, (p + 1) * rows)

        def rot(t):
            if not rope:
                return t
            return (t * cos_ref[sl, :] + pltpu.roll(t, LANES - 16, 1) * sa_ref[sl, :]
                    + pltpu.roll(t, 16, 1) * sb_ref[sl, :])

        xb = _norm_mod(h_ref[sl, :], g_ref[...], sh_ref[...], sc_ref[...]).astype(BF16)
        ya = jnp.dot(xb, w_ref[:, 3 * D_CONV:D_IN], preferred_element_type=F32)
        yc = jnp.dot(xb, w_ref[:, 0:3 * D_CONV], preferred_element_type=F32)
        cu_ref[sl, 0:D_CONV] = yc[:, 0:D_CONV].astype(BF16)
        cu_ref[sl, D_CONV:2 * D_CONV] = (yc[:, D_CONV:2 * D_CONV] * yc[:, 2 * D_CONV:3 * D_CONV]).astype(BF16)
        for j in range(D_ATTN // 256):
            qn = head_norm(ya[:, 256 * j:256 * j + 256], gq[:, 256 * j:256 * j + 256])
            for c in range(2):
                q_ref[sl, 256 * j + LANES * c:256 * j + LANES * (c + 1)] = rot(
                    qn[:, LANES * c:LANES * (c + 1)]).astype(BF16)
        k = rot(head_norm(ya[:, D_ATTN:D_ATTN + LANES], gk_ref[...]))
        v = ya[:, D_ATTN + LANES:D_ATTN + 2 * LANES]
        lo_half = lax.broadcasted_iota(jnp.int32, k.shape, 1) < HEAD_DIM
        for c, t in enumerate((k, v)):
            sw = pltpu.roll(t, HEAD_DIM, 1)
            kv_ref[sl, 2 * c * LANES:(2 * c + 1) * LANES] = jnp.where(lo_half, t, sw).astype(BF16)
            kv_ref[sl, (2 * c + 1) * LANES:(2 * c + 2) * LANES] = jnp.where(lo_half, sw, t).astype(BF16)


def _inproj(h, g, sh, sc, w, gq, gk, gm, tables, *, li, tm, seq):
    t, d = h.shape
    tpb = seq // tm
    rope = tables is not None
    row = lambda i: (i, 0)
    fix = lambda i: (0, 0)
    mod = lambda i: (i // tpb, 0, 0)
    in_specs = [pl.BlockSpec((tm, d), row),
                pl.BlockSpec((1, d), fix),
                pl.BlockSpec((None, 1, d), mod),
                pl.BlockSpec((None, 1, d), mod),
                pl.BlockSpec((None, d, D_IN), lambda i: (li, 0, 0)),
                pl.BlockSpec((1, D_ATTN), fix),
                pl.BlockSpec((1, LANES), fix),
                pl.BlockSpec((256, 256), fix)]
    args = [h, g, sh, sc, w, gq, gk, gm]
    if rope:
        in_specs += [pl.BlockSpec((tm, LANES), lambda i: (i % tpb, 0))] * 3
        args += list(tables)
    return pl.pallas_call(
        functools.partial(_inproj_kernel, rope=rope, parts=max(tm // PART_ROWS, 1)),
        grid=(t // tm,),
        in_specs=in_specs,
        out_specs=[pl.BlockSpec((tm, 2 * D_CONV), row),
                   pl.BlockSpec((tm, D_ATTN), row),
                   pl.BlockSpec((tm, 4 * LANES), row)],
        out_shape=[jax.ShapeDtypeStruct((t, 2 * D_CONV), BF16),
                   jax.ShapeDtypeStruct((t, D_ATTN), BF16),
                   jax.ShapeDtypeStruct((t, 4 * LANES), BF16)],
        compiler_params=_params(1),
        name="inproj_rope" if rope else "inproj_ctx",
    )(*args)


_NT = (((1,), (1,)), ((), ()))


def _mixer_kernel(*refs, tq, windowed):
    if windowed:
        (sink_ref, h_ref, cu_ref, cup_ref, cun_ref, q_ref, kvp_ref, kv_ref, kvn_ref, kvc_ref,
         cw_ref, gc_ref, ga_ref, wo_ref, g1_ref, out_ref, kw_ref, ya_ref) = refs
    else:
        (sink_ref, h_ref, cu_ref, q_ref, kvc_ref,
         cw_ref, gc_ref, ga_ref, wo_ref, g1_ref, out_ref, ya_ref) = refs
    i = pl.program_id(1)
    nt = pl.num_programs(1)
    nsub = tq // WINDOW

    part = 2 * WINDOW
    n_parts = tq // part
    cw = cw_ref[...]
    part_rows = lax.broadcasted_iota(jnp.int32, (part, 1), 0)

    def gated_conv(p):
        lo = p * part
        u = cu_ref[lo:lo + part, D_CONV:2 * D_CONV].astype(F32)
        bg = cu_ref[lo:lo + part, 0:D_CONV].astype(F32)
        if p > 0:
            up_row = cu_ref[lo - 16:lo, D_CONV:2 * D_CONV].astype(F32)[15:16, :]
        elif windowed:
            up_row = jnp.where(i > 0, cup_ref[:, D_CONV:2 * D_CONV].astype(F32)[15:16, :], 0.0)
        else:
            up_row = jnp.zeros((1, D_CONV), F32)
        if p < n_parts - 1:
            un_row = cu_ref[lo + part:lo + part + 16, D_CONV:2 * D_CONV].astype(F32)[0:1, :]
        elif windowed:
            un_row = jnp.where(i < nt - 1, cun_ref[:, D_CONV:2 * D_CONV].astype(F32)[0:1, :], 0.0)
        else:
            un_row = jnp.zeros((1, D_CONV), F32)
        u_prev = jnp.where(part_rows == 0, up_row, pltpu.roll(u, 1, 0))
        u_next = jnp.where(part_rows == part - 1, un_row, pltpu.roll(u, part - 1, 0))
        yc = bg * (cw[0:1, :] * u_prev + cw[1:2, :] * u + cw[2:3, :] * u_next)
        return yc * lax.rsqrt(jnp.mean(yc * yc, axis=-1, keepdims=True) + EPS) * gc_ref[...]

    if windowed:
        kw_ref[0:WINDOW, :] = kvp_ref[...]
        kw_ref[WINDOW:WINDOW + tq, :] = kv_ref[...]
        kw_ref[WINDOW + tq:2 * WINDOW + tq, :] = kvn_ref[...]
    lane_lo = lax.broadcasted_iota(jnp.int32, (WINDOW, LANES), 1) < HEAD_DIM
    kvc = kvc_ref[...]
    gqa = N_HEADS // N_KV
    head_of_row = lax.broadcasted_iota(jnp.int32, (N_HEADS * WINDOW, 1), 0) // WINDOW
    ones_c = jnp.ones((kvc.shape[0], LANES), BF16)
    if windowed:
        kk = lax.broadcasted_iota(jnp.int32, (WINDOW, WINDOW), 1)
        qq = lax.broadcasted_iota(jnp.int32, (WINDOW, WINDOW), 0)
        keep_before = kk >= qq
        keep_after = kk <= qq
        ones_w = jnp.ones((3 * WINDOW, LANES), BF16)

    def sub_block(s):
        r0 = s * WINDOW
        if windowed:
            kwin = kw_ref[pl.ds(r0, 3 * WINDOW), :]
            before, after = keep_before, keep_after
            if s == 0:
                before = before & (i > 0)
            if s == nsub - 1:
                after = after & (i < nt - 1)
        rows, sk = [], jnp.zeros((N_HEADS * WINDOW, 1), F32)
        for head in range(N_HEADS):
            qp = q_ref[pl.ds(r0, WINDOW), LANES * (head // 2):LANES * (head // 2 + 1)]
            zero = jnp.zeros_like(qp)
            qh = jnp.where(lane_lo, zero, qp) if head % 2 else jnp.where(lane_lo, qp, zero)
            rows.append(jnp.concatenate([qh, zero] if head < gqa else [zero, qh], axis=1))
            sk = jnp.where(head_of_row == head, sink_ref[head] * LOG2E, sk)
        q8 = jnp.concatenate(rows, axis=0)
        s_c = lax.dot_general(q8, kvc[:, 0:2 * LANES], _NT, preferred_element_type=F32)
        m = jnp.maximum(jnp.max(s_c, axis=-1, keepdims=True), sk)
        if windowed:
            s_w = lax.dot_general(q8, kwin[:, 0:2 * LANES], _NT,
                                  preferred_element_type=F32).reshape(N_HEADS, WINDOW, 3 * WINDOW)
            s_w = jnp.concatenate(
                [jnp.where(before[None], s_w[:, :, 0:WINDOW], NEG_INF),
                 s_w[:, :, WINDOW:2 * WINDOW],
                 jnp.where(after[None], s_w[:, :, 2 * WINDOW:3 * WINDOW], NEG_INF)],
                axis=-1).reshape(N_HEADS * WINDOW, 3 * WINDOW)
            m = jnp.maximum(m, jnp.max(s_w, axis=-1, keepdims=True))
            p_w = jnp.exp2(s_w - m).astype(BF16)
        p_c = jnp.exp2(s_c - m).astype(BF16)
        tail = jnp.exp2(sk - m)
        for grp in range(N_KV):
            gr = slice(grp * gqa * WINDOW, (grp + 1) * gqa * WINDOW)
            v_c = jnp.concatenate([kvc[:, LANES * (2 + grp):LANES * (3 + grp)], ones_c], axis=1)
            o = jnp.dot(p_c[gr], v_c, preferred_element_type=F32)
            if windowed:
                v_w = jnp.concatenate([kwin[:, LANES * (2 + grp):LANES * (3 + grp)], ones_w], axis=1)
                o = o + jnp.dot(p_w[gr], v_w, preferred_element_type=F32)
            o = o[:, 0:LANES] / (o[:, LANES:2 * LANES] + tail[gr])
            for pr in range(gqa // 2):
                pair = (gqa // 2) * grp + pr
                even = o[2 * pr * WINDOW:(2 * pr + 1) * WINDOW]
                odd = o[(2 * pr + 1) * WINDOW:(2 * pr + 2) * WINDOW]
                ya_ref[pl.ds(r0, WINDOW), LANES * pair:LANES * (pair + 1)] = jnp.where(lane_lo, even, odd)

    for p in range(n_parts):
        sub_block(2 * p)
        sub_block(2 * p + 1)
        sl = slice(p * part, (p + 1) * part)
        ya = ya_ref[sl, :]
        ya = ya * lax.rsqrt(jnp.mean(ya * ya, axis=-1, keepdims=True) + EPS) * ga_ref[...]
        y = (jnp.dot(gated_conv(p).astype(BF16), wo_ref[0:D_CONV, :], preferred_element_type=F32)
             + jnp.dot(ya.astype(BF16), wo_ref[D_CONV:2 * D_CONV, :], preferred_element_type=F32))
        out_ref[sl, :] = h_ref[sl, :] + g1_ref[...] * y


def _mixer(h, cu, q, kv, kvc, sink, cw, gc, ga, wo, g1, *, li, tq, seq, ctx_len, windowed):
    t, d = h.shape
    nt = seq // tq
    nb = t // seq
    row = lambda b, i: (b * nt + i, 0)
    fix = lambda b, i: (0, 0)
    smem = pl.BlockSpec(memory_space=pltpu.SMEM)
    tail = [pl.BlockSpec((3, D_CONV), fix),
            pl.BlockSpec((1, D_CONV), fix),
            pl.BlockSpec((1, D_ATTN), fix),
            pl.BlockSpec((None, d, d), lambda b, i: (li, 0, 0)),
            pl.BlockSpec((None, 1, d), lambda b, i: (b, 0, 0))]
    ctx_spec = pl.BlockSpec((ctx_len, 4 * LANES), lambda b, i: (b, 0))
    if windowed:
        r16 = tq // 16
        n16 = t // 16
        rw = tq // WINDOW
        nw = t // WINDOW
        in_specs = [smem,
                    pl.BlockSpec((tq, d), row),
                    pl.BlockSpec((tq, 2 * D_CONV), row),
                    pl.BlockSpec((16, 2 * D_CONV), lambda b, i: (jnp.maximum((b * nt + i) * r16 - 1, 0), 0)),
                    pl.BlockSpec((16, 2 * D_CONV), lambda b, i: (jnp.minimum((b * nt + i + 1) * r16, n16 - 1), 0)),
                    pl.BlockSpec((tq, D_ATTN), row),
                    pl.BlockSpec((WINDOW, 4 * LANES), lambda b, i: (jnp.maximum((b * nt + i) * rw - 1, 0), 0)),
                    pl.BlockSpec((tq, 4 * LANES), row),
                    pl.BlockSpec((WINDOW, 4 * LANES), lambda b, i: (jnp.minimum((b * nt + i + 1) * rw, nw - 1), 0)),
                    ctx_spec] + tail
        args = [sink, h, cu, cu, cu, q, kv, kv, kv, kvc, cw, gc, ga, wo, g1]
        scratch = [pltpu.VMEM((tq + 2 * WINDOW, 4 * LANES), BF16), pltpu.VMEM((tq, D_ATTN), F32)]
    else:
        in_specs = [smem,
                    pl.BlockSpec((tq, d), row),
                    pl.BlockSpec((tq, 2 * D_CONV), row),
                    pl.BlockSpec((tq, D_ATTN), row),
                    ctx_spec] + tail
        args = [sink, h, cu, q, kvc, cw, gc, ga, wo, g1]
        scratch = [pltpu.VMEM((tq, D_ATTN), F32)]
    return pl.pallas_call(
        functools.partial(_mixer_kernel, tq=tq, windowed=windowed),
        grid=(nb, nt),
        in_specs=in_specs,
        out_specs=pl.BlockSpec((tq, d), row),
        out_shape=jax.ShapeDtypeStruct((t, d), F32),
        scratch_shapes=scratch,
        compiler_params=_params(2),
        name="mixer_win" if windowed else "mixer_ctx",
    )(*args)


def _dense_ffn_kernel(h_ref, g_ref, sh_ref, sc_ref, gate_ref, w1_ref, w3_ref, w2_ref, out_ref, *, parts):
    rows = h_ref.shape[0] // parts
    for p in range(parts):
        sl = slice(p * rows, (p + 1) * rows)
        hp = h_ref[sl, :]
        xb = _norm_mod(hp, g_ref[...], sh_ref[...], sc_ref[...]).astype(BF16)
        h1 = jnp.dot(xb, w1_ref[...], preferred_element_type=F32)
        h3 = jnp.dot(xb, w3_ref[...], preferred_element_type=F32)
        a = (h1 * _sigmoid(h1) * h3).astype(BF16)
        y = jnp.dot(a, w2_ref[...], preferred_element_type=F32)
        out_ref[sl, :] = hp + gate_ref[...] * y


def _dense_ffn(h, g, sh, sc, gate, w1, w3, w2, *, li, tm, seq):
    t, d = h.shape
    f = w1.shape[2]
    wfix = lambda i: (li, 0, 0)
    tpb = seq // tm
    row = lambda i: (i, 0)
    fix = lambda i: (0, 0)
    mod = lambda i: (i // tpb, 0, 0)
    once = pl.Buffered(1)
    return pl.pallas_call(
        functools.partial(_dense_ffn_kernel, parts=max(tm // PART_ROWS, 1)),
        grid=(t // tm,),
        in_specs=[pl.BlockSpec((tm, d), row),
                  pl.BlockSpec((1, d), fix),
                  pl.BlockSpec((None, 1, d), mod),
                  pl.BlockSpec((None, 1, d), mod),
                  pl.BlockSpec((None, 1, d), mod),
                  pl.BlockSpec((None, d, f), wfix, pipeline_mode=once),
                  pl.BlockSpec((None, d, f), wfix, pipeline_mode=once),
                  pl.BlockSpec((None, f, d), wfix, pipeline_mode=once)],
        out_specs=pl.BlockSpec((tm, d), row),
        out_shape=jax.ShapeDtypeStruct((t, d), F32),
        compiler_params=_params(1),
        name="ffn_dense",
    )(h, g, sh, sc, gate, w1, w3, w2)


def _pack_w13_kernel(w1_ref, w3_ref, o_ref):
    fe = w1_ref.shape[1]
    o_ref[:, 0:fe] = w1_ref[...].astype(BF16)
    o_ref[:, fe:2 * fe] = w3_ref[...].astype(BF16)


def _pack_w13(w1, w3):
    nl, ne, d, fe = w1.shape
    rows = d // 2
    spec = pl.BlockSpec((None, None, rows, fe), lambda l, e, r: (l, e, r, 0))
    return pl.pallas_call(
        _pack_w13_kernel,
        grid=(nl, ne, d // rows),
        in_specs=[spec, spec],
        out_specs=pl.BlockSpec((None, None, rows, 2 * fe), lambda l, e, r: (l, e, r, 0)),
        out_shape=jax.ShapeDtypeStruct((nl, ne, d, 2 * fe), BF16),
        compiler_params=_params(3),
        name="pack_w13",
    )(w1, w3)


def _top2(logits):
    lane = lax.broadcasted_iota(jnp.int32, logits.shape, 1)
    lg = jnp.where(lane < N_EXPERTS, logits, NEG_INF)
    m1 = jnp.max(lg, axis=-1, keepdims=True)
    i1 = jnp.min(jnp.where(lg == m1, lane, LANES), axis=-1, keepdims=True)
    lg2 = jnp.where(lane == i1, NEG_INF, lg)
    m2 = jnp.max(lg2, axis=-1, keepdims=True)
    i2 = jnp.min(jnp.where(lg2 == m2, lane, LANES), axis=-1, keepdims=True)
    e2 = jnp.exp(m2 - m1)
    return i1, i2, 1.0 / (1.0 + e2), e2 / (1.0 + e2)


def _router_kernel(h_ref, g_ref, sh_ref, sc_ref, r_ref, tri_ref, xn_ref, route_ref, rt_ref, cnt_ref,
                   base_ref):
    tm = h_ref.shape[0]

    @pl.when(pl.program_id(0) == 0)
    def _():
        base_ref[...] = jnp.zeros_like(base_ref)

    xn = _norm_mod(h_ref[...], g_ref[...], sh_ref[...], sc_ref[...])
    xn_ref[...] = xn.reshape(xn_ref.shape)
    i1, i2, g1, g2 = _top2(jnp.dot(xn, r_ref[...], preferred_element_type=F32))
    lane = lax.broadcasted_iota(jnp.int32, (tm, LANES), 1)
    hit1 = lane == i1
    hit2 = lane == i2
    chosen = jnp.where(jnp.logical_or(hit1, hit2), 1.0, 0.0)
    before = base_ref[...] + jnp.dot(tri_ref[...], chosen.astype(BF16), preferred_element_type=F32)
    r1 = jnp.sum(jnp.where(hit1, before, 0.0), axis=-1, keepdims=True)
    r2 = jnp.sum(jnp.where(hit2, before, 0.0), axis=-1, keepdims=True)
    base_ref[...] += jnp.sum(chosen, axis=0, keepdims=True)
    cnt_ref[...] = base_ref[...]
    fields = (i1.astype(F32), i2.astype(F32), g1, g2, r1, r2)
    route = jnp.zeros((tm, LANES), F32)
    for k, f in enumerate(fields):
        route = jnp.where(lane == k, f, route)
    route_ref[...] = route
    rt_ref[...] = route.T[0:8, :]


def _router(h, g, sh, sc, router, *, tm, seq):
    t, d = h.shape
    tpb = seq // tm
    row = lambda i: (i, 0)
    fix = lambda i: (0, 0)
    mod = lambda i: (i // tpb, 0, 0)
    ids = jnp.arange(tm)
    tri = (ids[None, :] < ids[:, None]).astype(BF16)
    return pl.pallas_call(
        _router_kernel,
        grid=(t // tm,),
        in_specs=[pl.BlockSpec((tm, d), row), pl.BlockSpec((1, d), fix),
                  pl.BlockSpec((None, 1, d), mod), pl.BlockSpec((None, 1, d), mod),
                  pl.BlockSpec((d, LANES), fix), pl.BlockSpec((tm, tm), fix)],
        out_specs=[pl.BlockSpec((tm, d // LANES, LANES), lambda i: (i, 0, 0)),
                   pl.BlockSpec((tm, LANES), row),
                   pl.BlockSpec((None, 8, tm), lambda i: (i, 0, 0)),
                   pl.BlockSpec((1, LANES), fix)],
        out_shape=[jax.ShapeDtypeStruct((t, d // LANES, LANES), F32),
                   jax.ShapeDtypeStruct((t, LANES), F32),
                   jax.ShapeDtypeStruct((t // tm, 8, tm), F32),
                   jax.ShapeDtypeStruct((1, LANES), F32)],
        scratch_shapes=[pltpu.VMEM((1, LANES), F32)],
        compiler_params=_params(1),
        name="moe_router",
    )(h, g, sh, sc, router, tri)


def _route_plan(t, counts, tr):
    counts = counts[0, 0:N_EXPERTS].astype(jnp.int32)
    tiles = (counts + tr - 1) // tr
    tile_end = jnp.cumsum(tiles)
    tile_start = tile_end - tiles
    experts = jnp.arange(N_EXPERTS, dtype=jnp.int32)

    nt = 2 * t // tr + N_EXPERTS
    tid = jnp.arange(nt, dtype=jnp.int32)
    tile_expert = jnp.minimum(jnp.sum((tid[:, None] >= tile_end[None, :]).astype(jnp.int32), axis=1),
                              N_EXPERTS - 1)
    in_tile = tid - jnp.sum(jnp.where(tile_expert[:, None] == experts[None, :], tile_start[None, :], 0), axis=1)
    own = jnp.sum(jnp.where(tile_expert[:, None] == experts[None, :], counts[None, :], 0), axis=1)
    n_valid = jnp.where(tid < tile_end[-1], jnp.clip(own - in_tile * tr, 0, tr), 0)
    tail = tile_end[-1] + experts
    pad_tiles = jnp.concatenate([jnp.where(tiles > 0, tile_end - 1, -1), jnp.where(tail < nt, tail, -1)])
    return tile_expert, n_valid, tile_start * tr, pad_tiles


def _table_kernel(start_ref, rt_ref, tab_ref, *, nt):
    tm = rt_ref.shape[1]

    def position(e, r):
        start = jnp.zeros_like(r)
        for k in range(N_EXPERTS):
            start = jnp.where(e == float(k), start_ref[k].astype(F32), start)
        return (start + r).astype(jnp.int32)

    live = pl.program_id(0) < nt
    rt = rt_ref[...]
    for c in range(2):
        pos = position(rt[c:c + 1, :], rt[4 + c:5 + c, :])
        tab_ref[:, c * tm:(c + 1) * tm] = jnp.where(live, pos, 0)


def _position_table(route_t, row_start):
    nt, fields, tm = route_t.shape
    grid_spec = pltpu.PrefetchScalarGridSpec(
        num_scalar_prefetch=1,
        grid=(nt + 2,),
        in_specs=[pl.BlockSpec((None, fields, tm), lambda j, st: (jnp.minimum(j, nt - 1), 0, 0))],
        out_specs=pl.BlockSpec((None, 1, 2 * tm), lambda j, st: (j, 0, 0)))
    return pl.pallas_call(
        functools.partial(_table_kernel, nt=nt),
        grid_spec=grid_spec,
        out_shape=jax.ShapeDtypeStruct((nt + 2, 1, 2 * tm), jnp.int32),
        compiler_params=_params(1),
        name="moe_table",
    )(row_start, route_t)


def _row_copies(idx_smem, s_idx, tm, make):
    base = s_idx * (2 * tm)

    def body(r, c):
        make(r, idx_smem[base + r], idx_smem[base + tm + r])
        return c

    lax.fori_loop(0, tm, body, 0, unroll=8)


def _dispatch_kernel(zt_ref, idx_hbm, xn_hbm, xg_hbm, idx_smem, zbuf, xbuf,
                     sem_d, sem_i, sem_z, sem_in, *, tm, nt, tr):
    j = pl.program_id(0)
    slot = j % 2
    other = 1 - slot
    cur = j % 3
    nxt = (j + 1) % 3

    def in_copy(tile, s):
        return pltpu.make_async_copy(xn_hbm.at[pl.ds(tile * tm, tm)], xbuf.at[s], sem_in.at[s])

    @pl.when(j == 0)
    def _():
        zbuf[...] = jnp.zeros_like(zbuf)
        for k in range(zt_ref.shape[0]):
            fill = pltpu.make_async_copy(zbuf, xg_hbm.at[pl.ds(jnp.maximum(zt_ref[k], 0) * tr, tr)], sem_z)
            pl.when(zt_ref[k] >= 0)(fill.start)
        for k in range(zt_ref.shape[0]):
            fill = pltpu.make_async_copy(zbuf, xg_hbm.at[pl.ds(0, tr)], sem_z)
            pl.when(zt_ref[k] >= 0)(fill.wait)

    def idx_copy(row, s):
        return pltpu.make_async_copy(idx_hbm.at[row, 0], idx_smem.at[pl.ds(s * 2 * tm, 2 * tm)], sem_i.at[s])

    def wait_rows(s):
        for _ in range(2):
            pltpu.make_async_copy(xbuf.at[s], xg_hbm.at[pl.ds(0, tm)], sem_d.at[s]).wait()

    @pl.when(j == 0)
    def _():
        idx_copy(0, 0).start()
        in_copy(0, 0).start()

    @pl.when(j >= 2)
    def _():
        wait_rows(nxt)

    @pl.when(j + 1 < nt)
    def _():
        in_copy(j + 1, nxt).start()

    idx_copy(j, slot).wait()
    idx_copy(j + 1, other).start()
    in_copy(j, cur).wait()

    def make(r, p1, p2):
        src = xbuf.at[cur, r]
        pltpu.make_async_copy(src, xg_hbm.at[p1], sem_d.at[cur]).start(priority=0)
        pltpu.make_async_copy(src, xg_hbm.at[p2], sem_d.at[cur]).start(priority=1)

    _row_copies(idx_smem, slot, tm, make)

    @pl.when(j == nt - 1)
    def _():
        if nt > 1:
            wait_rows((nt - 2) % 3)
        wait_rows((nt - 1) % 3)
        idx_copy(j + 1, other).wait()


def _dispatch(xn3, table, pad_tiles, *, tm, tr, n_rows):
    t = xn3.shape[0]
    nt = t // tm
    any_spec = pl.BlockSpec(memory_space=pl.ANY)
    grid_spec = pltpu.PrefetchScalarGridSpec(
        num_scalar_prefetch=1,
        grid=(nt,),
        in_specs=[any_spec, any_spec],
        out_specs=any_spec,
        scratch_shapes=[pltpu.SMEM((4 * tm,), jnp.int32),
                        pltpu.VMEM((tr,) + xn3.shape[1:], F32),
                        pltpu.VMEM((3, tm) + xn3.shape[1:], F32),
                        pltpu.SemaphoreType.DMA((3,)), pltpu.SemaphoreType.DMA((2,)),
                        pltpu.SemaphoreType.DMA, pltpu.SemaphoreType.DMA((3,))])
    return pl.pallas_call(
        functools.partial(_dispatch_kernel, tm=tm, nt=nt, tr=tr),
        grid_spec=grid_spec,
        out_shape=jax.ShapeDtypeStruct((n_rows,) + xn3.shape[1:], F32),
        compiler_params=pltpu.CompilerParams(dimension_semantics=("arbitrary",),
                                             vmem_limit_bytes=VMEM_LIMIT,
                                             disable_bounds_checks=True),
        name="moe_dispatch",
    )(pad_tiles, table, xn3)


def _expert_kernel(te_ref, nv_ref, x_ref, w13_ref, w2_ref, y_ref):
    tr = x_ref.shape[0]
    nv = nv_ref[pl.program_id(0)]

    @pl.when(nv > 0)
    def _():
        half = tr // 2
        for p in range(2):
            x = x_ref[p * half:(p + 1) * half].reshape(half, D_MODEL).astype(BF16)
            h13 = jnp.dot(x, w13_ref[...], preferred_element_type=F32)
            fe = h13.shape[1] // 2
            h1, h3 = h13[:, 0:fe], h13[:, fe:2 * fe]
            a = (h1 * _sigmoid(h1) * h3).astype(BF16)
            y = jnp.dot(a, w2_ref[...], preferred_element_type=F32)
            y_ref[p * half:(p + 1) * half] = y.reshape((half,) + y_ref.shape[1:])

    @pl.when(nv == 0)
    def _():
        y_ref[...] = jnp.zeros_like(y_ref)


def _experts(xg3, tile_expert, n_valid, w13, w2, *, li, tr):
    n_rows, sl, ln = xg3.shape
    d = sl * ln
    fe = w2.shape[2]
    rows = lambda j, te, nv: (j, 0, 0)
    wsel = lambda j, te, nv: (li, te[j], 0, 0)
    grid_spec = pltpu.PrefetchScalarGridSpec(
        num_scalar_prefetch=2,
        grid=(n_rows // tr,),
        in_specs=[pl.BlockSpec((tr, sl, ln), rows),
                  pl.BlockSpec((None, None, d, 2 * fe), wsel),
                  pl.BlockSpec((None, None, fe, d), wsel)],
        out_specs=pl.BlockSpec((tr, sl, ln), rows))
    return pl.pallas_call(
        _expert_kernel,
        grid_spec=grid_spec,
        out_shape=jax.ShapeDtypeStruct(xg3.shape, F32),
        compiler_params=_params(1),
        name="moe_experts",
    )(tile_expert, n_valid, xg3, w13, w2)


def _combine_kernel(idx_hbm, h_ref, gate_ref, route_ref, yg_hbm, out_ref,
                    y1buf, y2buf, idx_smem, sem_y, sem_i, *, tm, nt):
    j = pl.program_id(0)
    slot = j % 2
    other = 1 - slot

    def idx_copy(row, s):
        return pltpu.make_async_copy(idx_hbm.at[row, 0], idx_smem.at[pl.ds(s * 2 * tm, 2 * tm)], sem_i.at[s])

    def fetch(s_idx, s_buf):
        def make(r, p1, p2):
            pltpu.make_async_copy(yg_hbm.at[p1], y1buf.at[s_buf, r], sem_y.at[s_buf]).start(priority=0)
            pltpu.make_async_copy(yg_hbm.at[p2], y2buf.at[s_buf, r], sem_y.at[s_buf]).start(priority=1)

        _row_copies(idx_smem, s_idx, tm, make)

    def wait_rows(s):
        pltpu.make_async_copy(yg_hbm.at[pl.ds(0, tm)], y1buf.at[s], sem_y.at[s]).wait()
        pltpu.make_async_copy(yg_hbm.at[pl.ds(0, tm)], y2buf.at[s], sem_y.at[s]).wait()

    @pl.when(j == 0)
    def _():
        first = idx_copy(0, 0)
        first.start()
        first.wait()
        fetch(0, 0)
        idx_copy(1, 1).start()

    idx_copy(j + 1, other).wait()
    fetch(other, other)
    idx_copy(j + 2, slot).start()
    wait_rows(slot)
    rt = route_ref[...]
    y1 = y1buf[slot].reshape(tm, D_MODEL)
    y2 = y2buf[slot].reshape(tm, D_MODEL)
    out_ref[...] = h_ref[...] + gate_ref[...] * (rt[:, 2:3] * y1 + rt[:, 3:4] * y2)

    @pl.when(j == nt - 1)
    def _():
        wait_rows(other)
        idx_copy(j + 2, slot).wait()


def _combine(h, gate, route, table, yg3, *, tm, seq):
    t, d = h.shape
    tpb = seq // tm
    nt = t // tm
    sl, ln = yg3.shape[1:]
    row = lambda i: (i, 0)
    any_spec = pl.BlockSpec(memory_space=pl.ANY)
    return pl.pallas_call(
        functools.partial(_combine_kernel, tm=tm, nt=nt),
        grid=(nt,),
        in_specs=[any_spec,
                  pl.BlockSpec((tm, d), row),
                  pl.BlockSpec((None, 1, d), lambda i: (i // tpb, 0, 0)),
                  pl.BlockSpec((tm, LANES), row),
                  any_spec],
        out_specs=pl.BlockSpec((tm, d), row),
        out_shape=jax.ShapeDtypeStruct((t, d), F32),
        scratch_shapes=[pltpu.VMEM((2, tm, sl, ln), F32), pltpu.VMEM((2, tm, sl, ln), F32),
                        pltpu.SMEM((4 * tm,), jnp.int32),
                        pltpu.SemaphoreType.DMA((2,)), pltpu.SemaphoreType.DMA((2,))],
        compiler_params=pltpu.CompilerParams(dimension_semantics=("arbitrary",),
                                             vmem_limit_bytes=VMEM_LIMIT,
                                             disable_bounds_checks=True),
        name="moe_combine",
    )(table, h, gate, route, yg3)


def _moe(h, g, sh, sc, gate, w13, w2, router, *, li, seq, tr, tm):
    t = h.shape[0]
    xn3, route, route_t, counts = _router(h, g, sh, sc, router, tm=tm, seq=seq)
    tile_expert, n_valid, row_start, pad_tiles = _route_plan(t, counts, tr)
    table = _position_table(route_t, row_start)
    xg3 = _dispatch(xn3, table, pad_tiles, tm=tm, tr=tr, n_rows=2 * t + N_EXPERTS * tr)
    yg3 = _experts(xg3, tile_expert, n_valid, w13, w2, li=li, tr=tr)
    return _combine(h, gate, route, table, yg3, tm=tm, seq=seq)


def _rope_tables(seq):
    rows = seq // GRID_W
    row, col = jnp.meshgrid(jnp.arange(rows, dtype=F32), jnp.arange(GRID_W, dtype=F32), indexing='ij')
    n_freq = HEAD_DIM // 4
    inv_freq = ROPE_THETA ** (-jnp.arange(n_freq, dtype=F32) / n_freq)
    ang_r = row.reshape(-1, 1) * inv_freq
    ang_c = col.reshape(-1, 1) * inv_freq
    ang = jnp.concatenate([ang_r, ang_r, ang_c, ang_c], axis=-1)
    cos, sin = jnp.cos(ang), jnp.sin(ang)
    first = (jnp.arange(HEAD_DIM) % (2 * n_freq)) < n_freq
    sin_a = jnp.where(first, -sin, 0.0)
    sin_b = jnp.where(first, 0.0, sin)
    rep = LANES // HEAD_DIM
    return tuple(jnp.tile(t, (1, rep)) for t in (cos, sin_a, sin_b))


def kernel(x, c, ctx, c_ctx, w_ada, b_ada, norm1_g, norm2_g, w_in, conv_w, q_norm_g, k_norm_g,
           attn_sink, out_norm_conv_g, out_norm_attn_g, w_out, ffn_w1, ffn_w3, ffn_w2,
           moe_router, moe_w1, moe_w3, moe_w2):
    b, s, d = x.shape
    lc = ctx.shape[1]
    depth = w_ada.shape[0]
    assert d == D_MODEL and s % BIG_TILE == 0 and lc % PART_ROWS == 0 and b + 1 <= 8

    c8 = jnp.zeros((8, d), F32).at[0:b].set(c).at[b].set(c_ctx)
    mod = _modulation(c8, w_ada, b_ada)

    tables = _rope_tables(s)
    ids = jnp.arange(256)
    gm = (ids[:, None] // HEAD_DIM == ids[None, :] // HEAD_DIM).astype(BF16)
    scale = HEAD_DIM ** -0.5 * LOG2E

    h = x.reshape(b * s, d)
    hc = ctx.reshape(b * lc, d)
    w_in_b, w_out_b = w_in.astype(BF16), w_out.astype(BF16)
    dense_w = [w.astype(BF16) for w in (ffn_w1, ffn_w3, ffn_w2)]
    moe_w13 = _pack_w13(moe_w1, moe_w3)
    moe_w2b = moe_w2.astype(BF16)
    for layer in range(depth):
        last = layer == depth - 1
        m = mod[layer]
        lat = [m[0:b, k * d:(k + 1) * d].reshape(b, 1, d) for k in range(6)]
        cx = [jnp.broadcast_to(m[b:b + 1, k * d:(k + 1) * d].reshape(1, 1, d), (b, 1, d)) for k in range(6)]
        g1n = norm1_g[layer].reshape(1, d)
        g2n = norm2_g[layer].reshape(1, d)
        gq = (jnp.tile(q_norm_g[layer], N_HEADS) * scale).reshape(1, D_ATTN)
        gk = jnp.tile(k_norm_g[layer], N_KV).reshape(1, LANES)
        gc = out_norm_conv_g[layer].reshape(1, D_CONV)
        ga = out_norm_attn_g[layer].reshape(1, D_ATTN)
        sink = attn_sink[layer]
        cw = conv_w[layer]

        cu, q, kv = _inproj(h, g1n, lat[0], lat[1], w_in_b, gq, gk, gm, tables, li=layer, tm=BIG_TILE, seq=s)
        cuc, qc, kvc = _inproj(hc, g1n, cx[0], cx[1], w_in_b, gq, gk, gm, None, li=layer, tm=lc, seq=lc)
        h = _mixer(h, cu, q, kv, kvc, sink, cw, gc, ga, w_out_b, lat[2],
                   li=layer, tq=BIG_TILE, seq=s, ctx_len=lc, windowed=True)
        if not last:
            hc = _mixer(hc, cuc, qc, None, kvc, sink, cw, gc, ga, w_out_b, cx[2],
                        li=layer, tq=lc, seq=lc, ctx_len=lc, windowed=False)

        i = layer // 2
        if layer % 2 == 0:
            h = _dense_ffn(h, g2n, lat[3], lat[4], lat[5], *dense_w, li=i, tm=BIG_TILE, seq=s)
            if not last:
                hc = _dense_ffn(hc, g2n, cx[3], cx[4], cx[5], *dense_w, li=i, tm=lc, seq=lc)
        else:
            router = jnp.zeros((d, LANES), F32).at[:, 0:N_EXPERTS].set(moe_router[i])
            h = _moe(h, g2n, lat[3], lat[4], lat[5], moe_w13, moe_w2b, router,
                     li=i, seq=s, tr=MOE_TILE, tm=MOE_TILE)
            if not last:
                hc = _moe(hc, g2n, cx[3], cx[4], cx[5], moe_w13, moe_w2b, router,
                          li=i, seq=lc, tr=lc, tm=lc)
    return h.reshape(b, s, d)
```

```python
import functools

import jax
import jax.numpy as jnp
from jax import lax
from jax.experimental import pallas as pl
from jax.experimental.pallas import tpu as pltpu

D_MODEL = 1024
GRID_W = 64
HEAD_DIM = 64
D_CONV = 512
D_ATTN = 512
N_HEADS = 8
N_KV = 2
WINDOW = 128
ROPE_THETA = 10000.0
N_EXPERTS = 8
EPS = 1e-6
KV_OFF = 3 * D_CONV + D_ATTN
D_IN = KV_OFF + 2 * N_KV * HEAD_DIM
LANES = 128
VMEM_LIMIT = 48 * 1024 * 1024
BIG_TILE = 1024
MOE_TILE = 512
PART_ROWS = 256

F32 = jnp.float32
BF16 = jnp.bfloat16
NEG_INF = float("-inf")
LOG2E = 1.4426950408889634


def _params(n_axes):
    return pltpu.CompilerParams(dimension_semantics=("arbitrary",) * n_axes,
                                vmem_limit_bytes=VMEM_LIMIT)


def _sigmoid(x):
    return 1.0 / (1.0 + jnp.exp(-x))


def _mod_kernel(c_ref, w_ref, b_ref, o_ref):
    c = c_ref[...]
    s = c * _sigmoid(c)
    o_ref[...] = jnp.dot(s, w_ref[...], preferred_element_type=F32) + b_ref[...]


def _modulation(c8, w_ada, b_ada):
    depth, d, n = w_ada.shape
    tn = 1536
    return pl.pallas_call(
        _mod_kernel,
        grid=(depth, n // tn),
        in_specs=[pl.BlockSpec((8, d), lambda l, j: (0, 0)),
                  pl.BlockSpec((None, d, tn), lambda l, j: (l, 0, j)),
                  pl.BlockSpec((None, 1, tn), lambda l, j: (l, 0, j))],
        out_specs=pl.BlockSpec((None, 8, tn), lambda l, j: (l, 0, j)),
        out_shape=jax.ShapeDtypeStruct((depth, 8, n), F32),
        compiler_params=_params(2),
        name="adaln_mod",
    )(c8, w_ada, b_ada.reshape(depth, 1, n))


def _norm_mod(x, g, sh, sc):
    ms = jnp.mean(x * x, axis=-1, keepdims=True)
    return (x * lax.rsqrt(ms + EPS) * g) * (1.0 + sc) + sh


def _inproj_kernel(*refs, rope, parts):
    if rope:
        (h_ref, g_ref, sh_ref, sc_ref, w_ref, gq_ref, gk_ref, gm_ref,
         cos_ref, sa_ref, sb_ref, cu_ref, q_ref, kv_ref) = refs
    else:
        (h_ref, g_ref, sh_ref, sc_ref, w_ref, gq_ref, gk_ref, gm_ref,
         cu_ref, q_ref, kv_ref) = refs
    gm = gm_ref[...]
    gq = gq_ref[...]
    rows = h_ref.shape[0] // parts

    def head_norm(t, gain):
        w = t.shape[1]
        ss = jnp.dot((t * t).astype(BF16), gm[0:w, 0:w], preferred_element_type=F32)
        return t * lax.rsqrt(ss * (1.0 / HEAD_DIM) + EPS) * gain

    for p in range(parts):
        sl = slice(p * rows, (p + 1) * rows)

        def rot(t):
            if not rope:
                return t
            return (t * cos_ref[sl, :] + pltpu.roll(t, LANES - 16, 1) * sa_ref[sl, :]
                    + pltpu.roll(t, 16, 1) * sb_ref[sl, :])

        xb = _norm_mod(h_ref[sl, :], g_ref[...], sh_ref[...], sc_ref[...]).astype(BF16)
        ya = jnp.dot(xb, w_ref[:, 3 * D_CONV:D_IN], preferred_element_type=F32)
        yc = jnp.dot(xb, w_ref[:, 0:3 * D_CONV], preferred_element_type=F32)
        cu_ref[sl, 0:D_CONV] = yc[:, 0:D_CONV].astype(BF16)
        cu_ref[sl, D_CONV:2 * D_CONV] = (yc[:, D_CONV:2 * D_CONV] * yc[:, 2 * D_CONV:3 * D_CONV]).astype(BF16)
        for j in range(D_ATTN // 256):
            qn = head_norm(ya[:, 256 * j:256 * j + 256], gq[:, 256 * j:256 * j + 256])
            for c in range(2):
                q_ref[sl, 256 * j + LANES * c:256 * j + LANES * (c + 1)] = rot(
                    qn[:, LANES * c:LANES * (c + 1)]).astype(BF16)
        k = rot(head_norm(ya[:, D_ATTN:D_ATTN + LANES], gk_ref[...]))
        v = ya[:, D_ATTN + LANES:D_ATTN + 2 * LANES]
        lo_half = lax.broadcasted_iota(jnp.int32, k.shape, 1) < HEAD_DIM
        for c, t in enumerate((k, v)):
            sw = pltpu.roll(t, HEAD_DIM, 1)
            kv_ref[sl, 2 * c * LANES:(2 * c + 1) * LANES] = jnp.where(lo_half, t, sw).astype(BF16)
            kv_ref[sl, (2 * c + 1) * LANES:(2 * c + 2) * LANES] = jnp.where(lo_half, sw, t).astype(BF16)


def _inproj(h, g, sh, sc, w, gq, gk, gm, tables, *, li, tm, seq):
    t, d = h.shape
    tpb = seq // tm
    rope = tables is not None
    row = lambda i: (i, 0)
    fix = lambda i: (0, 0)
    mod = lambda i: (i // tpb, 0, 0)
    in_specs = [pl.BlockSpec((tm, d), row),
                pl.BlockSpec((1, d), fix),
                pl.BlockSpec((None, 1, d), mod),
                pl.BlockSpec((None, 1, d), mod),
                pl.BlockSpec((None, d, D_IN), lambda i: (li, 0, 0)),
                pl.BlockSpec((1, D_ATTN), fix),
                pl.BlockSpec((1, LANES), fix),
                pl.BlockSpec((256, 256), fix)]
    args = [h, g, sh, sc, w, gq, gk, gm]
    if rope:
        in_specs += [pl.BlockSpec((tm, LANES), lambda i: (i % tpb, 0))] * 3
        args += list(tables)
    return pl.pallas_call(
        functools.partial(_inproj_kernel, rope=rope, parts=max(tm // PART_ROWS, 1)),
        grid=(t // tm,),
        in_specs=in_specs,
        out_specs=[pl.BlockSpec((tm, 2 * D_CONV), row),
                   pl.BlockSpec((tm, D_ATTN), row),
                   pl.BlockSpec((tm, 4 * LANES), row)],
        out_shape=[jax.ShapeDtypeStruct((t, 2 * D_CONV), BF16),
                   jax.ShapeDtypeStruct((t, D_ATTN), BF16),
                   jax.ShapeDtypeStruct((t, 4 * LANES), BF16)],
        compiler_params=_params(1),
        name="inproj_rope" if rope else "inproj_ctx",
    )(*args)


_NT = (((1,), (1,)), ((), ()))


def _mixer_kernel(*refs, tq, windowed):
    if windowed:
        (sink_ref, h_ref, cu_ref, cup_ref, cun_ref, q_ref, kvp_ref, kv_ref, kvn_ref, kvc_ref,
         cw_ref, gc_ref, ga_ref, wo_ref, g1_ref, out_ref, kw_ref, ya_ref) = refs
    else:
        (sink_ref, h_ref, cu_ref, q_ref, kvc_ref,
         cw_ref, gc_ref, ga_ref, wo_ref, g1_ref, out_ref, ya_ref) = refs
    i = pl.program_id(1)
    nt = pl.num_programs(1)
    nsub = tq // WINDOW

    cu = cu_ref[...]
    bg = cu[:, 0:D_CONV].astype(F32)
    u = cu[:, D_CONV:2 * D_CONV].astype(F32)
    rows = lax.broadcasted_iota(jnp.int32, (tq, 1), 0)
    if windowed:
        up_row = cup_ref[:, D_CONV:2 * D_CONV].astype(F32)[15:16, :]
        un_row = cun_ref[:, D_CONV:2 * D_CONV].astype(F32)[0:1, :]
        up_row = jnp.where(i > 0, up_row, 0.0)
        un_row = jnp.where(i < nt - 1, un_row, 0.0)
    else:
        up_row = jnp.zeros((1, D_CONV), F32)
        un_row = jnp.zeros((1, D_CONV), F32)
    u_prev = jnp.where(rows == 0, up_row, pltpu.roll(u, 1, 0))
    u_next = jnp.where(rows == tq - 1, un_row, pltpu.roll(u, tq - 1, 0))
    cw = cw_ref[...]
    yc = bg * (cw[0:1, :] * u_prev + cw[1:2, :] * u + cw[2:3, :] * u_next)
    yc = yc * lax.rsqrt(jnp.mean(yc * yc, axis=-1, keepdims=True) + EPS) * gc_ref[...]

    if windowed:
        kw_ref[0:WINDOW, :] = kvp_ref[...]
        kw_ref[WINDOW:WINDOW + tq, :] = kv_ref[...]
        kw_ref[WINDOW + tq:2 * WINDOW + tq, :] = kvn_ref[...]
    lane_lo = lax.broadcasted_iota(jnp.int32, (WINDOW, LANES), 1) < HEAD_DIM
    kvc = kvc_ref[...]
    gqa = N_HEADS // N_KV
    head_of_row = lax.broadcasted_iota(jnp.int32, (N_HEADS * WINDOW, 1), 0) // WINDOW
    ones_c = jnp.ones((kvc.shape[0], LANES), BF16)
    if windowed:
        kk = lax.broadcasted_iota(jnp.int32, (WINDOW, WINDOW), 1)
        qq = lax.broadcasted_iota(jnp.int32, (WINDOW, WINDOW), 0)
        keep_before = kk >= qq
        keep_after = kk <= qq
        ones_w = jnp.ones((3 * WINDOW, LANES), BF16)

    def sub_block(s):
        r0 = s * WINDOW
        if windowed:
            kwin = kw_ref[pl.ds(r0, 3 * WINDOW), :]
            before, after = keep_before, keep_after
            if s == 0:
                before = before & (i > 0)
            if s == nsub - 1:
                after = after & (i < nt - 1)
        rows, sk = [], jnp.zeros((N_HEADS * WINDOW, 1), F32)
        for head in range(N_HEADS):
            qp = q_ref[pl.ds(r0, WINDOW), LANES * (head // 2):LANES * (head // 2 + 1)]
            zero = jnp.zeros_like(qp)
            qh = jnp.where(lane_lo, zero, qp) if head % 2 else jnp.where(lane_lo, qp, zero)
            rows.append(jnp.concatenate([qh, zero] if head < gqa else [zero, qh], axis=1))
            sk = jnp.where(head_of_row == head, sink_ref[head] * LOG2E, sk)
        q8 = jnp.concatenate(rows, axis=0)
        s_c = lax.dot_general(q8, kvc[:, 0:2 * LANES], _NT, preferred_element_type=F32)
        m = jnp.maximum(jnp.max(s_c, axis=-1, keepdims=True), sk)
        if windowed:
            s_w = lax.dot_general(q8, kwin[:, 0:2 * LANES], _NT,
                                  preferred_element_type=F32).reshape(N_HEADS, WINDOW, 3 * WINDOW)
            s_w = jnp.concatenate(
                [jnp.where(before[None], s_w[:, :, 0:WINDOW], NEG_INF),
                 s_w[:, :, WINDOW:2 * WINDOW],
                 jnp.where(after[None], s_w[:, :, 2 * WINDOW:3 * WINDOW], NEG_INF)],
                axis=-1).reshape(N_HEADS * WINDOW, 3 * WINDOW)
            m = jnp.maximum(m, jnp.max(s_w, axis=-1, keepdims=True))
            p_w = jnp.exp2(s_w - m).astype(BF16)
        p_c = jnp.exp2(s_c - m).astype(BF16)
        tail = jnp.exp2(sk - m)
        for grp in range(N_KV):
            gr = slice(grp * gqa * WINDOW, (grp + 1) * gqa * WINDOW)
            v_c = jnp.concatenate([kvc[:, LANES * (2 + grp):LANES * (3 + grp)], ones_c], axis=1)
            o = jnp.dot(p_c[gr], v_c, preferred_element_type=F32)
            if windowed:
                v_w = jnp.concatenate([kwin[:, LANES * (2 + grp):LANES * (3 + grp)], ones_w], axis=1)
                o = o + jnp.dot(p_w[gr], v_w, preferred_element_type=F32)
            o = o[:, 0:LANES] / (o[:, LANES:2 * LANES] + tail[gr])
            for pr in range(gqa // 2):
                pair = (gqa // 2) * grp + pr
                even = o[2 * pr * WINDOW:(2 * pr + 1) * WINDOW]
                odd = o[(2 * pr + 1) * WINDOW:(2 * pr + 2) * WINDOW]
                ya_ref[pl.ds(r0, WINDOW), LANES * pair:LANES * (pair + 1)] = jnp.where(lane_lo, even, odd)

    per_part = min(4, nsub)
    part = per_part * WINDOW
    ycb = yc.astype(BF16)
    for p in range(tq // part):
        for s in range(per_part):
            sub_block(per_part * p + s)
        sl = slice(p * part, (p + 1) * part)
        ya = ya_ref[sl, :]
        ya = ya * lax.rsqrt(jnp.mean(ya * ya, axis=-1, keepdims=True) + EPS) * ga_ref[...]
        y = (jnp.dot(ycb[sl], wo_ref[0:D_CONV, :], preferred_element_type=F32)
             + jnp.dot(ya.astype(BF16), wo_ref[D_CONV:2 * D_CONV, :], preferred_element_type=F32))
        out_ref[sl, :] = h_ref[sl, :] + g1_ref[...] * y


def _mixer(h, cu, q, kv, kvc, sink, cw, gc, ga, wo, g1, *, li, tq, seq, ctx_len, windowed):
    t, d = h.shape
    nt = seq // tq
    nb = t // seq
    row = lambda b, i: (b * nt + i, 0)
    fix = lambda b, i: (0, 0)
    smem = pl.BlockSpec(memory_space=pltpu.SMEM)
    tail = [pl.BlockSpec((3, D_CONV), fix),
            pl.BlockSpec((1, D_CONV), fix),
            pl.BlockSpec((1, D_ATTN), fix),
            pl.BlockSpec((None, d, d), lambda b, i: (li, 0, 0)),
            pl.BlockSpec((None, 1, d), lambda b, i: (b, 0, 0))]
    ctx_spec = pl.BlockSpec((ctx_len, 4 * LANES), lambda b, i: (b, 0))
    if windowed:
        r16 = tq // 16
        n16 = t // 16
        rw = tq // WINDOW
        nw = t // WINDOW
        in_specs = [smem,
                    pl.BlockSpec((tq, d), row),
                    pl.BlockSpec((tq, 2 * D_CONV), row),
                    pl.BlockSpec((16, 2 * D_CONV), lambda b, i: (jnp.maximum((b * nt + i) * r16 - 1, 0), 0)),
                    pl.BlockSpec((16, 2 * D_CONV), lambda b, i: (jnp.minimum((b * nt + i + 1) * r16, n16 - 1), 0)),
                    pl.BlockSpec((tq, D_ATTN), row),
                    pl.BlockSpec((WINDOW, 4 * LANES), lambda b, i: (jnp.maximum((b * nt + i) * rw - 1, 0), 0)),
                    pl.BlockSpec((tq, 4 * LANES), row),
                    pl.BlockSpec((WINDOW, 4 * LANES), lambda b, i: (jnp.minimum((b * nt + i + 1) * rw, nw - 1), 0)),
                    ctx_spec] + tail
        args = [sink, h, cu, cu, cu, q, kv, kv, kv, kvc, cw, gc, ga, wo, g1]
        scratch = [pltpu.VMEM((tq + 2 * WINDOW, 4 * LANES), BF16), pltpu.VMEM((tq, D_ATTN), F32)]
    else:
        in_specs = [smem,
                    pl.BlockSpec((tq, d), row),
                    pl.BlockSpec((tq, 2 * D_CONV), row),
                    pl.BlockSpec((tq, D_ATTN), row),
                    ctx_spec] + tail
        args = [sink, h, cu, q, kvc, cw, gc, ga, wo, g1]
        scratch = [pltpu.VMEM((tq, D_ATTN), F32)]
    return pl.pallas_call(
        functools.partial(_mixer_kernel, tq=tq, windowed=windowed),
        grid=(nb, nt),
        in_specs=in_specs,
        out_specs=pl.BlockSpec((tq, d), row),
        out_shape=jax.ShapeDtypeStruct((t, d), F32),
        scratch_shapes=scratch,
        compiler_params=_params(2),
        name="mixer_win" if windowed else "mixer_ctx",
    )(*args)


def _dense_ffn_kernel(h_ref, g_ref, sh_ref, sc_ref, gate_ref, w1_ref, w3_ref, w2_ref, out_ref, *, parts):
    rows = h_ref.shape[0] // parts
    for p in range(parts):
        sl = slice(p * rows, (p + 1) * rows)
        hp = h_ref[sl, :]
        xb = _norm_mod(hp, g_ref[...], sh_ref[...], sc_ref[...]).astype(BF16)
        h1 = jnp.dot(xb, w1_ref[...], preferred_element_type=F32)
        h3 = jnp.dot(xb, w3_ref[...], preferred_element_type=F32)
        a = (h1 * _sigmoid(h1) * h3).astype(BF16)
        y = jnp.dot(a, w2_ref[...], preferred_element_type=F32)
        out_ref[sl, :] = hp + gate_ref[...] * y


def _dense_ffn(h, g, sh, sc, gate, w1, w3, w2, *, li, tm, seq):
    t, d = h.shape
    f = w1.shape[2]
    wfix = lambda i: (li, 0, 0)
    tpb = seq // tm
    row = lambda i: (i, 0)
    fix = lambda i: (0, 0)
    mod = lambda i: (i // tpb, 0, 0)
    once = pl.Buffered(1)
    return pl.pallas_call(
        functools.partial(_dense_ffn_kernel, parts=max(tm // PART_ROWS, 1)),
        grid=(t // tm,),
        in_specs=[pl.BlockSpec((tm, d), row),
                  pl.BlockSpec((1, d), fix),
                  pl.BlockSpec((None, 1, d), mod),
                  pl.BlockSpec((None, 1, d), mod),
                  pl.BlockSpec((None, 1, d), mod),
                  pl.BlockSpec((None, d, f), wfix, pipeline_mode=once),
                  pl.BlockSpec((None, d, f), wfix, pipeline_mode=once),
                  pl.BlockSpec((None, f, d), wfix, pipeline_mode=once)],
        out_specs=pl.BlockSpec((tm, d), row),
        out_shape=jax.ShapeDtypeStruct((t, d), F32),
        compiler_params=_params(1),
        name="ffn_dense",
    )(h, g, sh, sc, gate, w1, w3, w2)


def _pack_w13_kernel(w1_ref, w3_ref, o_ref):
    fe = w1_ref.shape[1]
    o_ref[:, 0:fe] = w1_ref[...].astype(BF16)
    o_ref[:, fe:2 * fe] = w3_ref[...].astype(BF16)


def _pack_w13(w1, w3):
    nl, ne, d, fe = w1.shape
    rows = d // 2
    spec = pl.BlockSpec((None, None, rows, fe), lambda l, e, r: (l, e, r, 0))
    return pl.pallas_call(
        _pack_w13_kernel,
        grid=(nl, ne, d // rows),
        in_specs=[spec, spec],
        out_specs=pl.BlockSpec((None, None, rows, 2 * fe), lambda l, e, r: (l, e, r, 0)),
        out_shape=jax.ShapeDtypeStruct((nl, ne, d, 2 * fe), BF16),
        compiler_params=_params(3),
        name="pack_w13",
    )(w1, w3)


def _top2(logits):
    lane = lax.broadcasted_iota(jnp.int32, logits.shape, 1)
    lg = jnp.where(lane < N_EXPERTS, logits, NEG_INF)
    m1 = jnp.max(lg, axis=-1, keepdims=True)
    i1 = jnp.min(jnp.where(lg == m1, lane, LANES), axis=-1, keepdims=True)
    lg2 = jnp.where(lane == i1, NEG_INF, lg)
    m2 = jnp.max(lg2, axis=-1, keepdims=True)
    i2 = jnp.min(jnp.where(lg2 == m2, lane, LANES), axis=-1, keepdims=True)
    e2 = jnp.exp(m2 - m1)
    return i1, i2, 1.0 / (1.0 + e2), e2 / (1.0 + e2)


def _router_kernel(h_ref, g_ref, sh_ref, sc_ref, r_ref, tri_ref, xn_ref, route_ref, rt_ref, cnt_ref,
                   base_ref):
    tm = h_ref.shape[0]

    @pl.when(pl.program_id(0) == 0)
    def _():
        base_ref[...] = jnp.zeros_like(base_ref)

    xn = _norm_mod(h_ref[...], g_ref[...], sh_ref[...], sc_ref[...])
    xn_ref[...] = xn.reshape(xn_ref.shape)
    i1, i2, g1, g2 = _top2(jnp.dot(xn, r_ref[...], preferred_element_type=F32))
    lane = lax.broadcasted_iota(jnp.int32, (tm, LANES), 1)
    hit1 = lane == i1
    hit2 = lane == i2
    chosen = jnp.where(jnp.logical_or(hit1, hit2), 1.0, 0.0)
    before = base_ref[...] + jnp.dot(tri_ref[...], chosen.astype(BF16), preferred_element_type=F32)
    r1 = jnp.sum(jnp.where(hit1, before, 0.0), axis=-1, keepdims=True)
    r2 = jnp.sum(jnp.where(hit2, before, 0.0), axis=-1, keepdims=True)
    base_ref[...] += jnp.sum(chosen, axis=0, keepdims=True)
    cnt_ref[...] = base_ref[...]
    fields = (i1.astype(F32), i2.astype(F32), g1, g2, r1, r2)
    route = jnp.zeros((tm, LANES), F32)
    for k, f in enumerate(fields):
        route = jnp.where(lane == k, f, route)
    route_ref[...] = route
    rt_ref[...] = route.T[0:8, :]


def _router(h, g, sh, sc, router, *, tm, seq):
    t, d = h.shape
    tpb = seq // tm
    row = lambda i: (i, 0)
    fix = lambda i: (0, 0)
    mod = lambda i: (i // tpb, 0, 0)
    ids = jnp.arange(tm)
    tri = (ids[None, :] < ids[:, None]).astype(BF16)
    return pl.pallas_call(
        _router_kernel,
        grid=(t // tm,),
        in_specs=[pl.BlockSpec((tm, d), row), pl.BlockSpec((1, d), fix),
                  pl.BlockSpec((None, 1, d), mod), pl.BlockSpec((None, 1, d), mod),
                  pl.BlockSpec((d, LANES), fix), pl.BlockSpec((tm, tm), fix)],
        out_specs=[pl.BlockSpec((tm, d // LANES, LANES), lambda i: (i, 0, 0)),
                   pl.BlockSpec((tm, LANES), row),
                   pl.BlockSpec((None, 8, tm), lambda i: (i, 0, 0)),
                   pl.BlockSpec((1, LANES), fix)],
        out_shape=[jax.ShapeDtypeStruct((t, d // LANES, LANES), F32),
                   jax.ShapeDtypeStruct((t, LANES), F32),
                   jax.ShapeDtypeStruct((t // tm, 8, tm), F32),
                   jax.ShapeDtypeStruct((1, LANES), F32)],
        scratch_shapes=[pltpu.VMEM((1, LANES), F32)],
        compiler_params=_params(1),
        name="moe_router",
    )(h, g, sh, sc, router, tri)


def _route_plan(t, counts, tr):
    counts = counts[0, 0:N_EXPERTS].astype(jnp.int32)
    tiles = (counts + tr - 1) // tr
    tile_end = jnp.cumsum(tiles)
    tile_start = tile_end - tiles
    experts = jnp.arange(N_EXPERTS, dtype=jnp.int32)

    nt = 2 * t // tr + N_EXPERTS
    tid = jnp.arange(nt, dtype=jnp.int32)
    tile_expert = jnp.minimum(jnp.sum((tid[:, None] >= tile_end[None, :]).astype(jnp.int32), axis=1),
                              N_EXPERTS - 1)
    in_tile = tid - jnp.sum(jnp.where(tile_expert[:, None] == experts[None, :], tile_start[None, :], 0), axis=1)
    own = jnp.sum(jnp.where(tile_expert[:, None] == experts[None, :], counts[None, :], 0), axis=1)
    n_valid = jnp.where(tid < tile_end[-1], jnp.clip(own - in_tile * tr, 0, tr), 0)
    tail = tile_end[-1] + experts
    pad_tiles = jnp.concatenate([jnp.where(tiles > 0, tile_end - 1, -1), jnp.where(tail < nt, tail, -1)])
    return tile_expert, n_valid, tile_start * tr, pad_tiles


def _table_kernel(start_ref, rt_ref, tab_ref, *, nt):
    tm = rt_ref.shape[1]

    def position(e, r):
        start = jnp.zeros_like(r)
        for k in range(N_EXPERTS):
            start = jnp.where(e == float(k), start_ref[k].astype(F32), start)
        return (start + r).astype(jnp.int32)

    live = pl.program_id(0) < nt
    rt = rt_ref[...]
    for c in range(2):
        pos = position(rt[c:c + 1, :], rt[4 + c:5 + c, :])
        tab_ref[:, c * tm:(c + 1) * tm] = jnp.where(live, pos, 0)


def _position_table(route_t, row_start):
    nt, fields, tm = route_t.shape
    grid_spec = pltpu.PrefetchScalarGridSpec(
        num_scalar_prefetch=1,
        grid=(nt + 2,),
        in_specs=[pl.BlockSpec((None, fields, tm), lambda j, st: (jnp.minimum(j, nt - 1), 0, 0))],
        out_specs=pl.BlockSpec((None, 1, 2 * tm), lambda j, st: (j, 0, 0)))
    return pl.pallas_call(
        functools.partial(_table_kernel, nt=nt),
        grid_spec=grid_spec,
        out_shape=jax.ShapeDtypeStruct((nt + 2, 1, 2 * tm), jnp.int32),
        compiler_params=_params(1),
        name="moe_table",
    )(row_start, route_t)


def _row_copies(idx_smem, s_idx, tm, make):
    base = s_idx * (2 * tm)

    def body(r, c):
        make(r, idx_smem[base + r], idx_smem[base + tm + r])
        return c

    lax.fori_loop(0, tm, body, 0, unroll=8)


def _dispatch_kernel(zt_ref, idx_hbm, xn_hbm, xg_hbm, idx_smem, zbuf, xbuf,
                     sem_d, sem_i, sem_z, sem_in, *, tm, nt, tr):
    j = pl.program_id(0)
    slot = j % 2
    other = 1 - slot
    cur = j % 3
    nxt = (j + 1) % 3

    def in_copy(tile, s):
        return pltpu.make_async_copy(xn_hbm.at[pl.ds(tile * tm, tm)], xbuf.at[s], sem_in.at[s])

    @pl.when(j == 0)
    def _():
        zbuf[...] = jnp.zeros_like(zbuf)
        for k in range(zt_ref.shape[0]):
            fill = pltpu.make_async_copy(zbuf, xg_hbm.at[pl.ds(jnp.maximum(zt_ref[k], 0) * tr, tr)], sem_z)
            pl.when(zt_ref[k] >= 0)(fill.start)
        for k in range(zt_ref.shape[0]):
            fill = pltpu.make_async_copy(zbuf, xg_hbm.at[pl.ds(0, tr)], sem_z)
            pl.when(zt_ref[k] >= 0)(fill.wait)

    def idx_copy(row, s):
        return pltpu.make_async_copy(idx_hbm.at[row, 0], idx_smem.at[pl.ds(s * 2 * tm, 2 * tm)], sem_i.at[s])

    def wait_rows(s):
        for _ in range(2):
            pltpu.make_async_copy(xbuf.at[s], xg_hbm.at[pl.ds(0, tm)], sem_d.at[s]).wait()

    @pl.when(j == 0)
    def _():
        idx_copy(0, 0).start()
        in_copy(0, 0).start()

    @pl.when(j >= 2)
    def _():
        wait_rows(nxt)

    @pl.when(j + 1 < nt)
    def _():
        in_copy(j + 1, nxt).start()

    idx_copy(j, slot).wait()
    idx_copy(j + 1, other).start()
    in_copy(j, cur).wait()

    def make(r, p1, p2):
        src = xbuf.at[cur, r]
        pltpu.make_async_copy(src, xg_hbm.at[p1], sem_d.at[cur]).start(priority=0)
        pltpu.make_async_copy(src, xg_hbm.at[p2], sem_d.at[cur]).start(priority=1)

    _row_copies(idx_smem, slot, tm, make)

    @pl.when(j == nt - 1)
    def _():
        if nt > 1:
            wait_rows((nt - 2) % 3)
        wait_rows((nt - 1) % 3)
        idx_copy(j + 1, other).wait()


def _dispatch(xn3, table, pad_tiles, *, tm, tr, n_rows):
    t = xn3.shape[0]
    nt = t // tm
    any_spec = pl.BlockSpec(memory_space=pl.ANY)
    grid_spec = pltpu.PrefetchScalarGridSpec(
        num_scalar_prefetch=1,
        grid=(nt,),
        in_specs=[any_spec, any_spec],
        out_specs=any_spec,
        scratch_shapes=[pltpu.SMEM((4 * tm,), jnp.int32),
                        pltpu.VMEM((tr,) + xn3.shape[1:], F32),
                        pltpu.VMEM((3, tm) + xn3.shape[1:], F32),
                        pltpu.SemaphoreType.DMA((3,)), pltpu.SemaphoreType.DMA((2,)),
                        pltpu.SemaphoreType.DMA, pltpu.SemaphoreType.DMA((3,))])
    return pl.pallas_call(
        functools.partial(_dispatch_kernel, tm=tm, nt=nt, tr=tr),
        grid_spec=grid_spec,
        out_shape=jax.ShapeDtypeStruct((n_rows,) + xn3.shape[1:], F32),
        compiler_params=pltpu.CompilerParams(dimension_semantics=("arbitrary",),
                                             vmem_limit_bytes=VMEM_LIMIT,
                                             disable_bounds_checks=True),
        name="moe_dispatch",
    )(pad_tiles, table, xn3)


def _expert_kernel(te_ref, nv_ref, x_ref, w13_ref, w2_ref, y_ref):
    tr = x_ref.shape[0]
    nv = nv_ref[pl.program_id(0)]

    @pl.when(nv > 0)
    def _():
        half = tr // 2
        for p in range(2):
            x = x_ref[p * half:(p + 1) * half].reshape(half, D_MODEL).astype(BF16)
            h13 = jnp.dot(x, w13_ref[...], preferred_element_type=F32)
            fe = h13.shape[1] // 2
            h1, h3 = h13[:, 0:fe], h13[:, fe:2 * fe]
            a = (h1 * _sigmoid(h1) * h3).astype(BF16)
            y = jnp.dot(a, w2_ref[...], preferred_element_type=F32)
            y_ref[p * half:(p + 1) * half] = y.reshape((half,) + y_ref.shape[1:])

    @pl.when(nv == 0)
    def _():
        y_ref[...] = jnp.zeros_like(y_ref)


def _experts(xg3, tile_expert, n_valid, w13, w2, *, li, tr):
    n_rows, sl, ln = xg3.shape
    d = sl * ln
    fe = w2.shape[2]
    rows = lambda j, te, nv: (j, 0, 0)
    wsel = lambda j, te, nv: (li, te[j], 0, 0)
    grid_spec = pltpu.PrefetchScalarGridSpec(
        num_scalar_prefetch=2,
        grid=(n_rows // tr,),
        in_specs=[pl.BlockSpec((tr, sl, ln), rows),
                  pl.BlockSpec((None, None, d, 2 * fe), wsel),
                  pl.BlockSpec((None, None, fe, d), wsel)],
        out_specs=pl.BlockSpec((tr, sl, ln), rows))
    return pl.pallas_call(
        _expert_kernel,
        grid_spec=grid_spec,
        out_shape=jax.ShapeDtypeStruct(xg3.shape, F32),
        compiler_params=_params(1),
        name="moe_experts",
    )(tile_expert, n_valid, xg3, w13, w2)


def _combine_kernel(idx_hbm, h_ref, gate_ref, route_ref, yg_hbm, out_ref,
                    y1buf, y2buf, idx_smem, sem_y, sem_i, *, tm, nt):
    j = pl.program_id(0)
    slot = j % 2
    other = 1 - slot

    def idx_copy(row, s):
        return pltpu.make_async_copy(idx_hbm.at[row, 0], idx_smem.at[pl.ds(s * 2 * tm, 2 * tm)], sem_i.at[s])

    def fetch(s_idx, s_buf):
        def make(r, p1, p2):
            pltpu.make_async_copy(yg_hbm.at[p1], y1buf.at[s_buf, r], sem_y.at[s_buf]).start(priority=0)
            pltpu.make_async_copy(yg_hbm.at[p2], y2buf.at[s_buf, r], sem_y.at[s_buf]).start(priority=1)

        _row_copies(idx_smem, s_idx, tm, make)

    def wait_rows(s):
        pltpu.make_async_copy(yg_hbm.at[pl.ds(0, tm)], y1buf.at[s], sem_y.at[s]).wait()
        pltpu.make_async_copy(yg_hbm.at[pl.ds(0, tm)], y2buf.at[s], sem_y.at[s]).wait()

    @pl.when(j == 0)
    def _():
        first = idx_copy(0, 0)
        first.start()
        first.wait()
        fetch(0, 0)
        idx_copy(1, 1).start()

    idx_copy(j + 1, other).wait()
    fetch(other, other)
    idx_copy(j + 2, slot).start()
    wait_rows(slot)
    rt = route_ref[...]
    y1 = y1buf[slot].reshape(tm, D_MODEL)
    y2 = y2buf[slot].reshape(tm, D_MODEL)
    out_ref[...] = h_ref[...] + gate_ref[...] * (rt[:, 2:3] * y1 + rt[:, 3:4] * y2)

    @pl.when(j == nt - 1)
    def _():
        wait_rows(other)
        idx_copy(j + 2, slot).wait()


def _combine(h, gate, route, table, yg3, *, tm, seq):
    t, d = h.shape
    tpb = seq // tm
    nt = t // tm
    sl, ln = yg3.shape[1:]
    row = lambda i: (i, 0)
    any_spec = pl.BlockSpec(memory_space=pl.ANY)
    return pl.pallas_call(
        functools.partial(_combine_kernel, tm=tm, nt=nt),
        grid=(nt,),
        in_specs=[any_spec,
                  pl.BlockSpec((tm, d), row),
                  pl.BlockSpec((None, 1, d), lambda i: (i // tpb, 0, 0)),
                  pl.BlockSpec((tm, LANES), row),
                  any_spec],
        out_specs=pl.BlockSpec((tm, d), row),
        out_shape=jax.ShapeDtypeStruct((t, d), F32),
        scratch_shapes=[pltpu.VMEM((2, tm, sl, ln), F32), pltpu.VMEM((2, tm, sl, ln), F32),
                        pltpu.SMEM((4 * tm,), jnp.int32),
                        pltpu.SemaphoreType.DMA((2,)), pltpu.SemaphoreType.DMA((2,))],
        compiler_params=pltpu.CompilerParams(dimension_semantics=("arbitrary",),
                                             vmem_limit_bytes=VMEM_LIMIT,
                                             disable_bounds_checks=True),
        name="moe_combine",
    )(table, h, gate, route, yg3)


def _moe(h, g, sh, sc, gate, w13, w2, router, *, li, seq, tr, tm):
    t = h.shape[0]
    xn3, route, route_t, counts = _router(h, g, sh, sc, router, tm=tm, seq=seq)
    tile_expert, n_valid, row_start, pad_tiles = _route_plan(t, counts, tr)
    table = _position_table(route_t, row_start)
    xg3 = _dispatch(xn3, table, pad_tiles, tm=tm, tr=tr, n_rows=2 * t + N_EXPERTS * tr)
    yg3 = _experts(xg3, tile_expert, n_valid, w13, w2, li=li, tr=tr)
    return _combine(h, gate, route, table, yg3, tm=tm, seq=seq)


def _rope_tables(seq):
    rows = seq // GRID_W
    row, col = jnp.meshgrid(jnp.arange(rows, dtype=F32), jnp.arange(GRID_W, dtype=F32), indexing='ij')
    n_freq = HEAD_DIM // 4
    inv_freq = ROPE_THETA ** (-jnp.arange(n_freq, dtype=F32) / n_freq)
    ang_r = row.reshape(-1, 1) * inv_freq
    ang_c = col.reshape(-1, 1) * inv_freq
    ang = jnp.concatenate([ang_r, ang_r, ang_c, ang_c], axis=-1)
    cos, sin = jnp.cos(ang), jnp.sin(ang)
    first = (jnp.arange(HEAD_DIM) % (2 * n_freq)) < n_freq
    sin_a = jnp.where(first, -sin, 0.0)
    sin_b = jnp.where(first, 0.0, sin)
    rep = LANES // HEAD_DIM
    return tuple(jnp.tile(t, (1, rep)) for t in (cos, sin_a, sin_b))


def kernel(x, c, ctx, c_ctx, w_ada, b_ada, norm1_g, norm2_g, w_in, conv_w, q_norm_g, k_norm_g,
           attn_sink, out_norm_conv_g, out_norm_attn_g, w_out, ffn_w1, ffn_w3, ffn_w2,
           moe_router, moe_w1, moe_w3, moe_w2):
    b, s, d = x.shape
    lc = ctx.shape[1]
    depth = w_ada.shape[0]
    assert d == D_MODEL and s % BIG_TILE == 0 and lc % PART_ROWS == 0 and b + 1 <= 8

    c8 = jnp.zeros((8, d), F32).at[0:b].set(c).at[b].set(c_ctx)
    mod = _modulation(c8, w_ada, b_ada)

    tables = _rope_tables(s)
    ids = jnp.arange(256)
    gm = (ids[:, None] // HEAD_DIM == ids[None, :] // HEAD_DIM).astype(BF16)
    scale = HEAD_DIM ** -0.5 * LOG2E

    h = x.reshape(b * s, d)
    hc = ctx.reshape(b * lc, d)
    w_in_b, w_out_b = w_in.astype(BF16), w_out.astype(BF16)
    dense_w = [w.astype(BF16) for w in (ffn_w1, ffn_w3, ffn_w2)]
    moe_w13 = _pack_w13(moe_w1, moe_w3)
    moe_w2b = moe_w2.astype(BF16)
    for layer in range(depth):
        last = layer == depth - 1
        m = mod[layer]
        lat = [m[0:b, k * d:(k + 1) * d].reshape(b, 1, d) for k in range(6)]
        cx = [jnp.broadcast_to(m[b:b + 1, k * d:(k + 1) * d].reshape(1, 1, d), (b, 1, d)) for k in range(6)]
        g1n = norm1_g[layer].reshape(1, d)
        g2n = norm2_g[layer].reshape(1, d)
        gq = (jnp.tile(q_norm_g[layer], N_HEADS) * scale).reshape(1, D_ATTN)
        gk = jnp.tile(k_norm_g[layer], N_KV).reshape(1, LANES)
        gc = out_norm_conv_g[layer].reshape(1, D_CONV)
        ga = out_norm_attn_g[layer].reshape(1, D_ATTN)
        sink = attn_sink[layer]
        cw = conv_w[layer]

        cu, q, kv = _inproj(h, g1n, lat[0], lat[1], w_in_b, gq, gk, gm, tables, li=layer, tm=BIG_TILE, seq=s)
        cuc, qc, kvc = _inproj(hc, g1n, cx[0], cx[1], w_in_b, gq, gk, gm, None, li=layer, tm=lc, seq=lc)
        h = _mixer(h, cu, q, kv, kvc, sink, cw, gc, ga, w_out_b, lat[2],
                   li=layer, tq=BIG_TILE, seq=s, ctx_len=lc, windowed=True)
        if not last:
            hc = _mixer(hc, cuc, qc, None, kvc, sink, cw, gc, ga, w_out_b, cx[2],
                        li=layer, tq=lc, seq=lc, ctx_len=lc, windowed=False)

        i = layer // 2
        if layer % 2 == 0:
            h = _dense_ffn(h, g2n, lat[3], lat[4], lat[5], *dense_w, li=i, tm=BIG_TILE, seq=s)
            if not last:
                hc = _dense_ffn(hc, g2n, cx[3], cx[4], cx[5], *dense_w, li=i, tm=lc, seq=lc)
        else:
            router = jnp.zeros((d, LANES), F32).at[:, 0:N_EXPERTS].set(moe_router[i])
            h = _moe(h, g2n, lat[3], lat[4], lat[5], moe_w13, moe_w2b, router,
                     li=i, seq=s, tr=MOE_TILE, tm=MOE_TILE)
            if not last:
                hc = _moe(hc, g2n, cx[3], cx[4], cx[5], moe_w13, moe_w2b, router,
                          li=i, seq=lc, tr=lc, tm=lc)
    return h.reshape(b, s, d)
```
